```python
import math, functools
import jax, jax.numpy as jnp
from jax import lax
import numpy as np

D_MODEL = 2048
BATCH = 4
SEQ = 8192
DEPTH = 2
DEC_BATCH = 32
DEC_SEQ = 16
PAST_LEN = 2048

CHUNK = 64
Q_BLOCK = 128
EPS = 1e-6
ROPE_BASE = 10000.0
D_FF = 5632
SSD_INNER = D_MODEL
SSD_HEAD_DIM = 64
SSD_HEADS = SSD_INNER // SSD_HEAD_DIM
SSD_GROUPS = 4
SSD_HPG = SSD_HEADS // SSD_GROUPS
SSD_STATE = 128
CONV_K = 4
CONV_DIM = SSD_INNER + 2 * SSD_GROUPS * SSD_STATE
RET_HEAD_DIM = 256
RET_INNER = D_MODEL
RET_HEADS = RET_INNER // RET_HEAD_DIM
AB_IN = SSD_INNER + CONV_DIM + SSD_HEADS + 4 * RET_INNER
AB_SPLITS = (SSD_INNER, SSD_INNER + CONV_DIM, SSD_INNER + CONV_DIM + SSD_HEADS,
             SSD_INNER + CONV_DIM + SSD_HEADS + RET_INNER,
             SSD_INNER + CONV_DIM + SSD_HEADS + 2 * RET_INNER,
             SSD_INNER + CONV_DIM + SSD_HEADS + 3 * RET_INNER)
AB_OUT = SSD_INNER + RET_INNER
MLA_HEADS = D_MODEL // 128
Q_LORA = 512
KV_LORA = 512
QK_NOPE = 128
QK_ROPE = 64
V_HEAD = 128
C_IN = Q_LORA + KV_LORA + QK_ROPE
MLA_SCALE = (QK_NOPE + QK_ROPE) ** -0.5
MEM_TOKENS = 256
MEM_HEADS = 4
MEM_HEAD_DIM = 128
MEM_INNER = MEM_HEADS * MEM_HEAD_DIM
N_EVEN = (DEPTH + 1) // 2
N_ODD = DEPTH // 2

kernel_name = 'hybrid_ssd_retention_mla_streaming_step'


def _rms(x):
    xf = x.astype(jnp.float32)
    return xf * lax.rsqrt(jnp.mean(xf * xf, axis=-1, keepdims=True) + EPS)


def rmsnorm(x, g):
    return (_rms(x) * g).astype(x.dtype)


def rope(x, pos):
    half = x.shape[-1] // 2
    inv = ROPE_BASE ** (-jnp.arange(half, dtype=jnp.float32) / half)
    ang = pos.astype(jnp.float32)[:, None] * inv[None, :]
    cos, sin = jnp.cos(ang)[:, None, :], jnp.sin(ang)[:, None, :]
    x1, x2 = x[..., :half], x[..., half:]
    return jnp.concatenate([x1 * cos - x2 * sin, x2 * cos + x1 * sin], axis=-1).astype(x.dtype)


def swiglu(xn, w1, w2):
    a, b = jnp.split(xn @ w1, 2, axis=-1)
    return (jax.nn.silu(a) * b) @ w2


def causal_conv(u, state, w, b):
    T = u.shape[1]
    up = jnp.concatenate([state.astype(u.dtype), u], axis=1)
    y = b + sum(up[:, j:j + T] * w[j] for j in range(CONV_K))
    return jax.nn.silu(y), up[:, T:]


def chunked_decay_scan(q, k, v, log_a, h0):
    f32 = jnp.float32
    Bsz, T, G, N = q.shape
    Hg, P = v.shape[3], v.shape[4]
    L = CHUNK if T % CHUNK == 0 else T
    nC = T // L
    q = q.astype(f32).reshape(Bsz, nC, L, G, N)
    k = k.astype(f32).reshape(Bsz, nC, L, G, N)
    v = v.astype(f32).reshape(Bsz, nC, L, G, Hg, P)
    cum = jnp.cumsum(log_a.astype(f32).reshape(Bsz, nC, L, G, Hg), axis=2)
    cum_t = jnp.moveaxis(cum, 2, -1)
    causal = jnp.tril(jnp.ones((L, L), dtype=bool))
    seg = cum_t[..., :, None] - cum_t[..., None, :]
    decay = jnp.where(causal, jnp.exp(jnp.where(causal, seg, 0.0)), 0.0)
    qk = jnp.einsum('bclgn,bcsgn->bcgls', q, k)
    y_intra = jnp.einsum('bcghls,bcsghp->bclghp', qk[:, :, :, None] * decay, v)

    def step(h, blk):
        q_c, k_c, v_c, cum_c = blk
        y_c = jnp.einsum('blgn,bghnp->blghp', q_c, h) * jnp.exp(cum_c)[..., None]
        last = cum_c[:, -1]
        w = jnp.exp(last[:, None] - cum_c)
        h = jnp.exp(last)[..., None, None] * h + jnp.einsum('blgn,blgh,blghp->bghnp', k_c, w, v_c)
        return h, y_c

    xs = tuple(jnp.moveaxis(a, 1, 0) for a in (q, k, v, cum))
    h_T, y_inter = lax.scan(step, h0.astype(f32), xs)
    y = y_intra + jnp.moveaxis(y_inter, 0, 1)
    return y.reshape(Bsz, T, G, Hg, P), h_T


def ssd_ret_mixer(xn, pos, conv_state, ssd_state, ret_state, w_in, conv_w, conv_b,
                  dt_bias, a_log, d_skip, ssd_norm, w_out):
    f32 = jnp.float32
    Bsz, T, _ = xn.shape
    z, xbc, dt_raw, q, k, v, gate = jnp.split(xn @ w_in, AB_SPLITS, axis=-1)
    xbc, new_conv = causal_conv(xbc, conv_state, conv_w, conv_b)
    xs, b_ssm, c_ssm = jnp.split(xbc, [SSD_INNER, SSD_INNER + SSD_GROUPS * SSD_STATE], axis=-1)
    xs = xs.reshape(Bsz, T, SSD_GROUPS, SSD_HPG, SSD_HEAD_DIM)
    b_ssm = b_ssm.reshape(Bsz, T, SSD_GROUPS, SSD_STATE)
    c_ssm = c_ssm.reshape(Bsz, T, SSD_GROUPS, SSD_STATE)
    dt = jax.nn.softplus((dt_raw + dt_bias).astype(f32)).reshape(Bsz, T, SSD_GROUPS, SSD_HPG)
    a = -jnp.exp(a_log.astype(f32)).reshape(SSD_GROUPS, SSD_HPG)
    h0 = ssd_state.reshape(Bsz, SSD_GROUPS, SSD_HPG, SSD_STATE, SSD_HEAD_DIM)
    y, h_T = chunked_decay_scan(c_ssm, b_ssm, xs * dt[..., None], dt * a, h0)
    y = y + d_skip.reshape(SSD_GROUPS, SSD_HPG, 1) * xs
    y = y.reshape(Bsz, T, SSD_INNER) * jax.nn.silu(z)
    y = rmsnorm(y.reshape(Bsz, T, SSD_GROUPS, SSD_INNER // SSD_GROUPS),
                ssd_norm.reshape(SSD_GROUPS, SSD_INNER // SSD_GROUPS)).reshape(Bsz, T, SSD_INNER)
    qr = rope(q.reshape(Bsz, T, RET_HEADS, RET_HEAD_DIM), pos)
    kr = rope(k.reshape(Bsz, T, RET_HEADS, RET_HEAD_DIM), pos) * RET_HEAD_DIM ** -0.5
    log_gamma = jnp.log1p(-jnp.exp2(-5.0 - jnp.arange(RET_HEADS, dtype=f32)))
    log_a = jnp.broadcast_to(log_gamma[:, None], (Bsz, T, RET_HEADS, 1))
    o, r_T = chunked_decay_scan(qr, kr, v.reshape(Bsz, T, RET_HEADS, 1, RET_HEAD_DIM), log_a,
                                ret_state[:, :, None])
    o = _rms(o[:, :, :, 0]).reshape(Bsz, T, RET_INNER) * jax.nn.silu(gate)
    out = jnp.concatenate([y.astype(f32), o.astype(f32)], axis=-1) @ w_out
    new_ssd = h_T.reshape(Bsz, SSD_HEADS, SSD_STATE, SSD_HEAD_DIM)
    return out.astype(xn.dtype), (new_conv, new_ssd, r_T[:, :, 0])


def mla_attend(q_nope, q_rope, q_pos, ckv, kpe, k_pos, w_uk, w_uv):
    q_lat = jnp.einsum('bqhd,chd->bqhc', q_nope, w_uk)
    s = (jnp.einsum('bqhc,bkc->bhqk', q_lat, ckv)
         + jnp.einsum('bqhr,bkr->bhqk', q_rope, kpe)).astype(jnp.float32) * MLA_SCALE
    visible = (k_pos[None, :] // CHUNK) <= (q_pos[:, None] // CHUNK)
    p = jax.nn.softmax(jnp.where(visible, s, -jnp.inf), axis=-1).astype(ckv.dtype)
    o_lat = jnp.einsum('bhqk,bkc->bqhc', p, ckv)
    return jnp.einsum('bqhc,chd->bqhd', o_lat, w_uv)


def mla_mixer(xn, pos, past_ckv, past_kpe, w_in, q_norm, kv_norm, w_uq, w_uk, w_uv, w_out):
    Bsz, T, _ = xn.shape
    cq, ckv, kpe = jnp.split(xn @ w_in, [Q_LORA, Q_LORA + KV_LORA], axis=-1)
    q = (rmsnorm(cq, q_norm) @ w_uq).reshape(Bsz, T, MLA_HEADS, QK_NOPE + QK_ROPE)
    q_nope, q_rope = q[..., :QK_NOPE], rope(q[..., QK_NOPE:], pos)
    ckv = rmsnorm(ckv, kv_norm)
    kpe = rope(kpe[:, :, None, :], pos)[:, :, 0]
    if past_ckv is None:
        keys_ckv, keys_kpe, k_pos = ckv, kpe, pos
    else:
        keys_ckv = jnp.concatenate([past_ckv.astype(ckv.dtype), ckv], axis=1)
        keys_kpe = jnp.concatenate([past_kpe.astype(kpe.dtype), kpe], axis=1)
        k_pos = jnp.concatenate([jnp.arange(past_ckv.shape[1]), pos])
    attend = functools.partial(mla_attend, ckv=keys_ckv, kpe=keys_kpe, k_pos=k_pos, w_uk=w_uk, w_uv=w_uv)
    if T % Q_BLOCK == 0:
        nb = T // Q_BLOCK
        def blocks(a):
            return jnp.moveaxis(a.reshape(Bsz, nb, Q_BLOCK, *a.shape[2:]), 1, 0)
        o = lax.map(lambda blk: attend(blk[0], blk[1], blk[2]),
                    (blocks(q_nope), blocks(q_rope), pos.reshape(nb, Q_BLOCK)))
        o = jnp.moveaxis(o, 0, 1).reshape(Bsz, T, MLA_HEADS, V_HEAD)
    else:
        o = attend(q_nope, q_rope, pos)
    out = o.reshape(Bsz, T, MLA_HEADS * V_HEAD) @ w_out
    return out.astype(xn.dtype), (ckv, kpe)


def mem_kv(mem, g, w_mkv):
    Bsz, M, _ = mem.shape
    k, v = jnp.split(rmsnorm(mem, g) @ w_mkv, 2, axis=-1)
    return (k.reshape(Bsz, M, MEM_HEADS, MEM_HEAD_DIM), v.reshape(Bsz, M, MEM_HEADS, MEM_HEAD_DIM))


def mem_attend(xn, mem_k, mem_v, w_mq, w_mo):
    Bsz, T, _ = xn.shape
    q = (xn @ w_mq).reshape(Bsz, T, MEM_HEADS, MEM_HEAD_DIM)
    s = jnp.einsum('bthd,bmhd->bhtm', q, mem_k.astype(q.dtype)).astype(jnp.float32) * MEM_HEAD_DIM ** -0.5
    p = jax.nn.softmax(s, axis=-1).astype(q.dtype)
    o = jnp.einsum('bhtm,bmhd->bthd', p, mem_v.astype(q.dtype)).reshape(Bsz, T, MEM_INNER)
    return (o @ w_mo).astype(xn.dtype)


def trunk_layer(x, mix_fn, norms, ffn_w1, ffn_w2, mem_k, mem_v, w_mq, w_mo):
    x = x + 0.5 * swiglu(rmsnorm(x, norms[0]), ffn_w1[0], ffn_w2[0])
    mixed, new_state = mix_fn(rmsnorm(x, norms[1]))
    x = x + mixed
    x = x + mem_attend(rmsnorm(x, norms[2]), mem_k, mem_v, w_mq, w_mo)
    x = x + 0.5 * swiglu(rmsnorm(x, norms[3]), ffn_w1[1], ffn_w2[1])
    return x, new_state


def setup_inputs(seed: int = 0) -> dict:
    key = jax.random.key(seed)
    ks = iter(jax.random.split(key, 48))

    def nrm(shape, scale):
        return jax.random.normal(next(ks), shape, jnp.float32) * scale

    def gain(shape):
        return 1.0 + nrm(shape, 0.01)

    dt_init = jnp.exp(jax.random.uniform(next(ks), (N_EVEN, SSD_HEADS), jnp.float32,
                                         math.log(1e-3), math.log(1e-1)))
    return {
        'x_prompt': nrm((BATCH, SEQ, D_MODEL), 1.0),
        'x_sample': nrm((DEC_BATCH, DEC_SEQ, D_MODEL), 1.0),
        'mem_prompt': nrm((BATCH, MEM_TOKENS, D_MODEL), 1.0),
        'state_conv': nrm((N_EVEN, DEC_BATCH, CONV_K - 1, CONV_DIM), 1.0),
        'state_ssd': nrm((N_EVEN, DEC_BATCH, SSD_HEADS, SSD_STATE, SSD_HEAD_DIM), 0.5),
        'state_ret': nrm((N_EVEN, DEC_BATCH, RET_HEADS, RET_HEAD_DIM, RET_HEAD_DIM), 0.3),
        'cache_ckv': nrm((N_ODD, DEC_BATCH, PAST_LEN, KV_LORA), 1.0),
        'cache_kpe': nrm((N_ODD, DEC_BATCH, PAST_LEN, QK_ROPE), 1.0),
        'cache_mem_k': nrm((DEPTH, DEC_BATCH, MEM_TOKENS, MEM_HEADS, MEM_HEAD_DIM), 1.0),
        'cache_mem_v': nrm((DEPTH, DEC_BATCH, MEM_TOKENS, MEM_HEADS, MEM_HEAD_DIM), 1.0),
        'norms': gain((DEPTH, 4, D_MODEL)),
        'ffn_w1': nrm((DEPTH, 2, D_MODEL, 2 * D_FF), D_MODEL ** -0.5),
        'ffn_w2': nrm((DEPTH, 2, D_FF, D_MODEL), D_FF ** -0.5),
        'mem_norm': gain((DEPTH, D_MODEL)),
        'w_mq': nrm((DEPTH, D_MODEL, MEM_INNER), D_MODEL ** -0.5),
        'w_mkv': nrm((DEPTH, D_MODEL, 2 * MEM_INNER), D_MODEL ** -0.5),
        'w_mo': nrm((DEPTH, MEM_INNER, D_MODEL), MEM_INNER ** -0.5),
        'ab_w_in': nrm((N_EVEN, D_MODEL, AB_IN), D_MODEL ** -0.5),
        'ab_conv_w': nrm((N_EVEN, CONV_K, CONV_DIM), CONV_K ** -0.5),
        'ab_conv_b': nrm((N_EVEN, CONV_DIM), 0.02),
        'ab_dt_bias': dt_init + jnp.log(-jnp.expm1(-dt_init)),
        'ab_a_log': jnp.log(jax.random.uniform(next(ks), (N_EVEN, SSD_HEADS), jnp.float32, 1.0, 16.0)),
        'ab_d_skip': gain((N_EVEN, SSD_HEADS)),
        'ab_ssd_norm': gain((N_EVEN, SSD_INNER)),
        'ab_w_out': nrm((N_EVEN, AB_OUT, D_MODEL), AB_OUT ** -0.5),
        'c_w_in': nrm((N_ODD, D_MODEL, C_IN), D_MODEL ** -0.5),
        'c_q_norm': gain((N_ODD, Q_LORA)),
        'c_kv_norm': gain((N_ODD, KV_LORA)),
        'c_w_uq': nrm((N_ODD, Q_LORA, MLA_HEADS * (QK_NOPE + QK_ROPE)), Q_LORA ** -0.5),
        'c_w_uk': nrm((N_ODD, KV_LORA, MLA_HEADS, QK_NOPE), KV_LORA ** -0.5),
        'c_w_uv': nrm((N_ODD, KV_LORA, MLA_HEADS, V_HEAD), KV_LORA ** -0.5),
        'c_w_out': nrm((N_ODD, MLA_HEADS * V_HEAD, D_MODEL), (MLA_HEADS * V_HEAD) ** -0.5),
        'final_norm': gain((D_MODEL,)),
    }


def reference(x_prompt, x_sample, mem_prompt, state_conv, state_ssd, state_ret, cache_ckv, cache_kpe,
              cache_mem_k, cache_mem_v, norms, ffn_w1, ffn_w2, mem_norm, w_mq, w_mkv, w_mo,
              ab_w_in, ab_conv_w, ab_conv_b, ab_dt_bias, ab_a_log, ab_d_skip, ab_ssd_norm, ab_w_out,
              c_w_in, c_q_norm, c_kv_norm, c_w_uq, c_w_uk, c_w_uv, c_w_out, final_norm):
    f32 = jnp.float32
    bp, tp = x_prompt.shape[0], x_prompt.shape[1]
    ts = x_sample.shape[1]
    past_len = cache_ckv.shape[2]
    pos_p = jnp.arange(tp)
    pos_s = past_len + jnp.arange(ts)
    xp, xs = x_prompt, x_sample
    conv_p, ssd_p, ret_p, ckv_p, kpe_p, memk_p, memv_p = [], [], [], [], [], [], []
    conv_s, ssd_s, ret_s, ckv_s, kpe_s = [], [], [], [], []
    for i in range(DEPTH):
        j = i // 2
        mk_p, mv_p = mem_kv(mem_prompt, mem_norm[i], w_mkv[i])
        memk_p.append(mk_p)
        memv_p.append(mv_p)
        if i % 2 == 0:
            ab = dict(w_in=ab_w_in[j], conv_w=ab_conv_w[j], conv_b=ab_conv_b[j], dt_bias=ab_dt_bias[j],
                      a_log=ab_a_log[j], d_skip=ab_d_skip[j], ssd_norm=ab_ssd_norm[j], w_out=ab_w_out[j])
            mix_p = functools.partial(
                ssd_ret_mixer, pos=pos_p,
                conv_state=jnp.zeros((bp, CONV_K - 1, CONV_DIM), x_prompt.dtype),
                ssd_state=jnp.zeros((bp, SSD_HEADS, SSD_STATE, SSD_HEAD_DIM), f32),
                ret_state=jnp.zeros((bp, RET_HEADS, RET_HEAD_DIM, RET_HEAD_DIM), f32), **ab)
            mix_s = functools.partial(ssd_ret_mixer, pos=pos_s, conv_state=state_conv[j],
                                      ssd_state=state_ssd[j], ret_state=state_ret[j], **ab)
        else:
            cp = dict(w_in=c_w_in[j], q_norm=c_q_norm[j], kv_norm=c_kv_norm[j], w_uq=c_w_uq[j],
                      w_uk=c_w_uk[j], w_uv=c_w_uv[j], w_out=c_w_out[j])
            mix_p = functools.partial(mla_mixer, pos=pos_p, past_ckv=None, past_kpe=None, **cp)
            mix_s = functools.partial(mla_mixer, pos=pos_s, past_ckv=cache_ckv[j], past_kpe=cache_kpe[j], **cp)
        layer = functools.partial(trunk_layer, norms=norms[i], ffn_w1=ffn_w1[i], ffn_w2=ffn_w2[i],
                                  w_mq=w_mq[i], w_mo=w_mo[i])
        xp, st_p = layer(xp, mix_p, mem_k=mk_p, mem_v=mv_p)
        xs, st_s = layer(xs, mix_s, mem_k=cache_mem_k[i], mem_v=cache_mem_v[i])
        if i % 2 == 0:
            conv_p.append(st_p[0]); ssd_p.append(st_p[1]); ret_p.append(st_p[2])
            conv_s.append(st_s[0]); ssd_s.append(st_s[1]); ret_s.append(st_s[2])
        else:
            ckv_p.append(st_p[0]); kpe_p.append(st_p[1])
            ckv_s.append(st_s[0]); kpe_s.append(st_s[1])
    y_prompt = rmsnorm(xp, final_norm)
    y_sample = rmsnorm(xs, final_norm)
    return (y_prompt, y_sample,
            jnp.stack(conv_p), jnp.stack(ssd_p), jnp.stack(ret_p), jnp.stack(ckv_p), jnp.stack(kpe_p),
            jnp.stack(memk_p), jnp.stack(memv_p),
            jnp.stack(conv_s), jnp.stack(ssd_s), jnp.stack(ret_s), jnp.stack(ckv_s), jnp.stack(kpe_s))
```

```python
import functools
import math

import jax
import jax.numpy as jnp
from jax import lax
from jax.experimental import pallas as pl
from jax.experimental.pallas import tpu as pltpu

F32 = jnp.float32
MXU_DTYPE = jnp.bfloat16

EPS = 1e-6
CHUNK = 64
ROPE_BASE = 10000.0
CONV_K = 4
NEG = -1e30

V7X_VMEM_BYTES = 64 * 1024 * 1024
V7X_LANES = 128
CONV_PAD_ROWS = 8


def _params(semantics, vmem_bytes):
    limit = min(int(vmem_bytes), V7X_VMEM_BYTES - (4 << 20))
    return pltpu.CompilerParams(dimension_semantics=semantics, vmem_limit_bytes=limit)


def _nbytes(shape, dtype):
    return math.prod(shape) * jnp.dtype(dtype).itemsize


def _pick(n, prefs):
    for p in prefs:
        if n % p == 0:
            return p
    return n


def _dot(a, b):
    return jnp.dot(a, b, preferred_element_type=F32)


def _dot_nt(a, b):
    return lax.dot_general(a, b, (((1,), (1,)), ((), ())), preferred_element_type=F32)


def _dot_tn(a, b):
    return lax.dot_general(a, b, (((0,), (0,)), ((), ())), preferred_element_type=F32)


def _split3(x):
    hi = x.astype(MXU_DTYPE)
    r = x - hi.astype(F32)
    mid = r.astype(MXU_DTYPE)
    lo = (r - mid.astype(F32)).astype(MXU_DTYPE)
    return hi, mid, lo


def _rms(xf, g=None):
    y = xf * lax.rsqrt(jnp.mean(xf * xf, axis=-1, keepdims=True) + EPS)
    return y if g is None else y * g


def _silu(a):
    return a * (1.0 / (1.0 + jnp.exp(-a)))


def _swap32(x):
    w = x.shape[-1]
    lane = lax.broadcasted_iota(jnp.int32, x.shape, x.ndim - 1)
    fwd = pltpu.roll(x, w - 32, x.ndim - 1)
    bwd = pltpu.roll(x, 32, x.ndim - 1)
    return jnp.where((lane & 63) < 32, fwd, bwd)


def _rms_matmul_kernel(x_ref, g_ref, w_ref, o_ref, xn_ref):
    @pl.when(pl.program_id(1) == 0)
    def _():
        xn_ref[...] = _rms(x_ref[...], g_ref[...]).astype(xn_ref.dtype)

    o_ref[...] = _dot(xn_ref[...], w_ref[...]).astype(o_ref.dtype)


def rms_matmul(x, g, w, *, out_dtype=F32):
    n, d = x.shape
    nout = w.shape[1]
    tm = _pick(n, (512, 256, 128))
    tn = _pick(nout, (1920, 1152, 1024, 512, 256, 128))
    vmem = (2 * (_nbytes((tm, d), F32) + _nbytes((d, tn), w.dtype) + _nbytes((tm, tn), out_dtype))
            + _nbytes((tm, d), MXU_DTYPE) + _nbytes((tm, tn), F32) + (8 << 20))
    return pl.pallas_call(
        _rms_matmul_kernel,
        out_shape=jax.ShapeDtypeStruct((n, nout), out_dtype),
        grid=(n // tm, nout // tn),
        in_specs=[pl.BlockSpec((tm, d), lambda i, j: (i, 0)),
                  pl.BlockSpec((1, d), lambda i, j: (0, 0)),
                  pl.BlockSpec((d, tn), lambda i, j: (0, j))],
        out_specs=pl.BlockSpec((tm, tn), lambda i, j: (i, j)),
        scratch_shapes=[pltpu.VMEM((tm, d), MXU_DTYPE)],
        compiler_params=_params(("parallel", "arbitrary"), vmem),
        name="rms_matmul",
    )(x, g.reshape(1, d), w)


def _ffn_kernel(x_ref, g_ref, w1a_ref, w1b_ref, w2_ref, o_ref, xn_ref, acc_ref):
    f = pl.program_id(1)

    @pl.when(f == 0)
    def _():
        xn_ref[...] = _rms(x_ref[...], g_ref[...]).astype(xn_ref.dtype)
        acc_ref[...] = jnp.zeros_like(acc_ref)

    xn = xn_ref[...]
    a = _dot(xn, w1a_ref[...])
    b = _dot(xn, w1b_ref[...])
    h = (_silu(a) * b).astype(MXU_DTYPE)
    acc_ref[...] += _dot(h, w2_ref[...])

    @pl.when(f == pl.num_programs(1) - 1)
    def _():
        o_ref[...] = x_ref[...] + 0.5 * acc_ref[...]


def ffn(x, g, w1, w2):
    n, d = x.shape
    dff = w2.shape[0]
    tm = _pick(n, (512, 256, 128))
    tf = _pick(dff, (512, 256, 128))
    nf = dff // tf
    vmem = (2 * (2 * _nbytes((tm, d), F32) + 3 * _nbytes((d, tf), w1.dtype))
            + _nbytes((tm, d), MXU_DTYPE) + _nbytes((tm, d), F32) + 4 * _nbytes((tm, tf), F32) + (8 << 20))
    return pl.pallas_call(
        _ffn_kernel,
        out_shape=jax.ShapeDtypeStruct((n, d), F32),
        grid=(n // tm, nf),
        in_specs=[pl.BlockSpec((tm, d), lambda i, f: (i, 0)),
                  pl.BlockSpec((1, d), lambda i, f: (0, 0)),
                  pl.BlockSpec((d, tf), lambda i, f: (0, f)),
                  pl.BlockSpec((d, tf), lambda i, f: (0, f + nf)),
                  pl.BlockSpec((tf, d), lambda i, f: (f, 0))],
        out_specs=pl.BlockSpec((tm, d), lambda i, f: (i, 0)),
        scratch_shapes=[pltpu.VMEM((tm, d), MXU_DTYPE), pltpu.VMEM((tm, d), F32)],
        compiler_params=_params(("parallel", "arbitrary"), vmem),
        name="ffn",
    )(x, g.reshape(1, d), w1, w1, w2)


def _matmul_residual_kernel(*refs, n_in):
    x_ref = refs[0]
    o_ref = refs[1 + 2 * n_in]
    acc = x_ref[...]
    for h_ref, w_ref in zip(refs[1:1 + n_in], refs[1 + n_in:1 + 2 * n_in]):
        acc = acc + _dot(h_ref[...], w_ref[...])
    o_ref[...] = acc


def matmul_residual(x, hs, ws):
    n, d = x.shape
    tm = _pick(n, (512, 256, 128))
    tn = _pick(d, (1024, 512, 256, 128))
    vmem = 2 * 2 * _nbytes((tm, tn), F32) + (8 << 20)
    in_specs = [pl.BlockSpec((tm, tn), lambda i, j: (i, j))]
    for h in hs:
        in_specs.append(pl.BlockSpec((tm, h.shape[1]), lambda i, j: (i, 0)))
        vmem += 2 * _nbytes((tm, h.shape[1]), h.dtype)
    for w in ws:
        in_specs.append(pl.BlockSpec((w.shape[0], tn), lambda i, j: (0, j)))
        vmem += 2 * _nbytes((w.shape[0], tn), w.dtype)
    return pl.pallas_call(
        functools.partial(_matmul_residual_kernel, n_in=len(hs)),
        out_shape=jax.ShapeDtypeStruct((n, d), F32),
        grid=(n // tm, d // tn),
        in_specs=in_specs,
        out_specs=pl.BlockSpec((tm, tn), lambda i, j: (i, j)),
        compiler_params=_params(("parallel", "arbitrary"), vmem),
        name="matmul_residual",
    )(x, *hs, *ws)


def _rmsnorm_kernel(x_ref, g_ref, o_ref):
    o_ref[...] = _rms(x_ref[...], g_ref[...])


def rmsnorm(x, g):
    n, d = x.shape
    tm = _pick(n, (512, 256, 128))
    return pl.pallas_call(
        _rmsnorm_kernel,
        out_shape=jax.ShapeDtypeStruct((n, d), F32),
        grid=(n // tm,),
        in_specs=[pl.BlockSpec((tm, d), lambda i: (i, 0)), pl.BlockSpec((1, d), lambda i: (0, 0))],
        out_specs=pl.BlockSpec((tm, d), lambda i: (i, 0)),
        compiler_params=_params(("parallel",), 6 * _nbytes((tm, d), F32) + (8 << 20)),
        name="rmsnorm",
    )(x, g.reshape(1, d))


def _mem_attn_kernel(x_ref, g_ref, wq_ref, k_ref, v_ref, wo_ref, o_ref, att_ref, *, bt, tq, heads, hd):
    xf = x_ref[...]
    xn = _rms(xf, g_ref[...]).astype(MXU_DTYPE)
    q = _dot(xn, wq_ref[...]).astype(MXU_DTYPE)
    scale = hd ** -0.5
    for b in range(bt):
        for h in range(heads):
            qh = q[b * tq:(b + 1) * tq, h * hd:(h + 1) * hd]
            kh = k_ref[b, :, h * hd:(h + 1) * hd].astype(MXU_DTYPE)
            vh = v_ref[b, :, h * hd:(h + 1) * hd].astype(MXU_DTYPE)
            s = _dot_nt(qh, kh) * scale
            m = jnp.max(s, axis=-1, keepdims=True)
            p = jnp.exp(s - m)
            l = jnp.sum(p, axis=-1, keepdims=True)
            oh = _dot(p.astype(MXU_DTYPE), vh) / l
            att_ref[b * tq:(b + 1) * tq, h * hd:(h + 1) * hd] = oh.astype(att_ref.dtype)
    o_ref[...] = xf + _dot(att_ref[...], wo_ref[...])


def mem_attn(x, g, wq, mem_k, mem_v, wo, *, batch, heads):
    n, d = x.shape
    t = n // batch
    m, inner = mem_k.shape[1], mem_k.shape[2]
    hd = inner // heads
    if t >= 128:
        bt, tq = 1, _pick(t, (512, 256, 128))
    else:
        bt, tq = _pick(batch, (8, 4, 2, 1)), t
    nt = t // tq
    rows = bt * tq
    vmem = (2 * (2 * _nbytes((rows, d), F32) + 2 * _nbytes((bt, m, inner), mem_k.dtype)
                 + 2 * _nbytes((d, inner), wq.dtype))
            + 2 * _nbytes((rows, d), F32) + (8 << 20))
    return pl.pallas_call(
        functools.partial(_mem_attn_kernel, bt=bt, tq=tq, heads=heads, hd=hd),
        out_shape=jax.ShapeDtypeStruct((n, d), F32),
        grid=(batch // bt, nt),
        in_specs=[pl.BlockSpec((rows, d), lambda b, i: (b * nt + i, 0)),
                  pl.BlockSpec((1, d), lambda b, i: (0, 0)),
                  pl.BlockSpec((d, inner), lambda b, i: (0, 0)),
                  pl.BlockSpec((bt, m, inner), lambda b, i: (b, 0, 0)),
                  pl.BlockSpec((bt, m, inner), lambda b, i: (b, 0, 0)),
                  pl.BlockSpec((inner, d), lambda b, i: (0, 0))],
        out_specs=pl.BlockSpec((rows, d), lambda b, i: (b * nt + i, 0)),
        scratch_shapes=[pltpu.VMEM((rows, inner), MXU_DTYPE)],
        compiler_params=_params(("parallel", "arbitrary"), vmem),
        name="mem_attn",
    )(x, g.reshape(1, d), wq, mem_k, mem_v, wo)


def _mla_prep_kernel(p_ref, qn_ref, kvn_ref, wuq_ref, cos_ref, sin_ref,
                     q_ref, ckv_ref, kpe_ref, kcat_ref, *, q_lora, kv_lora, rope_dim, nope_cols):
    cos = cos_ref[...]
    sin = sin_ref[...]
    cq = p_ref[:, 0:q_lora]
    cqn = _rms(cq, qn_ref[...]).astype(MXU_DTYPE)
    q = _dot(cqn, wuq_ref[...])
    q_ref[:, 0:nope_cols] = q[:, 0:nope_cols].astype(q_ref.dtype)
    for s in range((q.shape[1] - nope_cols) // V7X_LANES):
        lo = nope_cols + s * V7X_LANES
        xs = q[:, lo:lo + V7X_LANES]
        q_ref[:, lo:lo + V7X_LANES] = (xs * cos + _swap32(xs) * sin).astype(q_ref.dtype)
    ckv = _rms(p_ref[:, q_lora:q_lora + kv_lora], kvn_ref[...])
    ckv_ref[...] = ckv
    kp = p_ref[:, q_lora + kv_lora:q_lora + kv_lora + V7X_LANES]
    kpr = kp * cos + _swap32(kp) * sin
    kpe_ref[...] = kpr[:, 0:rope_dim]
    kcat_ref[:, 0:kv_lora] = ckv.astype(kcat_ref.dtype)
    kcat_ref[:, kv_lora:kv_lora + V7X_LANES] = kpr.astype(kcat_ref.dtype)


def mla_prep(proj, q_norm, kv_norm, w_uq, cos_rows, sin_rows, *, q_lora, kv_lora, rope_dim, nope_cols):
    n = proj.shape[0]
    qcols = w_uq.shape[1]
    tm = _pick(n, (256, 128))
    vmem = (2 * (_nbytes((tm, proj.shape[1]), F32) + _nbytes(w_uq.shape, w_uq.dtype)
                 + _nbytes((tm, qcols), MXU_DTYPE) + 3 * _nbytes((tm, kv_lora + V7X_LANES), F32))
            + 3 * _nbytes((tm, qcols), F32) + (8 << 20))
    return pl.pallas_call(
        functools.partial(_mla_prep_kernel, q_lora=q_lora, kv_lora=kv_lora, rope_dim=rope_dim,
                          nope_cols=nope_cols),
        out_shape=(jax.ShapeDtypeStruct((n, qcols), MXU_DTYPE),
                   jax.ShapeDtypeStruct((n, kv_lora), F32),
                   jax.ShapeDtypeStruct((n, rope_dim), F32),
                   jax.ShapeDtypeStruct((n, kv_lora + V7X_LANES), MXU_DTYPE)),
        grid=(n // tm,),
        in_specs=[pl.BlockSpec((tm, proj.shape[1]), lambda i: (i, 0)),
                  pl.BlockSpec((1, q_lora), lambda i: (0, 0)),
                  pl.BlockSpec((1, kv_lora), lambda i: (0, 0)),
                  pl.BlockSpec(w_uq.shape, lambda i: (0, 0)),
                  pl.BlockSpec((tm, V7X_LANES), lambda i: (i, 0)),
                  pl.BlockSpec((tm, V7X_LANES), lambda i: (i, 0))],
        out_specs=(pl.BlockSpec((tm, qcols), lambda i: (i, 0)),
                   pl.BlockSpec((tm, kv_lora), lambda i: (i, 0)),
                   pl.BlockSpec((tm, rope_dim), lambda i: (i, 0)),
                   pl.BlockSpec((tm, kv_lora + V7X_LANES), lambda i: (i, 0))),
        compiler_params=_params(("parallel",), vmem),
        name="mla_prep",
    )(proj, q_norm.reshape(1, -1), kv_norm.reshape(1, -1), w_uq, cos_rows, sin_rows)


def _last_kv_block(i, *, tq, tk, q_offset, nk):
    last_q = q_offset + (i + 1) * tq - 1
    last_key = (last_q // CHUNK + 1) * CHUNK - 1
    return jnp.minimum(last_key // tk, nk - 1)


def _mla_attn_kernel(q_ref, k_ref, wuk_ref, wuv_ref, o_ref, qs_ref, acc_ref, m_ref, l_ref,
                     *, heads, tq, tk, nope, lat, q_offset, kv_len, scale):
    i = pl.program_id(1)
    j = pl.program_id(2)
    nk = pl.num_programs(2)
    j_last = _last_kv_block(i, tq=tq, tk=tk, q_offset=q_offset, nk=nk)
    rope_lo = heads * nope

    @pl.when(j == 0)
    def _():
        for h in range(heads):
            qn = q_ref[0, :, h * nope:(h + 1) * nope]
            qs_ref[h * tq:(h + 1) * tq, 0:lat] = _dot(qn, wuk_ref[h]).astype(qs_ref.dtype)
            qs_ref[h * tq:(h + 1) * tq, lat:lat + V7X_LANES] = (
                q_ref[0, :, rope_lo + h * V7X_LANES:rope_lo + (h + 1) * V7X_LANES])
        m_ref[...] = jnp.full_like(m_ref, NEG)
        l_ref[...] = jnp.zeros_like(l_ref)
        acc_ref[...] = jnp.zeros_like(acc_ref)

    @pl.when(j <= j_last)
    def _():
        k = k_ref[0]
        s = _dot_nt(qs_ref[...], k) * scale
        qpos = q_offset + i * tq + lax.broadcasted_iota(jnp.int32, (tq, tk), 0)
        kpos = j * tk + lax.broadcasted_iota(jnp.int32, (tq, tk), 1)
        visible = ((kpos // CHUNK) <= (qpos // CHUNK)) & (kpos < kv_len)
        s = jnp.where(visible[None], s.reshape(heads, tq, tk), NEG).reshape(heads * tq, tk)
        m_old = m_ref[...]
        m_new = jnp.maximum(m_old, jnp.max(s, axis=-1, keepdims=True))
        alpha = jnp.exp(m_old - m_new)
        p = jnp.exp(s - m_new)
        l_ref[...] = alpha * l_ref[...] + jnp.sum(p, axis=-1, keepdims=True)
        acc_ref[...] = alpha * acc_ref[...] + _dot(p.astype(MXU_DTYPE), k[:, 0:lat])
        m_ref[...] = m_new

    @pl.when(j == j_last)
    def _():
        for h in range(heads):
            ol = acc_ref[h * tq:(h + 1) * tq, :] / l_ref[h * tq:(h + 1) * tq, :]
            o_ref[0, :, h * wuv_ref.shape[2]:(h + 1) * wuv_ref.shape[2]] = (
                _dot(ol.astype(MXU_DTYPE), wuv_ref[h]).astype(o_ref.dtype))


def mla_attn(q, kcat, w_uk, w_uv, *, tq, tk, q_offset, kv_len, scale):
    b, t, _ = q.shape
    s = kcat.shape[1]
    heads, nope, lat = w_uk.shape
    vd = w_uv.shape[2]
    nq, nk = t // tq, s // tk
    rows = heads * tq
    last = functools.partial(_last_kv_block, tq=tq, tk=tk, q_offset=q_offset, nk=nk)
    vmem = (2 * (_nbytes((tq, q.shape[2]), q.dtype) + _nbytes((tk, kcat.shape[2]), kcat.dtype)
                 + 2 * _nbytes(w_uk.shape, w_uk.dtype) + _nbytes((tq, heads * vd), MXU_DTYPE))
            + _nbytes((rows, lat + V7X_LANES), MXU_DTYPE) + _nbytes((rows, lat), F32)
            + 2 * _nbytes((rows, V7X_LANES), F32) + 4 * _nbytes((rows, tk), F32) + (8 << 20))
    return pl.pallas_call(
        functools.partial(_mla_attn_kernel, heads=heads, tq=tq, tk=tk, nope=nope, lat=lat,
                          q_offset=q_offset, kv_len=kv_len, scale=scale),
        out_shape=jax.ShapeDtypeStruct((b, t, heads * vd), MXU_DTYPE),
        grid=(b, nq, nk),
        in_specs=[pl.BlockSpec((1, tq, q.shape[2]), lambda bb, i, j: (bb, i, 0)),
                  pl.BlockSpec((1, tk, kcat.shape[2]), lambda bb, i, j: (bb, jnp.minimum(j, last(i)), 0)),
                  pl.BlockSpec(w_uk.shape, lambda bb, i, j: (0, 0, 0)),
                  pl.BlockSpec(w_uv.shape, lambda bb, i, j: (0, 0, 0))],
        out_specs=pl.BlockSpec((1, tq, heads * vd), lambda bb, i, j: (bb, i, 0)),
        scratch_shapes=[pltpu.VMEM((rows, lat + V7X_LANES), MXU_DTYPE),
                        pltpu.VMEM((rows, lat), F32),
                        pltpu.VMEM((rows, 1), F32),
                        pltpu.VMEM((rows, 1), F32)],
        compiler_params=_params(("parallel", "parallel", "arbitrary"), vmem),
        name="mla_attn",
    )(q, kcat, w_uk, w_uv)


def _ssd_kernel(*refs, L, groups, hpg, hdim, nstate, has_state):
    if has_state:
        (z_ref, xs_ref, b_ref, c_ref, dt_ref, cw_ref, cb_ref, dtb_ref, alog_ref, dsk_ref, nrm_ref,
         cst_ref, hst_ref, y_ref, ncv_ref, nst_ref, xbuf, h_scr, cum_scr, xdt_scr, yin_scr) = refs
    else:
        (z_ref, xs_ref, b_ref, c_ref, dt_ref, cw_ref, cb_ref, dtb_ref, alog_ref, dsk_ref, nrm_ref,
         cst_ref, y_ref, ncv_ref, nst_ref, xbuf, h_scr, cum_scr, xdt_scr, yin_scr) = refs
        hst_ref = None
    c = pl.program_id(1)
    nc = pl.num_programs(1)
    dx = groups * hpg * hdim
    dn = groups * nstate
    gw = hpg * hdim
    pad = CONV_PAD_ROWS

    @pl.when(c == 0)
    def _():
        xbuf[0:pad, :] = cst_ref[0]
        if has_state:
            h_scr[...] = hst_ref[0]
        else:
            h_scr[...] = jnp.zeros_like(h_scr)

    xbuf[pad:pad + L, 0:dx] = xs_ref[...]
    xbuf[pad:pad + L, dx:dx + dn] = b_ref[...]
    xbuf[pad:pad + L, dx + dn:dx + 2 * dn] = c_ref[...]
    acc = xbuf[pad - CONV_K + 1:pad - CONV_K + 1 + L, :] * cw_ref[0:1, :] + cb_ref[...]
    for jj in range(1, CONV_K):
        lo = pad - CONV_K + 1 + jj
        acc = acc + xbuf[lo:lo + L, :] * cw_ref[jj:jj + 1, :]
    xc = _silu(acc)
    tail = xbuf[L:L + pad, :]
    xbuf[0:pad, :] = tail

    @pl.when(c == nc - 1)
    def _():
        ncv_ref[0] = tail

    dtr = dt_ref[...] + dtb_ref[...]
    dt = jnp.maximum(dtr, 0.0) + jnp.log1p(jnp.exp(-jnp.abs(dtr)))
    la = dt * (-jnp.exp(alog_ref[...]))
    row = lax.broadcasted_iota(jnp.int32, (L, L), 0)
    col = lax.broadcasted_iota(jnp.int32, (L, L), 1)
    causal = row >= col
    tri = jnp.where(causal, 1.0, 0.0).astype(MXU_DTYPE)
    cum = sum(_dot(tri, piece) for piece in _split3(la))
    nh = groups * hpg
    eye = jnp.where(lax.broadcasted_iota(jnp.int32, (V7X_LANES, V7X_LANES), 0)
                    == lax.broadcasted_iota(jnp.int32, (V7X_LANES, V7X_LANES), 1), 1.0, 0.0).astype(MXU_DTYPE)
    cum_t = sum(_dot_nt(eye, piece) for piece in _split3(cum))

    half = lax.broadcasted_iota(jnp.int32, (L, V7X_LANES), 1) < hdim
    per_vreg = V7X_LANES // hdim
    for g in range(groups):
        bg = xc[:, dx + g * nstate:dx + (g + 1) * nstate]
        cg = xc[:, dx + dn + g * nstate:dx + dn + (g + 1) * nstate].astype(MXU_DTYPE)
        qk = _dot_nt(cg, bg.astype(MXU_DTYPE))
        for sl in range(gw // V7X_LANES):
            lane0 = g * gw + sl * V7X_LANES
            h0 = lane0 // hdim
            cb = [jnp.broadcast_to(cum[:, h0 + u:h0 + u + 1], (L, V7X_LANES)) for u in range(per_vreg)]
            db = [jnp.broadcast_to(dt[:, h0 + u:h0 + u + 1], (L, V7X_LANES)) for u in range(per_vreg)]
            cum_e = jnp.where(half, cb[0], cb[1])
            dt_e = jnp.where(half, db[0], db[1])
            xdt = xc[:, lane0:lane0 + V7X_LANES] * dt_e
            xdt_m = xdt.astype(MXU_DTYPE)
            ys = []
            for u in range(per_vreg):
                seg = cb[u][:, 0:L] - cum_t[h0 + u:h0 + u + 1, :]
                decay = jnp.exp(jnp.where(causal, seg, NEG))
                ys.append(_dot((qk * decay).astype(MXU_DTYPE), xdt_m))
            cum_scr[:, lane0:lane0 + V7X_LANES] = cum_e
            xdt_scr[:, lane0:lane0 + V7X_LANES] = xdt
            yin_scr[:, lane0:lane0 + V7X_LANES] = jnp.where(half, ys[0], ys[1])

    for g in range(groups):
        gs = slice(g * gw, (g + 1) * gw)
        bg = xc[:, dx + g * nstate:dx + (g + 1) * nstate].astype(MXU_DTYPE)
        cg = xc[:, dx + dn + g * nstate:dx + dn + (g + 1) * nstate].astype(MXU_DTYPE)
        cum_g = cum_scr[:, gs]
        last = cum_scr[L - 1:L, gs]
        hg = h_scr[g]
        y_inter = _dot(cg, hg.astype(MXU_DTYPE)) * jnp.exp(cum_g)
        wx = (jnp.exp(last - cum_g) * xdt_scr[:, gs]).astype(MXU_DTYPE)
        h_scr[g] = jnp.exp(last) * hg + _dot_tn(bg, wx)
        y = yin_scr[:, gs] + y_inter + dsk_ref[:, gs] * xc[:, gs]
        y = y * _silu(z_ref[:, gs])
        y_ref[:, gs] = (_rms(y) * nrm_ref[:, gs]).astype(y_ref.dtype)

    @pl.when(c == nc - 1)
    def _():
        nst_ref[0] = h_scr[...]


def ssd_scan(proj, col, conv_w, conv_b, dt_bias, a_log, d_skip_e, ssd_norm, conv_state, ssd_state,
             *, batch, L, groups, hpg, hdim, nstate):
    n = proj.shape[0]
    t = n // batch
    nc = t // L
    dx = groups * hpg * hdim
    dn = groups * nstate
    cdim = dx + 2 * dn
    has_state = ssd_state is not None
    rowmap = lambda b, c: b * nc + c
    in_specs = [pl.BlockSpec((L, dx), lambda b, c: (rowmap(b, c), col["z"])),
                pl.BlockSpec((L, dx), lambda b, c: (rowmap(b, c), col["xs"])),
                pl.BlockSpec((L, dn), lambda b, c: (rowmap(b, c), col["B"])),
                pl.BlockSpec((L, dn), lambda b, c: (rowmap(b, c), col["C"])),
                pl.BlockSpec((L, V7X_LANES), lambda b, c: (rowmap(b, c), col["dt"])),
                pl.BlockSpec((CONV_K, cdim), lambda b, c: (0, 0)),
                pl.BlockSpec((1, cdim), lambda b, c: (0, 0)),
                pl.BlockSpec((1, V7X_LANES), lambda b, c: (0, 0)),
                pl.BlockSpec((1, V7X_LANES), lambda b, c: (0, 0)),
                pl.BlockSpec((1, dx), lambda b, c: (0, 0)),
                pl.BlockSpec((1, dx), lambda b, c: (0, 0)),
                pl.BlockSpec((1, CONV_PAD_ROWS, cdim), lambda b, c: (b, 0, 0))]
    args = [proj, proj, proj, proj, proj, conv_w, conv_b.reshape(1, cdim), dt_bias, a_log,
            d_skip_e, ssd_norm.reshape(1, dx), conv_state]
    if has_state:
        in_specs.append(pl.BlockSpec((1, groups, nstate, hpg * hdim), lambda b, c: (b, 0, 0, 0)))
        args.append(ssd_state)
    state_bytes = _nbytes((groups, nstate, hpg * hdim), F32)
    vmem = (2 * (2 * _nbytes((L, dx), F32) + 2 * _nbytes((L, dn), F32) + _nbytes((L, dx), MXU_DTYPE)
                 + 2 * _nbytes((CONV_PAD_ROWS, cdim), F32) + 2 * state_bytes)
            + state_bytes + 12 * _nbytes((L + CONV_PAD_ROWS, cdim), F32) + (8 << 20))
    return pl.pallas_call(
        functools.partial(_ssd_kernel, L=L, groups=groups, hpg=hpg, hdim=hdim, nstate=nstate,
                          has_state=has_state),
        out_shape=(jax.ShapeDtypeStruct((n, dx), MXU_DTYPE),
                   jax.ShapeDtypeStruct((batch, CONV_PAD_ROWS, cdim), F32),
                   jax.ShapeDtypeStruct((batch, groups, nstate, hpg * hdim), F32)),
        grid=(batch, nc),
        in_specs=in_specs,
        out_specs=(pl.BlockSpec((L, dx), lambda b, c: (rowmap(b, c), 0)),
                   pl.BlockSpec((1, CONV_PAD_ROWS, cdim), lambda b, c: (b, 0, 0)),
                   pl.BlockSpec((1, groups, nstate, hpg * hdim), lambda b, c: (b, 0, 0, 0))),
        scratch_shapes=[pltpu.VMEM((L + CONV_PAD_ROWS, cdim), F32),
                        pltpu.VMEM((groups, nstate, hpg * hdim), F32),
                        pltpu.VMEM((L, dx), F32),
                        pltpu.VMEM((L, dx), F32),
                        pltpu.VMEM((L, dx), F32)],
        compiler_params=_params(("parallel", "arbitrary"), vmem),
        name="ssd_scan",
    )(*args)


def _ret_kernel(*refs, L, heads, hd, has_state):
    if has_state:
        (q_ref, k_ref, v_ref, gate_ref, cos_ref, sin_ref, dec_ref, ecum_ref, wv_ref, sdec_ref, st_ref,
         o_ref, nst_ref, s_scr) = refs
    else:
        (q_ref, k_ref, v_ref, gate_ref, cos_ref, sin_ref, dec_ref, ecum_ref, wv_ref, sdec_ref,
         o_ref, nst_ref, s_scr) = refs
        st_ref = None
    c = pl.program_id(1)
    nc = pl.num_programs(1)
    half = hd // 2

    @pl.when(c == 0)
    def _():
        if has_state:
            s_scr[...] = st_ref[0]
        else:
            s_scr[...] = jnp.zeros_like(s_scr)

    cos = cos_ref[...]
    sin = sin_ref[...]

    def rope(ref, h, mult):
        x1 = ref[:, h * hd:h * hd + half]
        x2 = ref[:, h * hd + half:(h + 1) * hd]
        out = jnp.concatenate([x1 * cos - x2 * sin, x2 * cos + x1 * sin], axis=-1)
        return (out * mult).astype(MXU_DTYPE) if mult != 1.0 else out.astype(MXU_DTYPE)

    for h in range(heads):
        hs = slice(h * hd, (h + 1) * hd)
        qr = rope(q_ref, h, 1.0)
        kr = rope(k_ref, h, hd ** -0.5)
        vf = v_ref[:, hs]
        qk = _dot_nt(qr, kr)
        y = _dot((qk * dec_ref[h]).astype(MXU_DTYPE), vf.astype(MXU_DTYPE))
        s_old = s_scr[h]
        y = y + _dot(qr, s_old.astype(MXU_DTYPE)) * ecum_ref[h]
        s_scr[h] = sdec_ref[h] * s_old + _dot_tn(kr, (wv_ref[h] * vf).astype(MXU_DTYPE))
        o_ref[:, hs] = (_rms(y) * _silu(gate_ref[:, hs])).astype(o_ref.dtype)

    @pl.when(c == nc - 1)
    def _():
        nst_ref[0] = s_scr[...]


def ret_scan(proj, col, cos_rows, sin_rows, ret_state, *, batch, L, heads, hd):
    n = proj.shape[0]
    t = n // batch
    nc = t // L
    inner = heads * hd
    has_state = ret_state is not None
    lg = jnp.log1p(-jnp.exp2(-5.0 - jnp.arange(heads, dtype=F32)))[:, None, None]
    li = jnp.arange(L, dtype=F32)
    diff = li[:, None] - li[None, :]
    dec = jnp.where(diff >= 0, jnp.exp(jnp.where(diff >= 0, diff, 0.0)[None] * lg), 0.0)
    ecum = jnp.broadcast_to(jnp.exp((li[None, :, None] + 1.0) * lg), (heads, L, hd))
    wv = jnp.broadcast_to(jnp.exp((L - 1.0 - li[None, :, None]) * lg), (heads, L, hd))
    sdec = jnp.broadcast_to(jnp.exp(L * lg), (heads, 1, hd))
    rowmap = lambda b, c: b * nc + c
    in_specs = [pl.BlockSpec((L, inner), lambda b, c: (rowmap(b, c), col["q"])),
                pl.BlockSpec((L, inner), lambda b, c: (rowmap(b, c), col["k"])),
                pl.BlockSpec((L, inner), lambda b, c: (rowmap(b, c), col["v"])),
                pl.BlockSpec((L, inner), lambda b, c: (rowmap(b, c), col["gate"])),
                pl.BlockSpec((L, hd // 2), lambda b, c: (rowmap(b, c), 0)),
                pl.BlockSpec((L, hd // 2), lambda b, c: (rowmap(b, c), 0)),
                pl.BlockSpec((heads, L, L), lambda b, c: (0, 0, 0)),
                pl.BlockSpec((heads, L, hd), lambda b, c: (0, 0, 0)),
                pl.BlockSpec((heads, L, hd), lambda b, c: (0, 0, 0)),
                pl.BlockSpec((heads, 1, hd), lambda b, c: (0, 0, 0))]
    args = [proj, proj, proj, proj, cos_rows, sin_rows, dec, ecum, wv, sdec]
    if has_state:
        in_specs.append(pl.BlockSpec((1, heads, hd, hd), lambda b, c: (b, 0, 0, 0)))
        args.append(ret_state)
    state_bytes = _nbytes((heads, hd, hd), F32)
    vmem = (2 * (4 * _nbytes((L, inner), F32) + _nbytes((L, inner), MXU_DTYPE) + _nbytes((heads, L, L), F32)
                 + 2 * _nbytes((heads, L, hd), F32) + 2 * state_bytes)
            + state_bytes + 16 * _nbytes((L, hd), F32) + (8 << 20))
    return pl.pallas_call(
        functools.partial(_ret_kernel, L=L, heads=heads, hd=hd, has_state=has_state),
        out_shape=(jax.ShapeDtypeStruct((n, inner), MXU_DTYPE),
                   jax.ShapeDtypeStruct((batch, heads, hd, hd), F32)),
        grid=(batch, nc),
        in_specs=in_specs,
        out_specs=(pl.BlockSpec((L, inner), lambda b, c: (rowmap(b, c), 0)),
                   pl.BlockSpec((1, heads, hd, hd), lambda b, c: (b, 0, 0, 0))),
        scratch_shapes=[pltpu.VMEM((heads, hd, hd), F32)],
        compiler_params=_params(("parallel", "arbitrary"), vmem),
        name="ret_scan",
    )(*args)


def _rope_tables(pos, half, reps, batch):
    inv = ROPE_BASE ** (-jnp.arange(half, dtype=F32) / half)
    ang = pos.astype(F32)[:, None] * inv[None, :]
    cos, sin = jnp.cos(ang), jnp.sin(ang)
    if reps == 0:
        return jnp.tile(cos, (batch, 1)), jnp.tile(sin, (batch, 1))
    c = jnp.tile(jnp.concatenate([cos, cos], axis=-1), (batch, reps))
    s = jnp.tile(jnp.concatenate([-sin, sin], axis=-1), (batch, reps))
    return c, s


def _mixer_even(x, batch, pos, norm_g, prm, conv_state, ssd_state, ret_state, dims, L_ssd, L_ret):
    d = x.shape[1]
    groups, hpg, hdim, nstate, rheads, rhd = dims
    proj = rms_matmul(x, norm_g, prm["w_in"])
    col = prm["col"]
    y, new_conv, new_ssd = ssd_scan(proj, col, prm["conv_w"], prm["conv_b"], prm["dt_bias"], prm["a_log"],
                                    prm["d_skip_e"], prm["ssd_norm"], conv_state, ssd_state,
                                    batch=batch, L=L_ssd, groups=groups, hpg=hpg, hdim=hdim, nstate=nstate)
    cos_rows, sin_rows = _rope_tables(pos, rhd // 2, 0, batch)
    o, new_ret = ret_scan(proj, col, cos_rows, sin_rows, ret_state, batch=batch, L=L_ret, heads=rheads, hd=rhd)
    x = matmul_residual(x, [y, o], [prm["w_out_ssd"], prm["w_out_ret"]])
    return x, (new_conv, new_ssd, new_ret)


def _prep_even(w_in, conv_w, conv_b, dt_bias, a_log, d_skip, ssd_norm, w_out, dims):
    groups, hpg, hdim, nstate, rheads, rhd = dims
    d = w_in.shape[0]
    dx = groups * hpg * hdim
    dn = groups * nstate
    nh = groups * hpg
    ri = rheads * rhd
    o_z, o_xs, o_b, o_c, o_dt = 0, dx, 2 * dx, 2 * dx + dn, 2 * dx + 2 * dn
    o_q = o_dt + nh
    seg = lambda lo, w: w_in[:, lo:lo + w]
    dt_pad = jnp.zeros((d, V7X_LANES - nh), w_in.dtype)
    w_new = jnp.concatenate([seg(o_z, dx), seg(o_q, ri), seg(o_q + ri, ri), seg(o_q + 2 * ri, ri),
                             seg(o_q + 3 * ri, ri), seg(o_xs, dx), seg(o_b, dn), seg(o_c, dn),
                             seg(o_dt, nh), dt_pad], axis=1).astype(MXU_DTYPE)
    assert dx == ri and dx % dn == 0 and dn % V7X_LANES == 0
    base = 5 * dx
    col = {"z": 0, "q": 1, "k": 2, "v": 3, "gate": 4, "xs": 5,
           "B": base // dn + dx // dn, "C": base // dn + dx // dn + 1,
           "dt": (base + dx + 2 * dn) // V7X_LANES}
    pad1 = lambda v: jnp.pad(v.astype(F32), (0, V7X_LANES - nh)).reshape(1, V7X_LANES)
    return {"w_in": w_new, "col": col, "conv_w": conv_w, "conv_b": conv_b,
            "dt_bias": pad1(dt_bias), "a_log": pad1(a_log),
            "d_skip_e": jnp.repeat(d_skip, hdim).reshape(1, dx), "ssd_norm": ssd_norm,
            "w_out_ssd": w_out[:dx].astype(MXU_DTYPE), "w_out_ret": w_out[dx:].astype(MXU_DTYPE)}


def _prep_odd(w_in, q_norm, kv_norm, w_uq, w_uk, w_uv, w_out, rope_dim):
    d = w_in.shape[0]
    q_lora = q_norm.shape[0]
    kv_lora, heads, nope = w_uk.shape
    kp = w_in[:, q_lora + kv_lora:]
    w_in_new = jnp.concatenate([w_in[:, :q_lora + kv_lora], kp, kp], axis=1).astype(MXU_DTYPE)
    assert 2 * rope_dim == V7X_LANES
    wq = w_uq.reshape(q_lora, heads, nope + rope_dim)
    wq_nope = wq[:, :, :nope].reshape(q_lora, heads * nope)
    wq_rope = jnp.pad(wq[:, :, nope:], ((0, 0), (0, 0), (0, V7X_LANES - rope_dim))).reshape(q_lora, heads * V7X_LANES)
    return {"w_in": w_in_new, "q_norm": q_norm, "kv_norm": kv_norm,
            "w_uq": jnp.concatenate([wq_nope, wq_rope], axis=1).astype(MXU_DTYPE),
            "w_uk": jnp.transpose(w_uk, (1, 2, 0)).astype(MXU_DTYPE),
            "w_uv": jnp.transpose(w_uv, (1, 0, 2)).astype(MXU_DTYPE),
            "w_out": w_out.astype(MXU_DTYPE), "nope_cols": heads * nope}


def _mixer_odd(x, batch, pos, norm_g, prm, past_kcat, rope_dim, q_offset):
    n, d = x.shape
    t = n // batch
    q_lora = prm["q_norm"].shape[0]
    kv_lora = prm["kv_norm"].shape[0]
    heads, nope, lat = prm["w_uk"].shape
    proj = rms_matmul(x, norm_g, prm["w_in"])
    cos_rows, sin_rows = _rope_tables(pos, rope_dim // 2, V7X_LANES // rope_dim, batch)
    q, ckv, kpe, kcat = mla_prep(proj, prm["q_norm"], prm["kv_norm"], prm["w_uq"], cos_rows, sin_rows,
                                 q_lora=q_lora, kv_lora=kv_lora, rope_dim=rope_dim, nope_cols=prm["nope_cols"])
    q = q.reshape(batch, t, q.shape[1])
    kcat = kcat.reshape(batch, t, kcat.shape[1])
    if past_kcat is None:
        keys, kv_len = kcat, t
        tq, tk = _pick(t, (128, 64)), _pick(t, (512, 256, 128, 64))
    else:
        kv_len = past_kcat.shape[1] + t
        padded = -(-kv_len // V7X_LANES) * V7X_LANES
        keys = jnp.concatenate([past_kcat, kcat, jnp.zeros((batch, padded - kv_len, kcat.shape[2]), kcat.dtype)], axis=1)
        tq, tk = t, padded
    o = mla_attn(q, keys, prm["w_uk"], prm["w_uv"], tq=tq, tk=tk, q_offset=q_offset, kv_len=kv_len,
                 scale=(nope + rope_dim) ** -0.5)
    x = matmul_residual(x, [o.reshape(n, o.shape[2])], [prm["w_out"]])
    return x, (ckv.reshape(batch, t, kv_lora), kpe.reshape(batch, t, rope_dim))


def kernel(x_prompt, x_sample, mem_prompt, state_conv, state_ssd, state_ret, cache_ckv, cache_kpe,
           cache_mem_k, cache_mem_v, norms, ffn_w1, ffn_w2, mem_norm, w_mq, w_mkv, w_mo,
           ab_w_in, ab_conv_w, ab_conv_b, ab_dt_bias, ab_a_log, ab_d_skip, ab_ssd_norm, ab_w_out,
           c_w_in, c_q_norm, c_kv_norm, c_w_uq, c_w_uk, c_w_uv, c_w_out, final_norm):
    bp, tp, d = x_prompt.shape
    bs, ts, _ = x_sample.shape
    depth = norms.shape[0]
    past_len = cache_ckv.shape[2]
    mem_tokens = mem_prompt.shape[1]
    mem_heads, mem_hd = cache_mem_k.shape[3], cache_mem_k.shape[4]
    mem_inner = mem_heads * mem_hd
    ssd_heads, nstate, hdim = state_ssd.shape[2], state_ssd.shape[3], state_ssd.shape[4]
    cdim = state_conv.shape[3]
    groups = (cdim - ssd_heads * hdim) // (2 * nstate)
    hpg = ssd_heads // groups
    rheads, rhd = state_ret.shape[2], state_ret.shape[3]
    dims = (groups, hpg, hdim, nstate, rheads, rhd)
    rope_dim = cache_kpe.shape[3]

    pos_p = jnp.arange(tp)
    pos_s = past_len + jnp.arange(ts)
    xp = x_prompt.reshape(bp * tp, d)
    xs = x_sample.reshape(bs * ts, d)
    L_ssd_p, L_ret_p = _pick(tp, (128, 64)), _pick(tp, (256, 128, 64))
    L_s = _pick(ts, (128, 64))

    outs = {k: [] for k in ("conv_p", "ssd_p", "ret_p", "ckv_p", "kpe_p", "memk_p", "memv_p",
                            "conv_s", "ssd_s", "ret_s", "ckv_s", "kpe_s")}

    def to_group_layout(st):
        b = st.shape[0]
        return st.reshape(b, groups, hpg, nstate, hdim).transpose(0, 1, 3, 2, 4).reshape(b, groups, nstate, hpg * hdim)

    def from_group_layout(st):
        b = st.shape[0]
        return st.reshape(b, groups, nstate, hpg, hdim).transpose(0, 1, 3, 2, 4).reshape(b, ssd_heads, nstate, hdim)

    for i in range(depth):
        j = i // 2
        w1 = ffn_w1[i].astype(MXU_DTYPE)
        w2 = ffn_w2[i].astype(MXU_DTYPE)
        wq_m = w_mq[i].astype(MXU_DTYPE)
        wo_m = w_mo[i].astype(MXU_DTYPE)
        mkv = rms_matmul(mem_prompt.reshape(bp * mem_tokens, d), mem_norm[i], w_mkv[i].astype(MXU_DTYPE))
        mk_p = mkv[:, :mem_inner].reshape(bp, mem_tokens, mem_inner)
        mv_p = mkv[:, mem_inner:].reshape(bp, mem_tokens, mem_inner)
        outs["memk_p"].append(mk_p.reshape(bp, mem_tokens, mem_heads, mem_hd))
        outs["memv_p"].append(mv_p.reshape(bp, mem_tokens, mem_heads, mem_hd))

        xp = ffn(xp, norms[i, 0], w1[0], w2[0])
        xs = ffn(xs, norms[i, 0], w1[0], w2[0])
        if i % 2 == 0:
            prm = _prep_even(ab_w_in[j], ab_conv_w[j], ab_conv_b[j], ab_dt_bias[j], ab_a_log[j],
                             ab_d_skip[j], ab_ssd_norm[j], ab_w_out[j], dims)
            zero_conv = jnp.zeros((bp, CONV_PAD_ROWS, cdim), F32)
            xp, st_p = _mixer_even(xp, bp, pos_p, norms[i, 1], prm, zero_conv, None, None, dims, L_ssd_p, L_ret_p)
            conv_in = jnp.pad(state_conv[j], ((0, 0), (CONV_PAD_ROWS - (CONV_K - 1), 0), (0, 0)))
            xs, st_s = _mixer_even(xs, bs, pos_s, norms[i, 1], prm, conv_in, to_group_layout(state_ssd[j]),
                                   state_ret[j], dims, L_s, L_s)
            for tag, st in (("p", st_p), ("s", st_s)):
                outs["conv_" + tag].append(st[0][:, CONV_PAD_ROWS - (CONV_K - 1):, :])
                outs["ssd_" + tag].append(from_group_layout(st[1]))
                outs["ret_" + tag].append(st[2])
        else:
            prm = _prep_odd(c_w_in[j], c_q_norm[j], c_kv_norm[j], c_w_uq[j], c_w_uk[j], c_w_uv[j], c_w_out[j],
                            rope_dim)
            xp, st_p = _mixer_odd(xp, bp, pos_p, norms[i, 1], prm, None, rope_dim, 0)
            past = jnp.concatenate([cache_ckv[j], cache_kpe[j], cache_kpe[j]], axis=-1).astype(MXU_DTYPE)
            xs, st_s = _mixer_odd(xs, bs, pos_s, norms[i, 1], prm, past, rope_dim, past_len)
            for tag, st in (("p", st_p), ("s", st_s)):
                outs["ckv_" + tag].append(st[0])
                outs["kpe_" + tag].append(st[1])
        xp = mem_attn(xp, norms[i, 2], wq_m, mk_p, mv_p, wo_m, batch=bp, heads=mem_heads)
        xs = mem_attn(xs, norms[i, 2], wq_m, cache_mem_k[i].reshape(bs, mem_tokens, mem_inner),
                      cache_mem_v[i].reshape(bs, mem_tokens, mem_inner), wo_m, batch=bs, heads=mem_heads)
        xp = ffn(xp, norms[i, 3], w1[1], w2[1])
        xs = ffn(xs, norms[i, 3], w1[1], w2[1])

    y_prompt = rmsnorm(xp, final_norm).reshape(bp, tp, d)
    y_sample = rmsnorm(xs, final_norm).reshape(bs, ts, d)
    st = lambda k: jnp.stack(outs[k])
    return (y_prompt, y_sample, st("conv_p"), st("ssd_p"), st("ret_p"), st("ckv_p"), st("kpe_p"),
            st("memk_p"), st("memv_p"), st("conv_s"), st("ssd_s"), st("ret_s"), st("ckv_s"), st("kpe_s"))
```

```python
import functools
import math

import jax
import jax.numpy as jnp
from jax import lax
from jax.experimental import pallas as pl
from jax.experimental.pallas import tpu as pltpu

F32 = jnp.float32
MXU_DTYPE = jnp.bfloat16

EPS = 1e-6
CHUNK = 64
ROPE_BASE = 10000.0
CONV_K = 4
NEG = -1e30

V7X_VMEM_BYTES = 64 * 1024 * 1024
V7X_LANES = 128
CONV_PAD_ROWS = 8
MLA_ROW_BLOCK = 512


def _params(semantics, vmem_bytes):
    limit = min(int(vmem_bytes), V7X_VMEM_BYTES - (4 << 20))
    return pltpu.CompilerParams(dimension_semantics=semantics, vmem_limit_bytes=limit)


def _nbytes(shape, dtype):
    return math.prod(shape) * jnp.dtype(dtype).itemsize


def _pick(n, prefs):
    for p in prefs:
        if n % p == 0:
            return p
    return n


def _dot(a, b):
    return jnp.dot(a, b, preferred_element_type=F32)


def _dot_nt(a, b):
    return lax.dot_general(a, b, (((1,), (1,)), ((), ())), preferred_element_type=F32)


def _dot_tn(a, b):
    return lax.dot_general(a, b, (((0,), (0,)), ((), ())), preferred_element_type=F32)


def _split3(x):
    hi = x.astype(MXU_DTYPE)
    r = x - hi.astype(F32)
    mid = r.astype(MXU_DTYPE)
    lo = (r - mid.astype(F32)).astype(MXU_DTYPE)
    return hi, mid, lo


def _rms(xf, g=None):
    y = xf * lax.rsqrt(jnp.mean(xf * xf, axis=-1, keepdims=True) + EPS)
    return y if g is None else y * g


def _silu(a):
    return a * (1.0 / (1.0 + jnp.exp(-a)))


def _swap32(x):
    w = x.shape[-1]
    lane = lax.broadcasted_iota(jnp.int32, x.shape, x.ndim - 1)
    fwd = pltpu.roll(x, w - 32, x.ndim - 1)
    bwd = pltpu.roll(x, 32, x.ndim - 1)
    return jnp.where((lane & 63) < 32, fwd, bwd)


def _rms_matmul_kernel(x_ref, g_ref, w_ref, o_ref, xn_ref):
    @pl.when(pl.program_id(1) == 0)
    def _():
        xn_ref[...] = _rms(x_ref[...], g_ref[...]).astype(xn_ref.dtype)

    o_ref[...] = _dot(xn_ref[...], w_ref[...]).astype(o_ref.dtype)


def rms_matmul(x, g, w, *, out_dtype=F32):
    n, d = x.shape
    nout = w.shape[1]
    tm = _pick(n, (512, 256, 128))
    tn = _pick(nout, (1920, 1152, 1024, 512, 256, 128))
    vmem = (2 * (_nbytes((tm, d), F32) + _nbytes((d, tn), w.dtype) + _nbytes((tm, tn), out_dtype))
            + _nbytes((tm, d), MXU_DTYPE) + _nbytes((tm, tn), F32) + (8 << 20))
    return pl.pallas_call(
        _rms_matmul_kernel,
        out_shape=jax.ShapeDtypeStruct((n, nout), out_dtype),
        grid=(n // tm, nout // tn),
        in_specs=[pl.BlockSpec((tm, d), lambda i, j: (i, 0)),
                  pl.BlockSpec((1, d), lambda i, j: (0, 0)),
                  pl.BlockSpec((d, tn), lambda i, j: (0, j))],
        out_specs=pl.BlockSpec((tm, tn), lambda i, j: (i, j)),
        scratch_shapes=[pltpu.VMEM((tm, d), MXU_DTYPE)],
        compiler_params=_params(("parallel", "arbitrary"), vmem),
        name="rms_matmul",
    )(x, g.reshape(1, d), w)


def _ffn_kernel(x_ref, g_ref, w1a_ref, w1b_ref, w2_ref, o_ref, xn_ref, acc_ref):
    f = pl.program_id(1)

    @pl.when(f == 0)
    def _():
        xn_ref[...] = _rms(x_ref[...], g_ref[...]).astype(xn_ref.dtype)
        acc_ref[...] = jnp.zeros_like(acc_ref)

    xn = xn_ref[...]
    a = _dot(xn, w1a_ref[...])
    b = _dot(xn, w1b_ref[...])
    h = (_silu(a) * b).astype(MXU_DTYPE)
    acc_ref[...] += _dot(h, w2_ref[...])

    @pl.when(f == pl.num_programs(1) - 1)
    def _():
        o_ref[...] = x_ref[...] + 0.5 * acc_ref[...]


def ffn(x, g, w1, w2):
    n, d = x.shape
    dff = w2.shape[0]
    tm = _pick(n, (512, 256, 128))
    tf = _pick(dff, (512, 256, 128))
    nf = dff // tf
    vmem = (2 * (2 * _nbytes((tm, d), F32) + 3 * _nbytes((d, tf), w1.dtype))
            + _nbytes((tm, d), MXU_DTYPE) + _nbytes((tm, d), F32) + 4 * _nbytes((tm, tf), F32) + (8 << 20))
    return pl.pallas_call(
        _ffn_kernel,
        out_shape=jax.ShapeDtypeStruct((n, d), F32),
        grid=(n // tm, nf),
        in_specs=[pl.BlockSpec((tm, d), lambda i, f: (i, 0)),
                  pl.BlockSpec((1, d), lambda i, f: (0, 0)),
                  pl.BlockSpec((d, tf), lambda i, f: (0, f)),
                  pl.BlockSpec((d, tf), lambda i, f: (0, f + nf)),
                  pl.BlockSpec((tf, d), lambda i, f: (f, 0))],
        out_specs=pl.BlockSpec((tm, d), lambda i, f: (i, 0)),
        scratch_shapes=[pltpu.VMEM((tm, d), MXU_DTYPE), pltpu.VMEM((tm, d), F32)],
        compiler_params=_params(("parallel", "arbitrary"), vmem),
        name="ffn",
    )(x, g.reshape(1, d), w1, w1, w2)


def _matmul_residual_kernel(*refs, n_in):
    x_ref = refs[0]
    o_ref = refs[1 + 2 * n_in]
    acc = x_ref[...]
    for h_ref, w_ref in zip(refs[1:1 + n_in], refs[1 + n_in:1 + 2 * n_in]):
        acc = acc + _dot(h_ref[...], w_ref[...])
    o_ref[...] = acc


def matmul_residual(x, hs, ws):
    n, d = x.shape
    tm = _pick(n, (512, 256, 128))
    tn = _pick(d, (1024, 512, 256, 128))
    vmem = 2 * 2 * _nbytes((tm, tn), F32) + (8 << 20)
    in_specs = [pl.BlockSpec((tm, tn), lambda i, j: (i, j))]
    for h in hs:
        in_specs.append(pl.BlockSpec((tm, h.shape[1]), lambda i, j: (i, 0)))
        vmem += 2 * _nbytes((tm, h.shape[1]), h.dtype)
    for w in ws:
        in_specs.append(pl.BlockSpec((w.shape[0], tn), lambda i, j: (0, j)))
        vmem += 2 * _nbytes((w.shape[0], tn), w.dtype)
    return pl.pallas_call(
        functools.partial(_matmul_residual_kernel, n_in=len(hs)),
        out_shape=jax.ShapeDtypeStruct((n, d), F32),
        grid=(n // tm, d // tn),
        in_specs=in_specs,
        out_specs=pl.BlockSpec((tm, tn), lambda i, j: (i, j)),
        compiler_params=_params(("parallel", "arbitrary"), vmem),
        name="matmul_residual",
    )(x, *hs, *ws)


def _rmsnorm_kernel(x_ref, g_ref, o_ref):
    o_ref[...] = _rms(x_ref[...], g_ref[...])


def rmsnorm(x, g):
    n, d = x.shape
    tm = _pick(n, (512, 256, 128))
    return pl.pallas_call(
        _rmsnorm_kernel,
        out_shape=jax.ShapeDtypeStruct((n, d), F32),
        grid=(n // tm,),
        in_specs=[pl.BlockSpec((tm, d), lambda i: (i, 0)), pl.BlockSpec((1, d), lambda i: (0, 0))],
        out_specs=pl.BlockSpec((tm, d), lambda i: (i, 0)),
        compiler_params=_params(("parallel",), 6 * _nbytes((tm, d), F32) + (8 << 20)),
        name="rmsnorm",
    )(x, g.reshape(1, d))


def _mem_attn_kernel(x_ref, g_ref, wq_ref, k_ref, v_ref, wo_ref, o_ref, att_ref, *, bt, tq, heads, hd):
    xf = x_ref[...]
    xn = _rms(xf, g_ref[...]).astype(MXU_DTYPE)
    q = _dot(xn, wq_ref[...]).astype(MXU_DTYPE)
    scale = hd ** -0.5
    for b in range(bt):
        for h in range(heads):
            qh = q[b * tq:(b + 1) * tq, h * hd:(h + 1) * hd]
            kh = k_ref[b, :, h * hd:(h + 1) * hd].astype(MXU_DTYPE)
            vh = v_ref[b, :, h * hd:(h + 1) * hd].astype(MXU_DTYPE)
            s = _dot_nt(qh, kh) * scale
            m = jnp.max(s, axis=-1, keepdims=True)
            p = jnp.exp(s - m)
            l = jnp.sum(p, axis=-1, keepdims=True)
            oh = _dot(p.astype(MXU_DTYPE), vh) / l
            att_ref[b * tq:(b + 1) * tq, h * hd:(h + 1) * hd] = oh.astype(att_ref.dtype)
    o_ref[...] = xf + _dot(att_ref[...], wo_ref[...])


def mem_attn(x, g, wq, mem_k, mem_v, wo, *, batch, heads):
    n, d = x.shape
    t = n // batch
    m, inner = mem_k.shape[1], mem_k.shape[2]
    hd = inner // heads
    if t >= 128:
        bt, tq = 1, _pick(t, (512, 256, 128))
    else:
        bt, tq = _pick(batch, (8, 4, 2, 1)), t
    nt = t // tq
    rows = bt * tq
    vmem = (2 * (2 * _nbytes((rows, d), F32) + 2 * _nbytes((bt, m, inner), mem_k.dtype)
                 + 2 * _nbytes((d, inner), wq.dtype))
            + 2 * _nbytes((rows, d), F32) + (8 << 20))
    return pl.pallas_call(
        functools.partial(_mem_attn_kernel, bt=bt, tq=tq, heads=heads, hd=hd),
        out_shape=jax.ShapeDtypeStruct((n, d), F32),
        grid=(batch // bt, nt),
        in_specs=[pl.BlockSpec((rows, d), lambda b, i: (b * nt + i, 0)),
                  pl.BlockSpec((1, d), lambda b, i: (0, 0)),
                  pl.BlockSpec((d, inner), lambda b, i: (0, 0)),
                  pl.BlockSpec((bt, m, inner), lambda b, i: (b, 0, 0)),
                  pl.BlockSpec((bt, m, inner), lambda b, i: (b, 0, 0)),
                  pl.BlockSpec((inner, d), lambda b, i: (0, 0))],
        out_specs=pl.BlockSpec((rows, d), lambda b, i: (b * nt + i, 0)),
        scratch_shapes=[pltpu.VMEM((rows, inner), MXU_DTYPE)],
        compiler_params=_params(("parallel", "arbitrary"), vmem),
        name="mem_attn",
    )(x, g.reshape(1, d), wq, mem_k, mem_v, wo)


def _mla_prep_kernel(p_ref, qn_ref, kvn_ref, wuq_ref, cos_ref, sin_ref,
                     q_ref, ckv_ref, kpe_ref, kcat_ref, *, q_lora, kv_lora, rope_dim, nope_cols):
    cos = cos_ref[...]
    sin = sin_ref[...]
    cq = p_ref[:, 0:q_lora]
    cqn = _rms(cq, qn_ref[...]).astype(MXU_DTYPE)
    q = _dot(cqn, wuq_ref[...])
    q_ref[:, 0:nope_cols] = q[:, 0:nope_cols].astype(q_ref.dtype)
    for s in range((q.shape[1] - nope_cols) // V7X_LANES):
        lo = nope_cols + s * V7X_LANES
        xs = q[:, lo:lo + V7X_LANES]
        q_ref[:, lo:lo + V7X_LANES] = (xs * cos + _swap32(xs) * sin).astype(q_ref.dtype)
    ckv = _rms(p_ref[:, q_lora:q_lora + kv_lora], kvn_ref[...])
    ckv_ref[...] = ckv
    kp = p_ref[:, q_lora + kv_lora:q_lora + kv_lora + V7X_LANES]
    kpr = kp * cos + _swap32(kp) * sin
    kpe_ref[...] = kpr[:, 0:rope_dim]
    kcat_ref[:, 0:kv_lora] = ckv.astype(kcat_ref.dtype)
    kcat_ref[:, kv_lora:kv_lora + V7X_LANES] = kpr.astype(kcat_ref.dtype)


def mla_prep(proj, q_norm, kv_norm, w_uq, cos_rows, sin_rows, *, q_lora, kv_lora, rope_dim, nope_cols):
    n = proj.shape[0]
    qcols = w_uq.shape[1]
    tm = _pick(n, (256, 128))
    vmem = (2 * (_nbytes((tm, proj.shape[1]), F32) + _nbytes(w_uq.shape, w_uq.dtype)
                 + _nbytes((tm, qcols), MXU_DTYPE) + 3 * _nbytes((tm, kv_lora + V7X_LANES), F32))
            + 3 * _nbytes((tm, qcols), F32) + (8 << 20))
    return pl.pallas_call(
        functools.partial(_mla_prep_kernel, q_lora=q_lora, kv_lora=kv_lora, rope_dim=rope_dim,
                          nope_cols=nope_cols),
        out_shape=(jax.ShapeDtypeStruct((n, qcols), MXU_DTYPE),
                   jax.ShapeDtypeStruct((n, kv_lora), F32),
                   jax.ShapeDtypeStruct((n, rope_dim), F32),
                   jax.ShapeDtypeStruct((n, kv_lora + V7X_LANES), MXU_DTYPE)),
        grid=(n // tm,),
        in_specs=[pl.BlockSpec((tm, proj.shape[1]), lambda i: (i, 0)),
                  pl.BlockSpec((1, q_lora), lambda i: (0, 0)),
                  pl.BlockSpec((1, kv_lora), lambda i: (0, 0)),
                  pl.BlockSpec(w_uq.shape, lambda i: (0, 0)),
                  pl.BlockSpec((tm, V7X_LANES), lambda i: (i, 0)),
                  pl.BlockSpec((tm, V7X_LANES), lambda i: (i, 0))],
        out_specs=(pl.BlockSpec((tm, qcols), lambda i: (i, 0)),
                   pl.BlockSpec((tm, kv_lora), lambda i: (i, 0)),
                   pl.BlockSpec((tm, rope_dim), lambda i: (i, 0)),
                   pl.BlockSpec((tm, kv_lora + V7X_LANES), lambda i: (i, 0))),
        compiler_params=_params(("parallel",), vmem),
        name="mla_prep",
    )(proj, q_norm.reshape(1, -1), kv_norm.reshape(1, -1), w_uq, cos_rows, sin_rows)


def _last_kv_block(i, *, tq, tk, q_offset, nk):
    last_q = q_offset + (i + 1) * tq - 1
    last_key = (last_q // CHUNK + 1) * CHUNK - 1
    return jnp.minimum(last_key // tk, nk - 1)


def _mla_attn_kernel(q_ref, k_ref, wuk_ref, wuv_ref, o_ref, qs_ref, acc_ref, m_ref, l_ref,
                     *, heads, hb, tq, tk, nope, lat, q_offset, kv_len, scale):
    i = pl.program_id(1)
    j = pl.program_id(2)
    nk = pl.num_programs(2)
    j_last = _last_kv_block(i, tq=tq, tk=tk, q_offset=q_offset, nk=nk)
    rope_lo = heads * nope

    @pl.when(j == 0)
    def _():
        for h in range(heads):
            qn = q_ref[0, :, h * nope:(h + 1) * nope]
            qs_ref[h * tq:(h + 1) * tq, 0:lat] = (_dot(qn, wuk_ref[h]) * scale).astype(qs_ref.dtype)
            qr = q_ref[0, :, rope_lo + h * V7X_LANES:rope_lo + (h + 1) * V7X_LANES]
            qs_ref[h * tq:(h + 1) * tq, lat:lat + V7X_LANES] = (qr.astype(F32) * scale).astype(qs_ref.dtype)
        m_ref[...] = jnp.full_like(m_ref, NEG)
        l_ref[...] = jnp.zeros_like(l_ref)
        acc_ref[...] = jnp.zeros_like(acc_ref)

    @pl.when(j <= j_last)
    def _():
        k = k_ref[0]
        v = k[:, 0:lat]
        qpos = q_offset + i * tq + lax.broadcasted_iota(jnp.int32, (tq, tk), 0)
        kpos = j * tk + lax.broadcasted_iota(jnp.int32, (tq, tk), 1)
        visible = ((kpos // CHUNK) <= (qpos // CHUNK)) & (kpos < kv_len)
        bias = jnp.where(visible, 0.0, NEG)
        rows = hb * tq
        for rb in range(heads // hb):
            rs = slice(rb * rows, (rb + 1) * rows)
            s = _dot_nt(qs_ref[rs, :], k)
            s = (s.reshape(hb, tq, tk) + bias[None]).reshape(rows, tk)
            m_old = m_ref[rs, :]
            m_new = jnp.maximum(m_old, jnp.max(s, axis=-1, keepdims=True))
            alpha = jnp.exp(m_old - m_new)
            p = jnp.exp(s - pltpu.repeat(m_new, tk // V7X_LANES, axis=1))
            l_ref[rs, :] = alpha * l_ref[rs, :] + jnp.sum(p, axis=-1, keepdims=True)
            m_ref[rs, :] = m_new
            acc_ref[rs, :] = (pltpu.repeat(alpha, lat // V7X_LANES, axis=1) * acc_ref[rs, :]
                              + _dot(p.astype(MXU_DTYPE), v))

    @pl.when(j == j_last)
    def _():
        vd = wuv_ref.shape[2]
        for h in range(heads):
            hs = slice(h * tq, (h + 1) * tq)
            ol = acc_ref[hs, :] / pltpu.repeat(l_ref[hs, :], lat // V7X_LANES, axis=1)
            o_ref[0, :, h * vd:(h + 1) * vd] = _dot(ol.astype(MXU_DTYPE), wuv_ref[h]).astype(o_ref.dtype)


def mla_attn(q, kcat, w_uk, w_uv, *, tq, tk, q_offset, kv_len, scale):
    b, t, _ = q.shape
    s = kcat.shape[1]
    heads, nope, lat = w_uk.shape
    vd = w_uv.shape[2]
    nq, nk = t // tq, s // tk
    rows = heads * tq
    hb = max(1, min(heads, MLA_ROW_BLOCK // tq))
    assert heads % hb == 0
    last = functools.partial(_last_kv_block, tq=tq, tk=tk, q_offset=q_offset, nk=nk)
    vmem = (2 * (_nbytes((tq, q.shape[2]), q.dtype) + _nbytes((tk, kcat.shape[2]), kcat.dtype)
                 + 2 * _nbytes(w_uk.shape, w_uk.dtype) + _nbytes((tq, heads * vd), MXU_DTYPE))
            + _nbytes((rows, lat + V7X_LANES), MXU_DTYPE) + _nbytes((rows, lat), F32)
            + 2 * _nbytes((rows, V7X_LANES), F32) + 4 * _nbytes((rows, tk), F32) + (8 << 20))
    return pl.pallas_call(
        functools.partial(_mla_attn_kernel, heads=heads, hb=hb, tq=tq, tk=tk, nope=nope, lat=lat,
                          q_offset=q_offset, kv_len=kv_len, scale=scale),
        out_shape=jax.ShapeDtypeStruct((b, t, heads * vd), MXU_DTYPE),
        grid=(b, nq, nk),
        in_specs=[pl.BlockSpec((1, tq, q.shape[2]), lambda bb, i, j: (bb, i, 0)),
                  pl.BlockSpec((1, tk, kcat.shape[2]), lambda bb, i, j: (bb, jnp.minimum(j, last(i)), 0)),
                  pl.BlockSpec(w_uk.shape, lambda bb, i, j: (0, 0, 0)),
                  pl.BlockSpec(w_uv.shape, lambda bb, i, j: (0, 0, 0))],
        out_specs=pl.BlockSpec((1, tq, heads * vd), lambda bb, i, j: (bb, i, 0)),
        scratch_shapes=[pltpu.VMEM((rows, lat + V7X_LANES), MXU_DTYPE),
                        pltpu.VMEM((rows, lat), F32),
                        pltpu.VMEM((rows, V7X_LANES), F32),
                        pltpu.VMEM((rows, V7X_LANES), F32)],
        compiler_params=_params(("parallel", "parallel", "arbitrary"), vmem),
        name="mla_attn",
    )(q, kcat, w_uk, w_uv)


def _ssd_kernel(*refs, L, groups, hpg, hdim, nstate, has_state):
    if has_state:
        (z_ref, xs_ref, b_ref, c_ref, dt_ref, cw_ref, cb_ref, dtb_ref, alog_ref, dsk_ref, nrm_ref,
         cst_ref, hst_ref, y_ref, ncv_ref, nst_ref, xbuf, h_scr, cum_scr, xdt_scr, yin_scr) = refs
    else:
        (z_ref, xs_ref, b_ref, c_ref, dt_ref, cw_ref, cb_ref, dtb_ref, alog_ref, dsk_ref, nrm_ref,
         cst_ref, y_ref, ncv_ref, nst_ref, xbuf, h_scr, cum_scr, xdt_scr, yin_scr) = refs
        hst_ref = None
    c = pl.program_id(1)
    nc = pl.num_programs(1)
    dx = groups * hpg * hdim
    dn = groups * nstate
    gw = hpg * hdim
    pad = CONV_PAD_ROWS

    @pl.when(c == 0)
    def _():
        xbuf[0:pad, :] = cst_ref[0]
        if has_state:
            h_scr[...] = hst_ref[0]
        else:
            h_scr[...] = jnp.zeros_like(h_scr)

    xbuf[pad:pad + L, 0:dx] = xs_ref[...]
    xbuf[pad:pad + L, dx:dx + dn] = b_ref[...]
    xbuf[pad:pad + L, dx + dn:dx + 2 * dn] = c_ref[...]
    acc = xbuf[pad - CONV_K + 1:pad - CONV_K + 1 + L, :] * cw_ref[0:1, :] + cb_ref[...]
    for jj in range(1, CONV_K):
        lo = pad - CONV_K + 1 + jj
        acc = acc + xbuf[lo:lo + L, :] * cw_ref[jj:jj + 1, :]
    xc = _silu(acc)
    tail = xbuf[L:L + pad, :]
    xbuf[0:pad, :] = tail

    @pl.when(c == nc - 1)
    def _():
        ncv_ref[0] = tail

    dtr = dt_ref[...] + dtb_ref[...]
    dt = jnp.maximum(dtr, 0.0) + jnp.log1p(jnp.exp(-jnp.abs(dtr)))
    la = dt * (-jnp.exp(alog_ref[...]))
    row = lax.broadcasted_iota(jnp.int32, (L, L), 0)
    col = lax.broadcasted_iota(jnp.int32, (L, L), 1)
    causal = row >= col
    tri = jnp.where(causal, 1.0, 0.0).astype(MXU_DTYPE)
    cum = sum(_dot(tri, piece) for piece in _split3(la))
    nh = groups * hpg
    eye = jnp.where(lax.broadcasted_iota(jnp.int32, (V7X_LANES, V7X_LANES), 0)
                    == lax.broadcasted_iota(jnp.int32, (V7X_LANES, V7X_LANES), 1), 1.0, 0.0).astype(MXU_DTYPE)
    cum_t = sum(_dot_nt(eye, piece) for piece in _split3(cum))

    half = lax.broadcasted_iota(jnp.int32, (L, V7X_LANES), 1) < hdim
    per_vreg = V7X_LANES // hdim
    for g in range(groups):
        bg = xc[:, dx + g * nstate:dx + (g + 1) * nstate]
        cg = xc[:, dx + dn + g * nstate:dx + dn + (g + 1) * nstate].astype(MXU_DTYPE)
        qk = _dot_nt(cg, bg.astype(MXU_DTYPE))
        for sl in range(gw // V7X_LANES):
            lane0 = g * gw + sl * V7X_LANES
            h0 = lane0 // hdim
            cb = [jnp.broadcast_to(cum[:, h0 + u:h0 + u + 1], (L, V7X_LANES)) for u in range(per_vreg)]
            db = [jnp.broadcast_to(dt[:, h0 + u:h0 + u + 1], (L, V7X_LANES)) for u in range(per_vreg)]
            cum_e = jnp.where(half, cb[0], cb[1])
            dt_e = jnp.where(half, db[0], db[1])
            xdt = xc[:, lane0:lane0 + V7X_LANES] * dt_e
            xdt_m = xdt.astype(MXU_DTYPE)
            ys = []
            for u in range(per_vreg):
                seg = cb[u][:, 0:L] - cum_t[h0 + u:h0 + u + 1, :]
                decay = jnp.exp(jnp.where(causal, seg, NEG))
                ys.append(_dot((qk * decay).astype(MXU_DTYPE), xdt_m))
            cum_scr[:, lane0:lane0 + V7X_LANES] = cum_e
            xdt_scr[:, lane0:lane0 + V7X_LANES] = xdt
            yin_scr[:, lane0:lane0 + V7X_LANES] = jnp.where(half, ys[0], ys[1])

    for g in range(groups):
        gs = slice(g * gw, (g + 1) * gw)
        bg = xc[:, dx + g * nstate:dx + (g + 1) * nstate].astype(MXU_DTYPE)
        cg = xc[:, dx + dn + g * nstate:dx + dn + (g + 1) * nstate].astype(MXU_DTYPE)
        cum_g = cum_scr[:, gs]
        last = cum_scr[L - 1:L, gs]
        hg = h_scr[g]
        y_inter = _dot(cg, hg.astype(MXU_DTYPE)) * jnp.exp(cum_g)
        wx = (jnp.exp(last - cum_g) * xdt_scr[:, gs]).astype(MXU_DTYPE)
        h_scr[g] = jnp.exp(last) * hg + _dot_tn(bg, wx)
        y = yin_scr[:, gs] + y_inter + dsk_ref[:, gs] * xc[:, gs]
        y = y * _silu(z_ref[:, gs])
        y_ref[:, gs] = (_rms(y) * nrm_ref[:, gs]).astype(y_ref.dtype)

    @pl.when(c == nc - 1)
    def _():
        nst_ref[0] = h_scr[...]


def ssd_scan(proj, col, conv_w, conv_b, dt_bias, a_log, d_skip_e, ssd_norm, conv_state, ssd_state,
             *, batch, L, groups, hpg, hdim, nstate):
    n = proj.shape[0]
    t = n // batch
    nc = t // L
    dx = groups * hpg * hdim
    dn = groups * nstate
    cdim = dx + 2 * dn
    has_state = ssd_state is not None
    rowmap = lambda b, c: b * nc + c
    in_specs = [pl.BlockSpec((L, dx), lambda b, c: (rowmap(b, c), col["z"])),
                pl.BlockSpec((L, dx), lambda b, c: (rowmap(b, c), col["xs"])),
                pl.BlockSpec((L, dn), lambda b, c: (rowmap(b, c), col["B"])),
                pl.BlockSpec((L, dn), lambda b, c: (rowmap(b, c), col["C"])),
                pl.BlockSpec((L, V7X_LANES), lambda b, c: (rowmap(b, c), col["dt"])),
                pl.BlockSpec((CONV_K, cdim), lambda b, c: (0, 0)),
                pl.BlockSpec((1, cdim), lambda b, c: (0, 0)),
                pl.BlockSpec((1, V7X_LANES), lambda b, c: (0, 0)),
                pl.BlockSpec((1, V7X_LANES), lambda b, c: (0, 0)),
                pl.BlockSpec((1, dx), lambda b, c: (0, 0)),
                pl.BlockSpec((1, dx), lambda b, c: (0, 0)),
                pl.BlockSpec((1, CONV_PAD_ROWS, cdim), lambda b, c: (b, 0, 0))]
    args = [proj, proj, proj, proj, proj, conv_w, conv_b.reshape(1, cdim), dt_bias, a_log,
            d_skip_e, ssd_norm.reshape(1, dx), conv_state]
    if has_state:
        in_specs.append(pl.BlockSpec((1, groups, nstate, hpg * hdim), lambda b, c: (b, 0, 0, 0)))
        args.append(ssd_state)
    state_bytes = _nbytes((groups, nstate, hpg * hdim), F32)
    vmem = (2 * (2 * _nbytes((L, dx), F32) + 2 * _nbytes((L, dn), F32) + _nbytes((L, dx), MXU_DTYPE)
                 + 2 * _nbytes((CONV_PAD_ROWS, cdim), F32) + 2 * state_bytes)
            + state_bytes + 12 * _nbytes((L + CONV_PAD_ROWS, cdim), F32) + (8 << 20))
    return pl.pallas_call(
        functools.partial(_ssd_kernel, L=L, groups=groups, hpg=hpg, hdim=hdim, nstate=nstate,
                          has_state=has_state),
        out_shape=(jax.ShapeDtypeStruct((n, dx), MXU_DTYPE),
                   jax.ShapeDtypeStruct((batch, CONV_PAD_ROWS, cdim), F32),
                   jax.ShapeDtypeStruct((batch, groups, nstate, hpg * hdim), F32)),
        grid=(batch, nc),
        in_specs=in_specs,
        out_specs=(pl.BlockSpec((L, dx), lambda b, c: (rowmap(b, c), 0)),
                   pl.BlockSpec((1, CONV_PAD_ROWS, cdim), lambda b, c: (b, 0, 0)),
                   pl.BlockSpec((1, groups, nstate, hpg * hdim), lambda b, c: (b, 0, 0, 0))),
        scratch_shapes=[pltpu.VMEM((L + CONV_PAD_ROWS, cdim), F32),
                        pltpu.VMEM((groups, nstate, hpg * hdim), F32),
                        pltpu.VMEM((L, dx), F32),
                        pltpu.VMEM((L, dx), F32),
                        pltpu.VMEM((L, dx), F32)],
        compiler_params=_params(("parallel", "arbitrary"), vmem),
        name="ssd_scan",
    )(*args)


def _ret_kernel(*refs, L, heads, hd, has_state):
    if has_state:
        (q_ref, k_ref, v_ref, gate_ref, cos_ref, sin_ref, dec_ref, ecum_ref, wv_ref, sdec_ref, st_ref,
         o_ref, nst_ref, s_scr) = refs
    else:
        (q_ref, k_ref, v_ref, gate_ref, cos_ref, sin_ref, dec_ref, ecum_ref, wv_ref, sdec_ref,
         o_ref, nst_ref, s_scr) = refs
        st_ref = None
    c = pl.program_id(1)
    nc = pl.num_programs(1)
    half = hd // 2

    @pl.when(c == 0)
    def _():
        if has_state:
            s_scr[...] = st_ref[0]
        else:
            s_scr[...] = jnp.zeros_like(s_scr)

    cos = cos_ref[...]
    sin = sin_ref[...]

    def rope(ref, h, mult):
        x1 = ref[:, h * hd:h * hd + half]
        x2 = ref[:, h * hd + half:(h + 1) * hd]
        out = jnp.concatenate([x1 * cos - x2 * sin, x2 * cos + x1 * sin], axis=-1)
        return (out * mult).astype(MXU_DTYPE) if mult != 1.0 else out.astype(MXU_DTYPE)

    for h in range(heads):
        hs = slice(h * hd, (h + 1) * hd)
        qr = rope(q_ref, h, 1.0)
        kr = rope(k_ref, h, hd ** -0.5)
        vf = v_ref[:, hs]
        qk = _dot_nt(qr, kr)
        y = _dot((qk * dec_ref[h]).astype(MXU_DTYPE), vf.astype(MXU_DTYPE))
        s_old = s_scr[h]
        y = y + _dot(qr, s_old.astype(MXU_DTYPE)) * ecum_ref[h]
        s_scr[h] = sdec_ref[h] * s_old + _dot_tn(kr, (wv_ref[h] * vf).astype(MXU_DTYPE))
        o_ref[:, hs] = (_rms(y) * _silu(gate_ref[:, hs])).astype(o_ref.dtype)

    @pl.when(c == nc - 1)
    def _():
        nst_ref[0] = s_scr[...]


def ret_scan(proj, col, cos_rows, sin_rows, ret_state, *, batch, L, heads, hd):
    n = proj.shape[0]
    t = n // batch
    nc = t // L
    inner = heads * hd
    has_state = ret_state is not None
    lg = jnp.log1p(-jnp.exp2(-5.0 - jnp.arange(heads, dtype=F32)))[:, None, None]
    li = jnp.arange(L, dtype=F32)
    diff = li[:, None] - li[None, :]
    dec = jnp.where(diff >= 0, jnp.exp(jnp.where(diff >= 0, diff, 0.0)[None] * lg), 0.0)
    ecum = jnp.broadcast_to(jnp.exp((li[None, :, None] + 1.0) * lg), (heads, L, hd))
    wv = jnp.broadcast_to(jnp.exp((L - 1.0 - li[None, :, None]) * lg), (heads, L, hd))
    sdec = jnp.broadcast_to(jnp.exp(L * lg), (heads, 1, hd))
    rowmap = lambda b, c: b * nc + c
    in_specs = [pl.BlockSpec((L, inner), lambda b, c: (rowmap(b, c), col["q"])),
                pl.BlockSpec((L, inner), lambda b, c: (rowmap(b, c), col["k"])),
                pl.BlockSpec((L, inner), lambda b, c: (rowmap(b, c), col["v"])),
                pl.BlockSpec((L, inner), lambda b, c: (rowmap(b, c), col["gate"])),
                pl.BlockSpec((L, hd // 2), lambda b, c: (rowmap(b, c), 0)),
                pl.BlockSpec((L, hd // 2), lambda b, c: (rowmap(b, c), 0)),
                pl.BlockSpec((heads, L, L), lambda b, c: (0, 0, 0)),
                pl.BlockSpec((heads, L, hd), lambda b, c: (0, 0, 0)),
                pl.BlockSpec((heads, L, hd), lambda b, c: (0, 0, 0)),
                pl.BlockSpec((heads, 1, hd), lambda b, c: (0, 0, 0))]
    args = [proj, proj, proj, proj, cos_rows, sin_rows, dec, ecum, wv, sdec]
    if has_state:
        in_specs.append(pl.BlockSpec((1, heads, hd, hd), lambda b, c: (b, 0, 0, 0)))
        args.append(ret_state)
    state_bytes = _nbytes((heads, hd, hd), F32)
    vmem = (2 * (4 * _nbytes((L, inner), F32) + _nbytes((L, inner), MXU_DTYPE) + _nbytes((heads, L, L), F32)
                 + 2 * _nbytes((heads, L, hd), F32) + 2 * state_bytes)
            + state_bytes + 16 * _nbytes((L, hd), F32) + (8 << 20))
    return pl.pallas_call(
        functools.partial(_ret_kernel, L=L, heads=heads, hd=hd, has_state=has_state),
        out_shape=(jax.ShapeDtypeStruct((n, inner), MXU_DTYPE),
                   jax.ShapeDtypeStruct((batch, heads, hd, hd), F32)),
        grid=(batch, nc),
        in_specs=in_specs,
        out_specs=(pl.BlockSpec((L, inner), lambda b, c: (rowmap(b, c), 0)),
                   pl.BlockSpec((1, heads, hd, hd), lambda b, c: (b, 0, 0, 0))),
        scratch_shapes=[pltpu.VMEM((heads, hd, hd), F32)],
        compiler_params=_params(("parallel", "arbitrary"), vmem),
        name="ret_scan",
    )(*args)


def _rope_tables(pos, half, reps, batch):
    inv = ROPE_BASE ** (-jnp.arange(half, dtype=F32) / half)
    ang = pos.astype(F32)[:, None] * inv[None, :]
    cos, sin = jnp.cos(ang), jnp.sin(ang)
    if reps == 0:
        return jnp.tile(cos, (batch, 1)), jnp.tile(sin, (batch, 1))
    c = jnp.tile(jnp.concatenate([cos, cos], axis=-1), (batch, reps))
    s = jnp.tile(jnp.concatenate([-sin, sin], axis=-1), (batch, reps))
    return c, s


def _mixer_even(x, batch, pos, norm_g, prm, conv_state, ssd_state, ret_state, dims, L_ssd, L_ret):
    d = x.shape[1]
    groups, hpg, hdim, nstate, rheads, rhd = dims
    proj = rms_matmul(x, norm_g, prm["w_in"])
    col = prm["col"]
    y, new_conv, new_ssd = ssd_scan(proj, col, prm["conv_w"], prm["conv_b"], prm["dt_bias"], prm["a_log"],
                                    prm["d_skip_e"], prm["ssd_norm"], conv_state, ssd_state,
                                    batch=batch, L=L_ssd, groups=groups, hpg=hpg, hdim=hdim, nstate=nstate)
    cos_rows, sin_rows = _rope_tables(pos, rhd // 2, 0, batch)
    o, new_ret = ret_scan(proj, col, cos_rows, sin_rows, ret_state, batch=batch, L=L_ret, heads=rheads, hd=rhd)
    x = matmul_residual(x, [y, o], [prm["w_out_ssd"], prm["w_out_ret"]])
    return x, (new_conv, new_ssd, new_ret)


def _prep_even(w_in, conv_w, conv_b, dt_bias, a_log, d_skip, ssd_norm, w_out, dims):
    groups, hpg, hdim, nstate, rheads, rhd = dims
    d = w_in.shape[0]
    dx = groups * hpg * hdim
    dn = groups * nstate
    nh = groups * hpg
    ri = rheads * rhd
    o_z, o_xs, o_b, o_c, o_dt = 0, dx, 2 * dx, 2 * dx + dn, 2 * dx + 2 * dn
    o_q = o_dt + nh
    seg = lambda lo, w: w_in[:, lo:lo + w]
    dt_pad = jnp.zeros((d, V7X_LANES - nh), w_in.dtype)
    w_new = jnp.concatenate([seg(o_z, dx), seg(o_q, ri), seg(o_q + ri, ri), seg(o_q + 2 * ri, ri),
                             seg(o_q + 3 * ri, ri), seg(o_xs, dx), seg(o_b, dn), seg(o_c, dn),
                             seg(o_dt, nh), dt_pad], axis=1).astype(MXU_DTYPE)
    assert dx == ri and dx % dn == 0 and dn % V7X_LANES == 0
    base = 5 * dx
    col = {"z": 0, "q": 1, "k": 2, "v": 3, "gate": 4, "xs": 5,
           "B": base // dn + dx // dn, "C": base // dn + dx // dn + 1,
           "dt": (base + dx + 2 * dn) // V7X_LANES}
    pad1 = lambda v: jnp.pad(v.astype(F32), (0, V7X_LANES - nh)).reshape(1, V7X_LANES)
    return {"w_in": w_new, "col": col, "conv_w": conv_w, "conv_b": conv_b,
            "dt_bias": pad1(dt_bias), "a_log": pad1(a_log),
            "d_skip_e": jnp.repeat(d_skip, hdim).reshape(1, dx), "ssd_norm": ssd_norm,
            "w_out_ssd": w_out[:dx].astype(MXU_DTYPE), "w_out_ret": w_out[dx:].astype(MXU_DTYPE)}


def _prep_odd(w_in, q_norm, kv_norm, w_uq, w_uk, w_uv, w_out, rope_dim):
    d = w_in.shape[0]
    q_lora = q_norm.shape[0]
    kv_lora, heads, nope = w_uk.shape
    kp = w_in[:, q_lora + kv_lora:]
    w_in_new = jnp.concatenate([w_in[:, :q_lora + kv_lora], kp, kp], axis=1).astype(MXU_DTYPE)
    assert 2 * rope_dim == V7X_LANES
    wq = w_uq.reshape(q_lora, heads, nope + rope_dim)
    wq_nope = wq[:, :, :nope].reshape(q_lora, heads * nope)
    wq_rope = jnp.pad(wq[:, :, nope:], ((0, 0), (0, 0), (0, V7X_LANES - rope_dim))).reshape(q_lora, heads * V7X_LANES)
    return {"w_in": w_in_new, "q_norm": q_norm, "kv_norm": kv_norm,
            "w_uq": jnp.concatenate([wq_nope, wq_rope], axis=1).astype(MXU_DTYPE),
            "w_uk": jnp.transpose(w_uk, (1, 2, 0)).astype(MXU_DTYPE),
            "w_uv": jnp.transpose(w_uv, (1, 0, 2)).astype(MXU_DTYPE),
            "w_out": w_out.astype(MXU_DTYPE), "nope_cols": heads * nope}


def _mixer_odd(x, batch, pos, norm_g, prm, past_kcat, rope_dim, q_offset):
    n, d = x.shape
    t = n // batch
    q_lora = prm["q_norm"].shape[0]
    kv_lora = prm["kv_norm"].shape[0]
    heads, nope, lat = prm["w_uk"].shape
    proj = rms_matmul(x, norm_g, prm["w_in"])
    cos_rows, sin_rows = _rope_tables(pos, rope_dim // 2, V7X_LANES // rope_dim, batch)
    q, ckv, kpe, kcat = mla_prep(proj, prm["q_norm"], prm["kv_norm"], prm["w_uq"], cos_rows, sin_rows,
                                 q_lora=q_lora, kv_lora=kv_lora, rope_dim=rope_dim, nope_cols=prm["nope_cols"])
    q = q.reshape(batch, t, q.shape[1])
    kcat = kcat.reshape(batch, t, kcat.shape[1])
    if past_kcat is None:
        keys, kv_len = kcat, t
        tq, tk = _pick(t, (256, 128, 64)), _pick(t, (512, 256, 128, 64))
    else:
        kv_len = past_kcat.shape[1] + t
        padded = -(-kv_len // V7X_LANES) * V7X_LANES
        keys = jnp.concatenate([past_kcat, kcat, jnp.zeros((batch, padded - kv_len, kcat.shape[2]), kcat.dtype)], axis=1)
        tq, tk = t, padded
    o = mla_attn(q, keys, prm["w_uk"], prm["w_uv"], tq=tq, tk=tk, q_offset=q_offset, kv_len=kv_len,
                 scale=(nope + rope_dim) ** -0.5)
    x = matmul_residual(x, [o.reshape(n, o.shape[2])], [prm["w_out"]])
    return x, (ckv.reshape(batch, t, kv_lora), kpe.reshape(batch, t, rope_dim))


def kernel(x_prompt, x_sample, mem_prompt, state_conv, state_ssd, state_ret, cache_ckv, cache_kpe,
           cache_mem_k, cache_mem_v, norms, ffn_w1, ffn_w2, mem_norm, w_mq, w_mkv, w_mo,
           ab_w_in, ab_conv_w, ab_conv_b, ab_dt_bias, ab_a_log, ab_d_skip, ab_ssd_norm, ab_w_out,
           c_w_in, c_q_norm, c_kv_norm, c_w_uq, c_w_uk, c_w_uv, c_w_out, final_norm):
    bp, tp, d = x_prompt.shape
    bs, ts, _ = x_sample.shape
    depth = norms.shape[0]
    past_len = cache_ckv.shape[2]
    mem_tokens = mem_prompt.shape[1]
    mem_heads, mem_hd = cache_mem_k.shape[3], cache_mem_k.shape[4]
    mem_inner = mem_heads * mem_hd
    ssd_heads, nstate, hdim = state_ssd.shape[2], state_ssd.shape[3], state_ssd.shape[4]
    cdim = state_conv.shape[3]
    groups = (cdim - ssd_heads * hdim) // (2 * nstate)
    hpg = ssd_heads // groups
    rheads, rhd = state_ret.shape[2], state_ret.shape[3]
    dims = (groups, hpg, hdim, nstate, rheads, rhd)
    rope_dim = cache_kpe.shape[3]

    pos_p = jnp.arange(tp)
    pos_s = past_len + jnp.arange(ts)
    xp = x_prompt.reshape(bp * tp, d)
    xs = x_sample.reshape(bs * ts, d)
    L_ssd_p, L_ret_p = _pick(tp, (128, 64)), _pick(tp, (256, 128, 64))
    L_s = _pick(ts, (128, 64))

    outs = {k: [] for k in ("conv_p", "ssd_p", "ret_p", "ckv_p", "kpe_p", "memk_p", "memv_p",
                            "conv_s", "ssd_s", "ret_s", "ckv_s", "kpe_s")}

    def to_group_layout(st):
        b = st.shape[0]
        return st.reshape(b, groups, hpg, nstate, hdim).transpose(0, 1, 3, 2, 4).reshape(b, groups, nstate, hpg * hdim)

    def from_group_layout(st):
        b = st.shape[0]
        return st.reshape(b, groups, nstate, hpg, hdim).transpose(0, 1, 3, 2, 4).reshape(b, ssd_heads, nstate, hdim)

    for i in range(depth):
        j = i // 2
        w1 = ffn_w1[i].astype(MXU_DTYPE)
        w2 = ffn_w2[i].astype(MXU_DTYPE)
        wq_m = w_mq[i].astype(MXU_DTYPE)
        wo_m = w_mo[i].astype(MXU_DTYPE)
        mkv = rms_matmul(mem_prompt.reshape(bp * mem_tokens, d), mem_norm[i], w_mkv[i].astype(MXU_DTYPE))
        mk_p = mkv[:, :mem_inner].reshape(bp, mem_tokens, mem_inner)
        mv_p = mkv[:, mem_inner:].reshape(bp, mem_tokens, mem_inner)
        outs["memk_p"].append(mk_p.reshape(bp, mem_tokens, mem_heads, mem_hd))
        outs["memv_p"].append(mv_p.reshape(bp, mem_tokens, mem_heads, mem_hd))

        xp = ffn(xp, norms[i, 0], w1[0], w2[0])
        xs = ffn(xs, norms[i, 0], w1[0], w2[0])
        if i % 2 == 0:
            prm = _prep_even(ab_w_in[j], ab_conv_w[j], ab_conv_b[j], ab_dt_bias[j], ab_a_log[j],
                             ab_d_skip[j], ab_ssd_norm[j], ab_w_out[j], dims)
            zero_conv = jnp.zeros((bp, CONV_PAD_ROWS, cdim), F32)
            xp, st_p = _mixer_even(xp, bp, pos_p, norms[i, 1], prm, zero_conv, None, None, dims, L_ssd_p, L_ret_p)
            conv_in = jnp.pad(state_conv[j], ((0, 0), (CONV_PAD_ROWS - (CONV_K - 1), 0), (0, 0)))
            xs, st_s = _mixer_even(xs, bs, pos_s, norms[i, 1], prm, conv_in, to_group_layout(state_ssd[j]),
                                   state_ret[j], dims, L_s, L_s)
            for tag, st in (("p", st_p), ("s", st_s)):
                outs["conv_" + tag].append(st[0][:, CONV_PAD_ROWS - (CONV_K - 1):, :])
                outs["ssd_" + tag].append(from_group_layout(st[1]))
                outs["ret_" + tag].append(st[2])
        else:
            prm = _prep_odd(c_w_in[j], c_q_norm[j], c_kv_norm[j], c_w_uq[j], c_w_uk[j], c_w_uv[j], c_w_out[j],
                            rope_dim)
            xp, st_p = _mixer_odd(xp, bp, pos_p, norms[i, 1], prm, None, rope_dim, 0)
            past = jnp.concatenate([cache_ckv[j], cache_kpe[j], cache_kpe[j]], axis=-1).astype(MXU_DTYPE)
            xs, st_s = _mixer_odd(xs, bs, pos_s, norms[i, 1], prm, past, rope_dim, past_len)
            for tag, st in (("p", st_p), ("s", st_s)):
                outs["ckv_" + tag].append(st[0])
                outs["kpe_" + tag].append(st[1])
        xp = mem_attn(xp, norms[i, 2], wq_m, mk_p, mv_p, wo_m, batch=bp, heads=mem_heads)
        xs = mem_attn(xs, norms[i, 2], wq_m, cache_mem_k[i].reshape(bs, mem_tokens, mem_inner),
                      cache_mem_v[i].reshape(bs, mem_tokens, mem_inner), wo_m, batch=bs, heads=mem_heads)
        xp = ffn(xp, norms[i, 3], w1[1], w2[1])
        xs = ffn(xs, norms[i, 3], w1[1], w2[1])

    y_prompt = rmsnorm(xp, final_norm).reshape(bp, tp, d)
    y_sample = rmsnorm(xs, final_norm).reshape(bs, ts, d)
    st = lambda k: jnp.stack(outs[k])
    return (y_prompt, y_sample, st("conv_p"), st("ssd_p"), st("ret_p"), st("ckv_p"), st("kpe_p"),
            st("memk_p"), st("memv_p"), st("conv_s"), st("ssd_s"), st("ret_s"), st("ckv_s"), st("kpe_s"))
```

```python
import functools
import math

import jax
import jax.numpy as jnp
from jax import lax
from jax.experimental import pallas as pl
from jax.experimental.pallas import tpu as pltpu

F32 = jnp.float32
MXU_DTYPE = jnp.bfloat16

EPS = 1e-6
CHUNK = 64
ROPE_BASE = 10000.0
CONV_K = 4
NEG = -1e30

V7X_VMEM_BYTES = 64 * 1024 * 1024
V7X_LANES = 128
CONV_PAD_ROWS = 8
MLA_ROW_BLOCK = 512


def _params(semantics, vmem_bytes):
    limit = min(int(vmem_bytes), V7X_VMEM_BYTES - (4 << 20))
    return pltpu.CompilerParams(dimension_semantics=semantics, vmem_limit_bytes=limit)


def _nbytes(shape, dtype):
    return math.prod(shape) * jnp.dtype(dtype).itemsize


def _pick(n, prefs):
    for p in prefs:
        if n % p == 0:
            return p
    return n


def _dot(a, b):
    return jnp.dot(a, b, preferred_element_type=F32)


def _dot_nt(a, b):
    return lax.dot_general(a, b, (((1,), (1,)), ((), ())), preferred_element_type=F32)


def _dot_tn(a, b):
    return lax.dot_general(a, b, (((0,), (0,)), ((), ())), preferred_element_type=F32)


def _split3(x):
    hi = x.astype(MXU_DTYPE)
    r = x - hi.astype(F32)
    mid = r.astype(MXU_DTYPE)
    lo = (r - mid.astype(F32)).astype(MXU_DTYPE)
    return hi, mid, lo


def _rms(xf, g=None):
    y = xf * lax.rsqrt(jnp.mean(xf * xf, axis=-1, keepdims=True) + EPS)
    return y if g is None else y * g


def _silu(a):
    return a * (1.0 / (1.0 + jnp.exp(-a)))


def _swap32(x):
    w = x.shape[-1]
    lane = lax.broadcasted_iota(jnp.int32, x.shape, x.ndim - 1)
    fwd = pltpu.roll(x, w - 32, x.ndim - 1)
    bwd = pltpu.roll(x, 32, x.ndim - 1)
    return jnp.where((lane & 63) < 32, fwd, bwd)


def _rms_matmul_kernel(x_ref, g_ref, w_ref, o_ref, xn_ref):
    @pl.when(pl.program_id(1) == 0)
    def _():
        xn_ref[...] = _rms(x_ref[...], g_ref[...]).astype(xn_ref.dtype)

    o_ref[...] = _dot(xn_ref[...], w_ref[...]).astype(o_ref.dtype)


def rms_matmul(x, g, w, *, out_dtype=F32):
    n, d = x.shape
    nout = w.shape[1]
    tm = _pick(n, (512, 256, 128))
    tn = _pick(nout, (1920, 1152, 1024, 512, 256, 128))
    vmem = (2 * (_nbytes((tm, d), F32) + _nbytes((d, tn), w.dtype) + _nbytes((tm, tn), out_dtype))
            + _nbytes((tm, d), MXU_DTYPE) + _nbytes((tm, tn), F32) + (8 << 20))
    return pl.pallas_call(
        _rms_matmul_kernel,
        out_shape=jax.ShapeDtypeStruct((n, nout), out_dtype),
        grid=(n // tm, nout // tn),
        in_specs=[pl.BlockSpec((tm, d), lambda i, j: (i, 0)),
                  pl.BlockSpec((1, d), lambda i, j: (0, 0)),
                  pl.BlockSpec((d, tn), lambda i, j: (0, j))],
        out_specs=pl.BlockSpec((tm, tn), lambda i, j: (i, j)),
        scratch_shapes=[pltpu.VMEM((tm, d), MXU_DTYPE)],
        compiler_params=_params(("parallel", "arbitrary"), vmem),
        name="rms_matmul",
    )(x, g.reshape(1, d), w)


def _ffn_kernel(x_ref, g_ref, w1a_ref, w1b_ref, w2_ref, o_ref, xn_ref, acc_ref):
    f = pl.program_id(1)

    @pl.when(f == 0)
    def _():
        xn_ref[...] = _rms(x_ref[...], g_ref[...]).astype(xn_ref.dtype)
        acc_ref[...] = jnp.zeros_like(acc_ref)

    xn = xn_ref[...]
    a = _dot(xn, w1a_ref[...])
    b = _dot(xn, w1b_ref[...])
    h = (_silu(a) * b).astype(MXU_DTYPE)
    acc_ref[...] += _dot(h, w2_ref[...])

    @pl.when(f == pl.num_programs(1) - 1)
    def _():
        o_ref[...] = x_ref[...] + 0.5 * acc_ref[...]


def ffn(x, g, w1, w2):
    n, d = x.shape
    dff = w2.shape[0]
    tm = _pick(n, (512, 256, 128))
    tf = _pick(dff, (512, 256, 128))
    nf = dff // tf
    vmem = (2 * (2 * _nbytes((tm, d), F32) + 3 * _nbytes((d, tf), w1.dtype))
            + _nbytes((tm, d), MXU_DTYPE) + _nbytes((tm, d), F32) + 4 * _nbytes((tm, tf), F32) + (8 << 20))
    return pl.pallas_call(
        _ffn_kernel,
        out_shape=jax.ShapeDtypeStruct((n, d), F32),
        grid=(n // tm, nf),
        in_specs=[pl.BlockSpec((tm, d), lambda i, f: (i, 0)),
                  pl.BlockSpec((1, d), lambda i, f: (0, 0)),
                  pl.BlockSpec((d, tf), lambda i, f: (0, f)),
                  pl.BlockSpec((d, tf), lambda i, f: (0, f + nf)),
                  pl.BlockSpec((tf, d), lambda i, f: (f, 0))],
        out_specs=pl.BlockSpec((tm, d), lambda i, f: (i, 0)),
        scratch_shapes=[pltpu.VMEM((tm, d), MXU_DTYPE), pltpu.VMEM((tm, d), F32)],
        compiler_params=_params(("parallel", "arbitrary"), vmem),
        name="ffn",
    )(x, g.reshape(1, d), w1, w1, w2)


def _matmul_residual_kernel(*refs, n_in):
    x_ref = refs[0]
    o_ref = refs[1 + 2 * n_in]
    acc = x_ref[...]
    for h_ref, w_ref in zip(refs[1:1 + n_in], refs[1 + n_in:1 + 2 * n_in]):
        acc = acc + _dot(h_ref[...], w_ref[...])
    o_ref[...] = acc


def matmul_residual(x, hs, ws):
    n, d = x.shape
    tm = _pick(n, (512, 256, 128))
    tn = _pick(d, (1024, 512, 256, 128))
    vmem = 2 * 2 * _nbytes((tm, tn), F32) + (8 << 20)
    in_specs = [pl.BlockSpec((tm, tn), lambda i, j: (i, j))]
    for h in hs:
        in_specs.append(pl.BlockSpec((tm, h.shape[1]), lambda i, j: (i, 0)))
        vmem += 2 * _nbytes((tm, h.shape[1]), h.dtype)
    for w in ws:
        in_specs.append(pl.BlockSpec((w.shape[0], tn), lambda i, j: (0, j)))
        vmem += 2 * _nbytes((w.shape[0], tn), w.dtype)
    return pl.pallas_call(
        functools.partial(_matmul_residual_kernel, n_in=len(hs)),
        out_shape=jax.ShapeDtypeStruct((n, d), F32),
        grid=(n // tm, d // tn),
        in_specs=in_specs,
        out_specs=pl.BlockSpec((tm, tn), lambda i, j: (i, j)),
        compiler_params=_params(("parallel", "arbitrary"), vmem),
        name="matmul_residual",
    )(x, *hs, *ws)


def _rmsnorm_kernel(x_ref, g_ref, o_ref):
    o_ref[...] = _rms(x_ref[...], g_ref[...])


def rmsnorm(x, g):
    n, d = x.shape
    tm = _pick(n, (512, 256, 128))
    return pl.pallas_call(
        _rmsnorm_kernel,
        out_shape=jax.ShapeDtypeStruct((n, d), F32),
        grid=(n // tm,),
        in_specs=[pl.BlockSpec((tm, d), lambda i: (i, 0)), pl.BlockSpec((1, d), lambda i: (0, 0))],
        out_specs=pl.BlockSpec((tm, d), lambda i: (i, 0)),
        compiler_params=_params(("parallel",), 6 * _nbytes((tm, d), F32) + (8 << 20)),
        name="rmsnorm",
    )(x, g.reshape(1, d))


def _mem_attn_kernel(x_ref, g_ref, wq_ref, k_ref, v_ref, wo_ref, o_ref, att_ref, *, bt, tq, heads, hd):
    xf = x_ref[...]
    xn = _rms(xf, g_ref[...]).astype(MXU_DTYPE)
    q = _dot(xn, wq_ref[...]).astype(MXU_DTYPE)
    scale = hd ** -0.5
    for b in range(bt):
        for h in range(heads):
            qh = q[b * tq:(b + 1) * tq, h * hd:(h + 1) * hd]
            kh = k_ref[b, :, h * hd:(h + 1) * hd].astype(MXU_DTYPE)
            vh = v_ref[b, :, h * hd:(h + 1) * hd].astype(MXU_DTYPE)
            s = _dot_nt(qh, kh) * scale
            m = jnp.max(s, axis=-1, keepdims=True)
            p = jnp.exp(s - m)
            l = jnp.sum(p, axis=-1, keepdims=True)
            oh = _dot(p.astype(MXU_DTYPE), vh) / l
            att_ref[b * tq:(b + 1) * tq, h * hd:(h + 1) * hd] = oh.astype(att_ref.dtype)
    o_ref[...] = xf + _dot(att_ref[...], wo_ref[...])


def mem_attn(x, g, wq, mem_k, mem_v, wo, *, batch, heads):
    n, d = x.shape
    t = n // batch
    m, inner = mem_k.shape[1], mem_k.shape[2]
    hd = inner // heads
    if t >= 128:
        bt, tq = 1, _pick(t, (512, 256, 128))
    else:
        bt, tq = _pick(batch, (8, 4, 2, 1)), t
    nt = t // tq
    rows = bt * tq
    vmem = (2 * (2 * _nbytes((rows, d), F32) + 2 * _nbytes((bt, m, inner), mem_k.dtype)
                 + 2 * _nbytes((d, inner), wq.dtype))
            + 2 * _nbytes((rows, d), F32) + (8 << 20))
    return pl.pallas_call(
        functools.partial(_mem_attn_kernel, bt=bt, tq=tq, heads=heads, hd=hd),
        out_shape=jax.ShapeDtypeStruct((n, d), F32),
        grid=(batch // bt, nt),
        in_specs=[pl.BlockSpec((rows, d), lambda b, i: (b * nt + i, 0)),
                  pl.BlockSpec((1, d), lambda b, i: (0, 0)),
                  pl.BlockSpec((d, inner), lambda b, i: (0, 0)),
                  pl.BlockSpec((bt, m, inner), lambda b, i: (b, 0, 0)),
                  pl.BlockSpec((bt, m, inner), lambda b, i: (b, 0, 0)),
                  pl.BlockSpec((inner, d), lambda b, i: (0, 0))],
        out_specs=pl.BlockSpec((rows, d), lambda b, i: (b * nt + i, 0)),
        scratch_shapes=[pltpu.VMEM((rows, inner), MXU_DTYPE)],
        compiler_params=_params(("parallel", "arbitrary"), vmem),
        name="mem_attn",
    )(x, g.reshape(1, d), wq, mem_k, mem_v, wo)


def _mla_prep_kernel(*refs, q_lora, kv_lora, rope_dim, rope_slabs, qscale, heads, nope):
    expand = len(refs) == 12
    if expand:
        (p_ref, qn_ref, kvn_ref, wuq_ref, cos_ref, sin_ref, wukv_ref,
         q_ref, ckv_ref, kpe_ref, k_ref, v_ref) = refs
    else:
        p_ref, qn_ref, kvn_ref, wuq_ref, cos_ref, sin_ref, q_ref, ckv_ref, kpe_ref, k_ref = refs
    cos = cos_ref[...]
    sin = sin_ref[...]
    cqn = _rms(p_ref[:, 0:q_lora], qn_ref[...]).astype(MXU_DTYPE)
    q = _dot(cqn, wuq_ref[...])
    if qscale != 1.0:
        q = q * qscale
    q_ref[...] = q.astype(q_ref.dtype)
    for lo in rope_slabs:
        xs = q[:, lo:lo + V7X_LANES]
        q_ref[:, lo:lo + V7X_LANES] = (xs * cos + _swap32(xs) * sin).astype(q_ref.dtype)
    ckv = _rms(p_ref[:, q_lora:q_lora + kv_lora], kvn_ref[...])
    ckv_ref[...] = ckv
    kp = p_ref[:, q_lora + kv_lora:q_lora + kv_lora + V7X_LANES]
    kpr = kp * cos + _swap32(kp) * sin
    kpe_ref[...] = kpr[:, 0:rope_dim]
    kpr_m = kpr.astype(k_ref.dtype)
    if expand:
        kv = _dot(ckv.astype(MXU_DTYPE), wukv_ref[...])
        hw = nope + V7X_LANES
        for h in range(heads):
            k_ref[:, h * hw:h * hw + nope] = kv[:, h * nope:(h + 1) * nope].astype(k_ref.dtype)
            k_ref[:, h * hw + nope:(h + 1) * hw] = kpr_m
        v_ref[...] = kv[:, heads * nope:].astype(v_ref.dtype)
    else:
        k_ref[:, 0:kv_lora] = ckv.astype(k_ref.dtype)
        k_ref[:, kv_lora:kv_lora + V7X_LANES] = kpr_m


def mla_prep(proj, q_norm, kv_norm, w_uq, cos_rows, sin_rows, w_ukv, *, q_lora, kv_lora, rope_dim,
             rope_slabs, qscale, heads, nope):
    n = proj.shape[0]
    qcols = w_uq.shape[1]
    tm = _pick(n, (512, 256, 128))
    expand = w_ukv is not None
    row = lambda w: pl.BlockSpec((tm, w), lambda i: (i, 0))
    full = lambda a: pl.BlockSpec(a.shape, lambda i: (0, 0))
    in_specs = [row(proj.shape[1]), pl.BlockSpec((1, q_lora), lambda i: (0, 0)),
                pl.BlockSpec((1, kv_lora), lambda i: (0, 0)), full(w_uq), row(V7X_LANES), row(V7X_LANES)]
    args = [proj, q_norm.reshape(1, -1), kv_norm.reshape(1, -1), w_uq, cos_rows, sin_rows]
    out_shape = [jax.ShapeDtypeStruct((n, qcols), MXU_DTYPE), jax.ShapeDtypeStruct((n, kv_lora), F32),
                 jax.ShapeDtypeStruct((n, rope_dim), F32)]
    out_specs = [row(qcols), row(kv_lora), row(rope_dim)]
    vmem = (2 * (_nbytes((tm, proj.shape[1]), F32) + _nbytes(w_uq.shape, w_uq.dtype)
                 + _nbytes((tm, qcols), MXU_DTYPE) + 3 * _nbytes((tm, kv_lora + V7X_LANES), F32))
            + 3 * _nbytes((tm, qcols), F32) + (8 << 20))
    if expand:
        kcols = heads * (nope + V7X_LANES)
        vcols = w_ukv.shape[1] - heads * nope
        in_specs.append(full(w_ukv))
        args.append(w_ukv)
        out_shape += [jax.ShapeDtypeStruct((n, kcols), MXU_DTYPE), jax.ShapeDtypeStruct((n, vcols), MXU_DTYPE)]
        out_specs += [row(kcols), row(vcols)]
        vmem += (2 * (_nbytes(w_ukv.shape, w_ukv.dtype) + _nbytes((tm, kcols + vcols), MXU_DTYPE))
                 + 2 * _nbytes((tm, w_ukv.shape[1]), F32))
    else:
        out_shape.append(jax.ShapeDtypeStruct((n, kv_lora + V7X_LANES), MXU_DTYPE))
        out_specs.append(row(kv_lora + V7X_LANES))
    return pl.pallas_call(
        functools.partial(_mla_prep_kernel, q_lora=q_lora, kv_lora=kv_lora, rope_dim=rope_dim,
                          rope_slabs=rope_slabs, qscale=qscale, heads=heads, nope=nope),
        out_shape=tuple(out_shape),
        grid=(n // tm,),
        in_specs=in_specs,
        out_specs=tuple(out_specs),
        compiler_params=_params(("parallel",), vmem),
        name="mla_prep",
    )(*args)


def _lanes(x, width):
    reps = width // V7X_LANES
    return x if reps == 1 else jnp.concatenate([x] * reps, axis=-1)


def _mha_attn_kernel(qi_ref, kj_ref, last_ref, q_ref, k_ref, v_ref, o_ref, acc_ref, m_ref, l_ref,
                     *, heads, hw, vd, tq, tk, q_offset, kv_len):
    step = pl.program_id(1)
    i = qi_ref[step]
    j = kj_ref[step]

    @pl.when(j == 0)
    def _():
        m_ref[...] = jnp.full_like(m_ref, NEG)
        l_ref[...] = jnp.zeros_like(l_ref)
        acc_ref[...] = jnp.zeros_like(acc_ref)

    qpos = q_offset + i * tq + lax.broadcasted_iota(jnp.int32, (tq, tk), 0)
    kpos = j * tk + lax.broadcasted_iota(jnp.int32, (tq, tk), 1)
    visible = ((kpos // CHUNK) <= (qpos // CHUNK)) & (kpos < kv_len)
    bias = jnp.where(visible, 0.0, NEG)
    for h in range(heads):
        s = _dot_nt(q_ref[0, :, h * hw:(h + 1) * hw], k_ref[0, :, h * hw:(h + 1) * hw]) + bias
        m_old = m_ref[h]
        m_new = jnp.maximum(m_old, jnp.max(s, axis=-1, keepdims=True))
        alpha = jnp.exp2(m_old - m_new)
        p = jnp.exp2(s - _lanes(m_new, tk))
        l_ref[h] = alpha * l_ref[h] + jnp.sum(p, axis=-1, keepdims=True)
        m_ref[h] = m_new
        vs = slice(h * vd, (h + 1) * vd)
        acc_ref[:, vs] = _lanes(alpha, vd) * acc_ref[:, vs] + _dot(p.astype(MXU_DTYPE), v_ref[0, :, vs])

    @pl.when(last_ref[step] == 1)
    def _():
        for h in range(heads):
            vs = slice(h * vd, (h + 1) * vd)
            o_ref[0, :, vs] = (acc_ref[:, vs] / _lanes(l_ref[h], vd)).astype(o_ref.dtype)


def mha_attn(q, k, v, *, heads, tq, tk, q_offset, kv_len):
    b, t, _ = q.shape
    s = k.shape[1]
    hw = q.shape[2] // heads
    vd = v.shape[2] // heads
    nq, nk = t // tq, s // tk
    qi, kj, last = [], [], []
    for i in range(nq):
        last_key = ((q_offset + (i + 1) * tq - 1) // CHUNK + 1) * CHUNK - 1
        jl = min(last_key // tk, nk - 1)
        for j in range(jl + 1):
            qi.append(i)
            kj.append(j)
            last.append(int(j == jl))
    sched = [jnp.asarray(a, jnp.int32) for a in (qi, kj, last)]
    vmem = (2 * (2 * _nbytes((tq, heads * hw), q.dtype) + 2 * _nbytes((tk, heads * vd), v.dtype)
                 + _nbytes((tq, heads * vd), MXU_DTYPE))
            + _nbytes((tq, heads * vd), F32) + 2 * _nbytes((heads, tq, V7X_LANES), F32)
            + 6 * _nbytes((tq, tk), F32) + (8 << 20))
    grid_spec = pltpu.PrefetchScalarGridSpec(
        num_scalar_prefetch=3,
        grid=(b, len(qi)),
        in_specs=[pl.BlockSpec((1, tq, heads * hw), lambda bb, p, qi_r, kj_r, l_r: (bb, qi_r[p], 0)),
                  pl.BlockSpec((1, tk, heads * hw), lambda bb, p, qi_r, kj_r, l_r: (bb, kj_r[p], 0)),
                  pl.BlockSpec((1, tk, heads * vd), lambda bb, p, qi_r, kj_r, l_r: (bb, kj_r[p], 0))],
        out_specs=pl.BlockSpec((1, tq, heads * vd), lambda bb, p, qi_r, kj_r, l_r: (bb, qi_r[p], 0)),
        scratch_shapes=[pltpu.VMEM((tq, heads * vd), F32),
                        pltpu.VMEM((heads, tq, V7X_LANES), F32),
                        pltpu.VMEM((heads, tq, V7X_LANES), F32)])
    return pl.pallas_call(
        functools.partial(_mha_attn_kernel, heads=heads, hw=hw, vd=vd, tq=tq, tk=tk,
                          q_offset=q_offset, kv_len=kv_len),
        out_shape=jax.ShapeDtypeStruct((b, t, heads * vd), MXU_DTYPE),
        grid_spec=grid_spec,
        compiler_params=_params(("parallel", "arbitrary"), vmem),
        name="mha_attn",
    )(*sched, q, k, v)


def _last_kv_block(i, *, tq, tk, q_offset, nk):
    last_q = q_offset + (i + 1) * tq - 1
    last_key = (last_q // CHUNK + 1) * CHUNK - 1
    return jnp.minimum(last_key // tk, nk - 1)


def _mla_attn_kernel(q_ref, k_ref, wuk_ref, wuv_ref, o_ref, qs_ref, acc_ref, m_ref, l_ref,
                     *, heads, hb, tq, tk, nope, lat, q_offset, kv_len, scale):
    i = pl.program_id(1)
    j = pl.program_id(2)
    nk = pl.num_programs(2)
    j_last = _last_kv_block(i, tq=tq, tk=tk, q_offset=q_offset, nk=nk)
    rope_lo = heads * nope

    @pl.when(j == 0)
    def _():
        for h in range(heads):
            qn = q_ref[0, :, h * nope:(h + 1) * nope]
            qs_ref[h * tq:(h + 1) * tq, 0:lat] = (_dot(qn, wuk_ref[h]) * scale).astype(qs_ref.dtype)
            qr = q_ref[0, :, rope_lo + h * V7X_LANES:rope_lo + (h + 1) * V7X_LANES]
            qs_ref[h * tq:(h + 1) * tq, lat:lat + V7X_LANES] = (qr.astype(F32) * scale).astype(qs_ref.dtype)
        m_ref[...] = jnp.full_like(m_ref, NEG)
        l_ref[...] = jnp.zeros_like(l_ref)
        acc_ref[...] = jnp.zeros_like(acc_ref)

    @pl.when(j <= j_last)
    def _():
        k = k_ref[0]
        v = k[:, 0:lat]
        qpos = q_offset + i * tq + lax.broadcasted_iota(jnp.int32, (tq, tk), 0)
        kpos = j * tk + lax.broadcasted_iota(jnp.int32, (tq, tk), 1)
        visible = ((kpos // CHUNK) <= (qpos // CHUNK)) & (kpos < kv_len)
        bias = jnp.where(visible, 0.0, NEG)
        rows = hb * tq
        for rb in range(heads // hb):
            rs = slice(rb * rows, (rb + 1) * rows)
            s = _dot_nt(qs_ref[rs, :], k)
            s = (s.reshape(hb, tq, tk) + bias[None]).reshape(rows, tk)
            m_old = m_ref[rs, :]
            m_new = jnp.maximum(m_old, jnp.max(s, axis=-1, keepdims=True))
            alpha = jnp.exp(m_old - m_new)
            p = jnp.exp(s - _lanes(m_new, tk))
            l_ref[rs, :] = alpha * l_ref[rs, :] + jnp.sum(p, axis=-1, keepdims=True)
            m_ref[rs, :] = m_new
            acc_ref[rs, :] = _lanes(alpha, lat) * acc_ref[rs, :] + _dot(p.astype(MXU_DTYPE), v)

    @pl.when(j == j_last)
    def _():
        vd = wuv_ref.shape[2]
        for h in range(heads):
            hs = slice(h * tq, (h + 1) * tq)
            ol = acc_ref[hs, :] / _lanes(l_ref[hs, :], lat)
            o_ref[0, :, h * vd:(h + 1) * vd] = _dot(ol.astype(MXU_DTYPE), wuv_ref[h]).astype(o_ref.dtype)


def mla_attn(q, kcat, w_uk, w_uv, *, tq, tk, q_offset, kv_len, scale):
    b, t, _ = q.shape
    s = kcat.shape[1]
    heads, nope, lat = w_uk.shape
    vd = w_uv.shape[2]
    nq, nk = t // tq, s // tk
    rows = heads * tq
    hb = max(1, min(heads, MLA_ROW_BLOCK // tq))
    assert heads % hb == 0
    last = functools.partial(_last_kv_block, tq=tq, tk=tk, q_offset=q_offset, nk=nk)
    vmem = (2 * (_nbytes((tq, q.shape[2]), q.dtype) + _nbytes((tk, kcat.shape[2]), kcat.dtype)
                 + 2 * _nbytes(w_uk.shape, w_uk.dtype) + _nbytes((tq, heads * vd), MXU_DTYPE))
            + _nbytes((rows, lat + V7X_LANES), MXU_DTYPE) + _nbytes((rows, lat), F32)
            + 2 * _nbytes((rows, V7X_LANES), F32) + 4 * _nbytes((rows, tk), F32) + (8 << 20))
    return pl.pallas_call(
        functools.partial(_mla_attn_kernel, heads=heads, hb=hb, tq=tq, tk=tk, nope=nope, lat=lat,
                          q_offset=q_offset, kv_len=kv_len, scale=scale),
        out_shape=jax.ShapeDtypeStruct((b, t, heads * vd), MXU_DTYPE),
        grid=(b, nq, nk),
        in_specs=[pl.BlockSpec((1, tq, q.shape[2]), lambda bb, i, j: (bb, i, 0)),
                  pl.BlockSpec((1, tk, kcat.shape[2]), lambda bb, i, j: (bb, jnp.minimum(j, last(i)), 0)),
                  pl.BlockSpec(w_uk.shape, lambda bb, i, j: (0, 0, 0)),
                  pl.BlockSpec(w_uv.shape, lambda bb, i, j: (0, 0, 0))],
        out_specs=pl.BlockSpec((1, tq, heads * vd), lambda bb, i, j: (bb, i, 0)),
        scratch_shapes=[pltpu.VMEM((rows, lat + V7X_LANES), MXU_DTYPE),
                        pltpu.VMEM((rows, lat), F32),
                        pltpu.VMEM((rows, V7X_LANES), F32),
                        pltpu.VMEM((rows, V7X_LANES), F32)],
        compiler_params=_params(("parallel", "parallel", "arbitrary"), vmem),
        name="mla_attn",
    )(q, kcat, w_uk, w_uv)


def _ssd_kernel(*refs, L, groups, hpg, hdim, nstate, has_state):
    if has_state:
        (z_ref, xs_ref, b_ref, c_ref, dt_ref, cw_ref, cb_ref, dtb_ref, alog_ref, dsk_ref, nrm_ref,
         cst_ref, hst_ref, y_ref, ncv_ref, nst_ref, xbuf, h_scr, cum_scr, xdt_scr, yin_scr) = refs
    else:
        (z_ref, xs_ref, b_ref, c_ref, dt_ref, cw_ref, cb_ref, dtb_ref, alog_ref, dsk_ref, nrm_ref,
         cst_ref, y_ref, ncv_ref, nst_ref, xbuf, h_scr, cum_scr, xdt_scr, yin_scr) = refs
        hst_ref = None
    c = pl.program_id(1)
    nc = pl.num_programs(1)
    dx = groups * hpg * hdim
    dn = groups * nstate
    gw = hpg * hdim
    pad = CONV_PAD_ROWS

    @pl.when(c == 0)
    def _():
        xbuf[0:pad, :] = cst_ref[0]
        if has_state:
            h_scr[...] = hst_ref[0]
        else:
            h_scr[...] = jnp.zeros_like(h_scr)

    xbuf[pad:pad + L, 0:dx] = xs_ref[...]
    xbuf[pad:pad + L, dx:dx + dn] = b_ref[...]
    xbuf[pad:pad + L, dx + dn:dx + 2 * dn] = c_ref[...]
    acc = xbuf[pad - CONV_K + 1:pad - CONV_K + 1 + L, :] * cw_ref[0:1, :] + cb_ref[...]
    for jj in range(1, CONV_K):
        lo = pad - CONV_K + 1 + jj
        acc = acc + xbuf[lo:lo + L, :] * cw_ref[jj:jj + 1, :]
    xc = _silu(acc)
    tail = xbuf[L:L + pad, :]
    xbuf[0:pad, :] = tail

    @pl.when(c == nc - 1)
    def _():
        ncv_ref[0] = tail

    dtr = dt_ref[...] + dtb_ref[...]
    dt = jnp.maximum(dtr, 0.0) + jnp.log1p(jnp.exp(-jnp.abs(dtr)))
    la = dt * (-jnp.exp(alog_ref[...]))
    row = lax.broadcasted_iota(jnp.int32, (L, L), 0)
    col = lax.broadcasted_iota(jnp.int32, (L, L), 1)
    causal = row >= col
    tri = jnp.where(causal, 1.0, 0.0).astype(MXU_DTYPE)
    cum = sum(_dot(tri, piece) for piece in _split3(la))
    nh = groups * hpg
    eye = jnp.where(lax.broadcasted_iota(jnp.int32, (V7X_LANES, V7X_LANES), 0)
                    == lax.broadcasted_iota(jnp.int32, (V7X_LANES, V7X_LANES), 1), 1.0, 0.0).astype(MXU_DTYPE)
    cum_t = sum(_dot_nt(eye, piece) for piece in _split3(cum))

    half = lax.broadcasted_iota(jnp.int32, (L, V7X_LANES), 1) < hdim
    per_vreg = V7X_LANES // hdim
    for g in range(groups):
        bg = xc[:, dx + g * nstate:dx + (g + 1) * nstate]
        cg = xc[:, dx + dn + g * nstate:dx + dn + (g + 1) * nstate].astype(MXU_DTYPE)
        qk = _dot_nt(cg, bg.astype(MXU_DTYPE))
        for sl in range(gw // V7X_LANES):
            lane0 = g * gw + sl * V7X_LANES
            h0 = lane0 // hdim
            cb = [jnp.broadcast_to(cum[:, h0 + u:h0 + u + 1], (L, V7X_LANES)) for u in range(per_vreg)]
            db = [jnp.broadcast_to(dt[:, h0 + u:h0 + u + 1], (L, V7X_LANES)) for u in range(per_vreg)]
            cum_e = jnp.where(half, cb[0], cb[1])
            dt_e = jnp.where(half, db[0], db[1])
            xdt = xc[:, lane0:lane0 + V7X_LANES] * dt_e
            xdt_m = xdt.astype(MXU_DTYPE)
            ys = []
            for u in range(per_vreg):
                seg = cb[u][:, 0:L] - cum_t[h0 + u:h0 + u + 1, :]
                decay = jnp.exp(jnp.where(causal, seg, NEG))
                ys.append(_dot((qk * decay).astype(MXU_DTYPE), xdt_m))
            cum_scr[:, lane0:lane0 + V7X_LANES] = cum_e
            xdt_scr[:, lane0:lane0 + V7X_LANES] = xdt
            yin_scr[:, lane0:lane0 + V7X_LANES] = jnp.where(half, ys[0], ys[1])

    for g in range(groups):
        gs = slice(g * gw, (g + 1) * gw)
        bg = xc[:, dx + g * nstate:dx + (g + 1) * nstate].astype(MXU_DTYPE)
        cg = xc[:, dx + dn + g * nstate:dx + dn + (g + 1) * nstate].astype(MXU_DTYPE)
        cum_g = cum_scr[:, gs]
        last = cum_scr[L - 1:L, gs]
        hg = h_scr[g]
        y_inter = _dot(cg, hg.astype(MXU_DTYPE)) * jnp.exp(cum_g)
        wx = (jnp.exp(last - cum_g) * xdt_scr[:, gs]).astype(MXU_DTYPE)
        h_scr[g] = jnp.exp(last) * hg + _dot_tn(bg, wx)
        y = yin_scr[:, gs] + y_inter + dsk_ref[:, gs] * xc[:, gs]
        y = y * _silu(z_ref[:, gs])
        y_ref[:, gs] = (_rms(y) * nrm_ref[:, gs]).astype(y_ref.dtype)

    @pl.when(c == nc - 1)
    def _():
        nst_ref[0] = h_scr[...]


def ssd_scan(proj, col, conv_w, conv_b, dt_bias, a_log, d_skip_e, ssd_norm, conv_state, ssd_state,
             *, batch, L, groups, hpg, hdim, nstate):
    n = proj.shape[0]
    t = n // batch
    nc = t // L
    dx = groups * hpg * hdim
    dn = groups * nstate
    cdim = dx + 2 * dn
    has_state = ssd_state is not None
    rowmap = lambda b, c: b * nc + c
    in_specs = [pl.BlockSpec((L, dx), lambda b, c: (rowmap(b, c), col["z"])),
                pl.BlockSpec((L, dx), lambda b, c: (rowmap(b, c), col["xs"])),
                pl.BlockSpec((L, dn), lambda b, c: (rowmap(b, c), col["B"])),
                pl.BlockSpec((L, dn), lambda b, c: (rowmap(b, c), col["C"])),
                pl.BlockSpec((L, V7X_LANES), lambda b, c: (rowmap(b, c), col["dt"])),
                pl.BlockSpec((CONV_K, cdim), lambda b, c: (0, 0)),
                pl.BlockSpec((1, cdim), lambda b, c: (0, 0)),
                pl.BlockSpec((1, V7X_LANES), lambda b, c: (0, 0)),
                pl.BlockSpec((1, V7X_LANES), lambda b, c: (0, 0)),
                pl.BlockSpec((1, dx), lambda b, c: (0, 0)),
                pl.BlockSpec((1, dx), lambda b, c: (0, 0)),
                pl.BlockSpec((1, CONV_PAD_ROWS, cdim), lambda b, c: (b, 0, 0))]
    args = [proj, proj, proj, proj, proj, conv_w, conv_b.reshape(1, cdim), dt_bias, a_log,
            d_skip_e, ssd_norm.reshape(1, dx), conv_state]
    if has_state:
        in_specs.append(pl.BlockSpec((1, groups, nstate, hpg * hdim), lambda b, c: (b, 0, 0, 0)))
        args.append(ssd_state)
    state_bytes = _nbytes((groups, nstate, hpg * hdim), F32)
    vmem = (2 * (2 * _nbytes((L, dx), F32) + 2 * _nbytes((L, dn), F32) + _nbytes((L, dx), MXU_DTYPE)
                 + 2 * _nbytes((CONV_PAD_ROWS, cdim), F32) + 2 * state_bytes)
            + state_bytes + 12 * _nbytes((L + CONV_PAD_ROWS, cdim), F32) + (8 << 20))
    return pl.pallas_call(
        functools.partial(_ssd_kernel, L=L, groups=groups, hpg=hpg, hdim=hdim, nstate=nstate,
                          has_state=has_state),
        out_shape=(jax.ShapeDtypeStruct((n, dx), MXU_DTYPE),
                   jax.ShapeDtypeStruct((batch, CONV_PAD_ROWS, cdim), F32),
                   jax.ShapeDtypeStruct((batch, groups, nstate, hpg * hdim), F32)),
        grid=(batch, nc),
        in_specs=in_specs,
        out_specs=(pl.BlockSpec((L, dx), lambda b, c: (rowmap(b, c), 0)),
                   pl.BlockSpec((1, CONV_PAD_ROWS, cdim), lambda b, c: (b, 0, 0)),
                   pl.BlockSpec((1, groups, nstate, hpg * hdim), lambda b, c: (b, 0, 0, 0))),
        scratch_shapes=[pltpu.VMEM((L + CONV_PAD_ROWS, cdim), F32),
                        pltpu.VMEM((groups, nstate, hpg * hdim), F32),
                        pltpu.VMEM((L, dx), F32),
                        pltpu.VMEM((L, dx), F32),
                        pltpu.VMEM((L, dx), F32)],
        compiler_params=_params(("parallel", "arbitrary"), vmem),
        name="ssd_scan",
    )(*args)


def _ret_kernel(*refs, L, heads, hd, has_state):
    if has_state:
        (q_ref, k_ref, v_ref, gate_ref, cos_ref, sin_ref, dec_ref, ecum_ref, wv_ref, sdec_ref, st_ref,
         o_ref, nst_ref, s_scr) = refs
    else:
        (q_ref, k_ref, v_ref, gate_ref, cos_ref, sin_ref, dec_ref, ecum_ref, wv_ref, sdec_ref,
         o_ref, nst_ref, s_scr) = refs
        st_ref = None
    c = pl.program_id(1)
    nc = pl.num_programs(1)
    half = hd // 2

    @pl.when(c == 0)
    def _():
        if has_state:
            s_scr[...] = st_ref[0]
        else:
            s_scr[...] = jnp.zeros_like(s_scr)

    cos = cos_ref[...]
    sin = sin_ref[...]

    def rope(ref, h, mult):
        x1 = ref[:, h * hd:h * hd + half]
        x2 = ref[:, h * hd + half:(h + 1) * hd]
        out = jnp.concatenate([x1 * cos - x2 * sin, x2 * cos + x1 * sin], axis=-1)
        return (out * mult).astype(MXU_DTYPE) if mult != 1.0 else out.astype(MXU_DTYPE)

    for h in range(heads):
        hs = slice(h * hd, (h + 1) * hd)
        qr = rope(q_ref, h, 1.0)
        kr = rope(k_ref, h, hd ** -0.5)
        vf = v_ref[:, hs]
        qk = _dot_nt(qr, kr)
        y = _dot((qk * dec_ref[h]).astype(MXU_DTYPE), vf.astype(MXU_DTYPE))
        s_old = s_scr[h]
        y = y + _dot(qr, s_old.astype(MXU_DTYPE)) * ecum_ref[h]
        s_scr[h] = sdec_ref[h] * s_old + _dot_tn(kr, (wv_ref[h] * vf).astype(MXU_DTYPE))
        o_ref[:, hs] = (_rms(y) * _silu(gate_ref[:, hs])).astype(o_ref.dtype)

    @pl.when(c == nc - 1)
    def _():
        nst_ref[0] = s_scr[...]


def ret_scan(proj, col, cos_rows, sin_rows, ret_state, *, batch, L, heads, hd):
    n = proj.shape[0]
    t = n // batch
    nc = t // L
    inner = heads * hd
    has_state = ret_state is not None
    lg = jnp.log1p(-jnp.exp2(-5.0 - jnp.arange(heads, dtype=F32)))[:, None, None]
    li = jnp.arange(L, dtype=F32)
    diff = li[:, None] - li[None, :]
    dec = jnp.where(diff >= 0, jnp.exp(jnp.where(diff >= 0, diff, 0.0)[None] * lg), 0.0)
    ecum = jnp.broadcast_to(jnp.exp((li[None, :, None] + 1.0) * lg), (heads, L, hd))
    wv = jnp.broadcast_to(jnp.exp((L - 1.0 - li[None, :, None]) * lg), (heads, L, hd))
    sdec = jnp.broadcast_to(jnp.exp(L * lg), (heads, 1, hd))
    rowmap = lambda b, c: b * nc + c
    in_specs = [pl.BlockSpec((L, inner), lambda b, c: (rowmap(b, c), col["q"])),
                pl.BlockSpec((L, inner), lambda b, c: (rowmap(b, c), col["k"])),
                pl.BlockSpec((L, inner), lambda b, c: (rowmap(b, c), col["v"])),
                pl.BlockSpec((L, inner), lambda b, c: (rowmap(b, c), col["gate"])),
                pl.BlockSpec((L, hd // 2), lambda b, c: (rowmap(b, c), 0)),
                pl.BlockSpec((L, hd // 2), lambda b, c: (rowmap(b, c), 0)),
                pl.BlockSpec((heads, L, L), lambda b, c: (0, 0, 0)),
                pl.BlockSpec((heads, L, hd), lambda b, c: (0, 0, 0)),
                pl.BlockSpec((heads, L, hd), lambda b, c: (0, 0, 0)),
                pl.BlockSpec((heads, 1, hd), lambda b, c: (0, 0, 0))]
    args = [proj, proj, proj, proj, cos_rows, sin_rows, dec, ecum, wv, sdec]
    if has_state:
        in_specs.append(pl.BlockSpec((1, heads, hd, hd), lambda b, c: (b, 0, 0, 0)))
        args.append(ret_state)
    state_bytes = _nbytes((heads, hd, hd), F32)
    vmem = (2 * (4 * _nbytes((L, inner), F32) + _nbytes((L, inner), MXU_DTYPE) + _nbytes((heads, L, L), F32)
                 + 2 * _nbytes((heads, L, hd), F32) + 2 * state_bytes)
            + state_bytes + 16 * _nbytes((L, hd), F32) + (8 << 20))
    return pl.pallas_call(
        functools.partial(_ret_kernel, L=L, heads=heads, hd=hd, has_state=has_state),
        out_shape=(jax.ShapeDtypeStruct((n, inner), MXU_DTYPE),
                   jax.ShapeDtypeStruct((batch, heads, hd, hd), F32)),
        grid=(batch, nc),
        in_specs=in_specs,
        out_specs=(pl.BlockSpec((L, inner), lambda b, c: (rowmap(b, c), 0)),
                   pl.BlockSpec((1, heads, hd, hd), lambda b, c: (b, 0, 0, 0))),
        scratch_shapes=[pltpu.VMEM((heads, hd, hd), F32)],
        compiler_params=_params(("parallel", "arbitrary"), vmem),
        name="ret_scan",
    )(*args)


def _rope_tables(pos, half, reps, batch):
    inv = ROPE_BASE ** (-jnp.arange(half, dtype=F32) / half)
    ang = pos.astype(F32)[:, None] * inv[None, :]
    cos, sin = jnp.cos(ang), jnp.sin(ang)
    if reps == 0:
        return jnp.tile(cos, (batch, 1)), jnp.tile(sin, (batch, 1))
    c = jnp.tile(jnp.concatenate([cos, cos], axis=-1), (batch, reps))
    s = jnp.tile(jnp.concatenate([-sin, sin], axis=-1), (batch, reps))
    return c, s


def _mixer_even(x, batch, pos, norm_g, prm, conv_state, ssd_state, ret_state, dims, L_ssd, L_ret):
    d = x.shape[1]
    groups, hpg, hdim, nstate, rheads, rhd = dims
    proj = rms_matmul(x, norm_g, prm["w_in"])
    col = prm["col"]
    y, new_conv, new_ssd = ssd_scan(proj, col, prm["conv_w"], prm["conv_b"], prm["dt_bias"], prm["a_log"],
                                    prm["d_skip_e"], prm["ssd_norm"], conv_state, ssd_state,
                                    batch=batch, L=L_ssd, groups=groups, hpg=hpg, hdim=hdim, nstate=nstate)
    cos_rows, sin_rows = _rope_tables(pos, rhd // 2, 0, batch)
    o, new_ret = ret_scan(proj, col, cos_rows, sin_rows, ret_state, batch=batch, L=L_ret, heads=rheads, hd=rhd)
    x = matmul_residual(x, [y, o], [prm["w_out_ssd"], prm["w_out_ret"]])
    return x, (new_conv, new_ssd, new_ret)


def _prep_even(w_in, conv_w, conv_b, dt_bias, a_log, d_skip, ssd_norm, w_out, dims):
    groups, hpg, hdim, nstate, rheads, rhd = dims
    d = w_in.shape[0]
    dx = groups * hpg * hdim
    dn = groups * nstate
    nh = groups * hpg
    ri = rheads * rhd
    o_z, o_xs, o_b, o_c, o_dt = 0, dx, 2 * dx, 2 * dx + dn, 2 * dx + 2 * dn
    o_q = o_dt + nh
    seg = lambda lo, w: w_in[:, lo:lo + w]
    dt_pad = jnp.zeros((d, V7X_LANES - nh), w_in.dtype)
    w_new = jnp.concatenate([seg(o_z, dx), seg(o_q, ri), seg(o_q + ri, ri), seg(o_q + 2 * ri, ri),
                             seg(o_q + 3 * ri, ri), seg(o_xs, dx), seg(o_b, dn), seg(o_c, dn),
                             seg(o_dt, nh), dt_pad], axis=1).astype(MXU_DTYPE)
    assert dx == ri and dx % dn == 0 and dn % V7X_LANES == 0
    base = 5 * dx
    col = {"z": 0, "q": 1, "k": 2, "v": 3, "gate": 4, "xs": 5,
           "B": base // dn + dx // dn, "C": base // dn + dx // dn + 1,
           "dt": (base + dx + 2 * dn) // V7X_LANES}
    pad1 = lambda v: jnp.pad(v.astype(F32), (0, V7X_LANES - nh)).reshape(1, V7X_LANES)
    return {"w_in": w_new, "col": col, "conv_w": conv_w, "conv_b": conv_b,
            "dt_bias": pad1(dt_bias), "a_log": pad1(a_log),
            "d_skip_e": jnp.repeat(d_skip, hdim).reshape(1, dx), "ssd_norm": ssd_norm,
            "w_out_ssd": w_out[:dx].astype(MXU_DTYPE), "w_out_ret": w_out[dx:].astype(MXU_DTYPE)}


def _prep_odd(w_in, q_norm, kv_norm, w_uq, w_uk, w_uv, w_out, rope_dim):
    d = w_in.shape[0]
    q_lora = q_norm.shape[0]
    kv_lora, heads, nope = w_uk.shape
    kp = w_in[:, q_lora + kv_lora:]
    w_in_new = jnp.concatenate([w_in[:, :q_lora + kv_lora], kp, kp], axis=1).astype(MXU_DTYPE)
    assert 2 * rope_dim == V7X_LANES
    wq = w_uq.reshape(q_lora, heads, nope + rope_dim)
    wq_nope = wq[:, :, :nope]
    wq_rope = jnp.pad(wq[:, :, nope:], ((0, 0), (0, 0), (0, V7X_LANES - rope_dim)))
    hw = nope + V7X_LANES
    return {"w_in": w_in_new, "q_norm": q_norm, "kv_norm": kv_norm,
            "w_uq_lat": jnp.concatenate([wq_nope.reshape(q_lora, heads * nope),
                                         wq_rope.reshape(q_lora, heads * V7X_LANES)], axis=1).astype(MXU_DTYPE),
            "slabs_lat": tuple(heads * nope + h * V7X_LANES for h in range(heads)),
            "w_uq_head": jnp.concatenate([wq_nope, wq_rope], axis=2).reshape(q_lora, heads * hw).astype(MXU_DTYPE),
            "slabs_head": tuple(h * hw + nope for h in range(heads)),
            "w_ukv": jnp.concatenate([w_uk.reshape(kv_lora, heads * nope),
                                      w_uv.reshape(kv_lora, -1)], axis=1).astype(MXU_DTYPE),
            "w_uk": jnp.transpose(w_uk, (1, 2, 0)).astype(MXU_DTYPE),
            "w_uv": jnp.transpose(w_uv, (1, 0, 2)).astype(MXU_DTYPE),
            "w_out": w_out.astype(MXU_DTYPE)}


def _mixer_odd(x, batch, pos, norm_g, prm, past_kcat, rope_dim, q_offset):
    n, d = x.shape
    t = n // batch
    q_lora = prm["q_norm"].shape[0]
    kv_lora = prm["kv_norm"].shape[0]
    heads, nope, lat = prm["w_uk"].shape
    proj = rms_matmul(x, norm_g, prm["w_in"])
    cos_rows, sin_rows = _rope_tables(pos, rope_dim // 2, V7X_LANES // rope_dim, batch)
    scale = (nope + rope_dim) ** -0.5
    dims = dict(q_lora=q_lora, kv_lora=kv_lora, rope_dim=rope_dim, heads=heads, nope=nope)
    if past_kcat is None:
        q, ckv, kpe, k, v = mla_prep(proj, prm["q_norm"], prm["kv_norm"], prm["w_uq_head"], cos_rows, sin_rows,
                                     prm["w_ukv"], rope_slabs=prm["slabs_head"], qscale=scale * math.log2(math.e),
                                     **dims)
        tq = tk = _pick(t, (512, 256, 128, 64))
        shp = lambda a: a.reshape(batch, t, a.shape[1])
        o = mha_attn(shp(q), shp(k), shp(v), heads=heads, tq=tq, tk=tk, q_offset=q_offset, kv_len=t)
    else:
        q, ckv, kpe, kcat = mla_prep(proj, prm["q_norm"], prm["kv_norm"], prm["w_uq_lat"], cos_rows, sin_rows,
                                     None, rope_slabs=prm["slabs_lat"], qscale=1.0, **dims)
        q = q.reshape(batch, t, q.shape[1])
        kcat = kcat.reshape(batch, t, kcat.shape[1])
        kv_len = past_kcat.shape[1] + t
        padded = -(-kv_len // V7X_LANES) * V7X_LANES
        keys = jnp.concatenate([past_kcat, kcat, jnp.zeros((batch, padded - kv_len, kcat.shape[2]), kcat.dtype)], axis=1)
        o = mla_attn(q, keys, prm["w_uk"], prm["w_uv"], tq=t, tk=padded, q_offset=q_offset, kv_len=kv_len,
                     scale=scale)
    x = matmul_residual(x, [o.reshape(n, o.shape[2])], [prm["w_out"]])
    return x, (ckv.reshape(batch, t, kv_lora), kpe.reshape(batch, t, rope_dim))


def kernel(x_prompt, x_sample, mem_prompt, state_conv, state_ssd, state_ret, cache_ckv, cache_kpe,
           cache_mem_k, cache_mem_v, norms, ffn_w1, ffn_w2, mem_norm, w_mq, w_mkv, w_mo,
           ab_w_in, ab_conv_w, ab_conv_b, ab_dt_bias, ab_a_log, ab_d_skip, ab_ssd_norm, ab_w_out,
           c_w_in, c_q_norm, c_kv_norm, c_w_uq, c_w_uk, c_w_uv, c_w_out, final_norm):
    bp, tp, d = x_prompt.shape
    bs, ts, _ = x_sample.shape
    depth = norms.shape[0]
    past_len = cache_ckv.shape[2]
    mem_tokens = mem_prompt.shape[1]
    mem_heads, mem_hd = cache_mem_k.shape[3], cache_mem_k.shape[4]
    mem_inner = mem_heads * mem_hd
    ssd_heads, nstate, hdim = state_ssd.shape[2], state_ssd.shape[3], state_ssd.shape[4]
    cdim = state_conv.shape[3]
    groups = (cdim - ssd_heads * hdim) // (2 * nstate)
    hpg = ssd_heads // groups
    rheads, rhd = state_ret.shape[2], state_ret.shape[3]
    dims = (groups, hpg, hdim, nstate, rheads, rhd)
    rope_dim = cache_kpe.shape[3]

    pos_p = jnp.arange(tp)
    pos_s = past_len + jnp.arange(ts)
    xp = x_prompt.reshape(bp * tp, d)
    xs = x_sample.reshape(bs * ts, d)
    L_ssd_p, L_ret_p = _pick(tp, (128, 64)), _pick(tp, (256, 128, 64))
    L_s = _pick(ts, (128, 64))

    outs = {k: [] for k in ("conv_p", "ssd_p", "ret_p", "ckv_p", "kpe_p", "memk_p", "memv_p",
                            "conv_s", "ssd_s", "ret_s", "ckv_s", "kpe_s")}

    def to_group_layout(st):
        b = st.shape[0]
        return st.reshape(b, groups, hpg, nstate, hdim).transpose(0, 1, 3, 2, 4).reshape(b, groups, nstate, hpg * hdim)

    def from_group_layout(st):
        b = st.shape[0]
        return st.reshape(b, groups, nstate, hpg, hdim).transpose(0, 1, 3, 2, 4).reshape(b, ssd_heads, nstate, hdim)

    for i in range(depth):
        j = i // 2
        w1 = ffn_w1[i].astype(MXU_DTYPE)
        w2 = ffn_w2[i].astype(MXU_DTYPE)
        wq_m = w_mq[i].astype(MXU_DTYPE)
        wo_m = w_mo[i].astype(MXU_DTYPE)
        mkv = rms_matmul(mem_prompt.reshape(bp * mem_tokens, d), mem_norm[i], w_mkv[i].astype(MXU_DTYPE))
        mk_p = mkv[:, :mem_inner].reshape(bp, mem_tokens, mem_inner)
        mv_p = mkv[:, mem_inner:].reshape(bp, mem_tokens, mem_inner)
        outs["memk_p"].append(mk_p.reshape(bp, mem_tokens, mem_heads, mem_hd))
        outs["memv_p"].append(mv_p.reshape(bp, mem_tokens, mem_heads, mem_hd))

        xp = ffn(xp, norms[i, 0], w1[0], w2[0])
        xs = ffn(xs, norms[i, 0], w1[0], w2[0])
        if i % 2 == 0:
            prm = _prep_even(ab_w_in[j], ab_conv_w[j], ab_conv_b[j], ab_dt_bias[j], ab_a_log[j],
                             ab_d_skip[j], ab_ssd_norm[j], ab_w_out[j], dims)
            zero_conv = jnp.zeros((bp, CONV_PAD_ROWS, cdim), F32)
            xp, st_p = _mixer_even(xp, bp, pos_p, norms[i, 1], prm, zero_conv, None, None, dims, L_ssd_p, L_ret_p)
            conv_in = jnp.pad(state_conv[j], ((0, 0), (CONV_PAD_ROWS - (CONV_K - 1), 0), (0, 0)))
            xs, st_s = _mixer_even(xs, bs, pos_s, norms[i, 1], prm, conv_in, to_group_layout(state_ssd[j]),
                                   state_ret[j], dims, L_s, L_s)
            for tag, st in (("p", st_p), ("s", st_s)):
                outs["conv_" + tag].append(st[0][:, CONV_PAD_ROWS - (CONV_K - 1):, :])
                outs["ssd_" + tag].append(from_group_layout(st[1]))
                outs["ret_" + tag].append(st[2])
        else:
            prm = _prep_odd(c_w_in[j], c_q_norm[j], c_kv_norm[j], c_w_uq[j], c_w_uk[j], c_w_uv[j], c_w_out[j],
                            rope_dim)
            xp, st_p = _mixer_odd(xp, bp, pos_p, norms[i, 1], prm, None, rope_dim, 0)
            past = jnp.concatenate([cache_ckv[j], cache_kpe[j], cache_kpe[j]], axis=-1).astype(MXU_DTYPE)
            xs, st_s = _mixer_odd(xs, bs, pos_s, norms[i, 1], prm, past, rope_dim, past_len)
            for tag, st in (("p", st_p), ("s", st_s)):
                outs["ckv_" + tag].append(st[0])
                outs["kpe_" + tag].append(st[1])
        xp = mem_attn(xp, norms[i, 2], wq_m, mk_p, mv_p, wo_m, batch=bp, heads=mem_heads)
        xs = mem_attn(xs, norms[i, 2], wq_m, cache_mem_k[i].reshape(bs, mem_tokens, mem_inner),
                      cache_mem_v[i].reshape(bs, mem_tokens, mem_inner), wo_m, batch=bs, heads=mem_heads)
        xp = ffn(xp, norms[i, 3], w1[1], w2[1])
        xs = ffn(xs, norms[i, 3], w1[1], w2[1])

    y_prompt = rmsnorm(xp, final_norm).reshape(bp, tp, d)
    y_sample = rmsnorm(xs, final_norm).reshape(bs, ts, d)
    st = lambda k: jnp.stack(outs[k])
    return (y_prompt, y_sample, st("conv_p"), st("ssd_p"), st("ret_p"), st("ckv_p"), st("kpe_p"),
            st("memk_p"), st("memv_p"), st("conv_s"), st("ssd_s"), st("ret_s"), st("ckv_s"), st("kpe_s"))
```

```python
import functools
import math

import jax
import jax.numpy as jnp
from jax import lax
from jax.experimental import pallas as pl
from jax.experimental.pallas import tpu as pltpu

F32 = jnp.float32
MXU_DTYPE = jnp.bfloat16

EPS = 1e-6
CHUNK = 64
ROPE_BASE = 10000.0
CONV_K = 4
NEG = -1e30

V7X_VMEM_BYTES = 64 * 1024 * 1024
V7X_LANES = 128
CONV_PAD_ROWS = 8
MLA_ROW_BLOCK = 512


def _params(semantics, vmem_bytes):
    limit = min(int(vmem_bytes), V7X_VMEM_BYTES - (4 << 20))
    return pltpu.CompilerParams(dimension_semantics=semantics, vmem_limit_bytes=limit)


def _nbytes(shape, dtype):
    return math.prod(shape) * jnp.dtype(dtype).itemsize


def _pick(n, prefs):
    for p in prefs:
        if n % p == 0:
            return p
    return n


def _dot(a, b):
    return jnp.dot(a, b, preferred_element_type=F32)


def _dot_nt(a, b):
    return lax.dot_general(a, b, (((1,), (1,)), ((), ())), preferred_element_type=F32)


def _dot_tn(a, b):
    return lax.dot_general(a, b, (((0,), (0,)), ((), ())), preferred_element_type=F32)


def _split3(x):
    hi = x.astype(MXU_DTYPE)
    r = x - hi.astype(F32)
    mid = r.astype(MXU_DTYPE)
    lo = (r - mid.astype(F32)).astype(MXU_DTYPE)
    return hi, mid, lo


def _rms(xf, g=None):
    y = xf * lax.rsqrt(jnp.mean(xf * xf, axis=-1, keepdims=True) + EPS)
    return y if g is None else y * g


def _silu(a):
    return a * (1.0 / (1.0 + jnp.exp(-a)))


def _swap32(x):
    w = x.shape[-1]
    lane = lax.broadcasted_iota(jnp.int32, x.shape, x.ndim - 1)
    fwd = pltpu.roll(x, w - 32, x.ndim - 1)
    bwd = pltpu.roll(x, 32, x.ndim - 1)
    return jnp.where((lane & 63) < 32, fwd, bwd)


def _rms_matmul_kernel(x_ref, g_ref, w_ref, o_ref, xn_ref):
    @pl.when(pl.program_id(1) == 0)
    def _():
        xn_ref[...] = _rms(x_ref[...], g_ref[...]).astype(xn_ref.dtype)

    o_ref[...] = _dot(xn_ref[...], w_ref[...]).astype(o_ref.dtype)


def rms_matmul(x, g, w, *, out_dtype=F32):
    n, d = x.shape
    nout = w.shape[1]
    tm = _pick(n, (512, 256, 128))
    tn = _pick(nout, (2048, 1920, 1152, 1024, 640, 512, 256, 128))
    vmem = (2 * (_nbytes((tm, d), F32) + _nbytes((d, tn), w.dtype) + _nbytes((tm, tn), out_dtype))
            + _nbytes((tm, d), MXU_DTYPE) + _nbytes((tm, tn), F32) + (8 << 20))
    return pl.pallas_call(
        _rms_matmul_kernel,
        out_shape=jax.ShapeDtypeStruct((n, nout), out_dtype),
        grid=(n // tm, nout // tn),
        in_specs=[pl.BlockSpec((tm, d), lambda i, j: (i, 0)),
                  pl.BlockSpec((1, d), lambda i, j: (0, 0)),
                  pl.BlockSpec((d, tn), lambda i, j: (0, j))],
        out_specs=pl.BlockSpec((tm, tn), lambda i, j: (i, j)),
        scratch_shapes=[pltpu.VMEM((tm, d), MXU_DTYPE)],
        compiler_params=_params(("parallel", "arbitrary"), vmem),
        name="rms_matmul",
    )(x, g.reshape(1, d), w)


def _ffn_kernel(*refs, final_norm):
    if final_norm:
        x_ref, g_ref, w1a_ref, w1b_ref, w2_ref, gf_ref, o_ref, xn_ref, acc_ref = refs
    else:
        x_ref, g_ref, w1a_ref, w1b_ref, w2_ref, o_ref, xn_ref, acc_ref = refs
    f = pl.program_id(1)

    @pl.when(f == 0)
    def _():
        xn_ref[...] = _rms(x_ref[...], g_ref[...]).astype(xn_ref.dtype)
        acc_ref[...] = jnp.zeros_like(acc_ref)

    xn = xn_ref[...]
    a = _dot(xn, w1a_ref[...])
    b = _dot(xn, w1b_ref[...])
    h = (_silu(a) * b).astype(MXU_DTYPE)
    acc_ref[...] += _dot(h, w2_ref[...])

    @pl.when(f == pl.num_programs(1) - 1)
    def _():
        y = x_ref[...] + 0.5 * acc_ref[...]
        o_ref[...] = _rms(y, gf_ref[...]) if final_norm else y


def ffn(x, g, w1, w2, layer, which, final_g=None):
    n, d = x.shape
    dff = w2.shape[2]
    tm = _pick(n, (512, 256, 128))
    tf = _pick(dff, (512, 256, 128))
    nf = dff // tf
    vmem = (2 * (2 * _nbytes((tm, d), F32) + 3 * _nbytes((d, tf), w1.dtype))
            + _nbytes((tm, d), MXU_DTYPE) + _nbytes((tm, d), F32) + 4 * _nbytes((tm, tf), F32) + (8 << 20))
    in_specs = [pl.BlockSpec((tm, d), lambda i, f: (i, 0)),
                pl.BlockSpec((1, d), lambda i, f: (0, 0)),
                pl.BlockSpec((None, None, d, tf), lambda i, f: (layer, which, 0, f)),
                pl.BlockSpec((None, None, d, tf), lambda i, f: (layer, which, 0, f + nf)),
                pl.BlockSpec((None, None, tf, d), lambda i, f: (layer, which, f, 0))]
    args = [x, g.reshape(1, d), w1, w1, w2]
    if final_g is not None:
        in_specs.append(pl.BlockSpec((1, d), lambda i, f: (0, 0)))
        args.append(final_g.reshape(1, d))
    return pl.pallas_call(
        functools.partial(_ffn_kernel, final_norm=final_g is not None),
        out_shape=jax.ShapeDtypeStruct((n, d), F32),
        grid=(n // tm, nf),
        in_specs=in_specs,
        out_specs=pl.BlockSpec((tm, d), lambda i, f: (i, 0)),
        scratch_shapes=[pltpu.VMEM((tm, d), MXU_DTYPE), pltpu.VMEM((tm, d), F32)],
        compiler_params=_params(("parallel", "arbitrary"), vmem),
        name="ffn",
    )(*args)


def _matmul_residual_kernel(*refs, n_in):
    x_ref = refs[0]
    o_ref = refs[1 + 2 * n_in]
    acc = x_ref[...]
    for h_ref, w_ref in zip(refs[1:1 + n_in], refs[1 + n_in:1 + 2 * n_in]):
        acc = acc + _dot(h_ref[...], w_ref[...])
    o_ref[...] = acc


def matmul_residual(x, hs, ws):
    n, d = x.shape
    tm = _pick(n, (512, 256, 128))
    tn = _pick(d, (1024, 512, 256, 128))
    vmem = 2 * 2 * _nbytes((tm, tn), F32) + (8 << 20)
    in_specs = [pl.BlockSpec((tm, tn), lambda i, j: (i, j))]
    for h in hs:
        in_specs.append(pl.BlockSpec((tm, h.shape[1]), lambda i, j: (i, 0)))
        vmem += 2 * _nbytes((tm, h.shape[1]), h.dtype)
    for w in ws:
        in_specs.append(pl.BlockSpec((w.shape[0], tn), lambda i, j: (0, j)))
        vmem += 2 * _nbytes((w.shape[0], tn), w.dtype)
    return pl.pallas_call(
        functools.partial(_matmul_residual_kernel, n_in=len(hs)),
        out_shape=jax.ShapeDtypeStruct((n, d), F32),
        grid=(n // tm, d // tn),
        in_specs=in_specs,
        out_specs=pl.BlockSpec((tm, tn), lambda i, j: (i, j)),
        compiler_params=_params(("parallel", "arbitrary"), vmem),
        name="matmul_residual",
    )(x, *hs, *ws)


def _mem_attn_kernel(x_ref, g_ref, wq_ref, k_ref, v_ref, wo_ref, o_ref, att_ref, *, bt, tq, heads, hd):
    xf = x_ref[...]
    xn = _rms(xf, g_ref[...]).astype(MXU_DTYPE)
    q = _dot(xn, wq_ref[...]).astype(MXU_DTYPE)
    scale = hd ** -0.5
    for b in range(bt):
        for h in range(heads):
            qh = q[b * tq:(b + 1) * tq, h * hd:(h + 1) * hd]
            kh = k_ref[b, :, h * hd:(h + 1) * hd].astype(MXU_DTYPE)
            vh = v_ref[b, :, h * hd:(h + 1) * hd].astype(MXU_DTYPE)
            s = _dot_nt(qh, kh) * scale
            m = jnp.max(s, axis=-1, keepdims=True)
            p = jnp.exp(s - m)
            l = jnp.sum(p, axis=-1, keepdims=True)
            oh = _dot(p.astype(MXU_DTYPE), vh) / l
            att_ref[b * tq:(b + 1) * tq, h * hd:(h + 1) * hd] = oh.astype(att_ref.dtype)
    o_ref[...] = xf + _dot(att_ref[...], wo_ref[...])


def mem_attn(x, g, wq, mem_k, mem_v, wo, *, batch, heads):
    n, d = x.shape
    t = n // batch
    m, inner = mem_k.shape[1], mem_k.shape[2]
    hd = inner // heads
    if t >= 128:
        bt, tq = 1, _pick(t, (512, 256, 128))
    else:
        bt, tq = _pick(batch, (8, 4, 2, 1)), t
    nt = t // tq
    rows = bt * tq
    vmem = (2 * (2 * _nbytes((rows, d), F32) + 2 * _nbytes((bt, m, inner), mem_k.dtype)
                 + 2 * _nbytes((d, inner), wq.dtype))
            + 2 * _nbytes((rows, d), F32) + (8 << 20))
    return pl.pallas_call(
        functools.partial(_mem_attn_kernel, bt=bt, tq=tq, heads=heads, hd=hd),
        out_shape=jax.ShapeDtypeStruct((n, d), F32),
        grid=(batch // bt, nt),
        in_specs=[pl.BlockSpec((rows, d), lambda b, i: (b * nt + i, 0)),
                  pl.BlockSpec((1, d), lambda b, i: (0, 0)),
                  pl.BlockSpec((d, inner), lambda b, i: (0, 0)),
                  pl.BlockSpec((bt, m, inner), lambda b, i: (b, 0, 0)),
                  pl.BlockSpec((bt, m, inner), lambda b, i: (b, 0, 0)),
                  pl.BlockSpec((inner, d), lambda b, i: (0, 0))],
        out_specs=pl.BlockSpec((rows, d), lambda b, i: (b * nt + i, 0)),
        scratch_shapes=[pltpu.VMEM((rows, inner), MXU_DTYPE)],
        compiler_params=_params(("parallel", "arbitrary"), vmem),
        name="mem_attn",
    )(x, g.reshape(1, d), wq, mem_k, mem_v, wo)


def _mla_prep_kernel(*refs, q_lora, kv_lora, rope_dim, rope_slabs, qscale, heads, nope):
    expand = len(refs) == 12
    if expand:
        (p_ref, qn_ref, kvn_ref, wuq_ref, cos_ref, sin_ref, wukv_ref,
         q_ref, ckv_ref, kpe_ref, k_ref, v_ref) = refs
    else:
        p_ref, qn_ref, kvn_ref, wuq_ref, cos_ref, sin_ref, q_ref, ckv_ref, kpe_ref, k_ref = refs
    cos = cos_ref[...]
    sin = sin_ref[...]
    cqn = _rms(p_ref[:, 0:q_lora], qn_ref[...]).astype(MXU_DTYPE)
    q = _dot(cqn, wuq_ref[...])
    if qscale != 1.0:
        q = q * qscale
    q_ref[...] = q.astype(q_ref.dtype)
    for lo in rope_slabs:
        xs = q[:, lo:lo + V7X_LANES]
        q_ref[:, lo:lo + V7X_LANES] = (xs * cos + _swap32(xs) * sin).astype(q_ref.dtype)
    ckv = _rms(p_ref[:, q_lora:q_lora + kv_lora], kvn_ref[...])
    ckv_ref[...] = ckv
    kp = p_ref[:, q_lora + kv_lora:q_lora + kv_lora + V7X_LANES]
    kpr = kp * cos + _swap32(kp) * sin
    kpe_ref[...] = kpr[:, 0:rope_dim]
    kpr_m = kpr.astype(k_ref.dtype)
    if expand:
        kv = _dot(ckv.astype(MXU_DTYPE), wukv_ref[...])
        hw = nope + V7X_LANES
        for h in range(heads):
            k_ref[:, h * hw:h * hw + nope] = kv[:, h * nope:(h + 1) * nope].astype(k_ref.dtype)
            k_ref[:, h * hw + nope:(h + 1) * hw] = kpr_m
        v_ref[...] = kv[:, heads * nope:].astype(v_ref.dtype)
    else:
        k_ref[:, 0:kv_lora] = ckv.astype(k_ref.dtype)
        k_ref[:, kv_lora:kv_lora + V7X_LANES] = kpr_m


def mla_prep(proj, q_norm, kv_norm, w_uq, cos_rows, sin_rows, w_ukv, *, q_lora, kv_lora, rope_dim,
             rope_slabs, qscale, heads, nope):
    n = proj.shape[0]
    qcols = w_uq.shape[1]
    tm = _pick(n, (512, 256, 128))
    expand = w_ukv is not None
    row = lambda w: pl.BlockSpec((tm, w), lambda i: (i, 0))
    full = lambda a: pl.BlockSpec(a.shape, lambda i: (0, 0))
    in_specs = [row(proj.shape[1]), pl.BlockSpec((1, q_lora), lambda i: (0, 0)),
                pl.BlockSpec((1, kv_lora), lambda i: (0, 0)), full(w_uq), row(V7X_LANES), row(V7X_LANES)]
    args = [proj, q_norm.reshape(1, -1), kv_norm.reshape(1, -1), w_uq, cos_rows, sin_rows]
    out_shape = [jax.ShapeDtypeStruct((n, qcols), MXU_DTYPE), jax.ShapeDtypeStruct((n, kv_lora), F32),
                 jax.ShapeDtypeStruct((n, rope_dim), F32)]
    out_specs = [row(qcols), row(kv_lora), row(rope_dim)]
    vmem = (2 * (_nbytes((tm, proj.shape[1]), F32) + _nbytes(w_uq.shape, w_uq.dtype)
                 + _nbytes((tm, qcols), MXU_DTYPE) + 3 * _nbytes((tm, kv_lora + V7X_LANES), F32))
            + 3 * _nbytes((tm, qcols), F32) + (8 << 20))
    if expand:
        kcols = heads * (nope + V7X_LANES)
        vcols = w_ukv.shape[1] - heads * nope
        in_specs.append(full(w_ukv))
        args.append(w_ukv)
        out_shape += [jax.ShapeDtypeStruct((n, kcols), MXU_DTYPE), jax.ShapeDtypeStruct((n, vcols), MXU_DTYPE)]
        out_specs += [row(kcols), row(vcols)]
        vmem += (2 * (_nbytes(w_ukv.shape, w_ukv.dtype) + _nbytes((tm, kcols + vcols), MXU_DTYPE))
                 + 2 * _nbytes((tm, w_ukv.shape[1]), F32))
    else:
        out_shape.append(jax.ShapeDtypeStruct((n, kv_lora + V7X_LANES), MXU_DTYPE))
        out_specs.append(row(kv_lora + V7X_LANES))
    return pl.pallas_call(
        functools.partial(_mla_prep_kernel, q_lora=q_lora, kv_lora=kv_lora, rope_dim=rope_dim,
                          rope_slabs=rope_slabs, qscale=qscale, heads=heads, nope=nope),
        out_shape=tuple(out_shape),
        grid=(n // tm,),
        in_specs=in_specs,
        out_specs=tuple(out_specs),
        compiler_params=_params(("parallel",), vmem),
        name="mla_prep",
    )(*args)


def _lanes(x, width):
    reps = width // V7X_LANES
    return x if reps == 1 else jnp.concatenate([x] * reps, axis=-1)


def _mha_attn_kernel(qi_ref, kj_ref, flag_ref, q_ref, k_ref, v_ref, o_ref, acc_ref, m_ref,
                     *, heads, hw, vd, tq, tk, q_offset, kv_len):
    step = pl.program_id(1)
    i = qi_ref[step]
    j = kj_ref[step]
    flags = flag_ref[step]
    aw = vd + V7X_LANES

    @pl.when(j == 0)
    def _():
        m_ref[...] = jnp.full_like(m_ref, NEG)
        acc_ref[...] = jnp.zeros_like(acc_ref)

    ones = jnp.ones((tk, V7X_LANES), MXU_DTYPE)

    def all_heads(bias):
        for h in range(heads):
            s = _dot_nt(q_ref[0, :, h * hw:(h + 1) * hw], k_ref[0, :, h * hw:(h + 1) * hw])
            if bias is not None:
                s = s + bias
            m_old = m_ref[h]
            m_new = jnp.maximum(m_old, jnp.max(s, axis=-1, keepdims=True))
            alpha = jnp.exp2(m_old - m_new)
            p = jnp.exp2(s - _lanes(m_new, tk)).astype(MXU_DTYPE)
            m_ref[h] = m_new
            v_ext = jnp.concatenate([v_ref[0, :, h * vd:(h + 1) * vd], ones], axis=-1)
            acs = slice(h * aw, (h + 1) * aw)
            acc_ref[:, acs] = _lanes(alpha, aw) * acc_ref[:, acs] + _dot(p, v_ext)

    @pl.when((flags & 2) == 0)
    def _():
        all_heads(None)

    @pl.when((flags & 2) != 0)
    def _():
        qpos = q_offset + i * tq + lax.broadcasted_iota(jnp.int32, (tq, tk), 0)
        kpos = j * tk + lax.broadcasted_iota(jnp.int32, (tq, tk), 1)
        visible = ((kpos // CHUNK) <= (qpos // CHUNK)) & (kpos < kv_len)
        all_heads(jnp.where(visible, 0.0, NEG))

    @pl.when((flags & 1) != 0)
    def _():
        for h in range(heads):
            num = acc_ref[:, h * aw:h * aw + vd]
            den = _lanes(acc_ref[:, h * aw + vd:(h + 1) * aw], vd)
            o_ref[0, :, h * vd:(h + 1) * vd] = (num / den).astype(o_ref.dtype)


def mha_attn(q, k, v, *, heads, tq, tk, q_offset, kv_len):
    b, t, _ = q.shape
    s = k.shape[1]
    hw = q.shape[2] // heads
    vd = v.shape[2] // heads
    nq, nk = t // tq, s // tk
    qi, kj, flags = [], [], []
    for i in range(nq):
        first_q = q_offset + i * tq
        last_key = ((first_q + tq - 1) // CHUNK + 1) * CHUNK - 1
        jl = min(last_key // tk, nk - 1)
        for j in range(jl + 1):
            fully_visible = ((j + 1) * tk - 1) // CHUNK <= first_q // CHUNK and (j + 1) * tk <= kv_len
            qi.append(i)
            kj.append(j)
            flags.append(int(j == jl) + 2 * int(not fully_visible))
    sched = [jnp.asarray(a, jnp.int32) for a in (qi, kj, flags)]
    aw = vd + V7X_LANES
    vmem = (2 * (2 * _nbytes((tq, heads * hw), q.dtype) + 2 * _nbytes((tk, heads * vd), v.dtype)
                 + _nbytes((tq, heads * vd), MXU_DTYPE))
            + _nbytes((tq, heads * aw), F32) + _nbytes((heads, tq, V7X_LANES), F32)
            + 6 * _nbytes((tq, tk), F32) + (8 << 20))
    grid_spec = pltpu.PrefetchScalarGridSpec(
        num_scalar_prefetch=3,
        grid=(b, len(qi)),
        in_specs=[pl.BlockSpec((1, tq, heads * hw), lambda bb, p, qi_r, kj_r, l_r: (bb, qi_r[p], 0)),
                  pl.BlockSpec((1, tk, heads * hw), lambda bb, p, qi_r, kj_r, l_r: (bb, kj_r[p], 0)),
                  pl.BlockSpec((1, tk, heads * vd), lambda bb, p, qi_r, kj_r, l_r: (bb, kj_r[p], 0))],
        out_specs=pl.BlockSpec((1, tq, heads * vd), lambda bb, p, qi_r, kj_r, l_r: (bb, qi_r[p], 0)),
        scratch_shapes=[pltpu.VMEM((tq, heads * aw), F32),
                        pltpu.VMEM((heads, tq, V7X_LANES), F32)])
    return pl.pallas_call(
        functools.partial(_mha_attn_kernel, heads=heads, hw=hw, vd=vd, tq=tq, tk=tk,
                          q_offset=q_offset, kv_len=kv_len),
        out_shape=jax.ShapeDtypeStruct((b, t, heads * vd), MXU_DTYPE),
        grid_spec=grid_spec,
        compiler_params=_params(("parallel", "arbitrary"), vmem),
        name="mha_attn",
    )(*sched, q, k, v)


def _last_kv_block(i, *, tq, tk, q_offset, nk):
    last_q = q_offset + (i + 1) * tq - 1
    last_key = (last_q // CHUNK + 1) * CHUNK - 1
    return jnp.minimum(last_key // tk, nk - 1)


def _mla_attn_kernel(q_ref, k_ref, wuk_ref, wuv_ref, o_ref, qs_ref, acc_ref, m_ref, l_ref,
                     *, heads, hb, tq, tk, nope, lat, q_offset, kv_len, scale):
    i = pl.program_id(1)
    j = pl.program_id(2)
    nk = pl.num_programs(2)
    j_last = _last_kv_block(i, tq=tq, tk=tk, q_offset=q_offset, nk=nk)
    rope_lo = heads * nope

    @pl.when(j == 0)
    def _():
        for h in range(heads):
            qn = q_ref[0, :, h * nope:(h + 1) * nope]
            qs_ref[h * tq:(h + 1) * tq, 0:lat] = (_dot(qn, wuk_ref[h]) * scale).astype(qs_ref.dtype)
            qr = q_ref[0, :, rope_lo + h * V7X_LANES:rope_lo + (h + 1) * V7X_LANES]
            qs_ref[h * tq:(h + 1) * tq, lat:lat + V7X_LANES] = (qr.astype(F32) * scale).astype(qs_ref.dtype)
        m_ref[...] = jnp.full_like(m_ref, NEG)
        l_ref[...] = jnp.zeros_like(l_ref)
        acc_ref[...] = jnp.zeros_like(acc_ref)

    @pl.when(j <= j_last)
    def _():
        k = k_ref[0]
        v = k[:, 0:lat]
        qpos = q_offset + i * tq + lax.broadcasted_iota(jnp.int32, (tq, tk), 0)
        kpos = j * tk + lax.broadcasted_iota(jnp.int32, (tq, tk), 1)
        visible = ((kpos // CHUNK) <= (qpos // CHUNK)) & (kpos < kv_len)
        bias = jnp.where(visible, 0.0, NEG)
        rows = hb * tq
        for rb in range(heads // hb):
            rs = slice(rb * rows, (rb + 1) * rows)
            s = _dot_nt(qs_ref[rs, :], k)
            s = (s.reshape(hb, tq, tk) + bias[None]).reshape(rows, tk)
            m_old = m_ref[rs, :]
            m_new = jnp.maximum(m_old, jnp.max(s, axis=-1, keepdims=True))
            alpha = jnp.exp(m_old - m_new)
            p = jnp.exp(s - _lanes(m_new, tk))
            l_ref[rs, :] = alpha * l_ref[rs, :] + jnp.sum(p, axis=-1, keepdims=True)
            m_ref[rs, :] = m_new
            acc_ref[rs, :] = _lanes(alpha, lat) * acc_ref[rs, :] + _dot(p.astype(MXU_DTYPE), v)

    @pl.when(j == j_last)
    def _():
        vd = wuv_ref.shape[2]
        for h in range(heads):
            hs = slice(h * tq, (h + 1) * tq)
            ol = acc_ref[hs, :] / _lanes(l_ref[hs, :], lat)
            o_ref[0, :, h * vd:(h + 1) * vd] = _dot(ol.astype(MXU_DTYPE), wuv_ref[h]).astype(o_ref.dtype)


def mla_attn(q, kcat, w_uk, w_uv, *, tq, tk, q_offset, kv_len, scale):
    b, t, _ = q.shape
    s = kcat.shape[1]
    heads, nope, lat = w_uk.shape
    vd = w_uv.shape[2]
    nq, nk = t // tq, s // tk
    rows = heads * tq
    hb = max(1, min(heads, MLA_ROW_BLOCK // tq))
    assert heads % hb == 0
    last = functools.partial(_last_kv_block, tq=tq, tk=tk, q_offset=q_offset, nk=nk)
    vmem = (2 * (_nbytes((tq, q.shape[2]), q.dtype) + _nbytes((tk, kcat.shape[2]), kcat.dtype)
                 + 2 * _nbytes(w_uk.shape, w_uk.dtype) + _nbytes((tq, heads * vd), MXU_DTYPE))
            + _nbytes((rows, lat + V7X_LANES), MXU_DTYPE) + _nbytes((rows, lat), F32)
            + 2 * _nbytes((rows, V7X_LANES), F32) + 4 * _nbytes((rows, tk), F32) + (8 << 20))
    return pl.pallas_call(
        functools.partial(_mla_attn_kernel, heads=heads, hb=hb, tq=tq, tk=tk, nope=nope, lat=lat,
                          q_offset=q_offset, kv_len=kv_len, scale=scale),
        out_shape=jax.ShapeDtypeStruct((b, t, heads * vd), MXU_DTYPE),
        grid=(b, nq, nk),
        in_specs=[pl.BlockSpec((1, tq, q.shape[2]), lambda bb, i, j: (bb, i, 0)),
                  pl.BlockSpec((1, tk, kcat.shape[2]), lambda bb, i, j: (bb, jnp.minimum(j, last(i)), 0)),
                  pl.BlockSpec(w_uk.shape, lambda bb, i, j: (0, 0, 0)),
                  pl.BlockSpec(w_uv.shape, lambda bb, i, j: (0, 0, 0))],
        out_specs=pl.BlockSpec((1, tq, heads * vd), lambda bb, i, j: (bb, i, 0)),
        scratch_shapes=[pltpu.VMEM((rows, lat + V7X_LANES), MXU_DTYPE),
                        pltpu.VMEM((rows, lat), F32),
                        pltpu.VMEM((rows, V7X_LANES), F32),
                        pltpu.VMEM((rows, V7X_LANES), F32)],
        compiler_params=_params(("parallel", "parallel", "arbitrary"), vmem),
        name="mla_attn",
    )(q, kcat, w_uk, w_uv)


def _ssd_kernel(*refs, L, groups, hpg, hdim, nstate, has_state):
    if has_state:
        (z_ref, xs_ref, b_ref, c_ref, dt_ref, cw_ref, cb_ref, dtb_ref, alog_ref, dsk_ref, nrm_ref,
         cst_ref, hst_ref, y_ref, ncv_ref, nst_ref, xbuf, h_scr, cum_scr, xdt_scr, yin_scr) = refs
    else:
        (z_ref, xs_ref, b_ref, c_ref, dt_ref, cw_ref, cb_ref, dtb_ref, alog_ref, dsk_ref, nrm_ref,
         cst_ref, y_ref, ncv_ref, nst_ref, xbuf, h_scr, cum_scr, xdt_scr, yin_scr) = refs
        hst_ref = None
    c = pl.program_id(1)
    nc = pl.num_programs(1)
    dx = groups * hpg * hdim
    dn = groups * nstate
    gw = hpg * hdim
    pad = CONV_PAD_ROWS

    @pl.when(c == 0)
    def _():
        xbuf[0:pad, :] = cst_ref[0]
        if has_state:
            h_scr[...] = hst_ref[0]
        else:
            h_scr[...] = jnp.zeros_like(h_scr)

    xbuf[pad:pad + L, 0:dx] = xs_ref[...]
    xbuf[pad:pad + L, dx:dx + dn] = b_ref[...]
    xbuf[pad:pad + L, dx + dn:dx + 2 * dn] = c_ref[...]
    acc = xbuf[pad - CONV_K + 1:pad - CONV_K + 1 + L, :] * cw_ref[0:1, :] + cb_ref[...]
    for jj in range(1, CONV_K):
        lo = pad - CONV_K + 1 + jj
        acc = acc + xbuf[lo:lo + L, :] * cw_ref[jj:jj + 1, :]
    xc = _silu(acc)
    tail = xbuf[L:L + pad, :]
    xbuf[0:pad, :] = tail

    @pl.when(c == nc - 1)
    def _():
        ncv_ref[0] = tail

    dtr = dt_ref[...] + dtb_ref[...]
    dt = jnp.maximum(dtr, 0.0) + jnp.log1p(jnp.exp(-jnp.abs(dtr)))
    la = dt * (-jnp.exp(alog_ref[...]))
    row = lax.broadcasted_iota(jnp.int32, (L, L), 0)
    col = lax.broadcasted_iota(jnp.int32, (L, L), 1)
    causal = row >= col
    tri = jnp.where(causal, 1.0, 0.0).astype(MXU_DTYPE)
    cum = sum(_dot(tri, piece) for piece in _split3(la))
    nh = groups * hpg
    eye = jnp.where(lax.broadcasted_iota(jnp.int32, (V7X_LANES, V7X_LANES), 0)
                    == lax.broadcasted_iota(jnp.int32, (V7X_LANES, V7X_LANES), 1), 1.0, 0.0).astype(MXU_DTYPE)
    cum_t = sum(_dot_nt(eye, piece) for piece in _split3(cum))

    half = lax.broadcasted_iota(jnp.int32, (L, V7X_LANES), 1) < hdim
    per_vreg = V7X_LANES // hdim
    for g in range(groups):
        bg = xc[:, dx + g * nstate:dx + (g + 1) * nstate]
        cg = xc[:, dx + dn + g * nstate:dx + dn + (g + 1) * nstate].astype(MXU_DTYPE)
        qk = _dot_nt(cg, bg.astype(MXU_DTYPE))
        for sl in range(gw // V7X_LANES):
            lane0 = g * gw + sl * V7X_LANES
            h0 = lane0 // hdim
            cb = [jnp.broadcast_to(cum[:, h0 + u:h0 + u + 1], (L, V7X_LANES)) for u in range(per_vreg)]
            db = [jnp.broadcast_to(dt[:, h0 + u:h0 + u + 1], (L, V7X_LANES)) for u in range(per_vreg)]
            cum_e = jnp.where(half, cb[0], cb[1])
            dt_e = jnp.where(half, db[0], db[1])
            xdt = xc[:, lane0:lane0 + V7X_LANES] * dt_e
            xdt_m = xdt.astype(MXU_DTYPE)
            ys = []
            for u in range(per_vreg):
                seg = cb[u][:, 0:L] - cum_t[h0 + u:h0 + u + 1, :]
                decay = jnp.exp(jnp.where(causal, seg, NEG))
                ys.append(_dot((qk * decay).astype(MXU_DTYPE), xdt_m))
            cum_scr[:, lane0:lane0 + V7X_LANES] = cum_e
            xdt_scr[:, lane0:lane0 + V7X_LANES] = xdt
            yin_scr[:, lane0:lane0 + V7X_LANES] = jnp.where(half, ys[0], ys[1])

    for g in range(groups):
        gs = slice(g * gw, (g + 1) * gw)
        bg = xc[:, dx + g * nstate:dx + (g + 1) * nstate].astype(MXU_DTYPE)
        cg = xc[:, dx + dn + g * nstate:dx + dn + (g + 1) * nstate].astype(MXU_DTYPE)
        cum_g = cum_scr[:, gs]
        last = cum_scr[L - 1:L, gs]
        hg = h_scr[g]
        y_inter = _dot(cg, hg.astype(MXU_DTYPE)) * jnp.exp(cum_g)
        wx = (jnp.exp(last - cum_g) * xdt_scr[:, gs]).astype(MXU_DTYPE)
        h_scr[g] = jnp.exp(last) * hg + _dot_tn(bg, wx)
        y = yin_scr[:, gs] + y_inter + dsk_ref[:, gs] * xc[:, gs]
        y = y * _silu(z_ref[:, gs].astype(F32))
        y_ref[:, gs] = (_rms(y) * nrm_ref[:, gs]).astype(y_ref.dtype)

    @pl.when(c == nc - 1)
    def _():
        nst_ref[0] = h_scr[...]


def ssd_scan(proj_g, proj, col, conv_w, conv_b, dt_bias, a_log, d_skip_e, ssd_norm, conv_state, ssd_state,
             *, batch, L, groups, hpg, hdim, nstate):
    n = proj.shape[0]
    t = n // batch
    nc = t // L
    dx = groups * hpg * hdim
    dn = groups * nstate
    cdim = dx + 2 * dn
    has_state = ssd_state is not None
    rowmap = lambda b, c: b * nc + c
    in_specs = [pl.BlockSpec((L, dx), lambda b, c: (rowmap(b, c), col["z"])),
                pl.BlockSpec((L, dx), lambda b, c: (rowmap(b, c), col["xs"])),
                pl.BlockSpec((L, dn), lambda b, c: (rowmap(b, c), col["B"])),
                pl.BlockSpec((L, dn), lambda b, c: (rowmap(b, c), col["C"])),
                pl.BlockSpec((L, V7X_LANES), lambda b, c: (rowmap(b, c), col["dt"])),
                pl.BlockSpec((CONV_K, cdim), lambda b, c: (0, 0)),
                pl.BlockSpec((1, cdim), lambda b, c: (0, 0)),
                pl.BlockSpec((1, V7X_LANES), lambda b, c: (0, 0)),
                pl.BlockSpec((1, V7X_LANES), lambda b, c: (0, 0)),
                pl.BlockSpec((1, dx), lambda b, c: (0, 0)),
                pl.BlockSpec((1, dx), lambda b, c: (0, 0)),
                pl.BlockSpec((1, CONV_PAD_ROWS, cdim), lambda b, c: (b, 0, 0))]
    args = [proj_g, proj, proj, proj, proj, conv_w, conv_b.reshape(1, cdim), dt_bias, a_log,
            d_skip_e, ssd_norm.reshape(1, dx), conv_state]
    if has_state:
        in_specs.append(pl.BlockSpec((1, groups, nstate, hpg * hdim), lambda b, c: (b, 0, 0, 0)))
        args.append(ssd_state)
    state_bytes = _nbytes((groups, nstate, hpg * hdim), F32)
    vmem = (2 * (2 * _nbytes((L, dx), F32) + 2 * _nbytes((L, dn), F32) + _nbytes((L, dx), MXU_DTYPE)
                 + 2 * _nbytes((CONV_PAD_ROWS, cdim), F32) + 2 * state_bytes)
            + state_bytes + 12 * _nbytes((L + CONV_PAD_ROWS, cdim), F32) + (8 << 20))
    return pl.pallas_call(
        functools.partial(_ssd_kernel, L=L, groups=groups, hpg=hpg, hdim=hdim, nstate=nstate,
                          has_state=has_state),
        out_shape=(jax.ShapeDtypeStruct((n, dx), MXU_DTYPE),
                   jax.ShapeDtypeStruct((batch, CONV_PAD_ROWS, cdim), F32),
                   jax.ShapeDtypeStruct((batch, groups, nstate, hpg * hdim), F32)),
        grid=(batch, nc),
        in_specs=in_specs,
        out_specs=(pl.BlockSpec((L, dx), lambda b, c: (rowmap(b, c), 0)),
                   pl.BlockSpec((1, CONV_PAD_ROWS, cdim), lambda b, c: (b, 0, 0)),
                   pl.BlockSpec((1, groups, nstate, hpg * hdim), lambda b, c: (b, 0, 0, 0))),
        scratch_shapes=[pltpu.VMEM((L + CONV_PAD_ROWS, cdim), F32),
                        pltpu.VMEM((groups, nstate, hpg * hdim), F32),
                        pltpu.VMEM((L, dx), F32),
                        pltpu.VMEM((L, dx), F32),
                        pltpu.VMEM((L, dx), F32)],
        compiler_params=_params(("parallel", "arbitrary"), vmem),
        name="ssd_scan",
    )(*args)


def _ret_kernel(*refs, L, heads, hd, has_state):
    if has_state:
        (q_ref, k_ref, v_ref, gate_ref, cos_ref, sin_ref, dec_ref, ecum_ref, wv_ref, sdec_ref, st_ref,
         o_ref, nst_ref, s_scr) = refs
    else:
        (q_ref, k_ref, v_ref, gate_ref, cos_ref, sin_ref, dec_ref, ecum_ref, wv_ref, sdec_ref,
         o_ref, nst_ref, s_scr) = refs
        st_ref = None
    c = pl.program_id(1)
    nc = pl.num_programs(1)
    half = hd // 2

    @pl.when(c == 0)
    def _():
        if has_state:
            s_scr[...] = st_ref[0]
        else:
            s_scr[...] = jnp.zeros_like(s_scr)

    cos = cos_ref[...]
    sin = sin_ref[...]

    def rope(ref, h, mult):
        x1 = ref[:, h * hd:h * hd + half].astype(F32)
        x2 = ref[:, h * hd + half:(h + 1) * hd].astype(F32)
        out = jnp.concatenate([x1 * cos - x2 * sin, x2 * cos + x1 * sin], axis=-1)
        return (out * mult).astype(MXU_DTYPE) if mult != 1.0 else out.astype(MXU_DTYPE)

    for h in range(heads):
        hs = slice(h * hd, (h + 1) * hd)
        qr = rope(q_ref, h, 1.0)
        kr = rope(k_ref, h, hd ** -0.5)
        vf = v_ref[:, hs]
        qk = _dot_nt(qr, kr)
        y = _dot((qk * dec_ref[h]).astype(MXU_DTYPE), vf.astype(MXU_DTYPE))
        s_old = s_scr[h]
        y = y + _dot(qr, s_old.astype(MXU_DTYPE)) * ecum_ref[h]
        s_scr[h] = sdec_ref[h] * s_old + _dot_tn(kr, (wv_ref[h] * vf.astype(F32)).astype(MXU_DTYPE))
        o_ref[:, hs] = (_rms(y) * _silu(gate_ref[:, hs].astype(F32))).astype(o_ref.dtype)

    @pl.when(c == nc - 1)
    def _():
        nst_ref[0] = s_scr[...]


def ret_scan(proj, col, cos_rows, sin_rows, ret_state, *, batch, L, heads, hd):
    n = proj.shape[0]
    t = n // batch
    nc = t // L
    inner = heads * hd
    has_state = ret_state is not None
    lg = jnp.log1p(-jnp.exp2(-5.0 - jnp.arange(heads, dtype=F32)))[:, None, None]
    li = jnp.arange(L, dtype=F32)
    diff = li[:, None] - li[None, :]
    dec = jnp.where(diff >= 0, jnp.exp(jnp.where(diff >= 0, diff, 0.0)[None] * lg), 0.0)
    ecum = jnp.broadcast_to(jnp.exp((li[None, :, None] + 1.0) * lg), (heads, L, hd))
    wv = jnp.broadcast_to(jnp.exp((L - 1.0 - li[None, :, None]) * lg), (heads, L, hd))
    sdec = jnp.broadcast_to(jnp.exp(L * lg), (heads, 1, hd))
    rowmap = lambda b, c: b * nc + c
    in_specs = [pl.BlockSpec((L, inner), lambda b, c: (rowmap(b, c), col["q"])),
                pl.BlockSpec((L, inner), lambda b, c: (rowmap(b, c), col["k"])),
                pl.BlockSpec((L, inner), lambda b, c: (rowmap(b, c), col["v"])),
                pl.BlockSpec((L, inner), lambda b, c: (rowmap(b, c), col["gate"])),
                pl.BlockSpec((L, hd // 2), lambda b, c: (rowmap(b, c), 0)),
                pl.BlockSpec((L, hd // 2), lambda b, c: (rowmap(b, c), 0)),
                pl.BlockSpec((heads, L, L), lambda b, c: (0, 0, 0)),
                pl.BlockSpec((heads, L, hd), lambda b, c: (0, 0, 0)),
                pl.BlockSpec((heads, L, hd), lambda b, c: (0, 0, 0)),
                pl.BlockSpec((heads, 1, hd), lambda b, c: (0, 0, 0))]
    args = [proj, proj, proj, proj, cos_rows, sin_rows, dec, ecum, wv, sdec]
    if has_state:
        in_specs.append(pl.BlockSpec((1, heads, hd, hd), lambda b, c: (b, 0, 0, 0)))
        args.append(ret_state)
    state_bytes = _nbytes((heads, hd, hd), F32)
    vmem = (2 * (4 * _nbytes((L, inner), F32) + _nbytes((L, inner), MXU_DTYPE) + _nbytes((heads, L, L), F32)
                 + 2 * _nbytes((heads, L, hd), F32) + 2 * state_bytes)
            + state_bytes + 16 * _nbytes((L, hd), F32) + (8 << 20))
    return pl.pallas_call(
        functools.partial(_ret_kernel, L=L, heads=heads, hd=hd, has_state=has_state),
        out_shape=(jax.ShapeDtypeStruct((n, inner), MXU_DTYPE),
                   jax.ShapeDtypeStruct((batch, heads, hd, hd), F32)),
        grid=(batch, nc),
        in_specs=in_specs,
        out_specs=(pl.BlockSpec((L, inner), lambda b, c: (rowmap(b, c), 0)),
                   pl.BlockSpec((1, heads, hd, hd), lambda b, c: (b, 0, 0, 0))),
        scratch_shapes=[pltpu.VMEM((heads, hd, hd), F32)],
        compiler_params=_params(("parallel", "arbitrary"), vmem),
        name="ret_scan",
    )(*args)


def _rope_tables(pos, half, reps, batch):
    inv = ROPE_BASE ** (-jnp.arange(half, dtype=F32) / half)
    ang = pos.astype(F32)[:, None] * inv[None, :]
    cos, sin = jnp.cos(ang), jnp.sin(ang)
    if reps == 0:
        return jnp.tile(cos, (batch, 1)), jnp.tile(sin, (batch, 1))
    c = jnp.tile(jnp.concatenate([cos, cos], axis=-1), (batch, reps))
    s = jnp.tile(jnp.concatenate([-sin, sin], axis=-1), (batch, reps))
    return c, s


def _mixer_even(x, batch, pos, norm_g, prm, conv_state, ssd_state, ret_state, dims, L_ssd, L_ret):
    d = x.shape[1]
    groups, hpg, hdim, nstate, rheads, rhd = dims
    proj_g = rms_matmul(x, norm_g, prm["w_in_g"], out_dtype=MXU_DTYPE)
    proj_s = rms_matmul(x, norm_g, prm["w_in_s"])
    col = prm["col"]
    y, new_conv, new_ssd = ssd_scan(proj_g, proj_s, col, prm["conv_w"], prm["conv_b"], prm["dt_bias"],
                                    prm["a_log"], prm["d_skip_e"], prm["ssd_norm"], conv_state, ssd_state,
                                    batch=batch, L=L_ssd, groups=groups, hpg=hpg, hdim=hdim, nstate=nstate)
    cos_rows, sin_rows = _rope_tables(pos, rhd // 2, 0, batch)
    o, new_ret = ret_scan(proj_g, col, cos_rows, sin_rows, ret_state, batch=batch, L=L_ret, heads=rheads, hd=rhd)
    x = matmul_residual(x, [y, o], [prm["w_out_ssd"], prm["w_out_ret"]])
    return x, (new_conv, new_ssd, new_ret)


def _prep_even(w_in, conv_w, conv_b, dt_bias, a_log, d_skip, ssd_norm, w_out, dims):
    groups, hpg, hdim, nstate, rheads, rhd = dims
    d = w_in.shape[0]
    dx = groups * hpg * hdim
    dn = groups * nstate
    nh = groups * hpg
    ri = rheads * rhd
    o_z, o_xs, o_b, o_c, o_dt = 0, dx, 2 * dx, 2 * dx + dn, 2 * dx + 2 * dn
    o_q = o_dt + nh
    seg = lambda lo, w: w_in[:, lo:lo + w]
    dt_pad = jnp.zeros((d, V7X_LANES - nh), w_in.dtype)
    w_g = jnp.concatenate([seg(o_z, dx), seg(o_q, ri), seg(o_q + ri, ri), seg(o_q + 2 * ri, ri),
                           seg(o_q + 3 * ri, ri)], axis=1).astype(MXU_DTYPE)
    w_s = jnp.concatenate([seg(o_xs, dx), seg(o_b, dn), seg(o_c, dn), seg(o_dt, nh), dt_pad],
                          axis=1).astype(MXU_DTYPE)
    assert dx == ri and dx % dn == 0 and dn % V7X_LANES == 0
    col = {"z": 0, "q": 1, "k": 2, "v": 3, "gate": 4,
           "xs": 0, "B": dx // dn, "C": dx // dn + 1, "dt": (dx + 2 * dn) // V7X_LANES}
    pad1 = lambda v: jnp.pad(v.astype(F32), (0, V7X_LANES - nh)).reshape(1, V7X_LANES)
    return {"w_in_g": w_g, "w_in_s": w_s, "col": col, "conv_w": conv_w, "conv_b": conv_b,
            "dt_bias": pad1(dt_bias), "a_log": pad1(a_log),
            "d_skip_e": jnp.repeat(d_skip, hdim).reshape(1, dx), "ssd_norm": ssd_norm,
            "w_out_ssd": w_out[:dx].astype(MXU_DTYPE), "w_out_ret": w_out[dx:].astype(MXU_DTYPE)}


def _prep_odd(w_in, q_norm, kv_norm, w_uq, w_uk, w_uv, w_out, rope_dim):
    d = w_in.shape[0]
    q_lora = q_norm.shape[0]
    kv_lora, heads, nope = w_uk.shape
    kp = w_in[:, q_lora + kv_lora:]
    w_in_new = jnp.concatenate([w_in[:, :q_lora + kv_lora], kp, kp], axis=1).astype(MXU_DTYPE)
    assert 2 * rope_dim == V7X_LANES
    wq = w_uq.reshape(q_lora, heads, nope + rope_dim)
    wq_nope = wq[:, :, :nope]
    wq_rope = jnp.pad(wq[:, :, nope:], ((0, 0), (0, 0), (0, V7X_LANES - rope_dim)))
    hw = nope + V7X_LANES
    return {"w_in": w_in_new, "q_norm": q_norm, "kv_norm": kv_norm,
            "w_uq_lat": jnp.concatenate([wq_nope.reshape(q_lora, heads * nope),
                                         wq_rope.reshape(q_lora, heads * V7X_LANES)], axis=1).astype(MXU_DTYPE),
            "slabs_lat": tuple(heads * nope + h * V7X_LANES for h in range(heads)),
            "w_uq_head": jnp.concatenate([wq_nope, wq_rope], axis=2).reshape(q_lora, heads * hw).astype(MXU_DTYPE),
            "slabs_head": tuple(h * hw + nope for h in range(heads)),
            "w_ukv": jnp.concatenate([w_uk.reshape(kv_lora, heads * nope),
                                      w_uv.reshape(kv_lora, -1)], axis=1).astype(MXU_DTYPE),
            "w_uk": jnp.transpose(w_uk, (1, 2, 0)).astype(MXU_DTYPE),
            "w_uv": jnp.transpose(w_uv, (1, 0, 2)).astype(MXU_DTYPE),
            "w_out": w_out.astype(MXU_DTYPE)}


def _mixer_odd(x, batch, pos, norm_g, prm, past_kcat, rope_dim, q_offset):
    n, d = x.shape
    t = n // batch
    q_lora = prm["q_norm"].shape[0]
    kv_lora = prm["kv_norm"].shape[0]
    heads, nope, lat = prm["w_uk"].shape
    proj = rms_matmul(x, norm_g, prm["w_in"])
    cos_rows, sin_rows = _rope_tables(pos, rope_dim // 2, V7X_LANES // rope_dim, batch)
    scale = (nope + rope_dim) ** -0.5
    dims = dict(q_lora=q_lora, kv_lora=kv_lora, rope_dim=rope_dim, heads=heads, nope=nope)
    if past_kcat is None:
        q, ckv, kpe, k, v = mla_prep(proj, prm["q_norm"], prm["kv_norm"], prm["w_uq_head"], cos_rows, sin_rows,
                                     prm["w_ukv"], rope_slabs=prm["slabs_head"], qscale=scale * math.log2(math.e),
                                     **dims)
        tq = tk = _pick(t, (512, 256, 128, 64))
        shp = lambda a: a.reshape(batch, t, a.shape[1])
        o = mha_attn(shp(q), shp(k), shp(v), heads=heads, tq=tq, tk=tk, q_offset=q_offset, kv_len=t)
    else:
        q, ckv, kpe, kcat = mla_prep(proj, prm["q_norm"], prm["kv_norm"], prm["w_uq_lat"], cos_rows, sin_rows,
                                     None, rope_slabs=prm["slabs_lat"], qscale=1.0, **dims)
        q = q.reshape(batch, t, q.shape[1])
        kcat = kcat.reshape(batch, t, kcat.shape[1])
        kv_len = past_kcat.shape[1] + t
        padded = -(-kv_len // V7X_LANES) * V7X_LANES
        keys = jnp.concatenate([past_kcat, kcat, jnp.zeros((batch, padded - kv_len, kcat.shape[2]), kcat.dtype)], axis=1)
        o = mla_attn(q, keys, prm["w_uk"], prm["w_uv"], tq=t, tk=padded, q_offset=q_offset, kv_len=kv_len,
                     scale=scale)
    x = matmul_residual(x, [o.reshape(n, o.shape[2])], [prm["w_out"]])
    return x, (ckv.reshape(batch, t, kv_lora), kpe.reshape(batch, t, rope_dim))


def kernel(x_prompt, x_sample, mem_prompt, state_conv, state_ssd, state_ret, cache_ckv, cache_kpe,
           cache_mem_k, cache_mem_v, norms, ffn_w1, ffn_w2, mem_norm, w_mq, w_mkv, w_mo,
           ab_w_in, ab_conv_w, ab_conv_b, ab_dt_bias, ab_a_log, ab_d_skip, ab_ssd_norm, ab_w_out,
           c_w_in, c_q_norm, c_kv_norm, c_w_uq, c_w_uk, c_w_uv, c_w_out, final_norm):
    bp, tp, d = x_prompt.shape
    bs, ts, _ = x_sample.shape
    depth = norms.shape[0]
    assert depth >= 1
    past_len = cache_ckv.shape[2]
    mem_tokens = mem_prompt.shape[1]
    mem_heads, mem_hd = cache_mem_k.shape[3], cache_mem_k.shape[4]
    mem_inner = mem_heads * mem_hd
    ssd_heads, nstate, hdim = state_ssd.shape[2], state_ssd.shape[3], state_ssd.shape[4]
    cdim = state_conv.shape[3]
    groups = (cdim - ssd_heads * hdim) // (2 * nstate)
    hpg = ssd_heads // groups
    rheads, rhd = state_ret.shape[2], state_ret.shape[3]
    dims = (groups, hpg, hdim, nstate, rheads, rhd)
    rope_dim = cache_kpe.shape[3]

    pos_p = jnp.arange(tp)
    pos_s = past_len + jnp.arange(ts)
    xp = x_prompt.reshape(bp * tp, d)
    xs = x_sample.reshape(bs * ts, d)
    L_ssd_p, L_ret_p = _pick(tp, (128, 64)), _pick(tp, (256, 128, 64))
    L_s = _pick(ts, (128, 64))

    outs = {k: [] for k in ("conv_p", "ssd_p", "ret_p", "ckv_p", "kpe_p", "memk_p", "memv_p",
                            "conv_s", "ssd_s", "ret_s", "ckv_s", "kpe_s")}

    def to_group_layout(st):
        b = st.shape[0]
        return st.reshape(b, groups, hpg, nstate, hdim).transpose(0, 1, 3, 2, 4).reshape(b, groups, nstate, hpg * hdim)

    def from_group_layout(st):
        b = st.shape[0]
        return st.reshape(b, groups, nstate, hpg, hdim).transpose(0, 1, 3, 2, 4).reshape(b, ssd_heads, nstate, hdim)

    w1 = ffn_w1.astype(MXU_DTYPE)
    w2 = ffn_w2.astype(MXU_DTYPE)
    for i in range(depth):
        j = i // 2
        closing = final_norm if i == depth - 1 else None
        wq_m = w_mq[i].astype(MXU_DTYPE)
        wo_m = w_mo[i].astype(MXU_DTYPE)
        mkv = rms_matmul(mem_prompt.reshape(bp * mem_tokens, d), mem_norm[i], w_mkv[i].astype(MXU_DTYPE))
        mk_p = mkv[:, :mem_inner].reshape(bp, mem_tokens, mem_inner)
        mv_p = mkv[:, mem_inner:].reshape(bp, mem_tokens, mem_inner)
        outs["memk_p"].append(mk_p.reshape(bp, mem_tokens, mem_heads, mem_hd))
        outs["memv_p"].append(mv_p.reshape(bp, mem_tokens, mem_heads, mem_hd))

        xp = ffn(xp, norms[i, 0], w1, w2, i, 0)
        xs = ffn(xs, norms[i, 0], w1, w2, i, 0)
        if i % 2 == 0:
            prm = _prep_even(ab_w_in[j], ab_conv_w[j], ab_conv_b[j], ab_dt_bias[j], ab_a_log[j],
                             ab_d_skip[j], ab_ssd_norm[j], ab_w_out[j], dims)
            zero_conv = jnp.zeros((bp, CONV_PAD_ROWS, cdim), F32)
            xp, st_p = _mixer_even(xp, bp, pos_p, norms[i, 1], prm, zero_conv, None, None, dims, L_ssd_p, L_ret_p)
            conv_in = jnp.pad(state_conv[j], ((0, 0), (CONV_PAD_ROWS - (CONV_K - 1), 0), (0, 0)))
            xs, st_s = _mixer_even(xs, bs, pos_s, norms[i, 1], prm, conv_in, to_group_layout(state_ssd[j]),
                                   state_ret[j], dims, L_s, L_s)
            for tag, st in (("p", st_p), ("s", st_s)):
                outs["conv_" + tag].append(st[0][:, CONV_PAD_ROWS - (CONV_K - 1):, :])
                outs["ssd_" + tag].append(from_group_layout(st[1]))
                outs["ret_" + tag].append(st[2])
        else:
            prm = _prep_odd(c_w_in[j], c_q_norm[j], c_kv_norm[j], c_w_uq[j], c_w_uk[j], c_w_uv[j], c_w_out[j],
                            rope_dim)
            xp, st_p = _mixer_odd(xp, bp, pos_p, norms[i, 1], prm, None, rope_dim, 0)
            past = jnp.concatenate([cache_ckv[j], cache_kpe[j], cache_kpe[j]], axis=-1).astype(MXU_DTYPE)
            xs, st_s = _mixer_odd(xs, bs, pos_s, norms[i, 1], prm, past, rope_dim, past_len)
            for tag, st in (("p", st_p), ("s", st_s)):
                outs["ckv_" + tag].append(st[0])
                outs["kpe_" + tag].append(st[1])
        xp = mem_attn(xp, norms[i, 2], wq_m, mk_p, mv_p, wo_m, batch=bp, heads=mem_heads)
        xs = mem_attn(xs, norms[i, 2], wq_m, cache_mem_k[i].reshape(bs, mem_tokens, mem_inner),
                      cache_mem_v[i].reshape(bs, mem_tokens, mem_inner), wo_m, batch=bs, heads=mem_heads)
        xp = ffn(xp, norms[i, 3], w1, w2, i, 1, final_g=closing)
        xs = ffn(xs, norms[i, 3], w1, w2, i, 1, final_g=closing)

    y_prompt = xp.reshape(bp, tp, d)
    y_sample = xs.reshape(bs, ts, d)
    st = lambda k: jnp.stack(outs[k])
    return (y_prompt, y_sample, st("conv_p"), st("ssd_p"), st("ret_p"), st("ckv_p"), st("kpe_p"),
            st("memk_p"), st("memv_p"), st("conv_s"), st("ssd_s"), st("ret_s"), st("ckv_s"), st("kpe_s"))
```

```python
import functools
import math

import jax
import jax.numpy as jnp
from jax import lax
from jax.experimental import pallas as pl
from jax.experimental.pallas import tpu as pltpu

F32 = jnp.float32
MXU_DTYPE = jnp.bfloat16

EPS = 1e-6
CHUNK = 64
ROPE_BASE = 10000.0
CONV_K = 4
NEG = -1e30

V7X_VMEM_BYTES = 64 * 1024 * 1024
TILE_VMEM_BUDGET = 48 * 1024 * 1024
V7X_LANES = 128
CONV_PAD_ROWS = 8
MLA_ROW_BLOCK = 512


def _params(semantics, vmem_bytes):
    limit = min(int(vmem_bytes), V7X_VMEM_BYTES - (4 << 20))
    return pltpu.CompilerParams(dimension_semantics=semantics, vmem_limit_bytes=limit)


def _nbytes(shape, dtype):
    return math.prod(shape) * jnp.dtype(dtype).itemsize


def _pick(n, prefs):
    for p in prefs:
        if n % p == 0:
            return p
    return n


def _dot(a, b):
    return jnp.dot(a, b, preferred_element_type=F32)


def _dot_nt(a, b):
    return lax.dot_general(a, b, (((1,), (1,)), ((), ())), preferred_element_type=F32)


def _dot_tn(a, b):
    return lax.dot_general(a, b, (((0,), (0,)), ((), ())), preferred_element_type=F32)


def _split3(x):
    hi = x.astype(MXU_DTYPE)
    r = x - hi.astype(F32)
    mid = r.astype(MXU_DTYPE)
    lo = (r - mid.astype(F32)).astype(MXU_DTYPE)
    return hi, mid, lo


def _rms(xf, g=None):
    y = xf * lax.rsqrt(jnp.mean(xf * xf, axis=-1, keepdims=True) + EPS)
    return y if g is None else y * g


def _silu(a):
    return a * (1.0 / (1.0 + jnp.exp(-a)))


def _swap32(x):
    w = x.shape[-1]
    lane = lax.broadcasted_iota(jnp.int32, x.shape, x.ndim - 1)
    fwd = pltpu.roll(x, w - 32, x.ndim - 1)
    bwd = pltpu.roll(x, 32, x.ndim - 1)
    return jnp.where((lane & 63) < 32, fwd, bwd)


def _rms_matmul_kernel(*refs, side):
    if side:
        x_ref, g_ref, w_ref, ws_ref, o_ref, os_ref, xn_ref = refs
    else:
        x_ref, g_ref, w_ref, o_ref, xn_ref = refs

    @pl.when(pl.program_id(1) == 0)
    def _():
        xn_ref[...] = _rms(x_ref[...], g_ref[...]).astype(xn_ref.dtype)
        if side:
            os_ref[...] = _dot(xn_ref[...], ws_ref[...])

    o_ref[...] = _dot(xn_ref[...], w_ref[...]).astype(o_ref.dtype)


def rms_matmul(x, g, w, *, out_dtype=F32, w_side=None):
    n, d = x.shape
    nout = w.shape[1]
    side = w_side is not None
    tn = _pick(nout, (1920, 1152, 1024, 512, 256, 128))

    def vmem_for(tm):
        return (2 * (_nbytes((tm, d), F32) + _nbytes((d, tn), w.dtype) + _nbytes((tm, tn), out_dtype))
                + _nbytes((tm, d), MXU_DTYPE) + _nbytes((tm, tn), F32) + (8 << 20))

    tm = next(t for t in (1024, 512, 256, 128, n) if n % t == 0 and (vmem_for(t) <= TILE_VMEM_BUDGET or t <= 128))
    vmem = vmem_for(tm)
    in_specs = [pl.BlockSpec((tm, d), lambda i, j: (i, 0)),
                pl.BlockSpec((1, d), lambda i, j: (0, 0)),
                pl.BlockSpec((d, tn), lambda i, j: (0, j))]
    args = [x, g.reshape(1, d), w]
    out_shape = jax.ShapeDtypeStruct((n, nout), out_dtype)
    out_specs = pl.BlockSpec((tm, tn), lambda i, j: (i, j))
    if side:
        ns = w_side.shape[1]
        in_specs.append(pl.BlockSpec((d, ns), lambda i, j: (0, 0)))
        args.append(w_side)
        out_shape = (out_shape, jax.ShapeDtypeStruct((n, ns), F32))
        out_specs = (out_specs, pl.BlockSpec((tm, ns), lambda i, j: (i, 0)))
        vmem += 2 * (_nbytes((d, ns), w_side.dtype) + _nbytes((tm, ns), F32))
    return pl.pallas_call(
        functools.partial(_rms_matmul_kernel, side=side),
        out_shape=out_shape,
        grid=(n // tm, nout // tn),
        in_specs=in_specs,
        out_specs=out_specs,
        scratch_shapes=[pltpu.VMEM((tm, d), MXU_DTYPE)],
        compiler_params=_params(("parallel", "arbitrary"), vmem),
        name="rms_matmul",
    )(*args)


def _ffn_kernel(*refs, final_norm):
    if final_norm:
        x_ref, g_ref, w1a_ref, w1b_ref, w2_ref, gf_ref, o_ref, xn_ref = refs
    else:
        x_ref, g_ref, w1a_ref, w1b_ref, w2_ref, o_ref, xn_ref = refs
    f = pl.program_id(1)

    @pl.when(f == 0)
    def _():
        xf = x_ref[...]
        xn_ref[...] = _rms(xf, g_ref[...]).astype(xn_ref.dtype)
        o_ref[...] = xf

    xn = xn_ref[...]
    a = _dot(xn, w1a_ref[...])
    b = _dot(xn, w1b_ref[...])
    h = (_silu(a) * b).astype(MXU_DTYPE)
    o_ref[...] += _dot(h, w2_ref[...])

    if final_norm:
        @pl.when(f == pl.num_programs(1) - 1)
        def _():
            o_ref[...] = _rms(o_ref[...], gf_ref[...])


def ffn(x, g, w1, w2, layer, which, final_g=None):
    n, d = x.shape
    dff = w2.shape[2]
    tm = _pick(n, (512, 256, 128))
    tf = _pick(dff, (512, 256, 128))
    nf = dff // tf
    vmem = (2 * (2 * _nbytes((tm, d), F32) + 3 * _nbytes((d, tf), w1.dtype))
            + _nbytes((tm, d), MXU_DTYPE) + _nbytes((tm, d), F32) + 4 * _nbytes((tm, tf), F32) + (8 << 20))
    in_specs = [pl.BlockSpec((tm, d), lambda i, f: (i, 0)),
                pl.BlockSpec((1, d), lambda i, f: (0, 0)),
                pl.BlockSpec((None, None, d, tf), lambda i, f: (layer, which, 0, f)),
                pl.BlockSpec((None, None, d, tf), lambda i, f: (layer, which, 0, f + nf)),
                pl.BlockSpec((None, None, tf, d), lambda i, f: (layer, which, f, 0))]
    args = [x, g.reshape(1, d), w1, w1, w2]
    if final_g is not None:
        in_specs.append(pl.BlockSpec((1, d), lambda i, f: (0, 0)))
        args.append(final_g.reshape(1, d))
    return pl.pallas_call(
        functools.partial(_ffn_kernel, final_norm=final_g is not None),
        out_shape=jax.ShapeDtypeStruct((n, d), F32),
        grid=(n // tm, nf),
        in_specs=in_specs,
        out_specs=pl.BlockSpec((tm, d), lambda i, f: (i, 0)),
        scratch_shapes=[pltpu.VMEM((tm, d), MXU_DTYPE)],
        compiler_params=_params(("parallel", "arbitrary"), vmem),
        name="ffn",
    )(*args)


def _matmul_residual_kernel(*refs, n_in):
    x_ref = refs[0]
    o_ref = refs[1 + 2 * n_in]
    acc = x_ref[...]
    for h_ref, w_ref in zip(refs[1:1 + n_in], refs[1 + n_in:1 + 2 * n_in]):
        acc = acc + _dot(h_ref[...], w_ref[...])
    o_ref[...] = acc


def matmul_residual(x, hs, ws):
    n, d = x.shape

    def vmem_for(tm, tn):
        return (2 * (2 * _nbytes((tm, tn), F32) + sum(_nbytes((tm, h.shape[1]), h.dtype) for h in hs)
                     + sum(_nbytes((w.shape[0], tn), w.dtype) for w in ws))
                + _nbytes((tm, tn), F32) + (8 << 20))

    tm, tn = next((a, b) for a, b in ((512, d), (1024, 1024), (512, 1024), (256, 512), (128, 128), (n, d))
                  if n % a == 0 and d % b == 0 and (vmem_for(a, b) <= TILE_VMEM_BUDGET or a <= 128))
    vmem = vmem_for(tm, tn)
    in_specs = [pl.BlockSpec((tm, tn), lambda i, j: (i, j))]
    for h in hs:
        in_specs.append(pl.BlockSpec((tm, h.shape[1]), lambda i, j: (i, 0)))
    for w in ws:
        in_specs.append(pl.BlockSpec((w.shape[0], tn), lambda i, j: (0, j)))
    return pl.pallas_call(
        functools.partial(_matmul_residual_kernel, n_in=len(hs)),
        out_shape=jax.ShapeDtypeStruct((n, d), F32),
        grid=(n // tm, d // tn),
        in_specs=in_specs,
        out_specs=pl.BlockSpec((tm, tn), lambda i, j: (i, j)),
        compiler_params=_params(("parallel", "arbitrary"), vmem),
        name="matmul_residual",
    )(x, *hs, *ws)


def _mem_attn_kernel(x_ref, g_ref, wq_ref, k_ref, v_ref, wo_ref, o_ref, att_ref, *, bt, tq, heads, hd):
    xf = x_ref[...]
    xn = _rms(xf, g_ref[...]).astype(MXU_DTYPE)
    q = _dot(xn, wq_ref[...]).astype(MXU_DTYPE)
    scale = hd ** -0.5
    for b in range(bt):
        for h in range(heads):
            qh = q[b * tq:(b + 1) * tq, h * hd:(h + 1) * hd]
            kh = k_ref[b, :, h * hd:(h + 1) * hd].astype(MXU_DTYPE)
            vh = v_ref[b, :, h * hd:(h + 1) * hd].astype(MXU_DTYPE)
            s = _dot_nt(qh, kh) * scale
            m = jnp.max(s, axis=-1, keepdims=True)
            p = jnp.exp(s - m)
            l = jnp.sum(p, axis=-1, keepdims=True)
            oh = _dot(p.astype(MXU_DTYPE), vh) / l
            att_ref[b * tq:(b + 1) * tq, h * hd:(h + 1) * hd] = oh.astype(att_ref.dtype)
    o_ref[...] = xf + _dot(att_ref[...], wo_ref[...])


def mem_attn(x, g, wq, mem_k, mem_v, wo, *, batch, heads):
    n, d = x.shape
    t = n // batch
    m, inner = mem_k.shape[1], mem_k.shape[2]
    hd = inner // heads
    if t >= 128:
        bt, tq = 1, _pick(t, (512, 256, 128))
    else:
        bt, tq = _pick(batch, (8, 4, 2, 1)), t
    nt = t // tq
    rows = bt * tq
    vmem = (2 * (2 * _nbytes((rows, d), F32) + 2 * _nbytes((bt, m, inner), mem_k.dtype)
                 + 2 * _nbytes((d, inner), wq.dtype))
            + 2 * _nbytes((rows, d), F32) + (8 << 20))
    return pl.pallas_call(
        functools.partial(_mem_attn_kernel, bt=bt, tq=tq, heads=heads, hd=hd),
        out_shape=jax.ShapeDtypeStruct((n, d), F32),
        grid=(batch // bt, nt),
        in_specs=[pl.BlockSpec((rows, d), lambda b, i: (b * nt + i, 0)),
                  pl.BlockSpec((1, d), lambda b, i: (0, 0)),
                  pl.BlockSpec((d, inner), lambda b, i: (0, 0)),
                  pl.BlockSpec((bt, m, inner), lambda b, i: (b, 0, 0)),
                  pl.BlockSpec((bt, m, inner), lambda b, i: (b, 0, 0)),
                  pl.BlockSpec((inner, d), lambda b, i: (0, 0))],
        out_specs=pl.BlockSpec((rows, d), lambda b, i: (b * nt + i, 0)),
        scratch_shapes=[pltpu.VMEM((rows, inner), MXU_DTYPE)],
        compiler_params=_params(("parallel", "arbitrary"), vmem),
        name="mem_attn",
    )(x, g.reshape(1, d), wq, mem_k, mem_v, wo)


def _mla_prep_kernel(*refs, q_lora, kv_lora, rope_dim, rope_slabs, qscale, heads, nope):
    expand = len(refs) == 12
    if expand:
        (p_ref, qn_ref, kvn_ref, wuq_ref, cos_ref, sin_ref, wukv_ref,
         q_ref, ckv_ref, kpe_ref, k_ref, v_ref) = refs
    else:
        p_ref, qn_ref, kvn_ref, wuq_ref, cos_ref, sin_ref, q_ref, ckv_ref, kpe_ref, k_ref = refs
    cos = cos_ref[...]
    sin = sin_ref[...]
    cqn = _rms(p_ref[:, 0:q_lora], qn_ref[...]).astype(MXU_DTYPE)
    q = _dot(cqn, wuq_ref[...])
    if qscale != 1.0:
        q = q * qscale
    q_ref[...] = q.astype(q_ref.dtype)
    for lo in rope_slabs:
        xs = q[:, lo:lo + V7X_LANES]
        q_ref[:, lo:lo + V7X_LANES] = (xs * cos + _swap32(xs) * sin).astype(q_ref.dtype)
    ckv = _rms(p_ref[:, q_lora:q_lora + kv_lora], kvn_ref[...])
    ckv_ref[...] = ckv
    kp = p_ref[:, q_lora + kv_lora:q_lora + kv_lora + V7X_LANES]
    kpr = kp * cos + _swap32(kp) * sin
    kpe_ref[...] = kpr[:, 0:rope_dim]
    kpr_m = kpr.astype(k_ref.dtype)
    if expand:
        kv = _dot(ckv.astype(MXU_DTYPE), wukv_ref[...])
        hw = nope + V7X_LANES
        for h in range(heads):
            k_ref[:, h * hw:h * hw + nope] = kv[:, h * nope:(h + 1) * nope].astype(k_ref.dtype)
            k_ref[:, h * hw + nope:(h + 1) * hw] = kpr_m
        v_ref[...] = kv[:, heads * nope:].astype(v_ref.dtype)
    else:
        k_ref[:, 0:kv_lora] = ckv.astype(k_ref.dtype)
        k_ref[:, kv_lora:kv_lora + V7X_LANES] = kpr_m


def mla_prep(proj, q_norm, kv_norm, w_uq, cos_rows, sin_rows, w_ukv, *, q_lora, kv_lora, rope_dim,
             rope_slabs, qscale, heads, nope):
    n = proj.shape[0]
    qcols = w_uq.shape[1]
    tm = _pick(n, (512, 256, 128))
    expand = w_ukv is not None
    row = lambda w: pl.BlockSpec((tm, w), lambda i: (i, 0))
    full = lambda a: pl.BlockSpec(a.shape, lambda i: (0, 0))
    in_specs = [row(proj.shape[1]), pl.BlockSpec((1, q_lora), lambda i: (0, 0)),
                pl.BlockSpec((1, kv_lora), lambda i: (0, 0)), full(w_uq), row(V7X_LANES), row(V7X_LANES)]
    args = [proj, q_norm.reshape(1, -1), kv_norm.reshape(1, -1), w_uq, cos_rows, sin_rows]
    out_shape = [jax.ShapeDtypeStruct((n, qcols), MXU_DTYPE), jax.ShapeDtypeStruct((n, kv_lora), F32),
                 jax.ShapeDtypeStruct((n, rope_dim), F32)]
    out_specs = [row(qcols), row(kv_lora), row(rope_dim)]
    vmem = (2 * (_nbytes((tm, proj.shape[1]), F32) + _nbytes(w_uq.shape, w_uq.dtype)
                 + _nbytes((tm, qcols), MXU_DTYPE) + 3 * _nbytes((tm, kv_lora + V7X_LANES), F32))
            + 3 * _nbytes((tm, qcols), F32) + (8 << 20))
    if expand:
        kcols = heads * (nope + V7X_LANES)
        vcols = w_ukv.shape[1] - heads * nope
        in_specs.append(full(w_ukv))
        args.append(w_ukv)
        out_shape += [jax.ShapeDtypeStruct((n, kcols), MXU_DTYPE), jax.ShapeDtypeStruct((n, vcols), MXU_DTYPE)]
        out_specs += [row(kcols), row(vcols)]
        vmem += (2 * (_nbytes(w_ukv.shape, w_ukv.dtype) + _nbytes((tm, kcols + vcols), MXU_DTYPE))
                 + 2 * _nbytes((tm, w_ukv.shape[1]), F32))
    else:
        out_shape.append(jax.ShapeDtypeStruct((n, kv_lora + V7X_LANES), MXU_DTYPE))
        out_specs.append(row(kv_lora + V7X_LANES))
    return pl.pallas_call(
        functools.partial(_mla_prep_kernel, q_lora=q_lora, kv_lora=kv_lora, rope_dim=rope_dim,
                          rope_slabs=rope_slabs, qscale=qscale, heads=heads, nope=nope),
        out_shape=tuple(out_shape),
        grid=(n // tm,),
        in_specs=in_specs,
        out_specs=tuple(out_specs),
        compiler_params=_params(("parallel",), vmem),
        name="mla_prep",
    )(*args)


def _lanes(x, width):
    reps = width // V7X_LANES
    return x if reps == 1 else jnp.concatenate([x] * reps, axis=-1)


def _mha_attn_kernel(qi_ref, kj_ref, flag_ref, q_ref, k_ref, v_ref, o_ref, acc_ref, m_ref,
                     *, heads, hw, vd, tq, tk, q_offset, kv_len):
    step = pl.program_id(1)
    i = qi_ref[step]
    j = kj_ref[step]
    flags = flag_ref[step]
    aw = vd + V7X_LANES

    @pl.when(j == 0)
    def _():
        m_ref[...] = jnp.full_like(m_ref, NEG)
        acc_ref[...] = jnp.zeros_like(acc_ref)

    ones = jnp.ones((tk, V7X_LANES), MXU_DTYPE)

    def all_heads(bias):
        for h in range(heads):
            s = _dot_nt(q_ref[0, :, h * hw:(h + 1) * hw], k_ref[0, :, h * hw:(h + 1) * hw])
            if bias is not None:
                s = s + bias
            m_old = m_ref[h]
            m_new = jnp.maximum(m_old, jnp.max(s, axis=-1, keepdims=True))
            alpha = jnp.exp2(m_old - m_new)
            p = jnp.exp2(s - _lanes(m_new, tk)).astype(MXU_DTYPE)
            m_ref[h] = m_new
            v_ext = jnp.concatenate([v_ref[0, :, h * vd:(h + 1) * vd], ones], axis=-1)
            acs = slice(h * aw, (h + 1) * aw)
            acc_ref[:, acs] = _lanes(alpha, aw) * acc_ref[:, acs] + _dot(p, v_ext)

    @pl.when((flags & 2) == 0)
    def _():
        all_heads(None)

    @pl.when((flags & 2) != 0)
    def _():
        qpos = q_offset + i * tq + lax.broadcasted_iota(jnp.int32, (tq, tk), 0)
        kpos = j * tk + lax.broadcasted_iota(jnp.int32, (tq, tk), 1)
        visible = ((kpos // CHUNK) <= (qpos // CHUNK)) & (kpos < kv_len)
        all_heads(jnp.where(visible, 0.0, NEG))

    @pl.when((flags & 1) != 0)
    def _():
        for h in range(heads):
            num = acc_ref[:, h * aw:h * aw + vd]
            den = _lanes(acc_ref[:, h * aw + vd:(h + 1) * aw], vd)
            o_ref[0, :, h * vd:(h + 1) * vd] = (num / den).astype(o_ref.dtype)


def mha_attn(q, k, v, *, heads, tq, tk, q_offset, kv_len):
    b, t, _ = q.shape
    s = k.shape[1]
    hw = q.shape[2] // heads
    vd = v.shape[2] // heads
    nq, nk = t // tq, s // tk
    qi, kj, flags = [], [], []
    for i in range(nq):
        first_q = q_offset + i * tq
        last_key = ((first_q + tq - 1) // CHUNK + 1) * CHUNK - 1
        jl = min(last_key // tk, nk - 1)
        for j in range(jl + 1):
            fully_visible = ((j + 1) * tk - 1) // CHUNK <= first_q // CHUNK and (j + 1) * tk <= kv_len
            qi.append(i)
            kj.append(j)
            flags.append(int(j == jl) + 2 * int(not fully_visible))
    sched = [jnp.asarray(a, jnp.int32) for a in (qi, kj, flags)]
    aw = vd + V7X_LANES
    vmem = (2 * (2 * _nbytes((tq, heads * hw), q.dtype) + 2 * _nbytes((tk, heads * vd), v.dtype)
                 + _nbytes((tq, heads * vd), MXU_DTYPE))
            + _nbytes((tq, heads * aw), F32) + _nbytes((heads, tq, V7X_LANES), F32)
            + 6 * _nbytes((tq, tk), F32) + (8 << 20))
    grid_spec = pltpu.PrefetchScalarGridSpec(
        num_scalar_prefetch=3,
        grid=(b, len(qi)),
        in_specs=[pl.BlockSpec((1, tq, heads * hw), lambda bb, p, qi_r, kj_r, l_r: (bb, qi_r[p], 0)),
                  pl.BlockSpec((1, tk, heads * hw), lambda bb, p, qi_r, kj_r, l_r: (bb, kj_r[p], 0)),
                  pl.BlockSpec((1, tk, heads * vd), lambda bb, p, qi_r, kj_r, l_r: (bb, kj_r[p], 0))],
        out_specs=pl.BlockSpec((1, tq, heads * vd), lambda bb, p, qi_r, kj_r, l_r: (bb, qi_r[p], 0)),
        scratch_shapes=[pltpu.VMEM((tq, heads * aw), F32),
                        pltpu.VMEM((heads, tq, V7X_LANES), F32)])
    return pl.pallas_call(
        functools.partial(_mha_attn_kernel, heads=heads, hw=hw, vd=vd, tq=tq, tk=tk,
                          q_offset=q_offset, kv_len=kv_len),
        out_shape=jax.ShapeDtypeStruct((b, t, heads * vd), MXU_DTYPE),
        grid_spec=grid_spec,
        compiler_params=_params(("parallel", "arbitrary"), vmem),
        name="mha_attn",
    )(*sched, q, k, v)


def _last_kv_block(i, *, tq, tk, q_offset, nk):
    last_q = q_offset + (i + 1) * tq - 1
    last_key = (last_q // CHUNK + 1) * CHUNK - 1
    return jnp.minimum(last_key // tk, nk - 1)


def _mla_attn_kernel(q_ref, k_ref, wuk_ref, wuv_ref, o_ref, qs_ref, acc_ref, m_ref, l_ref,
                     *, heads, hb, tq, tk, nope, lat, q_offset, kv_len, scale):
    i = pl.program_id(1)
    j = pl.program_id(2)
    nk = pl.num_programs(2)
    j_last = _last_kv_block(i, tq=tq, tk=tk, q_offset=q_offset, nk=nk)
    rope_lo = heads * nope

    @pl.when(j == 0)
    def _():
        for h in range(heads):
            qn = q_ref[0, :, h * nope:(h + 1) * nope]
            qs_ref[h * tq:(h + 1) * tq, 0:lat] = (_dot(qn, wuk_ref[h]) * scale).astype(qs_ref.dtype)
            qr = q_ref[0, :, rope_lo + h * V7X_LANES:rope_lo + (h + 1) * V7X_LANES]
            qs_ref[h * tq:(h + 1) * tq, lat:lat + V7X_LANES] = (qr.astype(F32) * scale).astype(qs_ref.dtype)
        m_ref[...] = jnp.full_like(m_ref, NEG)
        l_ref[...] = jnp.zeros_like(l_ref)
        acc_ref[...] = jnp.zeros_like(acc_ref)

    @pl.when(j <= j_last)
    def _():
        k = k_ref[0]
        v = k[:, 0:lat]
        qpos = q_offset + i * tq + lax.broadcasted_iota(jnp.int32, (tq, tk), 0)
        kpos = j * tk + lax.broadcasted_iota(jnp.int32, (tq, tk), 1)
        visible = ((kpos // CHUNK) <= (qpos // CHUNK)) & (kpos < kv_len)
        bias = jnp.where(visible, 0.0, NEG)
        rows = hb * tq
        for rb in range(heads // hb):
            rs = slice(rb * rows, (rb + 1) * rows)
            s = _dot_nt(qs_ref[rs, :], k)
            s = (s.reshape(hb, tq, tk) + bias[None]).reshape(rows, tk)
            m_old = m_ref[rs, :]
            m_new = jnp.maximum(m_old, jnp.max(s, axis=-1, keepdims=True))
            alpha = jnp.exp(m_old - m_new)
            p = jnp.exp(s - _lanes(m_new, tk))
            l_ref[rs, :] = alpha * l_ref[rs, :] + jnp.sum(p, axis=-1, keepdims=True)
            m_ref[rs, :] = m_new
            acc_ref[rs, :] = _lanes(alpha, lat) * acc_ref[rs, :] + _dot(p.astype(MXU_DTYPE), v)

    @pl.when(j == j_last)
    def _():
        vd = wuv_ref.shape[2]
        for h in range(heads):
            hs = slice(h * tq, (h + 1) * tq)
            ol = acc_ref[hs, :] / _lanes(l_ref[hs, :], lat)
            o_ref[0, :, h * vd:(h + 1) * vd] = _dot(ol.astype(MXU_DTYPE), wuv_ref[h]).astype(o_ref.dtype)


def mla_attn(q, kcat, w_uk, w_uv, *, tq, tk, q_offset, kv_len, scale):
    b, t, _ = q.shape
    s = kcat.shape[1]
    heads, nope, lat = w_uk.shape
    vd = w_uv.shape[2]
    nq, nk = t // tq, s // tk
    rows = heads * tq
    hb = max(1, min(heads, MLA_ROW_BLOCK // tq))
    assert heads % hb == 0
    last = functools.partial(_last_kv_block, tq=tq, tk=tk, q_offset=q_offset, nk=nk)
    vmem = (2 * (_nbytes((tq, q.shape[2]), q.dtype) + _nbytes((tk, kcat.shape[2]), kcat.dtype)
                 + 2 * _nbytes(w_uk.shape, w_uk.dtype) + _nbytes((tq, heads * vd), MXU_DTYPE))
            + _nbytes((rows, lat + V7X_LANES), MXU_DTYPE) + _nbytes((rows, lat), F32)
            + 2 * _nbytes((rows, V7X_LANES), F32) + 4 * _nbytes((rows, tk), F32) + (8 << 20))
    return pl.pallas_call(
        functools.partial(_mla_attn_kernel, heads=heads, hb=hb, tq=tq, tk=tk, nope=nope, lat=lat,
                          q_offset=q_offset, kv_len=kv_len, scale=scale),
        out_shape=jax.ShapeDtypeStruct((b, t, heads * vd), MXU_DTYPE),
        grid=(b, nq, nk),
        in_specs=[pl.BlockSpec((1, tq, q.shape[2]), lambda bb, i, j: (bb, i, 0)),
                  pl.BlockSpec((1, tk, kcat.shape[2]), lambda bb, i, j: (bb, jnp.minimum(j, last(i)), 0)),
                  pl.BlockSpec(w_uk.shape, lambda bb, i, j: (0, 0, 0)),
                  pl.BlockSpec(w_uv.shape, lambda bb, i, j: (0, 0, 0))],
        out_specs=pl.BlockSpec((1, tq, heads * vd), lambda bb, i, j: (bb, i, 0)),
        scratch_shapes=[pltpu.VMEM((rows, lat + V7X_LANES), MXU_DTYPE),
                        pltpu.VMEM((rows, lat), F32),
                        pltpu.VMEM((rows, V7X_LANES), F32),
                        pltpu.VMEM((rows, V7X_LANES), F32)],
        compiler_params=_params(("parallel", "parallel", "arbitrary"), vmem),
        name="mla_attn",
    )(q, kcat, w_uk, w_uv)


def _ssd_kernel(*refs, L, groups, hpg, hdim, nstate, has_state):
    if has_state:
        (z_ref, xs_ref, b_ref, c_ref, dt_ref, cw_ref, cb_ref, dtb_ref, alog_ref, dsk_ref, nrm_ref,
         cst_ref, hst_ref, y_ref, ncv_ref, nst_ref, xbuf, h_scr, cum_scr, xdt_scr, yin_scr) = refs
    else:
        (z_ref, xs_ref, b_ref, c_ref, dt_ref, cw_ref, cb_ref, dtb_ref, alog_ref, dsk_ref, nrm_ref,
         cst_ref, y_ref, ncv_ref, nst_ref, xbuf, h_scr, cum_scr, xdt_scr, yin_scr) = refs
        hst_ref = None
    c = pl.program_id(1)
    nc = pl.num_programs(1)
    dx = groups * hpg * hdim
    dn = groups * nstate
    gw = hpg * hdim
    pad = CONV_PAD_ROWS

    @pl.when(c == 0)
    def _():
        xbuf[0:pad, :] = cst_ref[0]
        if has_state:
            h_scr[...] = hst_ref[0]
        else:
            h_scr[...] = jnp.zeros_like(h_scr)

    xbuf[pad:pad + L, 0:dx] = xs_ref[...].astype(F32)
    xbuf[pad:pad + L, dx:dx + dn] = b_ref[...].astype(F32)
    xbuf[pad:pad + L, dx + dn:dx + 2 * dn] = c_ref[...].astype(F32)
    acc = xbuf[pad - CONV_K + 1:pad - CONV_K + 1 + L, :] * cw_ref[0:1, :] + cb_ref[...]
    for jj in range(1, CONV_K):
        lo = pad - CONV_K + 1 + jj
        acc = acc + xbuf[lo:lo + L, :] * cw_ref[jj:jj + 1, :]
    xc = _silu(acc)
    tail = xbuf[L:L + pad, :]
    xbuf[0:pad, :] = tail

    @pl.when(c == nc - 1)
    def _():
        ncv_ref[0] = tail

    dtr = dt_ref[...] + dtb_ref[...]
    dt = jnp.maximum(dtr, 0.0) + jnp.log1p(jnp.exp(-jnp.abs(dtr)))
    la = dt * (-jnp.exp(alog_ref[...]))
    row = lax.broadcasted_iota(jnp.int32, (L, L), 0)
    col = lax.broadcasted_iota(jnp.int32, (L, L), 1)
    causal = row >= col
    tri = jnp.where(causal, 1.0, 0.0).astype(MXU_DTYPE)
    cum = sum(_dot(tri, piece) for piece in _split3(la))
    nh = groups * hpg
    eye = jnp.where(lax.broadcasted_iota(jnp.int32, (V7X_LANES, V7X_LANES), 0)
                    == lax.broadcasted_iota(jnp.int32, (V7X_LANES, V7X_LANES), 1), 1.0, 0.0).astype(MXU_DTYPE)
    cum_t = sum(_dot_nt(eye, piece) for piece in _split3(cum))

    half = lax.broadcasted_iota(jnp.int32, (L, V7X_LANES), 1) < hdim
    per_vreg = V7X_LANES // hdim
    for g in range(groups):
        bg = xc[:, dx + g * nstate:dx + (g + 1) * nstate]
        cg = xc[:, dx + dn + g * nstate:dx + dn + (g + 1) * nstate].astype(MXU_DTYPE)
        qk = _dot_nt(cg, bg.astype(MXU_DTYPE))
        for sl in range(gw // V7X_LANES):
            lane0 = g * gw + sl * V7X_LANES
            h0 = lane0 // hdim
            cb = [jnp.broadcast_to(cum[:, h0 + u:h0 + u + 1], (L, V7X_LANES)) for u in range(per_vreg)]
            db = [jnp.broadcast_to(dt[:, h0 + u:h0 + u + 1], (L, V7X_LANES)) for u in range(per_vreg)]
            cum_e = jnp.where(half, cb[0], cb[1])
            dt_e = jnp.where(half, db[0], db[1])
            xdt = xc[:, lane0:lane0 + V7X_LANES] * dt_e
            xdt_m = xdt.astype(MXU_DTYPE)
            ys = []
            for u in range(per_vreg):
                seg = cb[u][:, 0:L] - cum_t[h0 + u:h0 + u + 1, :]
                decay = jnp.exp(jnp.where(causal, seg, NEG))
                ys.append(_dot((qk * decay).astype(MXU_DTYPE), xdt_m))
            cum_scr[:, lane0:lane0 + V7X_LANES] = cum_e
            xdt_scr[:, lane0:lane0 + V7X_LANES] = xdt
            yin_scr[:, lane0:lane0 + V7X_LANES] = jnp.where(half, ys[0], ys[1])

    for g in range(groups):
        gs = slice(g * gw, (g + 1) * gw)
        bg = xc[:, dx + g * nstate:dx + (g + 1) * nstate].astype(MXU_DTYPE)
        cg = xc[:, dx + dn + g * nstate:dx + dn + (g + 1) * nstate].astype(MXU_DTYPE)
        cum_g = cum_scr[:, gs]
        last = cum_scr[L - 1:L, gs]
        hg = h_scr[g]
        y_inter = _dot(cg, hg.astype(MXU_DTYPE)) * jnp.exp(cum_g)
        wx = (jnp.exp(last - cum_g) * xdt_scr[:, gs]).astype(MXU_DTYPE)
        h_scr[g] = jnp.exp(last) * hg + _dot_tn(bg, wx)
        y = yin_scr[:, gs] + y_inter + dsk_ref[:, gs] * xc[:, gs]
        y = y * _silu(z_ref[:, gs].astype(F32))
        y_ref[:, gs] = (_rms(y) * nrm_ref[:, gs]).astype(y_ref.dtype)

    @pl.when(c == nc - 1)
    def _():
        nst_ref[0] = h_scr[...]


def ssd_scan(proj, dt_raw, col, conv_w, conv_b, dt_bias, a_log, d_skip_e, ssd_norm, conv_state, ssd_state,
             *, batch, L, groups, hpg, hdim, nstate):
    n = proj.shape[0]
    t = n // batch
    nc = t // L
    dx = groups * hpg * hdim
    dn = groups * nstate
    cdim = dx + 2 * dn
    has_state = ssd_state is not None
    rowmap = lambda b, c: b * nc + c
    in_specs = [pl.BlockSpec((L, dx), lambda b, c: (rowmap(b, c), col["z"])),
                pl.BlockSpec((L, dx), lambda b, c: (rowmap(b, c), col["xs"])),
                pl.BlockSpec((L, dn), lambda b, c: (rowmap(b, c), col["B"])),
                pl.BlockSpec((L, dn), lambda b, c: (rowmap(b, c), col["C"])),
                pl.BlockSpec((L, V7X_LANES), lambda b, c: (rowmap(b, c), 0)),
                pl.BlockSpec((CONV_K, cdim), lambda b, c: (0, 0)),
                pl.BlockSpec((1, cdim), lambda b, c: (0, 0)),
                pl.BlockSpec((1, V7X_LANES), lambda b, c: (0, 0)),
                pl.BlockSpec((1, V7X_LANES), lambda b, c: (0, 0)),
                pl.BlockSpec((1, dx), lambda b, c: (0, 0)),
                pl.BlockSpec((1, dx), lambda b, c: (0, 0)),
                pl.BlockSpec((1, CONV_PAD_ROWS, cdim), lambda b, c: (b, 0, 0))]
    args = [proj, proj, proj, proj, dt_raw, conv_w, conv_b.reshape(1, cdim), dt_bias, a_log,
            d_skip_e, ssd_norm.reshape(1, dx), conv_state]
    if has_state:
        in_specs.append(pl.BlockSpec((1, groups, nstate, hpg * hdim), lambda b, c: (b, 0, 0, 0)))
        args.append(ssd_state)
    state_bytes = _nbytes((groups, nstate, hpg * hdim), F32)
    vmem = (2 * (2 * _nbytes((L, dx), F32) + 2 * _nbytes((L, dn), F32) + _nbytes((L, dx), MXU_DTYPE)
                 + 2 * _nbytes((CONV_PAD_ROWS, cdim), F32) + 2 * state_bytes)
            + state_bytes + 12 * _nbytes((L + CONV_PAD_ROWS, cdim), F32) + (8 << 20))
    return pl.pallas_call(
        functools.partial(_ssd_kernel, L=L, groups=groups, hpg=hpg, hdim=hdim, nstate=nstate,
                          has_state=has_state),
        out_shape=(jax.ShapeDtypeStruct((n, dx), MXU_DTYPE),
                   jax.ShapeDtypeStruct((batch, CONV_PAD_ROWS, cdim), F32),
                   jax.ShapeDtypeStruct((batch, groups, nstate, hpg * hdim), F32)),
        grid=(batch, nc),
        in_specs=in_specs,
        out_specs=(pl.BlockSpec((L, dx), lambda b, c: (rowmap(b, c), 0)),
                   pl.BlockSpec((1, CONV_PAD_ROWS, cdim), lambda b, c: (b, 0, 0)),
                   pl.BlockSpec((1, groups, nstate, hpg * hdim), lambda b, c: (b, 0, 0, 0))),
        scratch_shapes=[pltpu.VMEM((L + CONV_PAD_ROWS, cdim), F32),
                        pltpu.VMEM((groups, nstate, hpg * hdim), F32),
                        pltpu.VMEM((L, dx), F32),
                        pltpu.VMEM((L, dx), F32),
                        pltpu.VMEM((L, dx), F32)],
        compiler_params=_params(("parallel", "arbitrary"), vmem),
        name="ssd_scan",
    )(*args)


def _ret_kernel(*refs, L, heads, hd, has_state):
    if has_state:
        (q_ref, k_ref, v_ref, gate_ref, cos_ref, sin_ref, dec_ref, ecum_ref, wv_ref, sdec_ref, st_ref,
         o_ref, nst_ref, s_scr) = refs
    else:
        (q_ref, k_ref, v_ref, gate_ref, cos_ref, sin_ref, dec_ref, ecum_ref, wv_ref, sdec_ref,
         o_ref, nst_ref, s_scr) = refs
        st_ref = None
    c = pl.program_id(1)
    nc = pl.num_programs(1)
    half = hd // 2

    @pl.when(c == 0)
    def _():
        if has_state:
            s_scr[...] = st_ref[0]
        else:
            s_scr[...] = jnp.zeros_like(s_scr)

    cos = cos_ref[...]
    sin = sin_ref[...]

    def rope(ref, h, mult):
        x1 = ref[:, h * hd:h * hd + half].astype(F32)
        x2 = ref[:, h * hd + half:(h + 1) * hd].astype(F32)
        out = jnp.concatenate([x1 * cos - x2 * sin, x2 * cos + x1 * sin], axis=-1)
        return (out * mult).astype(MXU_DTYPE) if mult != 1.0 else out.astype(MXU_DTYPE)

    for h in range(heads):
        hs = slice(h * hd, (h + 1) * hd)
        qr = rope(q_ref, h, 1.0)
        kr = rope(k_ref, h, hd ** -0.5)
        vf = v_ref[:, hs]
        qk = _dot_nt(qr, kr)
        y = _dot((qk * dec_ref[h]).astype(MXU_DTYPE), vf.astype(MXU_DTYPE))
        s_old = s_scr[h]
        y = y + _dot(qr, s_old.astype(MXU_DTYPE)) * ecum_ref[h]
        s_scr[h] = sdec_ref[h] * s_old + _dot_tn(kr, (wv_ref[h] * vf.astype(F32)).astype(MXU_DTYPE))
        o_ref[:, hs] = (_rms(y) * _silu(gate_ref[:, hs].astype(F32))).astype(o_ref.dtype)

    @pl.when(c == nc - 1)
    def _():
        nst_ref[0] = s_scr[...]


def ret_scan(proj, col, cos_rows, sin_rows, ret_state, *, batch, L, heads, hd):
    n = proj.shape[0]
    t = n // batch
    nc = t // L
    inner = heads * hd
    has_state = ret_state is not None
    lg = jnp.log1p(-jnp.exp2(-5.0 - jnp.arange(heads, dtype=F32)))[:, None, None]
    li = jnp.arange(L, dtype=F32)
    diff = li[:, None] - li[None, :]
    dec = jnp.where(diff >= 0, jnp.exp(jnp.where(diff >= 0, diff, 0.0)[None] * lg), 0.0)
    ecum = jnp.broadcast_to(jnp.exp((li[None, :, None] + 1.0) * lg), (heads, L, hd))
    wv = jnp.broadcast_to(jnp.exp((L - 1.0 - li[None, :, None]) * lg), (heads, L, hd))
    sdec = jnp.broadcast_to(jnp.exp(L * lg), (heads, 1, hd))
    rowmap = lambda b, c: b * nc + c
    in_specs = [pl.BlockSpec((L, inner), lambda b, c: (rowmap(b, c), col["q"])),
                pl.BlockSpec((L, inner), lambda b, c: (rowmap(b, c), col["k"])),
                pl.BlockSpec((L, inner), lambda b, c: (rowmap(b, c), col["v"])),
                pl.BlockSpec((L, inner), lambda b, c: (rowmap(b, c), col["gate"])),
                pl.BlockSpec((L, hd // 2), lambda b, c: (rowmap(b, c), 0)),
                pl.BlockSpec((L, hd // 2), lambda b, c: (rowmap(b, c), 0)),
                pl.BlockSpec((heads, L, L), lambda b, c: (0, 0, 0)),
                pl.BlockSpec((heads, L, hd), lambda b, c: (0, 0, 0)),
                pl.BlockSpec((heads, L, hd), lambda b, c: (0, 0, 0)),
                pl.BlockSpec((heads, 1, hd), lambda b, c: (0, 0, 0))]
    args = [proj, proj, proj, proj, cos_rows, sin_rows, dec, ecum, wv, sdec]
    if has_state:
        in_specs.append(pl.BlockSpec((1, heads, hd, hd), lambda b, c: (b, 0, 0, 0)))
        args.append(ret_state)
    state_bytes = _nbytes((heads, hd, hd), F32)
    vmem = (2 * (4 * _nbytes((L, inner), F32) + _nbytes((L, inner), MXU_DTYPE) + _nbytes((heads, L, L), F32)
                 + 2 * _nbytes((heads, L, hd), F32) + 2 * state_bytes)
            + state_bytes + 16 * _nbytes((L, hd), F32) + (8 << 20))
    return pl.pallas_call(
        functools.partial(_ret_kernel, L=L, heads=heads, hd=hd, has_state=has_state),
        out_shape=(jax.ShapeDtypeStruct((n, inner), MXU_DTYPE),
                   jax.ShapeDtypeStruct((batch, heads, hd, hd), F32)),
        grid=(batch, nc),
        in_specs=in_specs,
        out_specs=(pl.BlockSpec((L, inner), lambda b, c: (rowmap(b, c), 0)),
                   pl.BlockSpec((1, heads, hd, hd), lambda b, c: (b, 0, 0, 0))),
        scratch_shapes=[pltpu.VMEM((heads, hd, hd), F32)],
        compiler_params=_params(("parallel", "arbitrary"), vmem),
        name="ret_scan",
    )(*args)


def _rope_tables(pos, half, reps, batch):
    inv = ROPE_BASE ** (-jnp.arange(half, dtype=F32) / half)
    ang = pos.astype(F32)[:, None] * inv[None, :]
    cos, sin = jnp.cos(ang), jnp.sin(ang)
    if reps == 0:
        return jnp.tile(cos, (batch, 1)), jnp.tile(sin, (batch, 1))
    c = jnp.tile(jnp.concatenate([cos, cos], axis=-1), (batch, reps))
    s = jnp.tile(jnp.concatenate([-sin, sin], axis=-1), (batch, reps))
    return c, s


def _mixer_even(x, batch, pos, norm_g, prm, conv_state, ssd_state, ret_state, dims, L_ssd, L_ret):
    d = x.shape[1]
    groups, hpg, hdim, nstate, rheads, rhd = dims
    proj, dt_raw = rms_matmul(x, norm_g, prm["w_in"], out_dtype=MXU_DTYPE, w_side=prm["w_dt"])
    col = prm["col"]
    y, new_conv, new_ssd = ssd_scan(proj, dt_raw, col, prm["conv_w"], prm["conv_b"], prm["dt_bias"],
                                    prm["a_log"], prm["d_skip_e"], prm["ssd_norm"], conv_state, ssd_state,
                                    batch=batch, L=L_ssd, groups=groups, hpg=hpg, hdim=hdim, nstate=nstate)
    cos_rows, sin_rows = _rope_tables(pos, rhd // 2, 0, batch)
    o, new_ret = ret_scan(proj, col, cos_rows, sin_rows, ret_state, batch=batch, L=L_ret, heads=rheads, hd=rhd)
    x = matmul_residual(x, [y, o], [prm["w_out_ssd"], prm["w_out_ret"]])
    return x, (new_conv, new_ssd, new_ret)


def _prep_even(w_in, conv_w, conv_b, dt_bias, a_log, d_skip, ssd_norm, w_out, dims):
    groups, hpg, hdim, nstate, rheads, rhd = dims
    d = w_in.shape[0]
    dx = groups * hpg * hdim
    dn = groups * nstate
    nh = groups * hpg
    ri = rheads * rhd
    o_z, o_xs, o_b, o_c, o_dt = 0, dx, 2 * dx, 2 * dx + dn, 2 * dx + 2 * dn
    o_q = o_dt + nh
    seg = lambda lo, w: w_in[:, lo:lo + w]
    dt_pad = jnp.zeros((d, V7X_LANES - nh), w_in.dtype)
    w_main = jnp.concatenate([seg(o_z, dx), seg(o_q, ri), seg(o_q + ri, ri), seg(o_q + 2 * ri, ri),
                              seg(o_q + 3 * ri, ri), seg(o_xs, dx), seg(o_b, dn), seg(o_c, dn)],
                             axis=1).astype(MXU_DTYPE)
    w_dt = jnp.concatenate([seg(o_dt, nh), dt_pad], axis=1).astype(MXU_DTYPE)
    assert dx == ri and dx % dn == 0 and dn % V7X_LANES == 0
    col = {"z": 0, "q": 1, "k": 2, "v": 3, "gate": 4, "xs": 5, "B": 6 * dx // dn, "C": 6 * dx // dn + 1}
    pad1 = lambda v: jnp.pad(v.astype(F32), (0, V7X_LANES - nh)).reshape(1, V7X_LANES)
    return {"w_in": w_main, "w_dt": w_dt, "col": col, "conv_w": conv_w, "conv_b": conv_b,
            "dt_bias": pad1(dt_bias), "a_log": pad1(a_log),
            "d_skip_e": jnp.repeat(d_skip, hdim).reshape(1, dx), "ssd_norm": ssd_norm,
            "w_out_ssd": w_out[:dx].astype(MXU_DTYPE), "w_out_ret": w_out[dx:].astype(MXU_DTYPE)}


def _prep_odd(w_in, q_norm, kv_norm, w_uq, w_uk, w_uv, w_out, rope_dim):
    d = w_in.shape[0]
    q_lora = q_norm.shape[0]
    kv_lora, heads, nope = w_uk.shape
    kp = w_in[:, q_lora + kv_lora:]
    w_in_new = jnp.concatenate([w_in[:, :q_lora + kv_lora], kp, kp], axis=1).astype(MXU_DTYPE)
    assert 2 * rope_dim == V7X_LANES
    wq = w_uq.reshape(q_lora, heads, nope + rope_dim)
    wq_nope = wq[:, :, :nope]
    wq_rope = jnp.pad(wq[:, :, nope:], ((0, 0), (0, 0), (0, V7X_LANES - rope_dim)))
    hw = nope + V7X_LANES
    return {"w_in": w_in_new, "q_norm": q_norm, "kv_norm": kv_norm,
            "w_uq_lat": jnp.concatenate([wq_nope.reshape(q_lora, heads * nope),
                                         wq_rope.reshape(q_lora, heads * V7X_LANES)], axis=1).astype(MXU_DTYPE),
            "slabs_lat": tuple(heads * nope + h * V7X_LANES for h in range(heads)),
            "w_uq_head": jnp.concatenate([wq_nope, wq_rope], axis=2).reshape(q_lora, heads * hw).astype(MXU_DTYPE),
            "slabs_head": tuple(h * hw + nope for h in range(heads)),
            "w_ukv": jnp.concatenate([w_uk.reshape(kv_lora, heads * nope),
                                      w_uv.reshape(kv_lora, -1)], axis=1).astype(MXU_DTYPE),
            "w_uk": jnp.transpose(w_uk, (1, 2, 0)).astype(MXU_DTYPE),
            "w_uv": jnp.transpose(w_uv, (1, 0, 2)).astype(MXU_DTYPE),
            "w_out": w_out.astype(MXU_DTYPE)}


def _mixer_odd(x, batch, pos, norm_g, prm, past_kcat, rope_dim, q_offset):
    n, d = x.shape
    t = n // batch
    q_lora = prm["q_norm"].shape[0]
    kv_lora = prm["kv_norm"].shape[0]
    heads, nope, lat = prm["w_uk"].shape
    proj = rms_matmul(x, norm_g, prm["w_in"])
    cos_rows, sin_rows = _rope_tables(pos, rope_dim // 2, V7X_LANES // rope_dim, batch)
    scale = (nope + rope_dim) ** -0.5
    dims = dict(q_lora=q_lora, kv_lora=kv_lora, rope_dim=rope_dim, heads=heads, nope=nope)
    if past_kcat is None:
        q, ckv, kpe, k, v = mla_prep(proj, prm["q_norm"], prm["kv_norm"], prm["w_uq_head"], cos_rows, sin_rows,
                                     prm["w_ukv"], rope_slabs=prm["slabs_head"], qscale=scale * math.log2(math.e),
                                     **dims)
        tq = tk = _pick(t, (512, 256, 128, 64))
        shp = lambda a: a.reshape(batch, t, a.shape[1])
        o = mha_attn(shp(q), shp(k), shp(v), heads=heads, tq=tq, tk=tk, q_offset=q_offset, kv_len=t)
    else:
        q, ckv, kpe, kcat = mla_prep(proj, prm["q_norm"], prm["kv_norm"], prm["w_uq_lat"], cos_rows, sin_rows,
                                     None, rope_slabs=prm["slabs_lat"], qscale=1.0, **dims)
        q = q.reshape(batch, t, q.shape[1])
        kcat = kcat.reshape(batch, t, kcat.shape[1])
        kv_len = past_kcat.shape[1] + t
        padded = -(-kv_len // V7X_LANES) * V7X_LANES
        keys = jnp.concatenate([past_kcat, kcat, jnp.zeros((batch, padded - kv_len, kcat.shape[2]), kcat.dtype)], axis=1)
        o = mla_attn(q, keys, prm["w_uk"], prm["w_uv"], tq=t, tk=padded, q_offset=q_offset, kv_len=kv_len,
                     scale=scale)
    x = matmul_residual(x, [o.reshape(n, o.shape[2])], [prm["w_out"]])
    return x, (ckv.reshape(batch, t, kv_lora), kpe.reshape(batch, t, rope_dim))


def kernel(x_prompt, x_sample, mem_prompt, state_conv, state_ssd, state_ret, cache_ckv, cache_kpe,
           cache_mem_k, cache_mem_v, norms, ffn_w1, ffn_w2, mem_norm, w_mq, w_mkv, w_mo,
           ab_w_in, ab_conv_w, ab_conv_b, ab_dt_bias, ab_a_log, ab_d_skip, ab_ssd_norm, ab_w_out,
           c_w_in, c_q_norm, c_kv_norm, c_w_uq, c_w_uk, c_w_uv, c_w_out, final_norm):
    bp, tp, d = x_prompt.shape
    bs, ts, _ = x_sample.shape
    depth = norms.shape[0]
    assert depth >= 1
    past_len = cache_ckv.shape[2]
    mem_tokens = mem_prompt.shape[1]
    mem_heads, mem_hd = cache_mem_k.shape[3], cache_mem_k.shape[4]
    mem_inner = mem_heads * mem_hd
    ssd_heads, nstate, hdim = state_ssd.shape[2], state_ssd.shape[3], state_ssd.shape[4]
    cdim = state_conv.shape[3]
    groups = (cdim - ssd_heads * hdim) // (2 * nstate)
    hpg = ssd_heads // groups
    rheads, rhd = state_ret.shape[2], state_ret.shape[3]
    dims = (groups, hpg, hdim, nstate, rheads, rhd)
    rope_dim = cache_kpe.shape[3]

    pos_p = jnp.arange(tp)
    pos_s = past_len + jnp.arange(ts)
    xp = x_prompt.reshape(bp * tp, d)
    xs = x_sample.reshape(bs * ts, d)
    L_ssd_p, L_ret_p = _pick(tp, (128, 64)), _pick(tp, (256, 128, 64))
    L_s = _pick(ts, (128, 64))

    outs = {k: [] for k in ("conv_p", "ssd_p", "ret_p", "ckv_p", "kpe_p", "memk_p", "memv_p",
                            "conv_s", "ssd_s", "ret_s", "ckv_s", "kpe_s")}

    def to_group_layout(st):
        b = st.shape[0]
        return st.reshape(b, groups, hpg, nstate, hdim).transpose(0, 1, 3, 2, 4).reshape(b, groups, nstate, hpg * hdim)

    def from_group_layout(st):
        b = st.shape[0]
        return st.reshape(b, groups, nstate, hpg, hdim).transpose(0, 1, 3, 2, 4).reshape(b, ssd_heads, nstate, hdim)

    w1 = ffn_w1.astype(MXU_DTYPE)
    w2 = (0.5 * ffn_w2).astype(MXU_DTYPE)
    for i in range(depth):
        j = i // 2
        closing = final_norm if i == depth - 1 else None
        wq_m = w_mq[i].astype(MXU_DTYPE)
        wo_m = w_mo[i].astype(MXU_DTYPE)
        mkv = rms_matmul(mem_prompt.reshape(bp * mem_tokens, d), mem_norm[i], w_mkv[i].astype(MXU_DTYPE))
        mk_p = mkv[:, :mem_inner].reshape(bp, mem_tokens, mem_inner)
        mv_p = mkv[:, mem_inner:].reshape(bp, mem_tokens, mem_inner)
        outs["memk_p"].append(mk_p.reshape(bp, mem_tokens, mem_heads, mem_hd))
        outs["memv_p"].append(mv_p.reshape(bp, mem_tokens, mem_heads, mem_hd))

        xp = ffn(xp, norms[i, 0], w1, w2, i, 0)
        xs = ffn(xs, norms[i, 0], w1, w2, i, 0)
        if i % 2 == 0:
            prm = _prep_even(ab_w_in[j], ab_conv_w[j], ab_conv_b[j], ab_dt_bias[j], ab_a_log[j],
                             ab_d_skip[j], ab_ssd_norm[j], ab_w_out[j], dims)
            zero_conv = jnp.zeros((bp, CONV_PAD_ROWS, cdim), F32)
            xp, st_p = _mixer_even(xp, bp, pos_p, norms[i, 1], prm, zero_conv, None, None, dims, L_ssd_p, L_ret_p)
            conv_in = jnp.pad(state_conv[j], ((0, 0), (CONV_PAD_ROWS - (CONV_K - 1), 0), (0, 0)))
            xs, st_s = _mixer_even(xs, bs, pos_s, norms[i, 1], prm, conv_in, to_group_layout(state_ssd[j]),
                                   state_ret[j], dims, L_s, L_s)
            for tag, st in (("p", st_p), ("s", st_s)):
                outs["conv_" + tag].append(st[0][:, CONV_PAD_ROWS - (CONV_K - 1):, :])
                outs["ssd_" + tag].append(from_group_layout(st[1]))
                outs["ret_" + tag].append(st[2])
        else:
            prm = _prep_odd(c_w_in[j], c_q_norm[j], c_kv_norm[j], c_w_uq[j], c_w_uk[j], c_w_uv[j], c_w_out[j],
                            rope_dim)
            xp, st_p = _mixer_odd(xp, bp, pos_p, norms[i, 1], prm, None, rope_dim, 0)
            past = jnp.concatenate([cache_ckv[j], cache_kpe[j], cache_kpe[j]], axis=-1).astype(MXU_DTYPE)
            xs, st_s = _mixer_odd(xs, bs, pos_s, norms[i, 1], prm, past, rope_dim, past_len)
            for tag, st in (("p", st_p), ("s", st_s)):
                outs["ckv_" + tag].append(st[0])
                outs["kpe_" + tag].append(st[1])
        xp = mem_attn(xp, norms[i, 2], wq_m, mk_p, mv_p, wo_m, batch=bp, heads=mem_heads)
        xs = mem_attn(xs, norms[i, 2], wq_m, cache_mem_k[i].reshape(bs, mem_tokens, mem_inner),
                      cache_mem_v[i].reshape(bs, mem_tokens, mem_inner), wo_m, batch=bs, heads=mem_heads)
        xp = ffn(xp, norms[i, 3], w1, w2, i, 1, final_g=closing)
        xs = ffn(xs, norms[i, 3], w1, w2, i, 1, final_g=closing)

    y_prompt = xp.reshape(bp, tp, d)
    y_sample = xs.reshape(bs, ts, d)
    st = lambda k: jnp.stack(outs[k])
    return (y_prompt, y_sample, st("conv_p"), st("ssd_p"), st("ret_p"), st("ckv_p"), st("kpe_p"),
            st("memk_p"), st("memv_p"), st("conv_s"), st("ssd_s"), st("ret_s"), st("ckv_s"), st("kpe_s"))
```

```python
import functools
import math
from typing import Callable, NamedTuple

import jax
import jax.numpy as jnp
from jax import lax
from jax.experimental import pallas as pl
from jax.experimental.pallas import tpu as pltpu

F32 = jnp.float32
MXU_DTYPE = jnp.bfloat16

EPS = 1e-6
CHUNK = 64
ROPE_BASE = 10000.0
CONV_K = 4
NEG = -1e30

V7X_VMEM_BYTES = 64 * 1024 * 1024
TILE_VMEM_BUDGET = 48 * 1024 * 1024
V7X_LANES = 128
CONV_PAD_ROWS = 8
MLA_ROW_BLOCK = 512


def _params(semantics, vmem_bytes):
    limit = min(int(vmem_bytes), V7X_VMEM_BYTES - (4 << 20))
    return pltpu.CompilerParams(dimension_semantics=semantics, vmem_limit_bytes=limit)


def _nbytes(shape, dtype):
    return math.prod(shape) * jnp.dtype(dtype).itemsize


def _pick(n, prefs):
    for p in prefs:
        if n % p == 0:
            return p
    return n


def _dot(a, b):
    return jnp.dot(a, b, preferred_element_type=F32)


def _dot_nt(a, b):
    return lax.dot_general(a, b, (((1,), (1,)), ((), ())), preferred_element_type=F32)


def _dot_tn(a, b):
    return lax.dot_general(a, b, (((0,), (0,)), ((), ())), preferred_element_type=F32)


def _split3(x):
    hi = x.astype(MXU_DTYPE)
    r = x - hi.astype(F32)
    mid = r.astype(MXU_DTYPE)
    lo = (r - mid.astype(F32)).astype(MXU_DTYPE)
    return hi, mid, lo


def _rms(xf, g=None):
    y = xf * lax.rsqrt(jnp.mean(xf * xf, axis=-1, keepdims=True) + EPS)
    return y if g is None else y * g


def _silu(a):
    return a * (1.0 / (1.0 + jnp.exp(-a)))


def _swap32(x):
    w = x.shape[-1]
    lane = lax.broadcasted_iota(jnp.int32, x.shape, x.ndim - 1)
    fwd = pltpu.roll(x, w - 32, x.ndim - 1)
    bwd = pltpu.roll(x, 32, x.ndim - 1)
    return jnp.where((lane & 63) < 32, fwd, bwd)


def _rms_matmul_kernel(*refs, side):
    if side:
        x_ref, g_ref, w_ref, ws_ref, o_ref, os_ref, xn_ref = refs
    else:
        x_ref, g_ref, w_ref, o_ref, xn_ref = refs

    @pl.when(pl.program_id(1) == 0)
    def _():
        xn_ref[...] = _rms(x_ref[...], g_ref[...]).astype(xn_ref.dtype)
        if side:
            os_ref[...] = _dot(xn_ref[...], ws_ref[...])

    o_ref[...] = _dot(xn_ref[...], w_ref[...]).astype(o_ref.dtype)


def rms_matmul(x, g, w, *, out_dtype=F32, w_side=None):
    n, d = x.shape
    nout = w.shape[1]
    side = w_side is not None
    tn = _pick(nout, (1920, 1152, 1024, 512, 256, 128))

    def vmem_for(tm):
        return (2 * (_nbytes((tm, d), F32) + _nbytes((d, tn), w.dtype) + _nbytes((tm, tn), out_dtype))
                + _nbytes((tm, d), MXU_DTYPE) + _nbytes((tm, tn), F32) + (8 << 20))

    tm = next(t for t in (1024, 512, 256, 128, n) if n % t == 0 and (vmem_for(t) <= TILE_VMEM_BUDGET or t <= 128))
    vmem = vmem_for(tm)
    in_specs = [pl.BlockSpec((tm, d), lambda i, j: (i, 0)),
                pl.BlockSpec((1, d), lambda i, j: (0, 0)),
                pl.BlockSpec((d, tn), lambda i, j: (0, j))]
    args = [x, g.reshape(1, d), w]
    out_shape = jax.ShapeDtypeStruct((n, nout), out_dtype)
    out_specs = pl.BlockSpec((tm, tn), lambda i, j: (i, j))
    if side:
        ns = w_side.shape[1]
        in_specs.append(pl.BlockSpec((d, ns), lambda i, j: (0, 0)))
        args.append(w_side)
        out_shape = (out_shape, jax.ShapeDtypeStruct((n, ns), F32))
        out_specs = (out_specs, pl.BlockSpec((tm, ns), lambda i, j: (i, 0)))
        vmem += 2 * (_nbytes((d, ns), w_side.dtype) + _nbytes((tm, ns), F32))
    return pl.pallas_call(
        functools.partial(_rms_matmul_kernel, side=side),
        out_shape=out_shape,
        grid=(n // tm, nout // tn),
        in_specs=in_specs,
        out_specs=out_specs,
        scratch_shapes=[pltpu.VMEM((tm, d), MXU_DTYPE)],
        compiler_params=_params(("parallel", "arbitrary"), vmem),
        name="rms_matmul",
    )(*args)


def _ffn_kernel(*refs, final_norm, nf):
    x_ref, g_ref = refs[0:2]
    blocks = (refs[2:5], refs[5:8])
    gf_ref = refs[8] if final_norm else None
    o_ref, xn_ref = refs[-2:]
    s = pl.program_id(1)

    @pl.when(s == 0)
    def _():
        xf = x_ref[...]
        xn_ref[...] = _rms(xf, g_ref[...]).astype(xn_ref.dtype)
        o_ref[...] = xf

    def down(w1a_ref, w1b_ref, w2_ref):
        xn = xn_ref[...]
        a = _dot(xn, w1a_ref[...])
        b = _dot(xn, w1b_ref[...])
        return _dot((_silu(a) * b).astype(MXU_DTYPE), w2_ref[...])

    if nf % 2 == 0:
        o_ref[...] += down(*blocks[0]) + down(*blocks[1])
    else:
        @pl.when(2 * s + 1 < nf)
        def _():
            o_ref[...] += down(*blocks[0]) + down(*blocks[1])

        @pl.when(2 * s + 1 >= nf)
        def _():
            o_ref[...] += down(*blocks[0])

    if final_norm:
        @pl.when(s == pl.num_programs(1) - 1)
        def _():
            o_ref[...] = _rms(o_ref[...], gf_ref[...])


def ffn(x, g, w1, w2, layer, which, final_g=None):
    n, d = x.shape
    dff = w2.shape[2]
    tm = _pick(n, (512, 256, 128))
    tf = _pick(dff, (512, 256, 128))
    nf = dff // tf
    vmem = (2 * (2 * _nbytes((tm, d), F32) + 6 * _nbytes((d, tf), w1.dtype))
            + _nbytes((tm, d), MXU_DTYPE) + _nbytes((tm, d), F32) + 6 * _nbytes((tm, tf), F32) + (6 << 20))
    in_specs = [pl.BlockSpec((tm, d), lambda i, s: (i, 0)),
                pl.BlockSpec((1, d), lambda i, s: (0, 0))]
    args = [x, g.reshape(1, d)]
    for half in range(2):
        blk = lambda s, half=half: jnp.minimum(2 * s + half, nf - 1)
        in_specs += [pl.BlockSpec((None, None, d, tf), lambda i, s, blk=blk: (layer, which, 0, blk(s))),
                     pl.BlockSpec((None, None, d, tf), lambda i, s, blk=blk: (layer, which, 0, blk(s) + nf)),
                     pl.BlockSpec((None, None, tf, d), lambda i, s, blk=blk: (layer, which, blk(s), 0))]
        args += [w1, w1, w2]
    if final_g is not None:
        in_specs.append(pl.BlockSpec((1, d), lambda i, s: (0, 0)))
        args.append(final_g.reshape(1, d))
    return pl.pallas_call(
        functools.partial(_ffn_kernel, final_norm=final_g is not None, nf=nf),
        out_shape=jax.ShapeDtypeStruct((n, d), F32),
        grid=(n // tm, -(-nf // 2)),
        in_specs=in_specs,
        out_specs=pl.BlockSpec((tm, d), lambda i, f: (i, 0)),
        scratch_shapes=[pltpu.VMEM((tm, d), MXU_DTYPE)],
        compiler_params=_params(("parallel", "arbitrary"), vmem),
        name="ffn",
    )(*args)


def _matmul_residual_kernel(*refs, n_in):
    x_ref = refs[0]
    o_ref = refs[1 + 2 * n_in]
    acc = x_ref[...]
    for h_ref, w_ref in zip(refs[1:1 + n_in], refs[1 + n_in:1 + 2 * n_in]):
        acc = acc + _dot(h_ref[...], w_ref[...])
    o_ref[...] = acc


def matmul_residual(x, hs, ws):
    n, d = x.shape

    def vmem_for(tm, tn):
        return (2 * (2 * _nbytes((tm, tn), F32) + sum(_nbytes((tm, h.shape[1]), h.dtype) for h in hs)
                     + sum(_nbytes((w.shape[0], tn), w.dtype) for w in ws))
                + _nbytes((tm, tn), F32) + (8 << 20))

    tm, tn = next((a, b) for a, b in ((512, d), (1024, 1024), (512, 1024), (256, 512), (128, 128), (n, d))
                  if n % a == 0 and d % b == 0 and (vmem_for(a, b) <= TILE_VMEM_BUDGET or a <= 128))
    vmem = vmem_for(tm, tn)
    in_specs = [pl.BlockSpec((tm, tn), lambda i, j: (i, j))]
    for h in hs:
        in_specs.append(pl.BlockSpec((tm, h.shape[1]), lambda i, j: (i, 0)))
    for w in ws:
        in_specs.append(pl.BlockSpec((w.shape[0], tn), lambda i, j: (0, j)))
    return pl.pallas_call(
        functools.partial(_matmul_residual_kernel, n_in=len(hs)),
        out_shape=jax.ShapeDtypeStruct((n, d), F32),
        grid=(n // tm, d // tn),
        in_specs=in_specs,
        out_specs=pl.BlockSpec((tm, tn), lambda i, j: (i, j)),
        compiler_params=_params(("parallel", "arbitrary"), vmem),
        name="matmul_residual",
    )(x, *hs, *ws)


def _mem_attn_kernel(x_ref, g_ref, wq_ref, k_ref, v_ref, wo_ref, o_ref, att_ref, *, bt, tq, heads, hd):
    xf = x_ref[...]
    xn = _rms(xf, g_ref[...]).astype(MXU_DTYPE)
    q = _dot(xn, wq_ref[...]).astype(MXU_DTYPE)
    scale = hd ** -0.5
    for b in range(bt):
        for h in range(heads):
            qh = q[b * tq:(b + 1) * tq, h * hd:(h + 1) * hd]
            kh = k_ref[b, :, h * hd:(h + 1) * hd].astype(MXU_DTYPE)
            vh = v_ref[b, :, h * hd:(h + 1) * hd].astype(MXU_DTYPE)
            s = _dot_nt(qh, kh) * scale
            m = jnp.max(s, axis=-1, keepdims=True)
            p = jnp.exp(s - m)
            l = jnp.sum(p, axis=-1, keepdims=True)
            oh = _dot(p.astype(MXU_DTYPE), vh) / l
            att_ref[b * tq:(b + 1) * tq, h * hd:(h + 1) * hd] = oh.astype(att_ref.dtype)
    o_ref[...] = xf + _dot(att_ref[...], wo_ref[...])


def mem_attn(x, g, wq, mem_k, mem_v, wo, *, batch, heads):
    n, d = x.shape
    t = n // batch
    m, inner = mem_k.shape[1], mem_k.shape[2]
    hd = inner // heads
    if t >= 128:
        bt, tq = 1, _pick(t, (512, 256, 128))
    else:
        bt, tq = _pick(batch, (8, 4, 2, 1)), t
    nt = t // tq
    rows = bt * tq
    vmem = (2 * (2 * _nbytes((rows, d), F32) + 2 * _nbytes((bt, m, inner), mem_k.dtype)
                 + 2 * _nbytes((d, inner), wq.dtype))
            + 2 * _nbytes((rows, d), F32) + (8 << 20))
    return pl.pallas_call(
        functools.partial(_mem_attn_kernel, bt=bt, tq=tq, heads=heads, hd=hd),
        out_shape=jax.ShapeDtypeStruct((n, d), F32),
        grid=(batch // bt, nt),
        in_specs=[pl.BlockSpec((rows, d), lambda b, i: (b * nt + i, 0)),
                  pl.BlockSpec((1, d), lambda b, i: (0, 0)),
                  pl.BlockSpec((d, inner), lambda b, i: (0, 0)),
                  pl.BlockSpec((bt, m, inner), lambda b, i: (b, 0, 0)),
                  pl.BlockSpec((bt, m, inner), lambda b, i: (b, 0, 0)),
                  pl.BlockSpec((inner, d), lambda b, i: (0, 0))],
        out_specs=pl.BlockSpec((rows, d), lambda b, i: (b * nt + i, 0)),
        scratch_shapes=[pltpu.VMEM((rows, inner), MXU_DTYPE)],
        compiler_params=_params(("parallel", "arbitrary"), vmem),
        name="mem_attn",
    )(x, g.reshape(1, d), wq, mem_k, mem_v, wo)


def _mla_prep_kernel(*refs, q_lora, kv_lora, rope_dim, rope_slabs, qscale, heads, nope):
    expand = len(refs) == 12
    if expand:
        (p_ref, qn_ref, kvn_ref, wuq_ref, cos_ref, sin_ref, wukv_ref,
         q_ref, ckv_ref, kpe_ref, k_ref, v_ref) = refs
    else:
        p_ref, qn_ref, kvn_ref, wuq_ref, cos_ref, sin_ref, q_ref, ckv_ref, kpe_ref, k_ref = refs
    cos = cos_ref[...]
    sin = sin_ref[...]
    cqn = _rms(p_ref[:, 0:q_lora], qn_ref[...]).astype(MXU_DTYPE)
    q = _dot(cqn, wuq_ref[...])
    if qscale != 1.0:
        q = q * qscale
    q_ref[...] = q.astype(q_ref.dtype)
    for lo in rope_slabs:
        xs = q[:, lo:lo + V7X_LANES]
        q_ref[:, lo:lo + V7X_LANES] = (xs * cos + _swap32(xs) * sin).astype(q_ref.dtype)
    ckv = _rms(p_ref[:, q_lora:q_lora + kv_lora], kvn_ref[...])
    ckv_ref[...] = ckv
    kp = p_ref[:, q_lora + kv_lora:q_lora + kv_lora + V7X_LANES]
    kpr = kp * cos + _swap32(kp) * sin
    kpe_ref[...] = kpr[:, 0:rope_dim]
    kpr_m = kpr.astype(k_ref.dtype)
    if expand:
        kv = _dot(ckv.astype(MXU_DTYPE), wukv_ref[...])
        hw = nope + V7X_LANES
        for h in range(heads):
            k_ref[:, h * hw:h * hw + nope] = kv[:, h * nope:(h + 1) * nope].astype(k_ref.dtype)
            k_ref[:, h * hw + nope:(h + 1) * hw] = kpr_m
        v_ref[...] = kv[:, heads * nope:].astype(v_ref.dtype)
    else:
        k_ref[:, 0:kv_lora] = ckv.astype(k_ref.dtype)
        k_ref[:, kv_lora:kv_lora + V7X_LANES] = kpr_m


def mla_prep(proj, q_norm, kv_norm, w_uq, cos_rows, sin_rows, w_ukv, *, q_lora, kv_lora, rope_dim,
             rope_slabs, qscale, heads, nope):
    n = proj.shape[0]
    qcols = w_uq.shape[1]
    tm = _pick(n, (512, 256, 128))
    expand = w_ukv is not None
    row = lambda w: pl.BlockSpec((tm, w), lambda i: (i, 0))
    full = lambda a: pl.BlockSpec(a.shape, lambda i: (0, 0))
    in_specs = [row(proj.shape[1]), pl.BlockSpec((1, q_lora), lambda i: (0, 0)),
                pl.BlockSpec((1, kv_lora), lambda i: (0, 0)), full(w_uq), row(V7X_LANES), row(V7X_LANES)]
    args = [proj, q_norm.reshape(1, -1), kv_norm.reshape(1, -1), w_uq, cos_rows, sin_rows]
    out_shape = [jax.ShapeDtypeStruct((n, qcols), MXU_DTYPE), jax.ShapeDtypeStruct((n, kv_lora), F32),
                 jax.ShapeDtypeStruct((n, rope_dim), F32)]
    out_specs = [row(qcols), row(kv_lora), row(rope_dim)]
    vmem = (2 * (_nbytes((tm, proj.shape[1]), F32) + _nbytes(w_uq.shape, w_uq.dtype)
                 + _nbytes((tm, qcols), MXU_DTYPE) + 3 * _nbytes((tm, kv_lora + V7X_LANES), F32))
            + 3 * _nbytes((tm, qcols), F32) + (8 << 20))
    if expand:
        kcols = heads * (nope + V7X_LANES)
        vcols = w_ukv.shape[1] - heads * nope
        in_specs.append(full(w_ukv))
        args.append(w_ukv)
        out_shape += [jax.ShapeDtypeStruct((n, kcols), MXU_DTYPE), jax.ShapeDtypeStruct((n, vcols), MXU_DTYPE)]
        out_specs += [row(kcols), row(vcols)]
        vmem += (2 * (_nbytes(w_ukv.shape, w_ukv.dtype) + _nbytes((tm, kcols + vcols), MXU_DTYPE))
                 + 2 * _nbytes((tm, w_ukv.shape[1]), F32))
    else:
        out_shape.append(jax.ShapeDtypeStruct((n, kv_lora + V7X_LANES), MXU_DTYPE))
        out_specs.append(row(kv_lora + V7X_LANES))
    return pl.pallas_call(
        functools.partial(_mla_prep_kernel, q_lora=q_lora, kv_lora=kv_lora, rope_dim=rope_dim,
                          rope_slabs=rope_slabs, qscale=qscale, heads=heads, nope=nope),
        out_shape=tuple(out_shape),
        grid=(n // tm,),
        in_specs=in_specs,
        out_specs=tuple(out_specs),
        compiler_params=_params(("parallel",), vmem),
        name="mla_prep",
    )(*args)


def _lanes(x, width):
    reps = width // V7X_LANES
    return x if reps == 1 else jnp.concatenate([x] * reps, axis=-1)


def _mha_attn_kernel(qi_ref, kj_ref, flag_ref, q_ref, k_ref, v_ref, o_ref, acc_ref, m_ref,
                     *, heads, hw, vd, tq, tk, q_offset, kv_len):
    step = pl.program_id(1)
    i = qi_ref[step]
    j = kj_ref[step]
    flags = flag_ref[step]
    aw = vd + V7X_LANES

    @pl.when(j == 0)
    def _():
        m_ref[...] = jnp.full_like(m_ref, NEG)
        acc_ref[...] = jnp.zeros_like(acc_ref)

    ones = jnp.ones((tk, V7X_LANES), MXU_DTYPE)

    def all_heads(bias):
        for h in range(heads):
            s = _dot_nt(q_ref[0, :, h * hw:(h + 1) * hw], k_ref[0, :, h * hw:(h + 1) * hw])
            if bias is not None:
                s = s + bias
            m_old = m_ref[h]
            m_new = jnp.maximum(m_old, jnp.max(s, axis=-1, keepdims=True))
            alpha = jnp.exp2(m_old - m_new)
            p = jnp.exp2(s - _lanes(m_new, tk)).astype(MXU_DTYPE)
            m_ref[h] = m_new
            v_ext = jnp.concatenate([v_ref[0, :, h * vd:(h + 1) * vd], ones], axis=-1)
            acs = slice(h * aw, (h + 1) * aw)
            acc_ref[:, acs] = _lanes(alpha, aw) * acc_ref[:, acs] + _dot(p, v_ext)

    @pl.when((flags & 2) == 0)
    def _():
        all_heads(None)

    @pl.when((flags & 2) != 0)
    def _():
        qpos = q_offset + i * tq + lax.broadcasted_iota(jnp.int32, (tq, tk), 0)
        kpos = j * tk + lax.broadcasted_iota(jnp.int32, (tq, tk), 1)
        visible = ((kpos // CHUNK) <= (qpos // CHUNK)) & (kpos < kv_len)
        all_heads(jnp.where(visible, 0.0, NEG))

    @pl.when((flags & 1) != 0)
    def _():
        for h in range(heads):
            num = acc_ref[:, h * aw:h * aw + vd]
            den = _lanes(acc_ref[:, h * aw + vd:(h + 1) * aw], vd)
            o_ref[0, :, h * vd:(h + 1) * vd] = (num / den).astype(o_ref.dtype)


def mha_attn(q, k, v, *, heads, tq, tk, q_offset, kv_len):
    b, t, _ = q.shape
    s = k.shape[1]
    hw = q.shape[2] // heads
    vd = v.shape[2] // heads
    nq, nk = t // tq, s // tk
    qi, kj, flags = [], [], []
    for i in range(nq):
        first_q = q_offset + i * tq
        last_key = ((first_q + tq - 1) // CHUNK + 1) * CHUNK - 1
        jl = min(last_key // tk, nk - 1)
        for j in range(jl + 1):
            fully_visible = ((j + 1) * tk - 1) // CHUNK <= first_q // CHUNK and (j + 1) * tk <= kv_len
            qi.append(i)
            kj.append(j)
            flags.append(int(j == jl) + 2 * int(not fully_visible))
    sched = [jnp.asarray(a, jnp.int32) for a in (qi, kj, flags)]
    aw = vd + V7X_LANES
    vmem = (2 * (2 * _nbytes((tq, heads * hw), q.dtype) + 2 * _nbytes((tk, heads * vd), v.dtype)
                 + _nbytes((tq, heads * vd), MXU_DTYPE))
            + _nbytes((tq, heads * aw), F32) + _nbytes((heads, tq, V7X_LANES), F32)
            + 6 * _nbytes((tq, tk), F32) + (8 << 20))
    grid_spec = pltpu.PrefetchScalarGridSpec(
        num_scalar_prefetch=3,
        grid=(b, len(qi)),
        in_specs=[pl.BlockSpec((1, tq, heads * hw), lambda bb, p, qi_r, kj_r, l_r: (bb, qi_r[p], 0)),
                  pl.BlockSpec((1, tk, heads * hw), lambda bb, p, qi_r, kj_r, l_r: (bb, kj_r[p], 0)),
                  pl.BlockSpec((1, tk, heads * vd), lambda bb, p, qi_r, kj_r, l_r: (bb, kj_r[p], 0))],
        out_specs=pl.BlockSpec((1, tq, heads * vd), lambda bb, p, qi_r, kj_r, l_r: (bb, qi_r[p], 0)),
        scratch_shapes=[pltpu.VMEM((tq, heads * aw), F32),
                        pltpu.VMEM((heads, tq, V7X_LANES), F32)])
    return pl.pallas_call(
        functools.partial(_mha_attn_kernel, heads=heads, hw=hw, vd=vd, tq=tq, tk=tk,
                          q_offset=q_offset, kv_len=kv_len),
        out_shape=jax.ShapeDtypeStruct((b, t, heads * vd), MXU_DTYPE),
        grid_spec=grid_spec,
        compiler_params=_params(("parallel", "arbitrary"), vmem),
        name="mha_attn",
    )(*sched, q, k, v)


def _last_kv_block(i, *, tq, tk, q_offset, nk):
    last_q = q_offset + (i + 1) * tq - 1
    last_key = (last_q // CHUNK + 1) * CHUNK - 1
    return jnp.minimum(last_key // tk, nk - 1)


def _mla_attn_kernel(q_ref, k_ref, wuk_ref, wuv_ref, o_ref, qs_ref, acc_ref, m_ref, l_ref,
                     *, heads, hb, tq, tk, nope, lat, q_offset, kv_len, scale):
    i = pl.program_id(1)
    j = pl.program_id(2)
    nk = pl.num_programs(2)
    j_last = _last_kv_block(i, tq=tq, tk=tk, q_offset=q_offset, nk=nk)
    rope_lo = heads * nope

    @pl.when(j == 0)
    def _():
        for h in range(heads):
            qn = q_ref[0, :, h * nope:(h + 1) * nope]
            qs_ref[h * tq:(h + 1) * tq, 0:lat] = (_dot(qn, wuk_ref[h]) * scale).astype(qs_ref.dtype)
            qr = q_ref[0, :, rope_lo + h * V7X_LANES:rope_lo + (h + 1) * V7X_LANES]
            qs_ref[h * tq:(h + 1) * tq, lat:lat + V7X_LANES] = (qr.astype(F32) * scale).astype(qs_ref.dtype)
        m_ref[...] = jnp.full_like(m_ref, NEG)
        l_ref[...] = jnp.zeros_like(l_ref)
        acc_ref[...] = jnp.zeros_like(acc_ref)

    @pl.when(j <= j_last)
    def _():
        k = k_ref[0]
        v = k[:, 0:lat]
        qpos = q_offset + i * tq + lax.broadcasted_iota(jnp.int32, (tq, tk), 0)
        kpos = j * tk + lax.broadcasted_iota(jnp.int32, (tq, tk), 1)
        visible = ((kpos // CHUNK) <= (qpos // CHUNK)) & (kpos < kv_len)
        bias = jnp.where(visible, 0.0, NEG)
        rows = hb * tq
        for rb in range(heads // hb):
            rs = slice(rb * rows, (rb + 1) * rows)
            s = _dot_nt(qs_ref[rs, :], k)
            s = (s.reshape(hb, tq, tk) + bias[None]).reshape(rows, tk)
            m_old = m_ref[rs, :]
            m_new = jnp.maximum(m_old, jnp.max(s, axis=-1, keepdims=True))
            alpha = jnp.exp(m_old - m_new)
            p = jnp.exp(s - _lanes(m_new, tk))
            l_ref[rs, :] = alpha * l_ref[rs, :] + jnp.sum(p, axis=-1, keepdims=True)
            m_ref[rs, :] = m_new
            acc_ref[rs, :] = _lanes(alpha, lat) * acc_ref[rs, :] + _dot(p.astype(MXU_DTYPE), v)

    @pl.when(j == j_last)
    def _():
        vd = wuv_ref.shape[2]
        for h in range(heads):
            hs = slice(h * tq, (h + 1) * tq)
            ol = acc_ref[hs, :] / _lanes(l_ref[hs, :], lat)
            o_ref[0, :, h * vd:(h + 1) * vd] = _dot(ol.astype(MXU_DTYPE), wuv_ref[h]).astype(o_ref.dtype)


def mla_attn(q, kcat, w_uk, w_uv, *, tq, tk, q_offset, kv_len, scale):
    b, t, _ = q.shape
    s = kcat.shape[1]
    heads, nope, lat = w_uk.shape
    vd = w_uv.shape[2]
    nq, nk = t // tq, s // tk
    rows = heads * tq
    hb = max(1, min(heads, MLA_ROW_BLOCK // tq))
    assert heads % hb == 0
    last = functools.partial(_last_kv_block, tq=tq, tk=tk, q_offset=q_offset, nk=nk)
    vmem = (2 * (_nbytes((tq, q.shape[2]), q.dtype) + _nbytes((tk, kcat.shape[2]), kcat.dtype)
                 + 2 * _nbytes(w_uk.shape, w_uk.dtype) + _nbytes((tq, heads * vd), MXU_DTYPE))
            + _nbytes((rows, lat + V7X_LANES), MXU_DTYPE) + _nbytes((rows, lat), F32)
            + 2 * _nbytes((rows, V7X_LANES), F32) + 4 * _nbytes((rows, tk), F32) + (8 << 20))
    return pl.pallas_call(
        functools.partial(_mla_attn_kernel, heads=heads, hb=hb, tq=tq, tk=tk, nope=nope, lat=lat,
                          q_offset=q_offset, kv_len=kv_len, scale=scale),
        out_shape=jax.ShapeDtypeStruct((b, t, heads * vd), MXU_DTYPE),
        grid=(b, nq, nk),
        in_specs=[pl.BlockSpec((1, tq, q.shape[2]), lambda bb, i, j: (bb, i, 0)),
                  pl.BlockSpec((1, tk, kcat.shape[2]), lambda bb, i, j: (bb, jnp.minimum(j, last(i)), 0)),
                  pl.BlockSpec(w_uk.shape, lambda bb, i, j: (0, 0, 0)),
                  pl.BlockSpec(w_uv.shape, lambda bb, i, j: (0, 0, 0))],
        out_specs=pl.BlockSpec((1, tq, heads * vd), lambda bb, i, j: (bb, i, 0)),
        scratch_shapes=[pltpu.VMEM((rows, lat + V7X_LANES), MXU_DTYPE),
                        pltpu.VMEM((rows, lat), F32),
                        pltpu.VMEM((rows, V7X_LANES), F32),
                        pltpu.VMEM((rows, V7X_LANES), F32)],
        compiler_params=_params(("parallel", "parallel", "arbitrary"), vmem),
        name="mla_attn",
    )(q, kcat, w_uk, w_uv)


class _ScanPlan(NamedTuple):
    body: Callable
    args: list
    in_specs: list
    out_shape: list
    out_specs: list
    scratch: list
    vmem: int


def _ssd_kernel(*refs, L, groups, hpg, hdim, nstate, has_state):
    if has_state:
        (z_ref, xs_ref, b_ref, c_ref, dt_ref, cw_ref, cb_ref, dtb_ref, alog_ref, dsk_ref, nrm_ref,
         cst_ref, hst_ref, y_ref, ncv_ref, nst_ref, xbuf, h_scr, cum_scr, xdt_scr, yin_scr) = refs
    else:
        (z_ref, xs_ref, b_ref, c_ref, dt_ref, cw_ref, cb_ref, dtb_ref, alog_ref, dsk_ref, nrm_ref,
         cst_ref, y_ref, ncv_ref, nst_ref, xbuf, h_scr, cum_scr, xdt_scr, yin_scr) = refs
        hst_ref = None
    c = pl.program_id(1)
    nc = pl.num_programs(1)
    dx = groups * hpg * hdim
    dn = groups * nstate
    gw = hpg * hdim
    pad = CONV_PAD_ROWS

    @pl.when(c == 0)
    def _():
        xbuf[0:pad, :] = cst_ref[0]
        if has_state:
            h_scr[...] = hst_ref[0]
        else:
            h_scr[...] = jnp.zeros_like(h_scr)

    xbuf[pad:pad + L, 0:dx] = xs_ref[...].astype(F32)
    xbuf[pad:pad + L, dx:dx + dn] = b_ref[...].astype(F32)
    xbuf[pad:pad + L, dx + dn:dx + 2 * dn] = c_ref[...].astype(F32)
    acc = xbuf[pad - CONV_K + 1:pad - CONV_K + 1 + L, :] * cw_ref[0:1, :] + cb_ref[...]
    for jj in range(1, CONV_K):
        lo = pad - CONV_K + 1 + jj
        acc = acc + xbuf[lo:lo + L, :] * cw_ref[jj:jj + 1, :]
    xc = _silu(acc)
    tail = xbuf[L:L + pad, :]
    xbuf[0:pad, :] = tail

    @pl.when(c == nc - 1)
    def _():
        ncv_ref[0] = tail

    dtr = dt_ref[...] + dtb_ref[...]
    dt = jnp.maximum(dtr, 0.0) + jnp.log1p(jnp.exp(-jnp.abs(dtr)))
    la = dt * (-jnp.exp(alog_ref[...]))
    row = lax.broadcasted_iota(jnp.int32, (L, L), 0)
    col = lax.broadcasted_iota(jnp.int32, (L, L), 1)
    causal = row >= col
    tri = jnp.where(causal, 1.0, 0.0).astype(MXU_DTYPE)
    cum = sum(_dot(tri, piece) for piece in _split3(la))
    nh = groups * hpg
    eye = jnp.where(lax.broadcasted_iota(jnp.int32, (V7X_LANES, V7X_LANES), 0)
                    == lax.broadcasted_iota(jnp.int32, (V7X_LANES, V7X_LANES), 1), 1.0, 0.0).astype(MXU_DTYPE)
    cum_t = sum(_dot_nt(eye, piece) for piece in _split3(cum))

    half = lax.broadcasted_iota(jnp.int32, (L, V7X_LANES), 1) < hdim
    per_vreg = V7X_LANES // hdim
    for g in range(groups):
        bg = xc[:, dx + g * nstate:dx + (g + 1) * nstate]
        cg = xc[:, dx + dn + g * nstate:dx + dn + (g + 1) * nstate].astype(MXU_DTYPE)
        qk = _dot_nt(cg, bg.astype(MXU_DTYPE))
        for sl in range(gw // V7X_LANES):
            lane0 = g * gw + sl * V7X_LANES
            h0 = lane0 // hdim
            cb = [jnp.broadcast_to(cum[:, h0 + u:h0 + u + 1], (L, V7X_LANES)) for u in range(per_vreg)]
            db = [jnp.broadcast_to(dt[:, h0 + u:h0 + u + 1], (L, V7X_LANES)) for u in range(per_vreg)]
            cum_e = jnp.where(half, cb[0], cb[1])
            dt_e = jnp.where(half, db[0], db[1])
            xdt = xc[:, lane0:lane0 + V7X_LANES] * dt_e
            xdt_m = xdt.astype(MXU_DTYPE)
            ys = []
            for u in range(per_vreg):
                seg = cb[u][:, 0:L] - cum_t[h0 + u:h0 + u + 1, :]
                decay = jnp.exp(jnp.where(causal, seg, NEG))
                ys.append(_dot((qk * decay).astype(MXU_DTYPE), xdt_m))
            cum_scr[:, lane0:lane0 + V7X_LANES] = cum_e
            xdt_scr[:, lane0:lane0 + V7X_LANES] = xdt
            yin_scr[:, lane0:lane0 + V7X_LANES] = jnp.where(half, ys[0], ys[1])

    for g in range(groups):
        gs = slice(g * gw, (g + 1) * gw)
        bg = xc[:, dx + g * nstate:dx + (g + 1) * nstate].astype(MXU_DTYPE)
        cg = xc[:, dx + dn + g * nstate:dx + dn + (g + 1) * nstate].astype(MXU_DTYPE)
        cum_g = cum_scr[:, gs]
        last = cum_scr[L - 1:L, gs]
        hg = h_scr[g]
        y_inter = _dot(cg, hg.astype(MXU_DTYPE)) * jnp.exp(cum_g)
        wx = (jnp.exp(last - cum_g) * xdt_scr[:, gs]).astype(MXU_DTYPE)
        h_scr[g] = jnp.exp(last) * hg + _dot_tn(bg, wx)
        y = yin_scr[:, gs] + y_inter + dsk_ref[:, gs] * xc[:, gs]
        y = y * _silu(z_ref[:, gs].astype(F32))
        y_ref[:, gs] = (_rms(y) * nrm_ref[:, gs]).astype(y_ref.dtype)

    @pl.when(c == nc - 1)
    def _():
        nst_ref[0] = h_scr[...]


def ssd_scan(proj, dt_raw, col, conv_w, conv_b, dt_bias, a_log, d_skip_e, ssd_norm, conv_state, ssd_state,
             *, batch, L, groups, hpg, hdim, nstate):
    n = proj.shape[0]
    t = n // batch
    nc = t // L
    dx = groups * hpg * hdim
    dn = groups * nstate
    cdim = dx + 2 * dn
    has_state = ssd_state is not None
    rowmap = lambda b, c: b * nc + c
    in_specs = [pl.BlockSpec((L, dx), lambda b, c: (rowmap(b, c), col["z"])),
                pl.BlockSpec((L, dx), lambda b, c: (rowmap(b, c), col["xs"])),
                pl.BlockSpec((L, dn), lambda b, c: (rowmap(b, c), col["B"])),
                pl.BlockSpec((L, dn), lambda b, c: (rowmap(b, c), col["C"])),
                pl.BlockSpec((L, V7X_LANES), lambda b, c: (rowmap(b, c), 0)),
                pl.BlockSpec((CONV_K, cdim), lambda b, c: (0, 0)),
                pl.BlockSpec((1, cdim), lambda b, c: (0, 0)),
                pl.BlockSpec((1, V7X_LANES), lambda b, c: (0, 0)),
                pl.BlockSpec((1, V7X_LANES), lambda b, c: (0, 0)),
                pl.BlockSpec((1, dx), lambda b, c: (0, 0)),
                pl.BlockSpec((1, dx), lambda b, c: (0, 0)),
                pl.BlockSpec((1, CONV_PAD_ROWS, cdim), lambda b, c: (b, 0, 0))]
    args = [proj, proj, proj, proj, dt_raw, conv_w, conv_b.reshape(1, cdim), dt_bias, a_log,
            d_skip_e, ssd_norm.reshape(1, dx), conv_state]
    if has_state:
        in_specs.append(pl.BlockSpec((1, groups, nstate, hpg * hdim), lambda b, c: (b, 0, 0, 0)))
        args.append(ssd_state)
    state_bytes = _nbytes((groups, nstate, hpg * hdim), F32)
    vmem = (2 * (2 * _nbytes((L, dx), F32) + 2 * _nbytes((L, dn), F32) + _nbytes((L, dx), MXU_DTYPE)
                 + 2 * _nbytes((CONV_PAD_ROWS, cdim), F32) + 2 * state_bytes)
            + state_bytes + 12 * _nbytes((L + CONV_PAD_ROWS, cdim), F32) + (8 << 20))
    return _ScanPlan(
        body=functools.partial(_ssd_kernel, L=L, groups=groups, hpg=hpg, hdim=hdim, nstate=nstate,
                               has_state=has_state),
        args=args, in_specs=in_specs,
        out_shape=[jax.ShapeDtypeStruct((n, dx), MXU_DTYPE),
                   jax.ShapeDtypeStruct((batch, CONV_PAD_ROWS, cdim), F32),
                   jax.ShapeDtypeStruct((batch, groups, nstate, hpg * hdim), F32)],
        out_specs=[pl.BlockSpec((L, dx), lambda b, c: (rowmap(b, c), 0)),
                   pl.BlockSpec((1, CONV_PAD_ROWS, cdim), lambda b, c: (b, 0, 0)),
                   pl.BlockSpec((1, groups, nstate, hpg * hdim), lambda b, c: (b, 0, 0, 0))],
        scratch=[pltpu.VMEM((L + CONV_PAD_ROWS, cdim), F32),
                 pltpu.VMEM((groups, nstate, hpg * hdim), F32),
                 pltpu.VMEM((L, dx), F32),
                 pltpu.VMEM((L, dx), F32),
                 pltpu.VMEM((L, dx), F32)],
        vmem=vmem)


def _ret_kernel(*refs, L, heads, hd, has_state):
    if has_state:
        (q_ref, k_ref, v_ref, gate_ref, cos_ref, sin_ref, dec_ref, ecum_ref, wv_ref, sdec_ref, st_ref,
         o_ref, nst_ref, s_scr) = refs
    else:
        (q_ref, k_ref, v_ref, gate_ref, cos_ref, sin_ref, dec_ref, ecum_ref, wv_ref, sdec_ref,
         o_ref, nst_ref, s_scr) = refs
        st_ref = None
    c = pl.program_id(1)
    nc = pl.num_programs(1)
    half = hd // 2

    @pl.when(c == 0)
    def _():
        if has_state:
            s_scr[...] = st_ref[0]
        else:
            s_scr[...] = jnp.zeros_like(s_scr)

    cos = cos_ref[...]
    sin = sin_ref[...]

    def rope(ref, h, mult):
        x1 = ref[:, h * hd:h * hd + half].astype(F32)
        x2 = ref[:, h * hd + half:(h + 1) * hd].astype(F32)
        out = jnp.concatenate([x1 * cos - x2 * sin, x2 * cos + x1 * sin], axis=-1)
        return (out * mult).astype(MXU_DTYPE) if mult != 1.0 else out.astype(MXU_DTYPE)

    for h in range(heads):
        hs = slice(h * hd, (h + 1) * hd)
        qr = rope(q_ref, h, 1.0)
        kr = rope(k_ref, h, hd ** -0.5)
        vf = v_ref[:, hs]
        qk = _dot_nt(qr, kr)
        y = _dot((qk * dec_ref[h]).astype(MXU_DTYPE), vf.astype(MXU_DTYPE))
        s_old = s_scr[h]
        y = y + _dot(qr, s_old.astype(MXU_DTYPE)) * ecum_ref[h]
        s_scr[h] = sdec_ref[h] * s_old + _dot_tn(kr, (wv_ref[h] * vf.astype(F32)).astype(MXU_DTYPE))
        o_ref[:, hs] = (_rms(y) * _silu(gate_ref[:, hs].astype(F32))).astype(o_ref.dtype)

    @pl.when(c == nc - 1)
    def _():
        nst_ref[0] = s_scr[...]


def ret_scan(proj, col, cos_rows, sin_rows, ret_state, *, batch, L, heads, hd):
    n = proj.shape[0]
    t = n // batch
    nc = t // L
    inner = heads * hd
    has_state = ret_state is not None
    lg = jnp.log1p(-jnp.exp2(-5.0 - jnp.arange(heads, dtype=F32)))[:, None, None]
    li = jnp.arange(L, dtype=F32)
    diff = li[:, None] - li[None, :]
    dec = jnp.where(diff >= 0, jnp.exp(jnp.where(diff >= 0, diff, 0.0)[None] * lg), 0.0)
    ecum = jnp.broadcast_to(jnp.exp((li[None, :, None] + 1.0) * lg), (heads, L, hd))
    wv = jnp.broadcast_to(jnp.exp((L - 1.0 - li[None, :, None]) * lg), (heads, L, hd))
    sdec = jnp.broadcast_to(jnp.exp(L * lg), (heads, 1, hd))
    rowmap = lambda b, c: b * nc + c
    in_specs = [pl.BlockSpec((L, inner), lambda b, c: (rowmap(b, c), col["q"])),
                pl.BlockSpec((L, inner), lambda b, c: (rowmap(b, c), col["k"])),
                pl.BlockSpec((L, inner), lambda b, c: (rowmap(b, c), col["v"])),
                pl.BlockSpec((L, inner), lambda b, c: (rowmap(b, c), col["gate"])),
                pl.BlockSpec((L, hd // 2), lambda b, c: (rowmap(b, c), 0)),
                pl.BlockSpec((L, hd // 2), lambda b, c: (rowmap(b, c), 0)),
                pl.BlockSpec((heads, L, L), lambda b, c: (0, 0, 0)),
                pl.BlockSpec((heads, L, hd), lambda b, c: (0, 0, 0)),
                pl.BlockSpec((heads, L, hd), lambda b, c: (0, 0, 0)),
                pl.BlockSpec((heads, 1, hd), lambda b, c: (0, 0, 0))]
    args = [proj, proj, proj, proj, cos_rows, sin_rows, dec, ecum, wv, sdec]
    if has_state:
        in_specs.append(pl.BlockSpec((1, heads, hd, hd), lambda b, c: (b, 0, 0, 0)))
        args.append(ret_state)
    state_bytes = _nbytes((heads, hd, hd), F32)
    vmem = (2 * (4 * _nbytes((L, inner), F32) + _nbytes((L, inner), MXU_DTYPE) + _nbytes((heads, L, L), F32)
                 + 2 * _nbytes((heads, L, hd), F32) + 2 * state_bytes)
            + state_bytes + 16 * _nbytes((L, hd), F32) + (8 << 20))
    return _ScanPlan(
        body=functools.partial(_ret_kernel, L=L, heads=heads, hd=hd, has_state=has_state),
        args=args, in_specs=in_specs,
        out_shape=[jax.ShapeDtypeStruct((n, inner), MXU_DTYPE),
                   jax.ShapeDtypeStruct((batch, heads, hd, hd), F32)],
        out_specs=[pl.BlockSpec((L, inner), lambda b, c: (rowmap(b, c), 0)),
                   pl.BlockSpec((1, heads, hd, hd), lambda b, c: (b, 0, 0, 0))],
        scratch=[pltpu.VMEM((heads, hd, hd), F32)],
        vmem=vmem)


def _scan_pair_kernel(*refs, bodies, n_in, n_out, n_scr):
    ins, outs, scrs = refs[:sum(n_in)], refs[sum(n_in):sum(n_in) + sum(n_out)], refs[sum(n_in) + sum(n_out):]
    for k, body in enumerate(bodies):
        take = lambda seq, counts: seq[sum(counts[:k]):sum(counts[:k + 1])]
        body(*take(ins, n_in), *take(outs, n_out), *take(scrs, n_scr))


def run_scans(plans, *, batch, nc, name):
    outs = pl.pallas_call(
        functools.partial(_scan_pair_kernel, bodies=tuple(p.body for p in plans),
                          n_in=tuple(len(p.args) for p in plans),
                          n_out=tuple(len(p.out_shape) for p in plans),
                          n_scr=tuple(len(p.scratch) for p in plans)),
        out_shape=tuple(s for p in plans for s in p.out_shape),
        grid=(batch, nc),
        in_specs=[s for p in plans for s in p.in_specs],
        out_specs=tuple(s for p in plans for s in p.out_specs),
        scratch_shapes=[s for p in plans for s in p.scratch],
        compiler_params=_params(("parallel", "arbitrary"), sum(p.vmem for p in plans)),
        name=name,
    )(*[a for p in plans for a in p.args])
    split, k = [], 0
    for p in plans:
        split.append(outs[k:k + len(p.out_shape)])
        k += len(p.out_shape)
    return split


def _rope_tables(pos, half, reps, batch):
    inv = ROPE_BASE ** (-jnp.arange(half, dtype=F32) / half)
    ang = pos.astype(F32)[:, None] * inv[None, :]
    cos, sin = jnp.cos(ang), jnp.sin(ang)
    if reps == 0:
        return jnp.tile(cos, (batch, 1)), jnp.tile(sin, (batch, 1))
    c = jnp.tile(jnp.concatenate([cos, cos], axis=-1), (batch, reps))
    s = jnp.tile(jnp.concatenate([-sin, sin], axis=-1), (batch, reps))
    return c, s


def _mixer_even(x, batch, pos, norm_g, prm, conv_state, ssd_state, ret_state, dims, L_ssd, L_ret):
    d = x.shape[1]
    groups, hpg, hdim, nstate, rheads, rhd = dims
    proj, dt_raw = rms_matmul(x, norm_g, prm["w_in"], out_dtype=MXU_DTYPE, w_side=prm["w_dt"])
    col = prm["col"]
    ssd_plan = ssd_scan(proj, dt_raw, col, prm["conv_w"], prm["conv_b"], prm["dt_bias"],
                        prm["a_log"], prm["d_skip_e"], prm["ssd_norm"], conv_state, ssd_state,
                        batch=batch, L=L_ssd, groups=groups, hpg=hpg, hdim=hdim, nstate=nstate)
    cos_rows, sin_rows = _rope_tables(pos, rhd // 2, 0, batch)
    ret_plan = ret_scan(proj, col, cos_rows, sin_rows, ret_state, batch=batch, L=L_ret, heads=rheads, hd=rhd)
    t = x.shape[0] // batch
    (y, new_conv, new_ssd), = run_scans([ssd_plan], batch=batch, nc=t // L_ssd, name="ssd_scan")
    (o, new_ret), = run_scans([ret_plan], batch=batch, nc=t // L_ret, name="ret_scan")
    x = matmul_residual(x, [y, o], [prm["w_out_ssd"], prm["w_out_ret"]])
    return x, (new_conv, new_ssd, new_ret)


def _prep_even(w_in, conv_w, conv_b, dt_bias, a_log, d_skip, ssd_norm, w_out, dims):
    groups, hpg, hdim, nstate, rheads, rhd = dims
    d = w_in.shape[0]
    dx = groups * hpg * hdim
    dn = groups * nstate
    nh = groups * hpg
    ri = rheads * rhd
    o_z, o_xs, o_b, o_c, o_dt = 0, dx, 2 * dx, 2 * dx + dn, 2 * dx + 2 * dn
    o_q = o_dt + nh
    seg = lambda lo, w: w_in[:, lo:lo + w]
    dt_pad = jnp.zeros((d, V7X_LANES - nh), w_in.dtype)
    w_main = jnp.concatenate([seg(o_z, dx), seg(o_q, ri), seg(o_q + ri, ri), seg(o_q + 2 * ri, ri),
                              seg(o_q + 3 * ri, ri), seg(o_xs, dx), seg(o_b, dn), seg(o_c, dn)],
                             axis=1).astype(MXU_DTYPE)
    w_dt = jnp.concatenate([seg(o_dt, nh), dt_pad], axis=1).astype(MXU_DTYPE)
    assert dx == ri and dx % dn == 0 and dn % V7X_LANES == 0
    col = {"z": 0, "q": 1, "k": 2, "v": 3, "gate": 4, "xs": 5, "B": 6 * dx // dn, "C": 6 * dx // dn + 1}
    pad1 = lambda v: jnp.pad(v.astype(F32), (0, V7X_LANES - nh)).reshape(1, V7X_LANES)
    return {"w_in": w_main, "w_dt": w_dt, "col": col, "conv_w": conv_w, "conv_b": conv_b,
            "dt_bias": pad1(dt_bias), "a_log": pad1(a_log),
            "d_skip_e": jnp.repeat(d_skip, hdim).reshape(1, dx), "ssd_norm": ssd_norm,
            "w_out_ssd": w_out[:dx].astype(MXU_DTYPE), "w_out_ret": w_out[dx:].astype(MXU_DTYPE)}


def _prep_odd(w_in, q_norm, kv_norm, w_uq, w_uk, w_uv, w_out, rope_dim):
    d = w_in.shape[0]
    q_lora = q_norm.shape[0]
    kv_lora, heads, nope = w_uk.shape
    kp = w_in[:, q_lora + kv_lora:]
    w_in_new = jnp.concatenate([w_in[:, :q_lora + kv_lora], kp, kp], axis=1).astype(MXU_DTYPE)
    assert 2 * rope_dim == V7X_LANES
    wq = w_uq.reshape(q_lora, heads, nope + rope_dim)
    wq_nope = wq[:, :, :nope]
    wq_rope = jnp.pad(wq[:, :, nope:], ((0, 0), (0, 0), (0, V7X_LANES - rope_dim)))
    hw = nope + V7X_LANES
    return {"w_in": w_in_new, "q_norm": q_norm, "kv_norm": kv_norm,
            "w_uq_lat": jnp.concatenate([wq_nope.reshape(q_lora, heads * nope),
                                         wq_rope.reshape(q_lora, heads * V7X_LANES)], axis=1).astype(MXU_DTYPE),
            "slabs_lat": tuple(heads * nope + h * V7X_LANES for h in range(heads)),
            "w_uq_head": jnp.concatenate([wq_nope, wq_rope], axis=2).reshape(q_lora, heads * hw).astype(MXU_DTYPE),
            "slabs_head": tuple(h * hw + nope for h in range(heads)),
            "w_ukv": jnp.concatenate([w_uk.reshape(kv_lora, heads * nope),
                                      w_uv.reshape(kv_lora, -1)], axis=1).astype(MXU_DTYPE),
            "w_uk": jnp.transpose(w_uk, (1, 2, 0)).astype(MXU_DTYPE),
            "w_uv": jnp.transpose(w_uv, (1, 0, 2)).astype(MXU_DTYPE),
            "w_out": w_out.astype(MXU_DTYPE)}


def _mixer_odd(x, batch, pos, norm_g, prm, past_kcat, rope_dim, q_offset):
    n, d = x.shape
    t = n // batch
    q_lora = prm["q_norm"].shape[0]
    kv_lora = prm["kv_norm"].shape[0]
    heads, nope, lat = prm["w_uk"].shape
    proj = rms_matmul(x, norm_g, prm["w_in"])
    cos_rows, sin_rows = _rope_tables(pos, rope_dim // 2, V7X_LANES // rope_dim, batch)
    scale = (nope + rope_dim) ** -0.5
    dims = dict(q_lora=q_lora, kv_lora=kv_lora, rope_dim=rope_dim, heads=heads, nope=nope)
    if past_kcat is None:
        q, ckv, kpe, k, v = mla_prep(proj, prm["q_norm"], prm["kv_norm"], prm["w_uq_head"], cos_rows, sin_rows,
                                     prm["w_ukv"], rope_slabs=prm["slabs_head"], qscale=scale * math.log2(math.e),
                                     **dims)
        tq = tk = _pick(t, (512, 256, 128, 64))
        shp = lambda a: a.reshape(batch, t, a.shape[1])
        o = mha_attn(shp(q), shp(k), shp(v), heads=heads, tq=tq, tk=tk, q_offset=q_offset, kv_len=t)
    else:
        q, ckv, kpe, kcat = mla_prep(proj, prm["q_norm"], prm["kv_norm"], prm["w_uq_lat"], cos_rows, sin_rows,
                                     None, rope_slabs=prm["slabs_lat"], qscale=1.0, **dims)
        q = q.reshape(batch, t, q.shape[1])
        kcat = kcat.reshape(batch, t, kcat.shape[1])
        kv_len = past_kcat.shape[1] + t
        padded = -(-kv_len // V7X_LANES) * V7X_LANES
        keys = jnp.concatenate([past_kcat, kcat, jnp.zeros((batch, padded - kv_len, kcat.shape[2]), kcat.dtype)], axis=1)
        o = mla_attn(q, keys, prm["w_uk"], prm["w_uv"], tq=t, tk=padded, q_offset=q_offset, kv_len=kv_len,
                     scale=scale)
    x = matmul_residual(x, [o.reshape(n, o.shape[2])], [prm["w_out"]])
    return x, (ckv.reshape(batch, t, kv_lora), kpe.reshape(batch, t, rope_dim))


def kernel(x_prompt, x_sample, mem_prompt, state_conv, state_ssd, state_ret, cache_ckv, cache_kpe,
           cache_mem_k, cache_mem_v, norms, ffn_w1, ffn_w2, mem_norm, w_mq, w_mkv, w_mo,
           ab_w_in, ab_conv_w, ab_conv_b, ab_dt_bias, ab_a_log, ab_d_skip, ab_ssd_norm, ab_w_out,
           c_w_in, c_q_norm, c_kv_norm, c_w_uq, c_w_uk, c_w_uv, c_w_out, final_norm):
    bp, tp, d = x_prompt.shape
    bs, ts, _ = x_sample.shape
    depth = norms.shape[0]
    assert depth >= 1
    past_len = cache_ckv.shape[2]
    mem_tokens = mem_prompt.shape[1]
    mem_heads, mem_hd = cache_mem_k.shape[3], cache_mem_k.shape[4]
    mem_inner = mem_heads * mem_hd
    ssd_heads, nstate, hdim = state_ssd.shape[2], state_ssd.shape[3], state_ssd.shape[4]
    cdim = state_conv.shape[3]
    groups = (cdim - ssd_heads * hdim) // (2 * nstate)
    hpg = ssd_heads // groups
    rheads, rhd = state_ret.shape[2], state_ret.shape[3]
    dims = (groups, hpg, hdim, nstate, rheads, rhd)
    rope_dim = cache_kpe.shape[3]

    pos_p = jnp.arange(tp)
    pos_s = past_len + jnp.arange(ts)
    xp = x_prompt.reshape(bp * tp, d)
    xs = x_sample.reshape(bs * ts, d)
    L_ssd_p, L_ret_p = _pick(tp, (128, 64)), _pick(tp, (256, 128, 64))
    L_s = _pick(ts, (128, 64))

    outs = {k: [] for k in ("conv_p", "ssd_p", "ret_p", "ckv_p", "kpe_p", "memk_p", "memv_p",
                            "conv_s", "ssd_s", "ret_s", "ckv_s", "kpe_s")}

    def to_group_layout(st):
        b = st.shape[0]
        return st.reshape(b, groups, hpg, nstate, hdim).transpose(0, 1, 3, 2, 4).reshape(b, groups, nstate, hpg * hdim)

    def from_group_layout(st):
        b = st.shape[0]
        return st.reshape(b, groups, nstate, hpg, hdim).transpose(0, 1, 3, 2, 4).reshape(b, ssd_heads, nstate, hdim)

    w1 = ffn_w1.astype(MXU_DTYPE)
    w2 = (0.5 * ffn_w2).astype(MXU_DTYPE)
    for i in range(depth):
        j = i // 2
        closing = final_norm if i == depth - 1 else None
        wq_m = w_mq[i].astype(MXU_DTYPE)
        wo_m = w_mo[i].astype(MXU_DTYPE)
        mkv = rms_matmul(mem_prompt.reshape(bp * mem_tokens, d), mem_norm[i], w_mkv[i].astype(MXU_DTYPE))
        mk_p = mkv[:, :mem_inner].reshape(bp, mem_tokens, mem_inner)
        mv_p = mkv[:, mem_inner:].reshape(bp, mem_tokens, mem_inner)
        outs["memk_p"].append(mk_p.reshape(bp, mem_tokens, mem_heads, mem_hd))
        outs["memv_p"].append(mv_p.reshape(bp, mem_tokens, mem_heads, mem_hd))

        xp = ffn(xp, norms[i, 0], w1, w2, i, 0)
        xs = ffn(xs, norms[i, 0], w1, w2, i, 0)
        if i % 2 == 0:
            prm = _prep_even(ab_w_in[j], ab_conv_w[j], ab_conv_b[j], ab_dt_bias[j], ab_a_log[j],
                             ab_d_skip[j], ab_ssd_norm[j], ab_w_out[j], dims)
            zero_conv = jnp.zeros((bp, CONV_PAD_ROWS, cdim), F32)
            xp, st_p = _mixer_even(xp, bp, pos_p, norms[i, 1], prm, zero_conv, None, None, dims, L_ssd_p, L_ret_p)
            conv_in = jnp.pad(state_conv[j], ((0, 0), (CONV_PAD_ROWS - (CONV_K - 1), 0), (0, 0)))
            xs, st_s = _mixer_even(xs, bs, pos_s, norms[i, 1], prm, conv_in, to_group_layout(state_ssd[j]),
                                   state_ret[j], dims, L_s, L_s)
            for tag, st in (("p", st_p), ("s", st_s)):
                outs["conv_" + tag].append(st[0][:, CONV_PAD_ROWS - (CONV_K - 1):, :])
                outs["ssd_" + tag].append(from_group_layout(st[1]))
                outs["ret_" + tag].append(st[2])
        else:
            prm = _prep_odd(c_w_in[j], c_q_norm[j], c_kv_norm[j], c_w_uq[j], c_w_uk[j], c_w_uv[j], c_w_out[j],
                            rope_dim)
            xp, st_p = _mixer_odd(xp, bp, pos_p, norms[i, 1], prm, None, rope_dim, 0)
            past = jnp.concatenate([cache_ckv[j], cache_kpe[j], cache_kpe[j]], axis=-1).astype(MXU_DTYPE)
            xs, st_s = _mixer_odd(xs, bs, pos_s, norms[i, 1], prm, past, rope_dim, past_len)
            for tag, st in (("p", st_p), ("s", st_s)):
                outs["ckv_" + tag].append(st[0])
                outs["kpe_" + tag].append(st[1])
        xp = mem_attn(xp, norms[i, 2], wq_m, mk_p, mv_p, wo_m, batch=bp, heads=mem_heads)
        xs = mem_attn(xs, norms[i, 2], wq_m, cache_mem_k[i].reshape(bs, mem_tokens, mem_inner),
                      cache_mem_v[i].reshape(bs, mem_tokens, mem_inner), wo_m, batch=bs, heads=mem_heads)
        xp = ffn(xp, norms[i, 3], w1, w2, i, 1, final_g=closing)
        xs = ffn(xs, norms[i, 3], w1, w2, i, 1, final_g=closing)

    y_prompt = xp.reshape(bp, tp, d)
    y_sample = xs.reshape(bs, ts, d)
    st = lambda k: jnp.stack(outs[k])
    return (y_prompt, y_sample, st("conv_p"), st("ssd_p"), st("ret_p"), st("ckv_p"), st("kpe_p"),
            st("memk_p"), st("memv_p"), st("conv_s"), st("ssd_s"), st("ret_s"), st("ckv_s"), st("kpe_s"))
```

```python
import functools
import math
from typing import Callable, NamedTuple

import jax
import jax.numpy as jnp
from jax import lax
from jax.experimental import pallas as pl
from jax.experimental.pallas import tpu as pltpu

F32 = jnp.float32
MXU_DTYPE = jnp.bfloat16

EPS = 1e-6
CHUNK = 64
ROPE_BASE = 10000.0
CONV_K = 4
NEG = -1e30

V7X_VMEM_BYTES = 64 * 1024 * 1024
TILE_VMEM_BUDGET = 48 * 1024 * 1024
V7X_LANES = 128
CONV_PAD_ROWS = 8
MLA_ROW_BLOCK = 512


def _params(semantics, vmem_bytes):
    limit = min(int(vmem_bytes), V7X_VMEM_BYTES - (4 << 20))
    return pltpu.CompilerParams(dimension_semantics=semantics, vmem_limit_bytes=limit)


def _nbytes(shape, dtype):
    return math.prod(shape) * jnp.dtype(dtype).itemsize


def _pick(n, prefs):
    for p in prefs:
        if n % p == 0:
            return p
    return n


def _dot(a, b):
    return jnp.dot(a, b, preferred_element_type=F32)


def _dot_nt(a, b):
    return lax.dot_general(a, b, (((1,), (1,)), ((), ())), preferred_element_type=F32)


def _dot_tn(a, b):
    return lax.dot_general(a, b, (((0,), (0,)), ((), ())), preferred_element_type=F32)


def _split3(x):
    hi = x.astype(MXU_DTYPE)
    r = x - hi.astype(F32)
    mid = r.astype(MXU_DTYPE)
    lo = (r - mid.astype(F32)).astype(MXU_DTYPE)
    return hi, mid, lo


def _rms(xf, g=None):
    y = xf * lax.rsqrt(jnp.mean(xf * xf, axis=-1, keepdims=True) + EPS)
    return y if g is None else y * g


def _silu(a):
    return a * (1.0 / (1.0 + jnp.exp(-a)))


def _swap32(x):
    w = x.shape[-1]
    lane = lax.broadcasted_iota(jnp.int32, x.shape, x.ndim - 1)
    fwd = pltpu.roll(x, w - 32, x.ndim - 1)
    bwd = pltpu.roll(x, 32, x.ndim - 1)
    return jnp.where((lane & 63) < 32, fwd, bwd)


def _rms_matmul_kernel(*refs, side):
    if side:
        x_ref, g_ref, w_ref, ws_ref, o_ref, os_ref, xn_ref = refs
    else:
        x_ref, g_ref, w_ref, o_ref, xn_ref = refs

    @pl.when(pl.program_id(1) == 0)
    def _():
        xn_ref[...] = _rms(x_ref[...], g_ref[...]).astype(xn_ref.dtype)
        if side:
            os_ref[...] = _dot(xn_ref[...], ws_ref[...])

    o_ref[...] = _dot(xn_ref[...], w_ref[...]).astype(o_ref.dtype)


def rms_matmul(x, g, w, *, out_dtype=F32, w_side=None):
    n, d = x.shape
    nout = w.shape[1]
    side = w_side is not None
    tn = _pick(nout, (1920, 1152, 1024, 512, 256, 128))

    def vmem_for(tm):
        return (2 * (_nbytes((tm, d), F32) + _nbytes((d, tn), w.dtype) + _nbytes((tm, tn), out_dtype))
                + _nbytes((tm, d), MXU_DTYPE) + _nbytes((tm, tn), F32) + (8 << 20))

    tm = next(t for t in (1024, 512, 256, 128, n) if n % t == 0 and (vmem_for(t) <= TILE_VMEM_BUDGET or t <= 128))
    vmem = vmem_for(tm)
    in_specs = [pl.BlockSpec((tm, d), lambda i, j: (i, 0)),
                pl.BlockSpec((1, d), lambda i, j: (0, 0)),
                pl.BlockSpec((d, tn), lambda i, j: (0, j))]
    args = [x, g.reshape(1, d), w]
    out_shape = jax.ShapeDtypeStruct((n, nout), out_dtype)
    out_specs = pl.BlockSpec((tm, tn), lambda i, j: (i, j))
    if side:
        ns = w_side.shape[1]
        in_specs.append(pl.BlockSpec((d, ns), lambda i, j: (0, 0)))
        args.append(w_side)
        out_shape = (out_shape, jax.ShapeDtypeStruct((n, ns), F32))
        out_specs = (out_specs, pl.BlockSpec((tm, ns), lambda i, j: (i, 0)))
        vmem += 2 * (_nbytes((d, ns), w_side.dtype) + _nbytes((tm, ns), F32))
    return pl.pallas_call(
        functools.partial(_rms_matmul_kernel, side=side),
        out_shape=out_shape,
        grid=(n // tm, nout // tn),
        in_specs=in_specs,
        out_specs=out_specs,
        scratch_shapes=[pltpu.VMEM((tm, d), MXU_DTYPE)],
        compiler_params=_params(("parallel", "arbitrary"), vmem),
        name="rms_matmul",
    )(*args)


def _ffn_kernel(*refs, final_norm, nf):
    x_ref, g_ref = refs[0:2]
    blocks = (refs[2:5], refs[5:8])
    gf_ref = refs[8] if final_norm else None
    o_ref, xn_ref = refs[-2:]
    s = pl.program_id(1)

    @pl.when(s == 0)
    def _():
        xf = x_ref[...]
        xn_ref[...] = _rms(xf, g_ref[...]).astype(xn_ref.dtype)
        o_ref[...] = xf

    def down(w1a_ref, w1b_ref, w2_ref):
        xn = xn_ref[...]
        a = _dot(xn, w1a_ref[...])
        b = _dot(xn, w1b_ref[...])
        return _dot((_silu(a) * b).astype(MXU_DTYPE), w2_ref[...])

    if nf % 2 == 0:
        o_ref[...] += down(*blocks[0]) + down(*blocks[1])
    else:
        @pl.when(s == 0)
        def _():
            o_ref[...] += down(*blocks[0])

        @pl.when(s > 0)
        def _():
            o_ref[...] += down(*blocks[0]) + down(*blocks[1])

    if final_norm:
        @pl.when(s == pl.num_programs(1) - 1)
        def _():
            o_ref[...] = _rms(o_ref[...], gf_ref[...])


def ffn(x, g, w1, w2, layer, which, final_g=None):
    n, d = x.shape
    dff = w2.shape[2]
    tm = _pick(n, (512, 256, 128))
    tf = _pick(dff, (512, 256, 128))
    nf = dff // tf
    vmem = (2 * (2 * _nbytes((tm, d), F32) + 6 * _nbytes((d, tf), w1.dtype))
            + _nbytes((tm, d), MXU_DTYPE) + _nbytes((tm, d), F32) + 6 * _nbytes((tm, tf), F32) + (6 << 20))
    in_specs = [pl.BlockSpec((tm, d), lambda i, s: (i, 0)),
                pl.BlockSpec((1, d), lambda i, s: (0, 0))]
    args = [x, g.reshape(1, d)]
    lead = nf % 2
    for half in range(2):
        blk = lambda s, half=half: jnp.maximum(2 * s - lead + half, half * (lead + 1))
        in_specs += [pl.BlockSpec((None, None, d, tf), lambda i, s, blk=blk: (layer, which, 0, blk(s))),
                     pl.BlockSpec((None, None, d, tf), lambda i, s, blk=blk: (layer, which, 0, blk(s) + nf)),
                     pl.BlockSpec((None, None, tf, d), lambda i, s, blk=blk: (layer, which, blk(s), 0))]
        args += [w1, w1, w2]
    if final_g is not None:
        in_specs.append(pl.BlockSpec((1, d), lambda i, s: (0, 0)))
        args.append(final_g.reshape(1, d))
    return pl.pallas_call(
        functools.partial(_ffn_kernel, final_norm=final_g is not None, nf=nf),
        out_shape=jax.ShapeDtypeStruct((n, d), F32),
        grid=(n // tm, -(-nf // 2)),
        in_specs=in_specs,
        out_specs=pl.BlockSpec((tm, d), lambda i, f: (i, 0)),
        scratch_shapes=[pltpu.VMEM((tm, d), MXU_DTYPE)],
        compiler_params=_params(("parallel", "arbitrary"), vmem),
        name="ffn",
    )(*args)


def _matmul_residual_kernel(*refs, n_in):
    x_ref = refs[0]
    o_ref = refs[1 + 2 * n_in]
    acc = x_ref[...]
    for h_ref, w_ref in zip(refs[1:1 + n_in], refs[1 + n_in:1 + 2 * n_in]):
        acc = acc + _dot(h_ref[...], w_ref[...])
    o_ref[...] = acc


def matmul_residual(x, hs, ws):
    n, d = x.shape

    def vmem_for(tm, tn):
        return (2 * (2 * _nbytes((tm, tn), F32) + sum(_nbytes((tm, h.shape[1]), h.dtype) for h in hs)
                     + sum(_nbytes((w.shape[0], tn), w.dtype) for w in ws))
                + _nbytes((tm, tn), F32) + (8 << 20))

    tm, tn = next((a, b) for a, b in ((512, d), (1024, 1024), (512, 1024), (256, 512), (128, 128), (n, d))
                  if n % a == 0 and d % b == 0 and (vmem_for(a, b) <= TILE_VMEM_BUDGET or a <= 128))
    vmem = vmem_for(tm, tn)
    in_specs = [pl.BlockSpec((tm, tn), lambda i, j: (i, j))]
    for h in hs:
        in_specs.append(pl.BlockSpec((tm, h.shape[1]), lambda i, j: (i, 0)))
    for w in ws:
        in_specs.append(pl.BlockSpec((w.shape[0], tn), lambda i, j: (0, j)))
    return pl.pallas_call(
        functools.partial(_matmul_residual_kernel, n_in=len(hs)),
        out_shape=jax.ShapeDtypeStruct((n, d), F32),
        grid=(n // tm, d // tn),
        in_specs=in_specs,
        out_specs=pl.BlockSpec((tm, tn), lambda i, j: (i, j)),
        compiler_params=_params(("parallel", "arbitrary"), vmem),
        name="matmul_residual",
    )(x, *hs, *ws)


def _mem_attn_kernel(x_ref, g_ref, wq_ref, k_ref, v_ref, wo_ref, o_ref, att_ref, *, bt, tq, heads, hd):
    xf = x_ref[...]
    xn = _rms(xf, g_ref[...]).astype(MXU_DTYPE)
    q = _dot(xn, wq_ref[...]).astype(MXU_DTYPE)
    scale = hd ** -0.5
    for b in range(bt):
        for h in range(heads):
            qh = q[b * tq:(b + 1) * tq, h * hd:(h + 1) * hd]
            kh = k_ref[b, :, h * hd:(h + 1) * hd].astype(MXU_DTYPE)
            vh = v_ref[b, :, h * hd:(h + 1) * hd].astype(MXU_DTYPE)
            s = _dot_nt(qh, kh) * scale
            m = jnp.max(s, axis=-1, keepdims=True)
            p = jnp.exp(s - m)
            l = jnp.sum(p, axis=-1, keepdims=True)
            oh = _dot(p.astype(MXU_DTYPE), vh) / l
            att_ref[b * tq:(b + 1) * tq, h * hd:(h + 1) * hd] = oh.astype(att_ref.dtype)
    o_ref[...] = xf + _dot(att_ref[...], wo_ref[...])


def mem_attn(x, g, wq, mem_k, mem_v, wo, *, batch, heads):
    n, d = x.shape
    t = n // batch
    m, inner = mem_k.shape[1], mem_k.shape[2]
    hd = inner // heads
    if t >= 128:
        bt, tq = 1, _pick(t, (512, 256, 128))
    else:
        bt, tq = _pick(batch, (8, 4, 2, 1)), t
    nt = t // tq
    rows = bt * tq
    vmem = (2 * (2 * _nbytes((rows, d), F32) + 2 * _nbytes((bt, m, inner), mem_k.dtype)
                 + 2 * _nbytes((d, inner), wq.dtype))
            + 2 * _nbytes((rows, d), F32) + (8 << 20))
    return pl.pallas_call(
        functools.partial(_mem_attn_kernel, bt=bt, tq=tq, heads=heads, hd=hd),
        out_shape=jax.ShapeDtypeStruct((n, d), F32),
        grid=(batch // bt, nt),
        in_specs=[pl.BlockSpec((rows, d), lambda b, i: (b * nt + i, 0)),
                  pl.BlockSpec((1, d), lambda b, i: (0, 0)),
                  pl.BlockSpec((d, inner), lambda b, i: (0, 0)),
                  pl.BlockSpec((bt, m, inner), lambda b, i: (b, 0, 0)),
                  pl.BlockSpec((bt, m, inner), lambda b, i: (b, 0, 0)),
                  pl.BlockSpec((inner, d), lambda b, i: (0, 0))],
        out_specs=pl.BlockSpec((rows, d), lambda b, i: (b * nt + i, 0)),
        scratch_shapes=[pltpu.VMEM((rows, inner), MXU_DTYPE)],
        compiler_params=_params(("parallel", "arbitrary"), vmem),
        name="mem_attn",
    )(x, g.reshape(1, d), wq, mem_k, mem_v, wo)


def _mla_prep_kernel(*refs, q_lora, kv_lora, rope_dim, rope_slabs, qscale, heads, nope):
    expand = len(refs) == 12
    if expand:
        (p_ref, qn_ref, kvn_ref, wuq_ref, cos_ref, sin_ref, wukv_ref,
         q_ref, ckv_ref, kpe_ref, k_ref, v_ref) = refs
    else:
        p_ref, qn_ref, kvn_ref, wuq_ref, cos_ref, sin_ref, q_ref, ckv_ref, kpe_ref, k_ref = refs
    cos = cos_ref[...]
    sin = sin_ref[...]
    cqn = _rms(p_ref[:, 0:q_lora], qn_ref[...]).astype(MXU_DTYPE)
    q = _dot(cqn, wuq_ref[...])
    if qscale != 1.0:
        q = q * qscale
    q_ref[...] = q.astype(q_ref.dtype)
    for lo in rope_slabs:
        xs = q[:, lo:lo + V7X_LANES]
        q_ref[:, lo:lo + V7X_LANES] = (xs * cos + _swap32(xs) * sin).astype(q_ref.dtype)
    ckv = _rms(p_ref[:, q_lora:q_lora + kv_lora], kvn_ref[...])
    ckv_ref[...] = ckv
    kp = p_ref[:, q_lora + kv_lora:q_lora + kv_lora + V7X_LANES]
    kpr = kp * cos + _swap32(kp) * sin
    kpe_ref[...] = kpr[:, 0:rope_dim]
    kpr_m = kpr.astype(k_ref.dtype)
    if expand:
        kv = _dot(ckv.astype(MXU_DTYPE), wukv_ref[...])
        hw = nope + V7X_LANES
        for h in range(heads):
            k_ref[:, h * hw:h * hw + nope] = kv[:, h * nope:(h + 1) * nope].astype(k_ref.dtype)
            k_ref[:, h * hw + nope:(h + 1) * hw] = kpr_m
        v_ref[...] = kv[:, heads * nope:].astype(v_ref.dtype)
    else:
        k_ref[:, 0:kv_lora] = ckv.astype(k_ref.dtype)
        k_ref[:, kv_lora:kv_lora + V7X_LANES] = kpr_m


def mla_prep(proj, q_norm, kv_norm, w_uq, cos_rows, sin_rows, w_ukv, *, q_lora, kv_lora, rope_dim,
             rope_slabs, qscale, heads, nope):
    n = proj.shape[0]
    qcols = w_uq.shape[1]
    tm = _pick(n, (512, 256, 128))
    expand = w_ukv is not None
    row = lambda w: pl.BlockSpec((tm, w), lambda i: (i, 0))
    full = lambda a: pl.BlockSpec(a.shape, lambda i: (0, 0))
    in_specs = [row(proj.shape[1]), pl.BlockSpec((1, q_lora), lambda i: (0, 0)),
                pl.BlockSpec((1, kv_lora), lambda i: (0, 0)), full(w_uq), row(V7X_LANES), row(V7X_LANES)]
    args = [proj, q_norm.reshape(1, -1), kv_norm.reshape(1, -1), w_uq, cos_rows, sin_rows]
    out_shape = [jax.ShapeDtypeStruct((n, qcols), MXU_DTYPE), jax.ShapeDtypeStruct((n, kv_lora), F32),
                 jax.ShapeDtypeStruct((n, rope_dim), F32)]
    out_specs = [row(qcols), row(kv_lora), row(rope_dim)]
    vmem = (2 * (_nbytes((tm, proj.shape[1]), F32) + _nbytes(w_uq.shape, w_uq.dtype)
                 + _nbytes((tm, qcols), MXU_DTYPE) + 3 * _nbytes((tm, kv_lora + V7X_LANES), F32))
            + 3 * _nbytes((tm, qcols), F32) + (8 << 20))
    if expand:
        kcols = heads * (nope + V7X_LANES)
        vcols = w_ukv.shape[1] - heads * nope
        in_specs.append(full(w_ukv))
        args.append(w_ukv)
        out_shape += [jax.ShapeDtypeStruct((n, kcols), MXU_DTYPE), jax.ShapeDtypeStruct((n, vcols), MXU_DTYPE)]
        out_specs += [row(kcols), row(vcols)]
        vmem += (2 * (_nbytes(w_ukv.shape, w_ukv.dtype) + _nbytes((tm, kcols + vcols), MXU_DTYPE))
                 + 2 * _nbytes((tm, w_ukv.shape[1]), F32))
    else:
        out_shape.append(jax.ShapeDtypeStruct((n, kv_lora + V7X_LANES), MXU_DTYPE))
        out_specs.append(row(kv_lora + V7X_LANES))
    return pl.pallas_call(
        functools.partial(_mla_prep_kernel, q_lora=q_lora, kv_lora=kv_lora, rope_dim=rope_dim,
                          rope_slabs=rope_slabs, qscale=qscale, heads=heads, nope=nope),
        out_shape=tuple(out_shape),
        grid=(n // tm,),
        in_specs=in_specs,
        out_specs=tuple(out_specs),
        compiler_params=_params(("parallel",), vmem),
        name="mla_prep",
    )(*args)


def _lanes(x, width):
    reps = width // V7X_LANES
    return x if reps == 1 else jnp.concatenate([x] * reps, axis=-1)


def _mha_attn_kernel(qi_ref, kj_ref, flag_ref, q_ref, k_ref, v_ref, o_ref, acc_ref, m_ref,
                     *, heads, hw, vd, tq, tk, q_offset, kv_len):
    step = pl.program_id(1)
    i = qi_ref[step]
    j = kj_ref[step]
    flags = flag_ref[step]
    aw = vd + V7X_LANES

    @pl.when(j == 0)
    def _():
        m_ref[...] = jnp.full_like(m_ref, NEG)
        acc_ref[...] = jnp.zeros_like(acc_ref)

    ones = jnp.ones((tk, V7X_LANES), MXU_DTYPE)

    def all_heads(bias):
        for h in range(heads):
            s = _dot_nt(q_ref[0, :, h * hw:(h + 1) * hw], k_ref[0, :, h * hw:(h + 1) * hw])
            if bias is not None:
                s = s + bias
            m_old = m_ref[h]
            m_new = jnp.maximum(m_old, jnp.max(s, axis=-1, keepdims=True))
            alpha = jnp.exp2(m_old - m_new)
            p = jnp.exp2(s - _lanes(m_new, tk)).astype(MXU_DTYPE)
            m_ref[h] = m_new
            v_ext = jnp.concatenate([v_ref[0, :, h * vd:(h + 1) * vd], ones], axis=-1)
            acs = slice(h * aw, (h + 1) * aw)
            acc_ref[:, acs] = _lanes(alpha, aw) * acc_ref[:, acs] + _dot(p, v_ext)

    @pl.when((flags & 2) == 0)
    def _():
        all_heads(None)

    @pl.when((flags & 2) != 0)
    def _():
        qpos = q_offset + i * tq + lax.broadcasted_iota(jnp.int32, (tq, tk), 0)
        kpos = j * tk + lax.broadcasted_iota(jnp.int32, (tq, tk), 1)
        visible = ((kpos // CHUNK) <= (qpos // CHUNK)) & (kpos < kv_len)
        all_heads(jnp.where(visible, 0.0, NEG))

    @pl.when((flags & 1) != 0)
    def _():
        for h in range(heads):
            num = acc_ref[:, h * aw:h * aw + vd]
            den = _lanes(acc_ref[:, h * aw + vd:(h + 1) * aw], vd)
            o_ref[0, :, h * vd:(h + 1) * vd] = (num / den).astype(o_ref.dtype)


def mha_attn(q, k, v, *, heads, tq, tk, q_offset, kv_len):
    b, t, _ = q.shape
    s = k.shape[1]
    hw = q.shape[2] // heads
    vd = v.shape[2] // heads
    nq, nk = t // tq, s // tk
    qi, kj, flags = [], [], []
    for i in range(nq):
        first_q = q_offset + i * tq
        last_key = ((first_q + tq - 1) // CHUNK + 1) * CHUNK - 1
        jl = min(last_key // tk, nk - 1)
        for j in range(jl + 1):
            fully_visible = ((j + 1) * tk - 1) // CHUNK <= first_q // CHUNK and (j + 1) * tk <= kv_len
            qi.append(i)
            kj.append(j)
            flags.append(int(j == jl) + 2 * int(not fully_visible))
    sched = [jnp.asarray(a, jnp.int32) for a in (qi, kj, flags)]
    aw = vd + V7X_LANES
    vmem = (2 * (2 * _nbytes((tq, heads * hw), q.dtype) + 2 * _nbytes((tk, heads * vd), v.dtype)
                 + _nbytes((tq, heads * vd), MXU_DTYPE))
            + _nbytes((tq, heads * aw), F32) + _nbytes((heads, tq, V7X_LANES), F32)
            + 6 * _nbytes((tq, tk), F32) + (8 << 20))
    grid_spec = pltpu.PrefetchScalarGridSpec(
        num_scalar_prefetch=3,
        grid=(b, len(qi)),
        in_specs=[pl.BlockSpec((1, tq, heads * hw), lambda bb, p, qi_r, kj_r, l_r: (bb, qi_r[p], 0)),
                  pl.BlockSpec((1, tk, heads * hw), lambda bb, p, qi_r, kj_r, l_r: (bb, kj_r[p], 0)),
                  pl.BlockSpec((1, tk, heads * vd), lambda bb, p, qi_r, kj_r, l_r: (bb, kj_r[p], 0))],
        out_specs=pl.BlockSpec((1, tq, heads * vd), lambda bb, p, qi_r, kj_r, l_r: (bb, qi_r[p], 0)),
        scratch_shapes=[pltpu.VMEM((tq, heads * aw), F32),
                        pltpu.VMEM((heads, tq, V7X_LANES), F32)])
    return pl.pallas_call(
        functools.partial(_mha_attn_kernel, heads=heads, hw=hw, vd=vd, tq=tq, tk=tk,
                          q_offset=q_offset, kv_len=kv_len),
        out_shape=jax.ShapeDtypeStruct((b, t, heads * vd), MXU_DTYPE),
        grid_spec=grid_spec,
        compiler_params=_params(("parallel", "arbitrary"), vmem),
        name="mha_attn",
    )(*sched, q, k, v)


def _last_kv_block(i, *, tq, tk, q_offset, nk):
    last_q = q_offset + (i + 1) * tq - 1
    last_key = (last_q // CHUNK + 1) * CHUNK - 1
    return jnp.minimum(last_key // tk, nk - 1)


def _mla_attn_kernel(q_ref, k_ref, wuk_ref, wuv_ref, o_ref, qs_ref, acc_ref, m_ref, l_ref,
                     *, heads, hb, tq, tk, nope, lat, q_offset, kv_len, scale):
    i = pl.program_id(1)
    j = pl.program_id(2)
    nk = pl.num_programs(2)
    j_last = _last_kv_block(i, tq=tq, tk=tk, q_offset=q_offset, nk=nk)
    rope_lo = heads * nope

    @pl.when(j == 0)
    def _():
        for h in range(heads):
            qn = q_ref[0, :, h * nope:(h + 1) * nope]
            qs_ref[h * tq:(h + 1) * tq, 0:lat] = (_dot(qn, wuk_ref[h]) * scale).astype(qs_ref.dtype)
            qr = q_ref[0, :, rope_lo + h * V7X_LANES:rope_lo + (h + 1) * V7X_LANES]
            qs_ref[h * tq:(h + 1) * tq, lat:lat + V7X_LANES] = (qr.astype(F32) * scale).astype(qs_ref.dtype)
        m_ref[...] = jnp.full_like(m_ref, NEG)
        l_ref[...] = jnp.zeros_like(l_ref)
        acc_ref[...] = jnp.zeros_like(acc_ref)

    @pl.when(j <= j_last)
    def _():
        k = k_ref[0]
        v = k[:, 0:lat]
        qpos = q_offset + i * tq + lax.broadcasted_iota(jnp.int32, (tq, tk), 0)
        kpos = j * tk + lax.broadcasted_iota(jnp.int32, (tq, tk), 1)
        visible = ((kpos // CHUNK) <= (qpos // CHUNK)) & (kpos < kv_len)
        bias = jnp.where(visible, 0.0, NEG)
        rows = hb * tq
        for rb in range(heads // hb):
            rs = slice(rb * rows, (rb + 1) * rows)
            s = _dot_nt(qs_ref[rs, :], k)
            s = (s.reshape(hb, tq, tk) + bias[None]).reshape(rows, tk)
            m_old = m_ref[rs, :]
            m_new = jnp.maximum(m_old, jnp.max(s, axis=-1, keepdims=True))
            alpha = jnp.exp(m_old - m_new)
            p = jnp.exp(s - _lanes(m_new, tk))
            l_ref[rs, :] = alpha * l_ref[rs, :] + jnp.sum(p, axis=-1, keepdims=True)
            m_ref[rs, :] = m_new
            acc_ref[rs, :] = _lanes(alpha, lat) * acc_ref[rs, :] + _dot(p.astype(MXU_DTYPE), v)

    @pl.when(j == j_last)
    def _():
        vd = wuv_ref.shape[2]
        for h in range(heads):
            hs = slice(h * tq, (h + 1) * tq)
            ol = acc_ref[hs, :] / _lanes(l_ref[hs, :], lat)
            o_ref[0, :, h * vd:(h + 1) * vd] = _dot(ol.astype(MXU_DTYPE), wuv_ref[h]).astype(o_ref.dtype)


def mla_attn(q, kcat, w_uk, w_uv, *, tq, tk, q_offset, kv_len, scale):
    b, t, _ = q.shape
    s = kcat.shape[1]
    heads, nope, lat = w_uk.shape
    vd = w_uv.shape[2]
    nq, nk = t // tq, s // tk
    rows = heads * tq
    hb = max(1, min(heads, MLA_ROW_BLOCK // tq))
    assert heads % hb == 0
    last = functools.partial(_last_kv_block, tq=tq, tk=tk, q_offset=q_offset, nk=nk)
    vmem = (2 * (_nbytes((tq, q.shape[2]), q.dtype) + _nbytes((tk, kcat.shape[2]), kcat.dtype)
                 + 2 * _nbytes(w_uk.shape, w_uk.dtype) + _nbytes((tq, heads * vd), MXU_DTYPE))
            + _nbytes((rows, lat + V7X_LANES), MXU_DTYPE) + _nbytes((rows, lat), F32)
            + 2 * _nbytes((rows, V7X_LANES), F32) + 4 * _nbytes((rows, tk), F32) + (8 << 20))
    return pl.pallas_call(
        functools.partial(_mla_attn_kernel, heads=heads, hb=hb, tq=tq, tk=tk, nope=nope, lat=lat,
                          q_offset=q_offset, kv_len=kv_len, scale=scale),
        out_shape=jax.ShapeDtypeStruct((b, t, heads * vd), MXU_DTYPE),
        grid=(b, nq, nk),
        in_specs=[pl.BlockSpec((1, tq, q.shape[2]), lambda bb, i, j: (bb, i, 0)),
                  pl.BlockSpec((1, tk, kcat.shape[2]), lambda bb, i, j: (bb, jnp.minimum(j, last(i)), 0)),
                  pl.BlockSpec(w_uk.shape, lambda bb, i, j: (0, 0, 0)),
                  pl.BlockSpec(w_uv.shape, lambda bb, i, j: (0, 0, 0))],
        out_specs=pl.BlockSpec((1, tq, heads * vd), lambda bb, i, j: (bb, i, 0)),
        scratch_shapes=[pltpu.VMEM((rows, lat + V7X_LANES), MXU_DTYPE),
                        pltpu.VMEM((rows, lat), F32),
                        pltpu.VMEM((rows, V7X_LANES), F32),
                        pltpu.VMEM((rows, V7X_LANES), F32)],
        compiler_params=_params(("parallel", "parallel", "arbitrary"), vmem),
        name="mla_attn",
    )(q, kcat, w_uk, w_uv)


class _ScanPlan(NamedTuple):
    body: Callable
    args: list
    in_specs: list
    out_shape: list
    out_specs: list
    scratch: list
    vmem: int


def _ssd_kernel(*refs, L, groups, hpg, hdim, nstate, has_state):
    if has_state:
        (z_ref, xs_ref, b_ref, c_ref, dt_ref, cw_ref, cb_ref, dtb_ref, alog_ref, dsk_ref, nrm_ref,
         cst_ref, hst_ref, y_ref, ncv_ref, nst_ref, xbuf, h_scr, cum_scr, xdt_scr, yin_scr) = refs
    else:
        (z_ref, xs_ref, b_ref, c_ref, dt_ref, cw_ref, cb_ref, dtb_ref, alog_ref, dsk_ref, nrm_ref,
         cst_ref, y_ref, ncv_ref, nst_ref, xbuf, h_scr, cum_scr, xdt_scr, yin_scr) = refs
        hst_ref = None
    c = pl.program_id(1)
    nc = pl.num_programs(1)
    dx = groups * hpg * hdim
    dn = groups * nstate
    gw = hpg * hdim
    pad = CONV_PAD_ROWS

    @pl.when(c == 0)
    def _():
        xbuf[0:pad, :] = cst_ref[0]
        if has_state:
            h_scr[...] = hst_ref[0]
        else:
            h_scr[...] = jnp.zeros_like(h_scr)

    xbuf[pad:pad + L, 0:dx] = xs_ref[...].astype(F32)
    xbuf[pad:pad + L, dx:dx + dn] = b_ref[...].astype(F32)
    xbuf[pad:pad + L, dx + dn:dx + 2 * dn] = c_ref[...].astype(F32)
    acc = xbuf[pad - CONV_K + 1:pad - CONV_K + 1 + L, :] * cw_ref[0:1, :] + cb_ref[...]
    for jj in range(1, CONV_K):
        lo = pad - CONV_K + 1 + jj
        acc = acc + xbuf[lo:lo + L, :] * cw_ref[jj:jj + 1, :]
    xc = _silu(acc)
    tail = xbuf[L:L + pad, :]
    xbuf[0:pad, :] = tail

    @pl.when(c == nc - 1)
    def _():
        ncv_ref[0] = tail

    dtr = dt_ref[...] + dtb_ref[...]
    dt = jnp.maximum(dtr, 0.0) + jnp.log1p(jnp.exp(-jnp.abs(dtr)))
    la = dt * (-jnp.exp(alog_ref[...]))
    row = lax.broadcasted_iota(jnp.int32, (L, L), 0)
    col = lax.broadcasted_iota(jnp.int32, (L, L), 1)
    causal = row >= col
    tri = jnp.where(causal, 1.0, 0.0).astype(MXU_DTYPE)
    cum = sum(_dot(tri, piece) for piece in _split3(la))
    nh = groups * hpg
    eye = jnp.where(lax.broadcasted_iota(jnp.int32, (V7X_LANES, V7X_LANES), 0)
                    == lax.broadcasted_iota(jnp.int32, (V7X_LANES, V7X_LANES), 1), 1.0, 0.0).astype(MXU_DTYPE)
    cum_t = sum(_dot_nt(eye, piece) for piece in _split3(cum))

    half = lax.broadcasted_iota(jnp.int32, (L, V7X_LANES), 1) < hdim
    per_vreg = V7X_LANES // hdim
    for g in range(groups):
        bg = xc[:, dx + g * nstate:dx + (g + 1) * nstate]
        cg = xc[:, dx + dn + g * nstate:dx + dn + (g + 1) * nstate].astype(MXU_DTYPE)
        qk = _dot_nt(cg, bg.astype(MXU_DTYPE))
        for sl in range(gw // V7X_LANES):
            lane0 = g * gw + sl * V7X_LANES
            h0 = lane0 // hdim
            cb = [jnp.broadcast_to(cum[:, h0 + u:h0 + u + 1], (L, V7X_LANES)) for u in range(per_vreg)]
            db = [jnp.broadcast_to(dt[:, h0 + u:h0 + u + 1], (L, V7X_LANES)) for u in range(per_vreg)]
            cum_e = jnp.where(half, cb[0], cb[1])
            dt_e = jnp.where(half, db[0], db[1])
            xdt = xc[:, lane0:lane0 + V7X_LANES] * dt_e
            xdt_m = xdt.astype(MXU_DTYPE)
            ys = []
            for u in range(per_vreg):
                seg = cb[u][:, 0:L] - cum_t[h0 + u:h0 + u + 1, :]
                decay = jnp.exp(jnp.where(causal, seg, NEG))
                ys.append(_dot((qk * decay).astype(MXU_DTYPE), xdt_m))
            cum_scr[:, lane0:lane0 + V7X_LANES] = cum_e
            xdt_scr[:, lane0:lane0 + V7X_LANES] = xdt
            yin_scr[:, lane0:lane0 + V7X_LANES] = jnp.where(half, ys[0], ys[1])

    for g in range(groups):
        gs = slice(g * gw, (g + 1) * gw)
        bg = xc[:, dx + g * nstate:dx + (g + 1) * nstate].astype(MXU_DTYPE)
        cg = xc[:, dx + dn + g * nstate:dx + dn + (g + 1) * nstate].astype(MXU_DTYPE)
        cum_g = cum_scr[:, gs]
        last = cum_scr[L - 1:L, gs]
        hg = h_scr[g]
        y_inter = _dot(cg, hg.astype(MXU_DTYPE)) * jnp.exp(cum_g)
        wx = (jnp.exp(last - cum_g) * xdt_scr[:, gs]).astype(MXU_DTYPE)
        h_scr[g] = jnp.exp(last) * hg + _dot_tn(bg, wx)
        y = yin_scr[:, gs] + y_inter + dsk_ref[:, gs] * xc[:, gs]
        y = y * _silu(z_ref[:, gs].astype(F32))
        y_ref[:, gs] = (_rms(y) * nrm_ref[:, gs]).astype(y_ref.dtype)

    @pl.when(c == nc - 1)
    def _():
        nst_ref[0] = h_scr[...]


def ssd_scan(proj, dt_raw, col, conv_w, conv_b, dt_bias, a_log, d_skip_e, ssd_norm, conv_state, ssd_state,
             *, batch, L, groups, hpg, hdim, nstate):
    n = proj.shape[0]
    t = n // batch
    nc = t // L
    dx = groups * hpg * hdim
    dn = groups * nstate
    cdim = dx + 2 * dn
    has_state = ssd_state is not None
    rowmap = lambda b, c: b * nc + c
    in_specs = [pl.BlockSpec((L, dx), lambda b, c: (rowmap(b, c), col["z"])),
                pl.BlockSpec((L, dx), lambda b, c: (rowmap(b, c), col["xs"])),
                pl.BlockSpec((L, dn), lambda b, c: (rowmap(b, c), col["B"])),
                pl.BlockSpec((L, dn), lambda b, c: (rowmap(b, c), col["C"])),
                pl.BlockSpec((L, V7X_LANES), lambda b, c: (rowmap(b, c), 0)),
                pl.BlockSpec((CONV_K, cdim), lambda b, c: (0, 0)),
                pl.BlockSpec((1, cdim), lambda b, c: (0, 0)),
                pl.BlockSpec((1, V7X_LANES), lambda b, c: (0, 0)),
                pl.BlockSpec((1, V7X_LANES), lambda b, c: (0, 0)),
                pl.BlockSpec((1, dx), lambda b, c: (0, 0)),
                pl.BlockSpec((1, dx), lambda b, c: (0, 0)),
                pl.BlockSpec((1, CONV_PAD_ROWS, cdim), lambda b, c: (b, 0, 0))]
    args = [proj, proj, proj, proj, dt_raw, conv_w, conv_b.reshape(1, cdim), dt_bias, a_log,
            d_skip_e, ssd_norm.reshape(1, dx), conv_state]
    if has_state:
        in_specs.append(pl.BlockSpec((1, groups, nstate, hpg * hdim), lambda b, c: (b, 0, 0, 0)))
        args.append(ssd_state)
    state_bytes = _nbytes((groups, nstate, hpg * hdim), F32)
    vmem = (2 * (2 * _nbytes((L, dx), F32) + 2 * _nbytes((L, dn), F32) + _nbytes((L, dx), MXU_DTYPE)
                 + 2 * _nbytes((CONV_PAD_ROWS, cdim), F32) + 2 * state_bytes)
            + state_bytes + 12 * _nbytes((L + CONV_PAD_ROWS, cdim), F32) + (8 << 20))
    return _ScanPlan(
        body=functools.partial(_ssd_kernel, L=L, groups=groups, hpg=hpg, hdim=hdim, nstate=nstate,
                               has_state=has_state),
        args=args, in_specs=in_specs,
        out_shape=[jax.ShapeDtypeStruct((n, dx), MXU_DTYPE),
                   jax.ShapeDtypeStruct((batch, CONV_PAD_ROWS, cdim), F32),
                   jax.ShapeDtypeStruct((batch, groups, nstate, hpg * hdim), F32)],
        out_specs=[pl.BlockSpec((L, dx), lambda b, c: (rowmap(b, c), 0)),
                   pl.BlockSpec((1, CONV_PAD_ROWS, cdim), lambda b, c: (b, 0, 0)),
                   pl.BlockSpec((1, groups, nstate, hpg * hdim), lambda b, c: (b, 0, 0, 0))],
        scratch=[pltpu.VMEM((L + CONV_PAD_ROWS, cdim), F32),
                 pltpu.VMEM((groups, nstate, hpg * hdim), F32),
                 pltpu.VMEM((L, dx), F32),
                 pltpu.VMEM((L, dx), F32),
                 pltpu.VMEM((L, dx), F32)],
        vmem=vmem)


def _ret_kernel(*refs, L, heads, hd, has_state):
    if has_state:
        (q_ref, k_ref, v_ref, gate_ref, cos_ref, sin_ref, dec_ref, ecum_ref, wv_ref, sdec_ref, st_ref,
         o_ref, nst_ref, s_scr) = refs
    else:
        (q_ref, k_ref, v_ref, gate_ref, cos_ref, sin_ref, dec_ref, ecum_ref, wv_ref, sdec_ref,
         o_ref, nst_ref, s_scr) = refs
        st_ref = None
    c = pl.program_id(1)
    nc = pl.num_programs(1)
    half = hd // 2

    @pl.when(c == 0)
    def _():
        if has_state:
            s_scr[...] = st_ref[0]
        else:
            s_scr[...] = jnp.zeros_like(s_scr)

    cos = cos_ref[...]
    sin = sin_ref[...]

    def rope(ref, h, mult):
        x1 = ref[:, h * hd:h * hd + half].astype(F32)
        x2 = ref[:, h * hd + half:(h + 1) * hd].astype(F32)
        out = jnp.concatenate([x1 * cos - x2 * sin, x2 * cos + x1 * sin], axis=-1)
        return (out * mult).astype(MXU_DTYPE) if mult != 1.0 else out.astype(MXU_DTYPE)

    for h in range(heads):
        hs = slice(h * hd, (h + 1) * hd)
        qr = rope(q_ref, h, 1.0)
        kr = rope(k_ref, h, hd ** -0.5)
        vf = v_ref[:, hs]
        qk = _dot_nt(qr, kr)
        y = _dot((qk * dec_ref[h]).astype(MXU_DTYPE), vf.astype(MXU_DTYPE))
        s_old = s_scr[h]
        y = y + _dot(qr, s_old.astype(MXU_DTYPE)) * ecum_ref[h]
        s_scr[h] = sdec_ref[h] * s_old + _dot_tn(kr, (wv_ref[h] * vf.astype(F32)).astype(MXU_DTYPE))
        o_ref[:, hs] = (_rms(y) * _silu(gate_ref[:, hs].astype(F32))).astype(o_ref.dtype)

    @pl.when(c == nc - 1)
    def _():
        nst_ref[0] = s_scr[...]


def ret_scan(proj, col, cos_rows, sin_rows, ret_state, *, batch, L, heads, hd):
    n = proj.shape[0]
    t = n // batch
    nc = t // L
    inner = heads * hd
    has_state = ret_state is not None
    lg = jnp.log1p(-jnp.exp2(-5.0 - jnp.arange(heads, dtype=F32)))[:, None, None]
    li = jnp.arange(L, dtype=F32)
    diff = li[:, None] - li[None, :]
    dec = jnp.where(diff >= 0, jnp.exp(jnp.where(diff >= 0, diff, 0.0)[None] * lg), 0.0)
    ecum = jnp.broadcast_to(jnp.exp((li[None, :, None] + 1.0) * lg), (heads, L, hd))
    wv = jnp.broadcast_to(jnp.exp((L - 1.0 - li[None, :, None]) * lg), (heads, L, hd))
    sdec = jnp.broadcast_to(jnp.exp(L * lg), (heads, 1, hd))
    rowmap = lambda b, c: b * nc + c
    in_specs = [pl.BlockSpec((L, inner), lambda b, c: (rowmap(b, c), col["q"])),
                pl.BlockSpec((L, inner), lambda b, c: (rowmap(b, c), col["k"])),
                pl.BlockSpec((L, inner), lambda b, c: (rowmap(b, c), col["v"])),
                pl.BlockSpec((L, inner), lambda b, c: (rowmap(b, c), col["gate"])),
                pl.BlockSpec((L, hd // 2), lambda b, c: (rowmap(b, c), 0)),
                pl.BlockSpec((L, hd // 2), lambda b, c: (rowmap(b, c), 0)),
                pl.BlockSpec((heads, L, L), lambda b, c: (0, 0, 0)),
                pl.BlockSpec((heads, L, hd), lambda b, c: (0, 0, 0)),
                pl.BlockSpec((heads, L, hd), lambda b, c: (0, 0, 0)),
                pl.BlockSpec((heads, 1, hd), lambda b, c: (0, 0, 0))]
    args = [proj, proj, proj, proj, cos_rows, sin_rows, dec, ecum, wv, sdec]
    if has_state:
        in_specs.append(pl.BlockSpec((1, heads, hd, hd), lambda b, c: (b, 0, 0, 0)))
        args.append(ret_state)
    state_bytes = _nbytes((heads, hd, hd), F32)
    vmem = (2 * (4 * _nbytes((L, inner), F32) + _nbytes((L, inner), MXU_DTYPE) + _nbytes((heads, L, L), F32)
                 + 2 * _nbytes((heads, L, hd), F32) + 2 * state_bytes)
            + state_bytes + 16 * _nbytes((L, hd), F32) + (8 << 20))
    return _ScanPlan(
        body=functools.partial(_ret_kernel, L=L, heads=heads, hd=hd, has_state=has_state),
        args=args, in_specs=in_specs,
        out_shape=[jax.ShapeDtypeStruct((n, inner), MXU_DTYPE),
                   jax.ShapeDtypeStruct((batch, heads, hd, hd), F32)],
        out_specs=[pl.BlockSpec((L, inner), lambda b, c: (rowmap(b, c), 0)),
                   pl.BlockSpec((1, heads, hd, hd), lambda b, c: (b, 0, 0, 0))],
        scratch=[pltpu.VMEM((heads, hd, hd), F32)],
        vmem=vmem)


def _scan_pair_kernel(*refs, bodies, n_in, n_out, n_scr):
    ins, outs, scrs = refs[:sum(n_in)], refs[sum(n_in):sum(n_in) + sum(n_out)], refs[sum(n_in) + sum(n_out):]
    for k, body in enumerate(bodies):
        take = lambda seq, counts: seq[sum(counts[:k]):sum(counts[:k + 1])]
        body(*take(ins, n_in), *take(outs, n_out), *take(scrs, n_scr))


def run_scans(plans, *, batch, nc, name):
    outs = pl.pallas_call(
        functools.partial(_scan_pair_kernel, bodies=tuple(p.body for p in plans),
                          n_in=tuple(len(p.args) for p in plans),
                          n_out=tuple(len(p.out_shape) for p in plans),
                          n_scr=tuple(len(p.scratch) for p in plans)),
        out_shape=tuple(s for p in plans for s in p.out_shape),
        grid=(batch, nc),
        in_specs=[s for p in plans for s in p.in_specs],
        out_specs=tuple(s for p in plans for s in p.out_specs),
        scratch_shapes=[s for p in plans for s in p.scratch],
        compiler_params=_params(("parallel", "arbitrary"), sum(p.vmem for p in plans)),
        name=name,
    )(*[a for p in plans for a in p.args])
    split, k = [], 0
    for p in plans:
        split.append(outs[k:k + len(p.out_shape)])
        k += len(p.out_shape)
    return split


def _rope_tables(pos, half, reps, batch):
    inv = ROPE_BASE ** (-jnp.arange(half, dtype=F32) / half)
    ang = pos.astype(F32)[:, None] * inv[None, :]
    cos, sin = jnp.cos(ang), jnp.sin(ang)
    if reps == 0:
        return jnp.tile(cos, (batch, 1)), jnp.tile(sin, (batch, 1))
    c = jnp.tile(jnp.concatenate([cos, cos], axis=-1), (batch, reps))
    s = jnp.tile(jnp.concatenate([-sin, sin], axis=-1), (batch, reps))
    return c, s


def _mixer_even(x, batch, pos, norm_g, prm, conv_state, ssd_state, ret_state, dims, L_ssd, L_ret):
    d = x.shape[1]
    groups, hpg, hdim, nstate, rheads, rhd = dims
    proj, dt_raw = rms_matmul(x, norm_g, prm["w_in"], out_dtype=MXU_DTYPE, w_side=prm["w_dt"])
    col = prm["col"]
    ssd_plan = ssd_scan(proj, dt_raw, col, prm["conv_w"], prm["conv_b"], prm["dt_bias"],
                        prm["a_log"], prm["d_skip_e"], prm["ssd_norm"], conv_state, ssd_state,
                        batch=batch, L=L_ssd, groups=groups, hpg=hpg, hdim=hdim, nstate=nstate)
    cos_rows, sin_rows = _rope_tables(pos, rhd // 2, 0, batch)
    ret_plan = ret_scan(proj, col, cos_rows, sin_rows, ret_state, batch=batch, L=L_ret, heads=rheads, hd=rhd)
    t = x.shape[0] // batch
    (y, new_conv, new_ssd), = run_scans([ssd_plan], batch=batch, nc=t // L_ssd, name="ssd_scan")
    (o, new_ret), = run_scans([ret_plan], batch=batch, nc=t // L_ret, name="ret_scan")
    x = matmul_residual(x, [y, o], [prm["w_out_ssd"], prm["w_out_ret"]])
    return x, (new_conv, new_ssd, new_ret)


def _prep_even(w_in, conv_w, conv_b, dt_bias, a_log, d_skip, ssd_norm, w_out, dims):
    groups, hpg, hdim, nstate, rheads, rhd = dims
    d = w_in.shape[0]
    dx = groups * hpg * hdim
    dn = groups * nstate
    nh = groups * hpg
    ri = rheads * rhd
    o_z, o_xs, o_b, o_c, o_dt = 0, dx, 2 * dx, 2 * dx + dn, 2 * dx + 2 * dn
    o_q = o_dt + nh
    seg = lambda lo, w: w_in[:, lo:lo + w]
    dt_pad = jnp.zeros((d, V7X_LANES - nh), w_in.dtype)
    w_main = jnp.concatenate([seg(o_z, dx), seg(o_q, ri), seg(o_q + ri, ri), seg(o_q + 2 * ri, ri),
                              seg(o_q + 3 * ri, ri), seg(o_xs, dx), seg(o_b, dn), seg(o_c, dn)],
                             axis=1).astype(MXU_DTYPE)
    w_dt = jnp.concatenate([seg(o_dt, nh), dt_pad], axis=1).astype(MXU_DTYPE)
    assert dx == ri and dx % dn == 0 and dn % V7X_LANES == 0
    col = {"z": 0, "q": 1, "k": 2, "v": 3, "gate": 4, "xs": 5, "B": 6 * dx // dn, "C": 6 * dx // dn + 1}
    pad1 = lambda v: jnp.pad(v.astype(F32), (0, V7X_LANES - nh)).reshape(1, V7X_LANES)
    return {"w_in": w_main, "w_dt": w_dt, "col": col, "conv_w": conv_w, "conv_b": conv_b,
            "dt_bias": pad1(dt_bias), "a_log": pad1(a_log),
            "d_skip_e": jnp.repeat(d_skip, hdim).reshape(1, dx), "ssd_norm": ssd_norm,
            "w_out_ssd": w_out[:dx].astype(MXU_DTYPE), "w_out_ret": w_out[dx:].astype(MXU_DTYPE)}


def _prep_odd(w_in, q_norm, kv_norm, w_uq, w_uk, w_uv, w_out, rope_dim):
    d = w_in.shape[0]
    q_lora = q_norm.shape[0]
    kv_lora, heads, nope = w_uk.shape
    kp = w_in[:, q_lora + kv_lora:]
    w_in_new = jnp.concatenate([w_in[:, :q_lora + kv_lora], kp, kp], axis=1).astype(MXU_DTYPE)
    assert 2 * rope_dim == V7X_LANES
    wq = w_uq.reshape(q_lora, heads, nope + rope_dim)
    wq_nope = wq[:, :, :nope]
    wq_rope = jnp.pad(wq[:, :, nope:], ((0, 0), (0, 0), (0, V7X_LANES - rope_dim)))
    hw = nope + V7X_LANES
    return {"w_in": w_in_new, "q_norm": q_norm, "kv_norm": kv_norm,
            "w_uq_lat": jnp.concatenate([wq_nope.reshape(q_lora, heads * nope),
                                         wq_rope.reshape(q_lora, heads * V7X_LANES)], axis=1).astype(MXU_DTYPE),
            "slabs_lat": tuple(heads * nope + h * V7X_LANES for h in range(heads)),
            "w_uq_head": jnp.concatenate([wq_nope, wq_rope], axis=2).reshape(q_lora, heads * hw).astype(MXU_DTYPE),
            "slabs_head": tuple(h * hw + nope for h in range(heads)),
            "w_ukv": jnp.concatenate([w_uk.reshape(kv_lora, heads * nope),
                                      w_uv.reshape(kv_lora, -1)], axis=1).astype(MXU_DTYPE),
            "w_uk": jnp.transpose(w_uk, (1, 2, 0)).astype(MXU_DTYPE),
            "w_uv": jnp.transpose(w_uv, (1, 0, 2)).astype(MXU_DTYPE),
            "w_out": w_out.astype(MXU_DTYPE)}


def _mixer_odd(x, batch, pos, norm_g, prm, past_kcat, rope_dim, q_offset):
    n, d = x.shape
    t = n // batch
    q_lora = prm["q_norm"].shape[0]
    kv_lora = prm["kv_norm"].shape[0]
    heads, nope, lat = prm["w_uk"].shape
    proj = rms_matmul(x, norm_g, prm["w_in"])
    cos_rows, sin_rows = _rope_tables(pos, rope_dim // 2, V7X_LANES // rope_dim, batch)
    scale = (nope + rope_dim) ** -0.5
    dims = dict(q_lora=q_lora, kv_lora=kv_lora, rope_dim=rope_dim, heads=heads, nope=nope)
    if past_kcat is None:
        q, ckv, kpe, k, v = mla_prep(proj, prm["q_norm"], prm["kv_norm"], prm["w_uq_head"], cos_rows, sin_rows,
                                     prm["w_ukv"], rope_slabs=prm["slabs_head"], qscale=scale * math.log2(math.e),
                                     **dims)
        tq = tk = _pick(t, (512, 256, 128, 64))
        shp = lambda a: a.reshape(batch, t, a.shape[1])
        o = mha_attn(shp(q), shp(k), shp(v), heads=heads, tq=tq, tk=tk, q_offset=q_offset, kv_len=t)
    else:
        q, ckv, kpe, kcat = mla_prep(proj, prm["q_norm"], prm["kv_norm"], prm["w_uq_lat"], cos_rows, sin_rows,
                                     None, rope_slabs=prm["slabs_lat"], qscale=1.0, **dims)
        q = q.reshape(batch, t, q.shape[1])
        kcat = kcat.reshape(batch, t, kcat.shape[1])
        kv_len = past_kcat.shape[1] + t
        padded = -(-kv_len // V7X_LANES) * V7X_LANES
        keys = jnp.concatenate([past_kcat, kcat, jnp.zeros((batch, padded - kv_len, kcat.shape[2]), kcat.dtype)], axis=1)
        o = mla_attn(q, keys, prm["w_uk"], prm["w_uv"], tq=t, tk=padded, q_offset=q_offset, kv_len=kv_len,
                     scale=scale)
    x = matmul_residual(x, [o.reshape(n, o.shape[2])], [prm["w_out"]])
    return x, (ckv.reshape(batch, t, kv_lora), kpe.reshape(batch, t, rope_dim))


def kernel(x_prompt, x_sample, mem_prompt, state_conv, state_ssd, state_ret, cache_ckv, cache_kpe,
           cache_mem_k, cache_mem_v, norms, ffn_w1, ffn_w2, mem_norm, w_mq, w_mkv, w_mo,
           ab_w_in, ab_conv_w, ab_conv_b, ab_dt_bias, ab_a_log, ab_d_skip, ab_ssd_norm, ab_w_out,
           c_w_in, c_q_norm, c_kv_norm, c_w_uq, c_w_uk, c_w_uv, c_w_out, final_norm):
    bp, tp, d = x_prompt.shape
    bs, ts, _ = x_sample.shape
    depth = norms.shape[0]
    assert depth >= 1
    past_len = cache_ckv.shape[2]
    mem_tokens = mem_prompt.shape[1]
    mem_heads, mem_hd = cache_mem_k.shape[3], cache_mem_k.shape[4]
    mem_inner = mem_heads * mem_hd
    ssd_heads, nstate, hdim = state_ssd.shape[2], state_ssd.shape[3], state_ssd.shape[4]
    cdim = state_conv.shape[3]
    groups = (cdim - ssd_heads * hdim) // (2 * nstate)
    hpg = ssd_heads // groups
    rheads, rhd = state_ret.shape[2], state_ret.shape[3]
    dims = (groups, hpg, hdim, nstate, rheads, rhd)
    rope_dim = cache_kpe.shape[3]

    pos_p = jnp.arange(tp)
    pos_s = past_len + jnp.arange(ts)
    xp = x_prompt.reshape(bp * tp, d)
    xs = x_sample.reshape(bs * ts, d)
    L_ssd_p, L_ret_p = _pick(tp, (128, 64)), _pick(tp, (256, 128, 64))
    L_s = _pick(ts, (128, 64))

    outs = {k: [] for k in ("conv_p", "ssd_p", "ret_p", "ckv_p", "kpe_p", "memk_p", "memv_p",
                            "conv_s", "ssd_s", "ret_s", "ckv_s", "kpe_s")}

    def to_group_layout(st):
        b = st.shape[0]
        return st.reshape(b, groups, hpg, nstate, hdim).transpose(0, 1, 3, 2, 4).reshape(b, groups, nstate, hpg * hdim)

    def from_group_layout(st):
        b = st.shape[0]
        return st.reshape(b, groups, nstate, hpg, hdim).transpose(0, 1, 3, 2, 4).reshape(b, ssd_heads, nstate, hdim)

    w1 = ffn_w1.astype(MXU_DTYPE)
    w2 = (0.5 * ffn_w2).astype(MXU_DTYPE)
    for i in range(depth):
        j = i // 2
        closing = final_norm if i == depth - 1 else None
        wq_m = w_mq[i].astype(MXU_DTYPE)
        wo_m = w_mo[i].astype(MXU_DTYPE)
        mkv = rms_matmul(mem_prompt.reshape(bp * mem_tokens, d), mem_norm[i], w_mkv[i].astype(MXU_DTYPE))
        mk_p = mkv[:, :mem_inner].reshape(bp, mem_tokens, mem_inner)
        mv_p = mkv[:, mem_inner:].reshape(bp, mem_tokens, mem_inner)
        outs["memk_p"].append(mk_p.reshape(bp, mem_tokens, mem_heads, mem_hd))
        outs["memv_p"].append(mv_p.reshape(bp, mem_tokens, mem_heads, mem_hd))

        xp = ffn(xp, norms[i, 0], w1, w2, i, 0)
        xs = ffn(xs, norms[i, 0], w1, w2, i, 0)
        if i % 2 == 0:
            prm = _prep_even(ab_w_in[j], ab_conv_w[j], ab_conv_b[j], ab_dt_bias[j], ab_a_log[j],
                             ab_d_skip[j], ab_ssd_norm[j], ab_w_out[j], dims)
            zero_conv = jnp.zeros((bp, CONV_PAD_ROWS, cdim), F32)
            xp, st_p = _mixer_even(xp, bp, pos_p, norms[i, 1], prm, zero_conv, None, None, dims, L_ssd_p, L_ret_p)
            conv_in = jnp.pad(state_conv[j], ((0, 0), (CONV_PAD_ROWS - (CONV_K - 1), 0), (0, 0)))
            xs, st_s = _mixer_even(xs, bs, pos_s, norms[i, 1], prm, conv_in, to_group_layout(state_ssd[j]),
                                   state_ret[j], dims, L_s, L_s)
            for tag, st in (("p", st_p), ("s", st_s)):
                outs["conv_" + tag].append(st[0][:, CONV_PAD_ROWS - (CONV_K - 1):, :])
                outs["ssd_" + tag].append(from_group_layout(st[1]))
                outs["ret_" + tag].append(st[2])
        else:
            prm = _prep_odd(c_w_in[j], c_q_norm[j], c_kv_norm[j], c_w_uq[j], c_w_uk[j], c_w_uv[j], c_w_out[j],
                            rope_dim)
            xp, st_p = _mixer_odd(xp, bp, pos_p, norms[i, 1], prm, None, rope_dim, 0)
            past = jnp.concatenate([cache_ckv[j], cache_kpe[j], cache_kpe[j]], axis=-1).astype(MXU_DTYPE)
            xs, st_s = _mixer_odd(xs, bs, pos_s, norms[i, 1], prm, past, rope_dim, past_len)
            for tag, st in (("p", st_p), ("s", st_s)):
                outs["ckv_" + tag].append(st[0])
                outs["kpe_" + tag].append(st[1])
        xp = mem_attn(xp, norms[i, 2], wq_m, mk_p, mv_p, wo_m, batch=bp, heads=mem_heads)
        xs = mem_attn(xs, norms[i, 2], wq_m, cache_mem_k[i].reshape(bs, mem_tokens, mem_inner),
                      cache_mem_v[i].reshape(bs, mem_tokens, mem_inner), wo_m, batch=bs, heads=mem_heads)
        xp = ffn(xp, norms[i, 3], w1, w2, i, 1, final_g=closing)
        xs = ffn(xs, norms[i, 3], w1, w2, i, 1, final_g=closing)

    y_prompt = xp.reshape(bp, tp, d)
    y_sample = xs.reshape(bs, ts, d)
    st = lambda k: jnp.stack(outs[k])
    return (y_prompt, y_sample, st("conv_p"), st("ssd_p"), st("ret_p"), st("ckv_p"), st("kpe_p"),
            st("memk_p"), st("memv_p"), st("conv_s"), st("ssd_s"), st("ret_s"), st("ckv_s"), st("kpe_s"))
```

```python
import functools
import math
from typing import Callable, NamedTuple

import jax
import jax.numpy as jnp
from jax import lax
from jax.experimental import pallas as pl
from jax.experimental.pallas import tpu as pltpu

F32 = jnp.float32
MXU_DTYPE = jnp.bfloat16

EPS = 1e-6
CHUNK = 64
ROPE_BASE = 10000.0
CONV_K = 4
NEG = -1e30

V7X_VMEM_BYTES = 64 * 1024 * 1024
TILE_VMEM_BUDGET = 48 * 1024 * 1024
V7X_LANES = 128
V7X_SUBLANES = 8
CONV_PAD_ROWS = 8


def _params(semantics, vmem_bytes):
    limit = min(int(vmem_bytes), V7X_VMEM_BYTES - (4 << 20))
    return pltpu.CompilerParams(dimension_semantics=semantics, vmem_limit_bytes=limit)


def _nbytes(shape, dtype):
    return math.prod(shape) * jnp.dtype(dtype).itemsize


def _pick(n, prefs):
    for p in prefs:
        if n % p == 0:
            return p
    return n


def _dot(a, b):
    return jnp.dot(a, b, preferred_element_type=F32)


def _dot_nt(a, b):
    return lax.dot_general(a, b, (((1,), (1,)), ((), ())), preferred_element_type=F32)


def _dot_tn(a, b):
    return lax.dot_general(a, b, (((0,), (0,)), ((), ())), preferred_element_type=F32)


def _split3(x):
    hi = x.astype(MXU_DTYPE)
    r = x - hi.astype(F32)
    mid = r.astype(MXU_DTYPE)
    lo = (r - mid.astype(F32)).astype(MXU_DTYPE)
    return hi, mid, lo


def _rms(xf, g=None):
    y = xf * lax.rsqrt(jnp.mean(xf * xf, axis=-1, keepdims=True) + EPS)
    return y if g is None else y * g


def _silu(a):
    return a * (1.0 / (1.0 + jnp.exp(-a)))


def _swap32(x):
    w = x.shape[-1]
    lane = lax.broadcasted_iota(jnp.int32, x.shape, x.ndim - 1)
    fwd = pltpu.roll(x, w - 32, x.ndim - 1)
    bwd = pltpu.roll(x, 32, x.ndim - 1)
    return jnp.where((lane & 63) < 32, fwd, bwd)


def _rms_matmul_kernel(*refs, side):
    if side:
        x_ref, g_ref, w_ref, ws_ref, o_ref, os_ref, xn_ref = refs
    else:
        x_ref, g_ref, w_ref, o_ref, xn_ref = refs

    @pl.when(pl.program_id(1) == 0)
    def _():
        xn_ref[...] = _rms(x_ref[...], g_ref[...]).astype(xn_ref.dtype)
        if side:
            os_ref[...] = _dot(xn_ref[...], ws_ref[...])

    o_ref[...] = _dot(xn_ref[...], w_ref[...]).astype(o_ref.dtype)


def rms_matmul(x, g, w, *, out_dtype=F32, w_side=None):
    n, d = x.shape
    nout = w.shape[1]
    side = w_side is not None
    tn = _pick(nout, (1920, 1152, 1024, 512, 256, 128))

    def vmem_for(tm):
        return (2 * (_nbytes((tm, d), F32) + _nbytes((d, tn), w.dtype) + _nbytes((tm, tn), out_dtype))
                + _nbytes((tm, d), MXU_DTYPE) + _nbytes((tm, tn), F32) + (8 << 20))

    tm = next(t for t in (1024, 512, 256, 128, n) if n % t == 0 and (vmem_for(t) <= TILE_VMEM_BUDGET or t <= 128))
    vmem = vmem_for(tm)
    in_specs = [pl.BlockSpec((tm, d), lambda i, j: (i, 0)),
                pl.BlockSpec((1, d), lambda i, j: (0, 0)),
                pl.BlockSpec((d, tn), lambda i, j: (0, j))]
    args = [x, g.reshape(1, d), w]
    out_shape = jax.ShapeDtypeStruct((n, nout), out_dtype)
    out_specs = pl.BlockSpec((tm, tn), lambda i, j: (i, j))
    if side:
        ns = w_side.shape[1]
        in_specs.append(pl.BlockSpec((d, ns), lambda i, j: (0, 0)))
        args.append(w_side)
        out_shape = (out_shape, jax.ShapeDtypeStruct((n, ns), F32))
        out_specs = (out_specs, pl.BlockSpec((tm, ns), lambda i, j: (i, 0)))
        vmem += 2 * (_nbytes((d, ns), w_side.dtype) + _nbytes((tm, ns), F32))
    return pl.pallas_call(
        functools.partial(_rms_matmul_kernel, side=side),
        out_shape=out_shape,
        grid=(n // tm, nout // tn),
        in_specs=in_specs,
        out_specs=out_specs,
        scratch_shapes=[pltpu.VMEM((tm, d), MXU_DTYPE)],
        compiler_params=_params(("parallel", "arbitrary"), vmem),
        name="rms_matmul",
    )(*args)


def _ffn_kernel(*refs, final_norm, nf):
    x_ref, g_ref = refs[0:2]
    blocks = (refs[2:5], refs[5:8])
    gf_ref = refs[8] if final_norm else None
    o_ref, xn_ref = refs[-2:]
    s = pl.program_id(1)

    @pl.when(s == 0)
    def _():
        xf = x_ref[...]
        xn_ref[...] = _rms(xf, g_ref[...]).astype(xn_ref.dtype)
        o_ref[...] = xf

    def down(w1a_ref, w1b_ref, w2_ref):
        xn = xn_ref[...]
        a = _dot(xn, w1a_ref[...])
        b = _dot(xn, w1b_ref[...])
        return _dot((_silu(a) * b).astype(MXU_DTYPE), w2_ref[...])

    if nf % 2 == 0:
        o_ref[...] += down(*blocks[0]) + down(*blocks[1])
    else:
        @pl.when(s == 0)
        def _():
            o_ref[...] += down(*blocks[0])

        @pl.when(s > 0)
        def _():
            o_ref[...] += down(*blocks[0]) + down(*blocks[1])

    if final_norm:
        @pl.when(s == pl.num_programs(1) - 1)
        def _():
            o_ref[...] = _rms(o_ref[...], gf_ref[...])


def ffn(x, g, w1, w2, layer, which, final_g=None):
    n, d = x.shape
    dff = w2.shape[2]
    tm = _pick(n, (512, 256, 128))
    tf = _pick(dff, (512, 256, 128))
    nf = dff // tf
    vmem = (2 * (2 * _nbytes((tm, d), F32) + 6 * _nbytes((d, tf), w1.dtype))
            + _nbytes((tm, d), MXU_DTYPE) + _nbytes((tm, d), F32) + 6 * _nbytes((tm, tf), F32) + (6 << 20))
    in_specs = [pl.BlockSpec((tm, d), lambda i, s: (i, 0)),
                pl.BlockSpec((1, d), lambda i, s: (0, 0))]
    args = [x, g.reshape(1, d)]
    lead = nf % 2
    for half in range(2):
        blk = lambda s, half=half: jnp.maximum(2 * s - lead + half, half * (lead + 1))
        in_specs += [pl.BlockSpec((None, None, d, tf), lambda i, s, blk=blk: (layer, which, 0, blk(s))),
                     pl.BlockSpec((None, None, d, tf), lambda i, s, blk=blk: (layer, which, 0, blk(s) + nf)),
                     pl.BlockSpec((None, None, tf, d), lambda i, s, blk=blk: (layer, which, blk(s), 0))]
        args += [w1, w1, w2]
    if final_g is not None:
        in_specs.append(pl.BlockSpec((1, d), lambda i, s: (0, 0)))
        args.append(final_g.reshape(1, d))
    return pl.pallas_call(
        functools.partial(_ffn_kernel, final_norm=final_g is not None, nf=nf),
        out_shape=jax.ShapeDtypeStruct((n, d), F32),
        grid=(n // tm, -(-nf // 2)),
        in_specs=in_specs,
        out_specs=pl.BlockSpec((tm, d), lambda i, f: (i, 0)),
        scratch_shapes=[pltpu.VMEM((tm, d), MXU_DTYPE)],
        compiler_params=_params(("parallel", "arbitrary"), vmem),
        name="ffn",
    )(*args)


def _matmul_residual_kernel(*refs, n_in):
    x_ref = refs[0]
    o_ref = refs[1 + 2 * n_in]
    acc = x_ref[...]
    for h_ref, w_ref in zip(refs[1:1 + n_in], refs[1 + n_in:1 + 2 * n_in]):
        acc = acc + _dot(h_ref[...], w_ref[...])
    o_ref[...] = acc


def matmul_residual(x, hs, ws):
    n, d = x.shape

    def vmem_for(tm, tn):
        return (2 * (2 * _nbytes((tm, tn), F32) + sum(_nbytes((tm, h.shape[1]), h.dtype) for h in hs)
                     + sum(_nbytes((w.shape[0], tn), w.dtype) for w in ws))
                + _nbytes((tm, tn), F32) + (8 << 20))

    tm, tn = next((a, b) for a, b in ((512, d), (1024, 1024), (512, 1024), (256, 512), (128, 128), (n, d))
                  if n % a == 0 and d % b == 0 and (vmem_for(a, b) <= TILE_VMEM_BUDGET or a <= 128))
    vmem = vmem_for(tm, tn)
    in_specs = [pl.BlockSpec((tm, tn), lambda i, j: (i, j))]
    for h in hs:
        in_specs.append(pl.BlockSpec((tm, h.shape[1]), lambda i, j: (i, 0)))
    for w in ws:
        in_specs.append(pl.BlockSpec((w.shape[0], tn), lambda i, j: (0, j)))
    return pl.pallas_call(
        functools.partial(_matmul_residual_kernel, n_in=len(hs)),
        out_shape=jax.ShapeDtypeStruct((n, d), F32),
        grid=(n // tm, d // tn),
        in_specs=in_specs,
        out_specs=pl.BlockSpec((tm, tn), lambda i, j: (i, j)),
        compiler_params=_params(("parallel", "arbitrary"), vmem),
        name="matmul_residual",
    )(x, *hs, *ws)


def _mem_attn_kernel(x_ref, g_ref, wq_ref, k_ref, v_ref, wo_ref, o_ref, att_ref, *, bt, tq, heads, hd):
    head_axis = len(k_ref.shape) == 4
    xf = x_ref[...]
    xn = _rms(xf, g_ref[...]).astype(MXU_DTYPE)
    q = _dot(xn, wq_ref[...]).astype(MXU_DTYPE)
    scale = hd ** -0.5
    for b in range(bt):
        for h in range(heads):
            qh = q[b * tq:(b + 1) * tq, h * hd:(h + 1) * hd]
            if head_axis:
                kh = k_ref[b, :, h, :].astype(MXU_DTYPE)
                vh = v_ref[b, :, h, :].astype(MXU_DTYPE)
            else:
                kh = k_ref[b, :, h * hd:(h + 1) * hd].astype(MXU_DTYPE)
                vh = v_ref[b, :, h * hd:(h + 1) * hd].astype(MXU_DTYPE)
            s = _dot_nt(qh, kh) * scale
            m = jnp.max(s, axis=-1, keepdims=True)
            p = jnp.exp(s - m)
            l = jnp.sum(p, axis=-1, keepdims=True)
            oh = _dot(p.astype(MXU_DTYPE), vh) / l
            att_ref[b * tq:(b + 1) * tq, h * hd:(h + 1) * hd] = oh.astype(att_ref.dtype)
    o_ref[...] = xf + _dot(att_ref[...], wo_ref[...])


def mem_attn(x, g, wq, mem_k, mem_v, wo, *, batch, heads, layer=None):
    n, d = x.shape
    t = n // batch
    cached = layer is not None
    m = mem_k.shape[2] if cached else mem_k.shape[1]
    inner = wq.shape[1]
    hd = inner // heads
    if t >= 128:
        bt, tq = 1, _pick(t, (512, 256, 128))
    else:
        bt, tq = _pick(batch, (4, 2, 1)), t
    nt = t // tq
    rows = bt * tq
    if cached:
        kv_spec = pl.BlockSpec((None, bt, m, heads, hd), lambda b, i: (layer, b, 0, 0, 0))
        kv_bytes = _nbytes((bt, m, max(heads, V7X_SUBLANES), hd), mem_k.dtype)
    else:
        kv_spec = pl.BlockSpec((bt, m, inner), lambda b, i: (b, 0, 0))
        kv_bytes = _nbytes((bt, m, inner), mem_k.dtype)
    vmem = (2 * (2 * _nbytes((rows, d), F32) + 2 * kv_bytes + 2 * _nbytes((d, inner), wq.dtype))
            + 2 * _nbytes((rows, d), F32) + (8 << 20))
    return pl.pallas_call(
        functools.partial(_mem_attn_kernel, bt=bt, tq=tq, heads=heads, hd=hd),
        out_shape=jax.ShapeDtypeStruct((n, d), F32),
        grid=(batch // bt, nt),
        in_specs=[pl.BlockSpec((rows, d), lambda b, i: (b * nt + i, 0)),
                  pl.BlockSpec((1, d), lambda b, i: (0, 0)),
                  pl.BlockSpec((d, inner), lambda b, i: (0, 0)),
                  kv_spec, kv_spec,
                  pl.BlockSpec((inner, d), lambda b, i: (0, 0))],
        out_specs=pl.BlockSpec((rows, d), lambda b, i: (b * nt + i, 0)),
        scratch_shapes=[pltpu.VMEM((rows, inner), MXU_DTYPE)],
        compiler_params=_params(("parallel", "arbitrary"), vmem),
        name="mem_attn",
    )(x, g.reshape(1, d), wq, mem_k, mem_v, wo)


def _mla_prep_kernel(*refs, q_lora, kv_lora, rope_dim, rope_slabs, qscale, heads, nope):
    expand = len(refs) == 12
    if expand:
        (p_ref, qn_ref, kvn_ref, wuq_ref, cos_ref, sin_ref, wukv_ref,
         q_ref, ckv_ref, kpe_ref, k_ref, v_ref) = refs
    else:
        p_ref, qn_ref, kvn_ref, wuq_ref, cos_ref, sin_ref, q_ref, ckv_ref, kpe_ref, k_ref = refs
    cos = cos_ref[...]
    sin = sin_ref[...]
    cqn = _rms(p_ref[:, 0:q_lora], qn_ref[...]).astype(MXU_DTYPE)
    q = _dot(cqn, wuq_ref[...])
    if qscale != 1.0:
        q = q * qscale
    q_ref[...] = q.astype(q_ref.dtype)
    for lo in rope_slabs:
        xs = q[:, lo:lo + V7X_LANES]
        q_ref[:, lo:lo + V7X_LANES] = (xs * cos + _swap32(xs) * sin).astype(q_ref.dtype)
    ckv = _rms(p_ref[:, q_lora:q_lora + kv_lora], kvn_ref[...])
    ckv_ref[...] = ckv
    kp = p_ref[:, q_lora + kv_lora:q_lora + kv_lora + V7X_LANES]
    kpr = kp * cos + _swap32(kp) * sin
    kpe_ref[...] = kpr[:, 0:rope_dim]
    kpr_m = kpr.astype(k_ref.dtype)
    if expand:
        kv = _dot(ckv.astype(MXU_DTYPE), wukv_ref[...])
        hw = nope + V7X_LANES
        for h in range(heads):
            k_ref[:, h * hw:h * hw + nope] = kv[:, h * nope:(h + 1) * nope].astype(k_ref.dtype)
            k_ref[:, h * hw + nope:(h + 1) * hw] = kpr_m
        v_ref[...] = kv[:, heads * nope:].astype(v_ref.dtype)
    else:
        k_ref[:, 0:kv_lora] = ckv.astype(k_ref.dtype)
        k_ref[:, kv_lora:kv_lora + V7X_LANES] = kpr_m


def mla_prep(proj, q_norm, kv_norm, w_uq, cos_rows, sin_rows, w_ukv, *, q_lora, kv_lora, rope_dim,
             rope_slabs, qscale, heads, nope):
    n = proj.shape[0]
    qcols = w_uq.shape[1]
    tm = _pick(n, (512, 256, 128))
    expand = w_ukv is not None
    row = lambda w: pl.BlockSpec((tm, w), lambda i: (i, 0))
    full = lambda a: pl.BlockSpec(a.shape, lambda i: (0, 0))
    in_specs = [row(proj.shape[1]), pl.BlockSpec((1, q_lora), lambda i: (0, 0)),
                pl.BlockSpec((1, kv_lora), lambda i: (0, 0)), full(w_uq), row(V7X_LANES), row(V7X_LANES)]
    args = [proj, q_norm.reshape(1, -1), kv_norm.reshape(1, -1), w_uq, cos_rows, sin_rows]
    out_shape = [jax.ShapeDtypeStruct((n, qcols), MXU_DTYPE), jax.ShapeDtypeStruct((n, kv_lora), F32),
                 jax.ShapeDtypeStruct((n, rope_dim), F32)]
    out_specs = [row(qcols), row(kv_lora), row(rope_dim)]
    vmem = (2 * (_nbytes((tm, proj.shape[1]), F32) + _nbytes(w_uq.shape, w_uq.dtype)
                 + _nbytes((tm, qcols), MXU_DTYPE) + 3 * _nbytes((tm, kv_lora + V7X_LANES), F32))
            + 3 * _nbytes((tm, qcols), F32) + (8 << 20))
    if expand:
        kcols = heads * (nope + V7X_LANES)
        vcols = w_ukv.shape[1] - heads * nope
        in_specs.append(full(w_ukv))
        args.append(w_ukv)
        out_shape += [jax.ShapeDtypeStruct((n, kcols), MXU_DTYPE), jax.ShapeDtypeStruct((n, vcols), MXU_DTYPE)]
        out_specs += [row(kcols), row(vcols)]
        vmem += (2 * (_nbytes(w_ukv.shape, w_ukv.dtype) + _nbytes((tm, kcols + vcols), MXU_DTYPE))
                 + 2 * _nbytes((tm, w_ukv.shape[1]), F32))
    else:
        out_shape.append(jax.ShapeDtypeStruct((n, kv_lora + V7X_LANES), MXU_DTYPE))
        out_specs.append(row(kv_lora + V7X_LANES))
    return pl.pallas_call(
        functools.partial(_mla_prep_kernel, q_lora=q_lora, kv_lora=kv_lora, rope_dim=rope_dim,
                          rope_slabs=rope_slabs, qscale=qscale, heads=heads, nope=nope),
        out_shape=tuple(out_shape),
        grid=(n // tm,),
        in_specs=in_specs,
        out_specs=tuple(out_specs),
        compiler_params=_params(("parallel",), vmem),
        name="mla_prep",
    )(*args)


def _lanes(x, width):
    reps = width // V7X_LANES
    return x if reps == 1 else jnp.concatenate([x] * reps, axis=-1)


def _mha_attn_kernel(qi_ref, kj_ref, flag_ref, q_ref, k_ref, v_ref, o_ref, acc_ref, m_ref,
                     *, heads, hw, vd, tq, tk, q_offset, kv_len):
    step = pl.program_id(1)
    i = qi_ref[step]
    j = kj_ref[step]
    flags = flag_ref[step]
    aw = vd + V7X_LANES

    @pl.when(j == 0)
    def _():
        m_ref[...] = jnp.full_like(m_ref, NEG)
        acc_ref[...] = jnp.zeros_like(acc_ref)

    ones = jnp.ones((tk, V7X_LANES), MXU_DTYPE)

    def all_heads(bias):
        for h in range(heads):
            s = _dot_nt(q_ref[0, :, h * hw:(h + 1) * hw], k_ref[0, :, h * hw:(h + 1) * hw])
            if bias is not None:
                s = s + bias
            m_old = m_ref[h]
            m_new = jnp.maximum(m_old, jnp.max(s, axis=-1, keepdims=True))
            alpha = jnp.exp2(m_old - m_new)
            p = jnp.exp2(s - _lanes(m_new, tk)).astype(MXU_DTYPE)
            m_ref[h] = m_new
            v_ext = jnp.concatenate([v_ref[0, :, h * vd:(h + 1) * vd], ones], axis=-1)
            acs = slice(h * aw, (h + 1) * aw)
            acc_ref[:, acs] = _lanes(alpha, aw) * acc_ref[:, acs] + _dot(p, v_ext)

    @pl.when((flags & 2) == 0)
    def _():
        all_heads(None)

    @pl.when((flags & 2) != 0)
    def _():
        qpos = q_offset + i * tq + lax.broadcasted_iota(jnp.int32, (tq, tk), 0)
        kpos = j * tk + lax.broadcasted_iota(jnp.int32, (tq, tk), 1)
        visible = ((kpos // CHUNK) <= (qpos // CHUNK)) & (kpos < kv_len)
        all_heads(jnp.where(visible, 0.0, NEG))

    @pl.when((flags & 1) != 0)
    def _():
        for h in range(heads):
            num = acc_ref[:, h * aw:h * aw + vd]
            den = _lanes(acc_ref[:, h * aw + vd:(h + 1) * aw], vd)
            o_ref[0, :, h * vd:(h + 1) * vd] = (num / den).astype(o_ref.dtype)


def mha_attn(q, k, v, *, heads, tq, tk, q_offset, kv_len):
    b, t, _ = q.shape
    s = k.shape[1]
    hw = q.shape[2] // heads
    vd = v.shape[2] // heads
    nq, nk = t // tq, s // tk
    qi, kj, flags = [], [], []
    for i in range(nq):
        first_q = q_offset + i * tq
        last_key = ((first_q + tq - 1) // CHUNK + 1) * CHUNK - 1
        jl = min(last_key // tk, nk - 1)
        for j in range(jl + 1):
            fully_visible = ((j + 1) * tk - 1) // CHUNK <= first_q // CHUNK and (j + 1) * tk <= kv_len
            qi.append(i)
            kj.append(j)
            flags.append(int(j == jl) + 2 * int(not fully_visible))
    sched = [jnp.asarray(a, jnp.int32) for a in (qi, kj, flags)]
    aw = vd + V7X_LANES
    vmem = (2 * (2 * _nbytes((tq, heads * hw), q.dtype) + 2 * _nbytes((tk, heads * vd), v.dtype)
                 + _nbytes((tq, heads * vd), MXU_DTYPE))
            + _nbytes((tq, heads * aw), F32) + _nbytes((heads, tq, V7X_LANES), F32)
            + 6 * _nbytes((tq, tk), F32) + (8 << 20))
    grid_spec = pltpu.PrefetchScalarGridSpec(
        num_scalar_prefetch=3,
        grid=(b, len(qi)),
        in_specs=[pl.BlockSpec((1, tq, heads * hw), lambda bb, p, qi_r, kj_r, l_r: (bb, qi_r[p], 0)),
                  pl.BlockSpec((1, tk, heads * hw), lambda bb, p, qi_r, kj_r, l_r: (bb, kj_r[p], 0)),
                  pl.BlockSpec((1, tk, heads * vd), lambda bb, p, qi_r, kj_r, l_r: (bb, kj_r[p], 0))],
        out_specs=pl.BlockSpec((1, tq, heads * vd), lambda bb, p, qi_r, kj_r, l_r: (bb, qi_r[p], 0)),
        scratch_shapes=[pltpu.VMEM((tq, heads * aw), F32),
                        pltpu.VMEM((heads, tq, V7X_LANES), F32)])
    return pl.pallas_call(
        functools.partial(_mha_attn_kernel, heads=heads, hw=hw, vd=vd, tq=tq, tk=tk,
                          q_offset=q_offset, kv_len=kv_len),
        out_shape=jax.ShapeDtypeStruct((b, t, heads * vd), MXU_DTYPE),
        grid_spec=grid_spec,
        compiler_params=_params(("parallel", "arbitrary"), vmem),
        name="mha_attn",
    )(*sched, q, k, v)


def _mla_decode_kernel(q_ref, pc_ref, pr_ref, kn_ref, wuk_ref, wuv_ref, o_ref, qs_ref,
                       *, heads, t, nope, lat, rope_dim, q_offset, scale):
    past = pc_ref.shape[1]
    rows = heads * t
    rope_lo = heads * nope
    for h in range(heads):
        qn = q_ref[0, :, h * nope:(h + 1) * nope]
        qs_ref[h * t:(h + 1) * t, 0:lat] = (_dot(qn, wuk_ref[h]) * scale).astype(qs_ref.dtype)
        qr = q_ref[0, :, rope_lo + h * V7X_LANES:rope_lo + (h + 1) * V7X_LANES]
        qs_ref[h * t:(h + 1) * t, lat:lat + V7X_LANES] = (qr.astype(F32) * scale).astype(qs_ref.dtype)

    def bias(k0, n):
        qpos = q_offset + lax.broadcasted_iota(jnp.int32, (t, n), 0)
        kpos = k0 + lax.broadcasted_iota(jnp.int32, (t, n), 1)
        return jnp.where((kpos // CHUNK) <= (qpos // CHUNK), 0.0, NEG)

    def masked(s, b):
        return (s.reshape(heads, t, s.shape[1]) + b[None]).reshape(rows, s.shape[1])

    kc = pc_ref[0].astype(MXU_DTYPE)
    kr = pr_ref[0].astype(MXU_DTYPE)
    kn = kn_ref[0]
    s_past = masked(_dot_nt(qs_ref[:, 0:lat], kc) + _dot_nt(qs_ref[:, lat:lat + rope_dim], kr), bias(0, past))
    s_new = masked(_dot_nt(qs_ref[...], kn), bias(q_offset, t))
    m = jnp.maximum(jnp.max(s_past, axis=-1, keepdims=True), jnp.max(s_new, axis=-1, keepdims=True))
    p_past = jnp.exp(s_past - m)
    p_new = jnp.exp(s_new - m)
    l = jnp.sum(p_past, axis=-1, keepdims=True) + jnp.sum(p_new, axis=-1, keepdims=True)
    ol = (_dot(p_past.astype(MXU_DTYPE), kc) + _dot(p_new.astype(MXU_DTYPE), kn[:, 0:lat])) / l
    vd = wuv_ref.shape[2]
    for h in range(heads):
        o_ref[0, :, h * vd:(h + 1) * vd] = (
            _dot(ol[h * t:(h + 1) * t, :].astype(MXU_DTYPE), wuv_ref[h]).astype(o_ref.dtype))


def mla_decode(q, past_ckv, past_kpe, kcat, w_uk, w_uv, *, q_offset, scale):
    b, t, _ = q.shape
    past, rope_dim = past_kpe.shape[1], past_kpe.shape[2]
    heads, nope, lat = w_uk.shape
    vd = w_uv.shape[2]
    rows = heads * t
    vmem = (2 * (_nbytes((t, q.shape[2]), q.dtype) + _nbytes((past, lat + V7X_LANES), F32)
                 + 2 * _nbytes(w_uk.shape, w_uk.dtype) + _nbytes((t, heads * vd), MXU_DTYPE))
            + _nbytes((rows, lat + V7X_LANES), MXU_DTYPE) + _nbytes((past, lat + V7X_LANES), MXU_DTYPE)
            + 4 * _nbytes((rows, past), F32) + (8 << 20))
    assert vmem <= V7X_VMEM_BYTES, "cached rows must fit one VMEM block"
    return pl.pallas_call(
        functools.partial(_mla_decode_kernel, heads=heads, t=t, nope=nope, lat=lat, rope_dim=rope_dim,
                          q_offset=q_offset, scale=scale),
        out_shape=jax.ShapeDtypeStruct((b, t, heads * vd), MXU_DTYPE),
        grid=(b,),
        in_specs=[pl.BlockSpec((1, t, q.shape[2]), lambda bb: (bb, 0, 0)),
                  pl.BlockSpec((1, past, lat), lambda bb: (bb, 0, 0)),
                  pl.BlockSpec((1, past, rope_dim), lambda bb: (bb, 0, 0)),
                  pl.BlockSpec((1, t, kcat.shape[2]), lambda bb: (bb, 0, 0)),
                  pl.BlockSpec(w_uk.shape, lambda bb: (0, 0, 0)),
                  pl.BlockSpec(w_uv.shape, lambda bb: (0, 0, 0))],
        out_specs=pl.BlockSpec((1, t, heads * vd), lambda bb: (bb, 0, 0)),
        scratch_shapes=[pltpu.VMEM((rows, lat + V7X_LANES), MXU_DTYPE)],
        compiler_params=_params(("parallel",), vmem),
        name="mla_decode",
    )(q, past_ckv, past_kpe, kcat, w_uk, w_uv)


class _ScanPlan(NamedTuple):
    body: Callable
    args: list
    in_specs: list
    out_shape: list
    out_specs: list
    scratch: list
    vmem: int


def _ssd_kernel(*refs, L, groups, hpg, hdim, nstate, has_state):
    if has_state:
        (z_ref, xs_ref, b_ref, c_ref, dt_ref, cw_ref, cb_ref, dtb_ref, alog_ref, dsk_ref, nrm_ref,
         cst_ref, hst_ref, y_ref, ncv_ref, nst_ref, xbuf, h_scr, cum_scr, xdt_scr, yin_scr) = refs
    else:
        (z_ref, xs_ref, b_ref, c_ref, dt_ref, cw_ref, cb_ref, dtb_ref, alog_ref, dsk_ref, nrm_ref,
         cst_ref, y_ref, ncv_ref, nst_ref, xbuf, h_scr, cum_scr, xdt_scr, yin_scr) = refs
        hst_ref = None
    c = pl.program_id(1)
    nc = pl.num_programs(1)
    dx = groups * hpg * hdim
    dn = groups * nstate
    gw = hpg * hdim
    pad = CONV_PAD_ROWS

    @pl.when(c == 0)
    def _():
        xbuf[0:pad, :] = cst_ref[0]
        if has_state:
            h_scr[...] = hst_ref[0]
        else:
            h_scr[...] = jnp.zeros_like(h_scr)

    xbuf[pad:pad + L, 0:dx] = xs_ref[...].astype(F32)
    xbuf[pad:pad + L, dx:dx + dn] = b_ref[...].astype(F32)
    xbuf[pad:pad + L, dx + dn:dx + 2 * dn] = c_ref[...].astype(F32)
    acc = xbuf[pad - CONV_K + 1:pad - CONV_K + 1 + L, :] * cw_ref[0:1, :] + cb_ref[...]
    for jj in range(1, CONV_K):
        lo = pad - CONV_K + 1 + jj
        acc = acc + xbuf[lo:lo + L, :] * cw_ref[jj:jj + 1, :]
    xc = _silu(acc)
    tail = xbuf[L:L + pad, :]
    xbuf[0:pad, :] = tail

    @pl.when(c == nc - 1)
    def _():
        ncv_ref[0] = tail

    dtr = dt_ref[...] + dtb_ref[...]
    dt = jnp.maximum(dtr, 0.0) + jnp.log1p(jnp.exp(-jnp.abs(dtr)))
    la = dt * (-jnp.exp(alog_ref[...]))
    row = lax.broadcasted_iota(jnp.int32, (L, L), 0)
    col = lax.broadcasted_iota(jnp.int32, (L, L), 1)
    causal = row >= col
    tri = jnp.where(causal, 1.0, 0.0).astype(MXU_DTYPE)
    cum = sum(_dot(tri, piece) for piece in _split3(la))
    nh = groups * hpg
    eye = jnp.where(lax.broadcasted_iota(jnp.int32, (V7X_LANES, V7X_LANES), 0)
                    == lax.broadcasted_iota(jnp.int32, (V7X_LANES, V7X_LANES), 1), 1.0, 0.0).astype(MXU_DTYPE)
    cum_t = sum(_dot_nt(eye, piece) for piece in _split3(cum))

    half = lax.broadcasted_iota(jnp.int32, (L, V7X_LANES), 1) < hdim
    per_vreg = V7X_LANES // hdim
    for g in range(groups):
        bg = xc[:, dx + g * nstate:dx + (g + 1) * nstate]
        cg = xc[:, dx + dn + g * nstate:dx + dn + (g + 1) * nstate].astype(MXU_DTYPE)
        qk = _dot_nt(cg, bg.astype(MXU_DTYPE))
        for sl in range(gw // V7X_LANES):
            lane0 = g * gw + sl * V7X_LANES
            h0 = lane0 // hdim
            cb = [jnp.broadcast_to(cum[:, h0 + u:h0 + u + 1], (L, V7X_LANES)) for u in range(per_vreg)]
            db = [jnp.broadcast_to(dt[:, h0 + u:h0 + u + 1], (L, V7X_LANES)) for u in range(per_vreg)]
            cum_e = jnp.where(half, cb[0], cb[1])
            dt_e = jnp.where(half, db[0], db[1])
            xdt = xc[:, lane0:lane0 + V7X_LANES] * dt_e
            xdt_m = xdt.astype(MXU_DTYPE)
            ys = []
            for u in range(per_vreg):
                seg = cb[u][:, 0:L] - cum_t[h0 + u:h0 + u + 1, :]
                decay = jnp.exp(jnp.where(causal, seg, NEG))
                ys.append(_dot((qk * decay).astype(MXU_DTYPE), xdt_m))
            cum_scr[:, lane0:lane0 + V7X_LANES] = cum_e
            xdt_scr[:, lane0:lane0 + V7X_LANES] = xdt
            yin_scr[:, lane0:lane0 + V7X_LANES] = jnp.where(half, ys[0], ys[1])

    for g in range(groups):
        gs = slice(g * gw, (g + 1) * gw)
        bg = xc[:, dx + g * nstate:dx + (g + 1) * nstate].astype(MXU_DTYPE)
        cg = xc[:, dx + dn + g * nstate:dx + dn + (g + 1) * nstate].astype(MXU_DTYPE)
        cum_g = cum_scr[:, gs]
        last = cum_scr[L - 1:L, gs]
        hg = h_scr[g]
        y_inter = _dot(cg, hg.astype(MXU_DTYPE)) * jnp.exp(cum_g)
        wx = (jnp.exp(last - cum_g) * xdt_scr[:, gs]).astype(MXU_DTYPE)
        h_scr[g] = jnp.exp(last) * hg + _dot_tn(bg, wx)
        y = yin_scr[:, gs] + y_inter + dsk_ref[:, gs] * xc[:, gs]
        y = y * _silu(z_ref[:, gs].astype(F32))
        y_ref[:, gs] = (_rms(y) * nrm_ref[:, gs]).astype(y_ref.dtype)

    @pl.when(c == nc - 1)
    def _():
        nst_ref[0] = h_scr[...]


def ssd_scan(proj, dt_raw, col, conv_w, conv_b, dt_bias, a_log, d_skip_e, ssd_norm, conv_state, ssd_state,
             *, batch, L, groups, hpg, hdim, nstate):
    n = proj.shape[0]
    t = n // batch
    nc = t // L
    dx = groups * hpg * hdim
    dn = groups * nstate
    cdim = dx + 2 * dn
    has_state = ssd_state is not None
    rowmap = lambda b, c: b * nc + c
    in_specs = [pl.BlockSpec((L, dx), lambda b, c: (rowmap(b, c), col["z"])),
                pl.BlockSpec((L, dx), lambda b, c: (rowmap(b, c), col["xs"])),
                pl.BlockSpec((L, dn), lambda b, c: (rowmap(b, c), col["B"])),
                pl.BlockSpec((L, dn), lambda b, c: (rowmap(b, c), col["C"])),
                pl.BlockSpec((L, V7X_LANES), lambda b, c: (rowmap(b, c), 0)),
                pl.BlockSpec((CONV_K, cdim), lambda b, c: (0, 0)),
                pl.BlockSpec((1, cdim), lambda b, c: (0, 0)),
                pl.BlockSpec((1, V7X_LANES), lambda b, c: (0, 0)),
                pl.BlockSpec((1, V7X_LANES), lambda b, c: (0, 0)),
                pl.BlockSpec((1, dx), lambda b, c: (0, 0)),
                pl.BlockSpec((1, dx), lambda b, c: (0, 0)),
                pl.BlockSpec((1, CONV_PAD_ROWS, cdim), lambda b, c: (b, 0, 0))]
    args = [proj, proj, proj, proj, dt_raw, conv_w, conv_b.reshape(1, cdim), dt_bias, a_log,
            d_skip_e, ssd_norm.reshape(1, dx), conv_state]
    if has_state:
        in_specs.append(pl.BlockSpec((1, groups, nstate, hpg * hdim), lambda b, c: (b, 0, 0, 0)))
        args.append(ssd_state)
    state_bytes = _nbytes((groups, nstate, hpg * hdim), F32)
    vmem = (2 * (2 * _nbytes((L, dx), F32) + 2 * _nbytes((L, dn), F32) + _nbytes((L, dx), MXU_DTYPE)
                 + 2 * _nbytes((CONV_PAD_ROWS, cdim), F32) + 2 * state_bytes)
            + state_bytes + 12 * _nbytes((L + CONV_PAD_ROWS, cdim), F32) + (8 << 20))
    return _ScanPlan(
        body=functools.partial(_ssd_kernel, L=L, groups=groups, hpg=hpg, hdim=hdim, nstate=nstate,
                               has_state=has_state),
        args=args, in_specs=in_specs,
        out_shape=[jax.ShapeDtypeStruct((n, dx), MXU_DTYPE),
                   jax.ShapeDtypeStruct((batch, CONV_PAD_ROWS, cdim), F32),
                   jax.ShapeDtypeStruct((batch, groups, nstate, hpg * hdim), F32)],
        out_specs=[pl.BlockSpec((L, dx), lambda b, c: (rowmap(b, c), 0)),
                   pl.BlockSpec((1, CONV_PAD_ROWS, cdim), lambda b, c: (b, 0, 0)),
                   pl.BlockSpec((1, groups, nstate, hpg * hdim), lambda b, c: (b, 0, 0, 0))],
        scratch=[pltpu.VMEM((L + CONV_PAD_ROWS, cdim), F32),
                 pltpu.VMEM((groups, nstate, hpg * hdim), F32),
                 pltpu.VMEM((L, dx), F32),
                 pltpu.VMEM((L, dx), F32),
                 pltpu.VMEM((L, dx), F32)],
        vmem=vmem)


def _ret_kernel(*refs, L, heads, hd, has_state):
    if has_state:
        (q_ref, k_ref, v_ref, gate_ref, cos_ref, sin_ref, dec_ref, ecum_ref, wv_ref, sdec_ref, st_ref,
         o_ref, nst_ref, s_scr) = refs
    else:
        (q_ref, k_ref, v_ref, gate_ref, cos_ref, sin_ref, dec_ref, ecum_ref, wv_ref, sdec_ref,
         o_ref, nst_ref, s_scr) = refs
        st_ref = None
    c = pl.program_id(1)
    nc = pl.num_programs(1)
    half = hd // 2

    @pl.when(c == 0)
    def _():
        if has_state:
            s_scr[...] = st_ref[0]
        else:
            s_scr[...] = jnp.zeros_like(s_scr)

    cos = cos_ref[...]
    sin = sin_ref[...]

    def rope(ref, h, mult):
        x1 = ref[:, h * hd:h * hd + half].astype(F32)
        x2 = ref[:, h * hd + half:(h + 1) * hd].astype(F32)
        out = jnp.concatenate([x1 * cos - x2 * sin, x2 * cos + x1 * sin], axis=-1)
        return (out * mult).astype(MXU_DTYPE) if mult != 1.0 else out.astype(MXU_DTYPE)

    for h in range(heads):
        hs = slice(h * hd, (h + 1) * hd)
        qr = rope(q_ref, h, 1.0)
        kr = rope(k_ref, h, hd ** -0.5)
        vf = v_ref[:, hs]
        qk = _dot_nt(qr, kr)
        y = _dot((qk * dec_ref[h]).astype(MXU_DTYPE), vf.astype(MXU_DTYPE))
        s_old = s_scr[h]
        y = y + _dot(qr, s_old.astype(MXU_DTYPE)) * ecum_ref[h]
        s_scr[h] = sdec_ref[h] * s_old + _dot_tn(kr, (wv_ref[h] * vf.astype(F32)).astype(MXU_DTYPE))
        o_ref[:, hs] = (_rms(y) * _silu(gate_ref[:, hs].astype(F32))).astype(o_ref.dtype)

    @pl.when(c == nc - 1)
    def _():
        nst_ref[0] = s_scr[...]


def ret_scan(proj, col, cos_rows, sin_rows, ret_state, *, batch, L, heads, hd):
    n = proj.shape[0]
    t = n // batch
    nc = t // L
    inner = heads * hd
    has_state = ret_state is not None
    lg = jnp.log1p(-jnp.exp2(-5.0 - jnp.arange(heads, dtype=F32)))[:, None, None]
    li = jnp.arange(L, dtype=F32)
    diff = li[:, None] - li[None, :]
    dec = jnp.where(diff >= 0, jnp.exp(jnp.where(diff >= 0, diff, 0.0)[None] * lg), 0.0)
    ecum = jnp.broadcast_to(jnp.exp((li[None, :, None] + 1.0) * lg), (heads, L, hd))
    wv = jnp.broadcast_to(jnp.exp((L - 1.0 - li[None, :, None]) * lg), (heads, L, hd))
    sdec = jnp.broadcast_to(jnp.exp(L * lg), (heads, 1, hd))
    rowmap = lambda b, c: b * nc + c
    in_specs = [pl.BlockSpec((L, inner), lambda b, c: (rowmap(b, c), col["q"])),
                pl.BlockSpec((L, inner), lambda b, c: (rowmap(b, c), col["k"])),
                pl.BlockSpec((L, inner), lambda b, c: (rowmap(b, c), col["v"])),
                pl.BlockSpec((L, inner), lambda b, c: (rowmap(b, c), col["gate"])),
                pl.BlockSpec((L, hd // 2), lambda b, c: (rowmap(b, c), 0)),
                pl.BlockSpec((L, hd // 2), lambda b, c: (rowmap(b, c), 0)),
                pl.BlockSpec((heads, L, L), lambda b, c: (0, 0, 0)),
                pl.BlockSpec((heads, L, hd), lambda b, c: (0, 0, 0)),
                pl.BlockSpec((heads, L, hd), lambda b, c: (0, 0, 0)),
                pl.BlockSpec((heads, 1, hd), lambda b, c: (0, 0, 0))]
    args = [proj, proj, proj, proj, cos_rows, sin_rows, dec, ecum, wv, sdec]
    if has_state:
        in_specs.append(pl.BlockSpec((1, heads, hd, hd), lambda b, c: (b, 0, 0, 0)))
        args.append(ret_state)
    state_bytes = _nbytes((heads, hd, hd), F32)
    vmem = (2 * (4 * _nbytes((L, inner), F32) + _nbytes((L, inner), MXU_DTYPE) + _nbytes((heads, L, L), F32)
                 + 2 * _nbytes((heads, L, hd), F32) + 2 * state_bytes)
            + state_bytes + 16 * _nbytes((L, hd), F32) + (8 << 20))
    return _ScanPlan(
        body=functools.partial(_ret_kernel, L=L, heads=heads, hd=hd, has_state=has_state),
        args=args, in_specs=in_specs,
        out_shape=[jax.ShapeDtypeStruct((n, inner), MXU_DTYPE),
                   jax.ShapeDtypeStruct((batch, heads, hd, hd), F32)],
        out_specs=[pl.BlockSpec((L, inner), lambda b, c: (rowmap(b, c), 0)),
                   pl.BlockSpec((1, heads, hd, hd), lambda b, c: (b, 0, 0, 0))],
        scratch=[pltpu.VMEM((heads, hd, hd), F32)],
        vmem=vmem)


def _scan_pair_kernel(*refs, bodies, n_in, n_out, n_scr):
    ins, outs, scrs = refs[:sum(n_in)], refs[sum(n_in):sum(n_in) + sum(n_out)], refs[sum(n_in) + sum(n_out):]
    for k, body in enumerate(bodies):
        take = lambda seq, counts: seq[sum(counts[:k]):sum(counts[:k + 1])]
        body(*take(ins, n_in), *take(outs, n_out), *take(scrs, n_scr))


def run_scans(plans, *, batch, nc, name):
    outs = pl.pallas_call(
        functools.partial(_scan_pair_kernel, bodies=tuple(p.body for p in plans),
                          n_in=tuple(len(p.args) for p in plans),
                          n_out=tuple(len(p.out_shape) for p in plans),
                          n_scr=tuple(len(p.scratch) for p in plans)),
        out_shape=tuple(s for p in plans for s in p.out_shape),
        grid=(batch, nc),
        in_specs=[s for p in plans for s in p.in_specs],
        out_specs=tuple(s for p in plans for s in p.out_specs),
        scratch_shapes=[s for p in plans for s in p.scratch],
        compiler_params=_params(("parallel", "arbitrary"), sum(p.vmem for p in plans)),
        name=name,
    )(*[a for p in plans for a in p.args])
    split, k = [], 0
    for p in plans:
        split.append(outs[k:k + len(p.out_shape)])
        k += len(p.out_shape)
    return split


def _rope_tables(pos, half, reps, batch):
    inv = ROPE_BASE ** (-jnp.arange(half, dtype=F32) / half)
    ang = pos.astype(F32)[:, None] * inv[None, :]
    cos, sin = jnp.cos(ang), jnp.sin(ang)
    if reps == 0:
        return jnp.tile(cos, (batch, 1)), jnp.tile(sin, (batch, 1))
    c = jnp.tile(jnp.concatenate([cos, cos], axis=-1), (batch, reps))
    s = jnp.tile(jnp.concatenate([-sin, sin], axis=-1), (batch, reps))
    return c, s


def _mixer_even(x, batch, pos, norm_g, prm, conv_state, ssd_state, ret_state, dims, L_ssd, L_ret):
    d = x.shape[1]
    groups, hpg, hdim, nstate, rheads, rhd = dims
    proj, dt_raw = rms_matmul(x, norm_g, prm["w_in"], out_dtype=MXU_DTYPE, w_side=prm["w_dt"])
    col = prm["col"]
    ssd_plan = ssd_scan(proj, dt_raw, col, prm["conv_w"], prm["conv_b"], prm["dt_bias"],
                        prm["a_log"], prm["d_skip_e"], prm["ssd_norm"], conv_state, ssd_state,
                        batch=batch, L=L_ssd, groups=groups, hpg=hpg, hdim=hdim, nstate=nstate)
    cos_rows, sin_rows = _rope_tables(pos, rhd // 2, 0, batch)
    ret_plan = ret_scan(proj, col, cos_rows, sin_rows, ret_state, batch=batch, L=L_ret, heads=rheads, hd=rhd)
    t = x.shape[0] // batch
    (y, new_conv, new_ssd), = run_scans([ssd_plan], batch=batch, nc=t // L_ssd, name="ssd_scan")
    (o, new_ret), = run_scans([ret_plan], batch=batch, nc=t // L_ret, name="ret_scan")
    x = matmul_residual(x, [y, o], [prm["w_out_ssd"], prm["w_out_ret"]])
    return x, (new_conv, new_ssd, new_ret)


def _prep_even(w_in, conv_w, conv_b, dt_bias, a_log, d_skip, ssd_norm, w_out, dims):
    groups, hpg, hdim, nstate, rheads, rhd = dims
    d = w_in.shape[0]
    dx = groups * hpg * hdim
    dn = groups * nstate
    nh = groups * hpg
    ri = rheads * rhd
    o_z, o_xs, o_b, o_c, o_dt = 0, dx, 2 * dx, 2 * dx + dn, 2 * dx + 2 * dn
    o_q = o_dt + nh
    seg = lambda lo, w: w_in[:, lo:lo + w]
    dt_pad = jnp.zeros((d, V7X_LANES - nh), w_in.dtype)
    w_main = jnp.concatenate([seg(o_z, dx), seg(o_q, ri), seg(o_q + ri, ri), seg(o_q + 2 * ri, ri),
                              seg(o_q + 3 * ri, ri), seg(o_xs, dx), seg(o_b, dn), seg(o_c, dn)],
                             axis=1).astype(MXU_DTYPE)
    w_dt = jnp.concatenate([seg(o_dt, nh), dt_pad], axis=1).astype(MXU_DTYPE)
    assert dx == ri and dx % dn == 0 and dn % V7X_LANES == 0
    col = {"z": 0, "q": 1, "k": 2, "v": 3, "gate": 4, "xs": 5, "B": 6 * dx // dn, "C": 6 * dx // dn + 1}
    pad1 = lambda v: jnp.pad(v.astype(F32), (0, V7X_LANES - nh)).reshape(1, V7X_LANES)
    return {"w_in": w_main, "w_dt": w_dt, "col": col, "conv_w": conv_w, "conv_b": conv_b,
            "dt_bias": pad1(dt_bias), "a_log": pad1(a_log),
            "d_skip_e": jnp.repeat(d_skip, hdim).reshape(1, dx), "ssd_norm": ssd_norm,
            "w_out_ssd": w_out[:dx].astype(MXU_DTYPE), "w_out_ret": w_out[dx:].astype(MXU_DTYPE)}


def _prep_odd(w_in, q_norm, kv_norm, w_uq, w_uk, w_uv, w_out, rope_dim):
    d = w_in.shape[0]
    q_lora = q_norm.shape[0]
    kv_lora, heads, nope = w_uk.shape
    kp = w_in[:, q_lora + kv_lora:]
    w_in_new = jnp.concatenate([w_in[:, :q_lora + kv_lora], kp, kp], axis=1).astype(MXU_DTYPE)
    assert 2 * rope_dim == V7X_LANES
    wq = w_uq.reshape(q_lora, heads, nope + rope_dim)
    wq_nope = wq[:, :, :nope]
    wq_rope = jnp.pad(wq[:, :, nope:], ((0, 0), (0, 0), (0, V7X_LANES - rope_dim)))
    hw = nope + V7X_LANES
    return {"w_in": w_in_new, "q_norm": q_norm, "kv_norm": kv_norm,
            "w_uq_lat": jnp.concatenate([wq_nope.reshape(q_lora, heads * nope),
                                         wq_rope.reshape(q_lora, heads * V7X_LANES)], axis=1).astype(MXU_DTYPE),
            "slabs_lat": tuple(heads * nope + h * V7X_LANES for h in range(heads)),
            "w_uq_head": jnp.concatenate([wq_nope, wq_rope], axis=2).reshape(q_lora, heads * hw).astype(MXU_DTYPE),
            "slabs_head": tuple(h * hw + nope for h in range(heads)),
            "w_ukv": jnp.concatenate([w_uk.reshape(kv_lora, heads * nope),
                                      w_uv.reshape(kv_lora, -1)], axis=1).astype(MXU_DTYPE),
            "w_uk": jnp.transpose(w_uk, (1, 2, 0)).astype(MXU_DTYPE),
            "w_uv": jnp.transpose(w_uv, (1, 0, 2)).astype(MXU_DTYPE),
            "w_out": w_out.astype(MXU_DTYPE)}


def _mixer_odd(x, batch, pos, norm_g, prm, past, rope_dim, q_offset):
    n, d = x.shape
    t = n // batch
    q_lora = prm["q_norm"].shape[0]
    kv_lora = prm["kv_norm"].shape[0]
    heads, nope, lat = prm["w_uk"].shape
    proj = rms_matmul(x, norm_g, prm["w_in"])
    cos_rows, sin_rows = _rope_tables(pos, rope_dim // 2, V7X_LANES // rope_dim, batch)
    scale = (nope + rope_dim) ** -0.5
    dims = dict(q_lora=q_lora, kv_lora=kv_lora, rope_dim=rope_dim, heads=heads, nope=nope)
    if past is None:
        q, ckv, kpe, k, v = mla_prep(proj, prm["q_norm"], prm["kv_norm"], prm["w_uq_head"], cos_rows, sin_rows,
                                     prm["w_ukv"], rope_slabs=prm["slabs_head"], qscale=scale * math.log2(math.e),
                                     **dims)
        tq = tk = _pick(t, (512, 256, 128, 64))
        shp = lambda a: a.reshape(batch, t, a.shape[1])
        o = mha_attn(shp(q), shp(k), shp(v), heads=heads, tq=tq, tk=tk, q_offset=q_offset, kv_len=t)
    else:
        q, ckv, kpe, kcat = mla_prep(proj, prm["q_norm"], prm["kv_norm"], prm["w_uq_lat"], cos_rows, sin_rows,
                                     None, rope_slabs=prm["slabs_lat"], qscale=1.0, **dims)
        q = q.reshape(batch, t, q.shape[1])
        kcat = kcat.reshape(batch, t, kcat.shape[1])
        o = mla_decode(q, past[0], past[1], kcat, prm["w_uk"], prm["w_uv"], q_offset=q_offset, scale=scale)
    x = matmul_residual(x, [o.reshape(n, o.shape[2])], [prm["w_out"]])
    return x, (ckv.reshape(batch, t, kv_lora), kpe.reshape(batch, t, rope_dim))


def kernel(x_prompt, x_sample, mem_prompt, state_conv, state_ssd, state_ret, cache_ckv, cache_kpe,
           cache_mem_k, cache_mem_v, norms, ffn_w1, ffn_w2, mem_norm, w_mq, w_mkv, w_mo,
           ab_w_in, ab_conv_w, ab_conv_b, ab_dt_bias, ab_a_log, ab_d_skip, ab_ssd_norm, ab_w_out,
           c_w_in, c_q_norm, c_kv_norm, c_w_uq, c_w_uk, c_w_uv, c_w_out, final_norm):
    bp, tp, d = x_prompt.shape
    bs, ts, _ = x_sample.shape
    depth = norms.shape[0]
    assert depth >= 1
    past_len = cache_ckv.shape[2]
    mem_tokens = mem_prompt.shape[1]
    mem_heads, mem_hd = cache_mem_k.shape[3], cache_mem_k.shape[4]
    mem_inner = mem_heads * mem_hd
    ssd_heads, nstate, hdim = state_ssd.shape[2], state_ssd.shape[3], state_ssd.shape[4]
    cdim = state_conv.shape[3]
    groups = (cdim - ssd_heads * hdim) // (2 * nstate)
    hpg = ssd_heads // groups
    rheads, rhd = state_ret.shape[2], state_ret.shape[3]
    dims = (groups, hpg, hdim, nstate, rheads, rhd)
    rope_dim = cache_kpe.shape[3]

    pos_p = jnp.arange(tp)
    pos_s = past_len + jnp.arange(ts)
    xp = x_prompt.reshape(bp * tp, d)
    xs = x_sample.reshape(bs * ts, d)
    L_ssd_p, L_ret_p = _pick(tp, (128, 64)), _pick(tp, (256, 128, 64))
    L_s = _pick(ts, (128, 64))

    outs = {k: [] for k in ("conv_p", "ssd_p", "ret_p", "ckv_p", "kpe_p", "memk_p", "memv_p",
                            "conv_s", "ssd_s", "ret_s", "ckv_s", "kpe_s")}

    def to_group_layout(st):
        b = st.shape[0]
        return st.reshape(b, groups, hpg, nstate, hdim).transpose(0, 1, 3, 2, 4).reshape(b, groups, nstate, hpg * hdim)

    def from_group_layout(st):
        b = st.shape[0]
        return st.reshape(b, groups, nstate, hpg, hdim).transpose(0, 1, 3, 2, 4).reshape(b, ssd_heads, nstate, hdim)

    w1 = ffn_w1.astype(MXU_DTYPE)
    w2 = (0.5 * ffn_w2).astype(MXU_DTYPE)
    for i in range(depth):
        j = i // 2
        closing = final_norm if i == depth - 1 else None
        wq_m = w_mq[i].astype(MXU_DTYPE)
        wo_m = w_mo[i].astype(MXU_DTYPE)
        mkv = rms_matmul(mem_prompt.reshape(bp * mem_tokens, d), mem_norm[i], w_mkv[i].astype(MXU_DTYPE))
        mk_p = mkv[:, :mem_inner].reshape(bp, mem_tokens, mem_inner)
        mv_p = mkv[:, mem_inner:].reshape(bp, mem_tokens, mem_inner)
        outs["memk_p"].append(mk_p.reshape(bp, mem_tokens, mem_heads, mem_hd))
        outs["memv_p"].append(mv_p.reshape(bp, mem_tokens, mem_heads, mem_hd))

        xp = ffn(xp, norms[i, 0], w1, w2, i, 0)
        xs = ffn(xs, norms[i, 0], w1, w2, i, 0)
        if i % 2 == 0:
            prm = _prep_even(ab_w_in[j], ab_conv_w[j], ab_conv_b[j], ab_dt_bias[j], ab_a_log[j],
                             ab_d_skip[j], ab_ssd_norm[j], ab_w_out[j], dims)
            zero_conv = jnp.zeros((bp, CONV_PAD_ROWS, cdim), F32)
            xp, st_p = _mixer_even(xp, bp, pos_p, norms[i, 1], prm, zero_conv, None, None, dims, L_ssd_p, L_ret_p)
            conv_in = jnp.pad(state_conv[j], ((0, 0), (CONV_PAD_ROWS - (CONV_K - 1), 0), (0, 0)))
            xs, st_s = _mixer_even(xs, bs, pos_s, norms[i, 1], prm, conv_in, to_group_layout(state_ssd[j]),
                                   state_ret[j], dims, L_s, L_s)
            for tag, st in (("p", st_p), ("s", st_s)):
                outs["conv_" + tag].append(st[0][:, CONV_PAD_ROWS - (CONV_K - 1):, :])
                outs["ssd_" + tag].append(from_group_layout(st[1]))
                outs["ret_" + tag].append(st[2])
        else:
            prm = _prep_odd(c_w_in[j], c_q_norm[j], c_kv_norm[j], c_w_uq[j], c_w_uk[j], c_w_uv[j], c_w_out[j],
                            rope_dim)
            xp, st_p = _mixer_odd(xp, bp, pos_p, norms[i, 1], prm, None, rope_dim, 0)
            xs, st_s = _mixer_odd(xs, bs, pos_s, norms[i, 1], prm, (cache_ckv[j], cache_kpe[j]), rope_dim, past_len)
            for tag, st in (("p", st_p), ("s", st_s)):
                outs["ckv_" + tag].append(st[0])
                outs["kpe_" + tag].append(st[1])
        xp = mem_attn(xp, norms[i, 2], wq_m, mk_p, mv_p, wo_m, batch=bp, heads=mem_heads)
        xs = mem_attn(xs, norms[i, 2], wq_m, cache_mem_k, cache_mem_v, wo_m, batch=bs, heads=mem_heads, layer=i)
        xp = ffn(xp, norms[i, 3], w1, w2, i, 1, final_g=closing)
        xs = ffn(xs, norms[i, 3], w1, w2, i, 1, final_g=closing)

    y_prompt = xp.reshape(bp, tp, d)
    y_sample = xs.reshape(bs, ts, d)
    st = lambda k: jnp.stack(outs[k])
    return (y_prompt, y_sample, st("conv_p"), st("ssd_p"), st("ret_p"), st("ckv_p"), st("kpe_p"),
            st("memk_p"), st("memv_p"), st("conv_s"), st("ssd_s"), st("ret_s"), st("ckv_s"), st("kpe_s"))
```

```python
import functools
import math
from typing import Callable, NamedTuple

import jax
import jax.numpy as jnp
from jax import lax
from jax.experimental import pallas as pl
from jax.experimental.pallas import tpu as pltpu

F32 = jnp.float32
MXU_DTYPE = jnp.bfloat16

EPS = 1e-6
CHUNK = 64
ROPE_BASE = 10000.0
CONV_K = 4
NEG = -1e30
LOG2E = math.log2(math.e)

V7X_VMEM_BYTES = 64 * 1024 * 1024
TILE_VMEM_BUDGET = 48 * 1024 * 1024
V7X_LANES = 128
V7X_SUBLANES = 8
CONV_PAD_ROWS = 8


def _params(semantics, vmem_bytes):
    limit = min(int(vmem_bytes), V7X_VMEM_BYTES - (4 << 20))
    return pltpu.CompilerParams(dimension_semantics=semantics, vmem_limit_bytes=limit)


def _nbytes(shape, dtype):
    return math.prod(shape) * jnp.dtype(dtype).itemsize


def _pick(n, prefs):
    for p in prefs:
        if n % p == 0:
            return p
    return n


def _dot(a, b):
    return jnp.dot(a, b, preferred_element_type=F32)


def _dot_nt(a, b):
    return lax.dot_general(a, b, (((1,), (1,)), ((), ())), preferred_element_type=F32)


def _dot_tn(a, b):
    return lax.dot_general(a, b, (((0,), (0,)), ((), ())), preferred_element_type=F32)


def _split3(x):
    hi = x.astype(MXU_DTYPE)
    r = x - hi.astype(F32)
    mid = r.astype(MXU_DTYPE)
    lo = (r - mid.astype(F32)).astype(MXU_DTYPE)
    return hi, mid, lo


def _rms(xf, g=None):
    y = xf * lax.rsqrt(jnp.mean(xf * xf, axis=-1, keepdims=True) + EPS)
    return y if g is None else y * g


def _silu(a):
    return a * (1.0 / (1.0 + jnp.exp(-a)))


def _swap32(x):
    w = x.shape[-1]
    lane = lax.broadcasted_iota(jnp.int32, x.shape, x.ndim - 1)
    fwd = pltpu.roll(x, w - 32, x.ndim - 1)
    bwd = pltpu.roll(x, 32, x.ndim - 1)
    return jnp.where((lane & 63) < 32, fwd, bwd)


def _rms_matmul_kernel(*refs, side):
    if side:
        x_ref, g_ref, w_ref, ws_ref, o_ref, os_ref, xn_ref = refs
    else:
        x_ref, g_ref, w_ref, o_ref, xn_ref = refs

    @pl.when(pl.program_id(1) == 0)
    def _():
        xn_ref[...] = _rms(x_ref[...], g_ref[...]).astype(xn_ref.dtype)
        if side:
            os_ref[...] = _dot(xn_ref[...], ws_ref[...])

    o_ref[...] = _dot(xn_ref[...], w_ref[...]).astype(o_ref.dtype)


def rms_matmul(x, g, w, *, out_dtype=F32, w_side=None):
    n, d = x.shape
    nout = w.shape[1]
    side = w_side is not None
    tn = _pick(nout, (1920, 1152, 1024, 512, 256, 128))

    def vmem_for(tm):
        return (2 * (_nbytes((tm, d), F32) + _nbytes((d, tn), w.dtype) + _nbytes((tm, tn), out_dtype))
                + _nbytes((tm, d), MXU_DTYPE) + _nbytes((tm, tn), F32) + (8 << 20))

    tm = next(t for t in (1024, 512, 256, 128, n) if n % t == 0 and (vmem_for(t) <= TILE_VMEM_BUDGET or t <= 128))
    vmem = vmem_for(tm)
    in_specs = [pl.BlockSpec((tm, d), lambda i, j: (i, 0)),
                pl.BlockSpec((1, d), lambda i, j: (0, 0)),
                pl.BlockSpec((d, tn), lambda i, j: (0, j))]
    args = [x, g.reshape(1, d), w]
    out_shape = jax.ShapeDtypeStruct((n, nout), out_dtype)
    out_specs = pl.BlockSpec((tm, tn), lambda i, j: (i, j))
    if side:
        ns = w_side.shape[1]
        in_specs.append(pl.BlockSpec((d, ns), lambda i, j: (0, 0)))
        args.append(w_side)
        out_shape = (out_shape, jax.ShapeDtypeStruct((n, ns), F32))
        out_specs = (out_specs, pl.BlockSpec((tm, ns), lambda i, j: (i, 0)))
        vmem += 2 * (_nbytes((d, ns), w_side.dtype) + _nbytes((tm, ns), F32))
    return pl.pallas_call(
        functools.partial(_rms_matmul_kernel, side=side),
        out_shape=out_shape,
        grid=(n // tm, nout // tn),
        in_specs=in_specs,
        out_specs=out_specs,
        scratch_shapes=[pltpu.VMEM((tm, d), MXU_DTYPE)],
        compiler_params=_params(("parallel", "arbitrary"), vmem),
        name="rms_matmul",
    )(*args)


def _ffn_kernel(*refs, final_norm, nf):
    x_ref, g_ref = refs[0:2]
    blocks = (refs[2:5], refs[5:8])
    gf_ref = refs[8] if final_norm else None
    o_ref, xn_ref = refs[-2:]
    s = pl.program_id(1)

    @pl.when(s == 0)
    def _():
        xf = x_ref[...]
        xn_ref[...] = _rms(xf, g_ref[...]).astype(xn_ref.dtype)
        o_ref[...] = xf

    def down(w1a_ref, w1b_ref, w2_ref):
        xn = xn_ref[...]
        a = _dot(xn, w1a_ref[...])
        b = _dot(xn, w1b_ref[...])
        return _dot((_silu(a) * b).astype(MXU_DTYPE), w2_ref[...])

    if nf % 2 == 0:
        o_ref[...] += down(*blocks[0]) + down(*blocks[1])
    else:
        @pl.when(s == 0)
        def _():
            o_ref[...] += down(*blocks[0])

        @pl.when(s > 0)
        def _():
            o_ref[...] += down(*blocks[0]) + down(*blocks[1])

    if final_norm:
        @pl.when(s == pl.num_programs(1) - 1)
        def _():
            o_ref[...] = _rms(o_ref[...], gf_ref[...])


def ffn(x, g, w1, w2, layer, which, final_g=None):
    n, d = x.shape
    dff = w2.shape[2]
    tm = _pick(n, (512, 256, 128))
    tf = _pick(dff, (512, 256, 128))
    nf = dff // tf
    vmem = (2 * (2 * _nbytes((tm, d), F32) + 6 * _nbytes((d, tf), w1.dtype))
            + _nbytes((tm, d), MXU_DTYPE) + _nbytes((tm, d), F32) + 6 * _nbytes((tm, tf), F32) + (6 << 20))
    in_specs = [pl.BlockSpec((tm, d), lambda i, s: (i, 0)),
                pl.BlockSpec((1, d), lambda i, s: (0, 0))]
    args = [x, g.reshape(1, d)]
    lead = nf % 2
    for half in range(2):
        blk = lambda s, half=half: jnp.maximum(2 * s - lead + half, half * (lead + 1))
        in_specs += [pl.BlockSpec((None, None, d, tf), lambda i, s, blk=blk: (layer, which, 0, blk(s))),
                     pl.BlockSpec((None, None, d, tf), lambda i, s, blk=blk: (layer, which, 0, blk(s) + nf)),
                     pl.BlockSpec((None, None, tf, d), lambda i, s, blk=blk: (layer, which, blk(s), 0))]
        args += [w1, w1, w2]
    if final_g is not None:
        in_specs.append(pl.BlockSpec((1, d), lambda i, s: (0, 0)))
        args.append(final_g.reshape(1, d))
    return pl.pallas_call(
        functools.partial(_ffn_kernel, final_norm=final_g is not None, nf=nf),
        out_shape=jax.ShapeDtypeStruct((n, d), F32),
        grid=(n // tm, -(-nf // 2)),
        in_specs=in_specs,
        out_specs=pl.BlockSpec((tm, d), lambda i, f: (i, 0)),
        scratch_shapes=[pltpu.VMEM((tm, d), MXU_DTYPE)],
        compiler_params=_params(("parallel", "arbitrary"), vmem),
        name="ffn",
    )(*args)


def _matmul_residual_kernel(*refs, n_in):
    x_ref = refs[0]
    o_ref = refs[1 + 2 * n_in]
    acc = x_ref[...]
    for h_ref, w_ref in zip(refs[1:1 + n_in], refs[1 + n_in:1 + 2 * n_in]):
        acc = acc + _dot(h_ref[...], w_ref[...])
    o_ref[...] = acc


def matmul_residual(x, hs, ws):
    n, d = x.shape

    def vmem_for(tm, tn):
        return (2 * (2 * _nbytes((tm, tn), F32) + sum(_nbytes((tm, h.shape[1]), h.dtype) for h in hs)
                     + sum(_nbytes((w.shape[0], tn), w.dtype) for w in ws))
                + _nbytes((tm, tn), F32) + (8 << 20))

    tm, tn = next((a, b) for a, b in ((512, d), (1024, 1024), (512, 1024), (256, 512), (128, 128), (n, d))
                  if n % a == 0 and d % b == 0 and (vmem_for(a, b) <= TILE_VMEM_BUDGET or a <= 128))
    vmem = vmem_for(tm, tn)
    in_specs = [pl.BlockSpec((tm, tn), lambda i, j: (i, j))]
    for h in hs:
        in_specs.append(pl.BlockSpec((tm, h.shape[1]), lambda i, j: (i, 0)))
    for w in ws:
        in_specs.append(pl.BlockSpec((w.shape[0], tn), lambda i, j: (0, j)))
    return pl.pallas_call(
        functools.partial(_matmul_residual_kernel, n_in=len(hs)),
        out_shape=jax.ShapeDtypeStruct((n, d), F32),
        grid=(n // tm, d // tn),
        in_specs=in_specs,
        out_specs=pl.BlockSpec((tm, tn), lambda i, j: (i, j)),
        compiler_params=_params(("parallel", "arbitrary"), vmem),
        name="matmul_residual",
    )(x, *hs, *ws)


def _mem_attn_kernel(x_ref, g_ref, wq_ref, k_ref, v_ref, wo_ref, o_ref, att_ref, *, bt, tq, heads, hd):
    head_axis = len(k_ref.shape) == 4
    xf = x_ref[...]
    xn = _rms(xf, g_ref[...]).astype(MXU_DTYPE)
    q = _dot(xn, wq_ref[...]).astype(MXU_DTYPE)
    scale = hd ** -0.5
    for b in range(bt):
        for h in range(heads):
            qh = q[b * tq:(b + 1) * tq, h * hd:(h + 1) * hd]
            if head_axis:
                kh = k_ref[b, :, h, :].astype(MXU_DTYPE)
                vh = v_ref[b, :, h, :].astype(MXU_DTYPE)
            else:
                kh = k_ref[b, :, h * hd:(h + 1) * hd].astype(MXU_DTYPE)
                vh = v_ref[b, :, h * hd:(h + 1) * hd].astype(MXU_DTYPE)
            s = _dot_nt(qh, kh) * scale
            m = jnp.max(s, axis=-1, keepdims=True)
            p = jnp.exp(s - m)
            l = jnp.sum(p, axis=-1, keepdims=True)
            oh = _dot(p.astype(MXU_DTYPE), vh) / l
            att_ref[b * tq:(b + 1) * tq, h * hd:(h + 1) * hd] = oh.astype(att_ref.dtype)
    o_ref[...] = xf + _dot(att_ref[...], wo_ref[...])


def mem_attn(x, g, wq, mem_k, mem_v, wo, *, batch, heads, layer=None):
    n, d = x.shape
    t = n // batch
    cached = layer is not None
    m = mem_k.shape[2] if cached else mem_k.shape[1]
    inner = wq.shape[1]
    hd = inner // heads
    if t >= 128:
        bt, tq = 1, _pick(t, (512, 256, 128))
    else:
        bt, tq = _pick(batch, (4, 2, 1)), t
    nt = t // tq
    rows = bt * tq
    if cached:
        kv_spec = pl.BlockSpec((None, bt, m, heads, hd), lambda b, i: (layer, b, 0, 0, 0))
        kv_bytes = _nbytes((bt, m, max(heads, V7X_SUBLANES), hd), mem_k.dtype)
    else:
        kv_spec = pl.BlockSpec((bt, m, inner), lambda b, i: (b, 0, 0))
        kv_bytes = _nbytes((bt, m, inner), mem_k.dtype)
    vmem = (2 * (2 * _nbytes((rows, d), F32) + 2 * kv_bytes + 2 * _nbytes((d, inner), wq.dtype))
            + 2 * _nbytes((rows, d), F32) + (8 << 20))
    return pl.pallas_call(
        functools.partial(_mem_attn_kernel, bt=bt, tq=tq, heads=heads, hd=hd),
        out_shape=jax.ShapeDtypeStruct((n, d), F32),
        grid=(batch // bt, nt),
        in_specs=[pl.BlockSpec((rows, d), lambda b, i: (b * nt + i, 0)),
                  pl.BlockSpec((1, d), lambda b, i: (0, 0)),
                  pl.BlockSpec((d, inner), lambda b, i: (0, 0)),
                  kv_spec, kv_spec,
                  pl.BlockSpec((inner, d), lambda b, i: (0, 0))],
        out_specs=pl.BlockSpec((rows, d), lambda b, i: (b * nt + i, 0)),
        scratch_shapes=[pltpu.VMEM((rows, inner), MXU_DTYPE)],
        compiler_params=_params(("parallel", "arbitrary"), vmem),
        name="mem_attn",
    )(x, g.reshape(1, d), wq, mem_k, mem_v, wo)


def _mla_prep_kernel(*refs, q_lora, kv_lora, rope_dim, rope_slabs, qscale, heads, nope):
    expand = len(refs) == 12
    if expand:
        (p_ref, qn_ref, kvn_ref, wuq_ref, cos_ref, sin_ref, wukv_ref,
         q_ref, ckv_ref, kpe_ref, k_ref, v_ref) = refs
    else:
        p_ref, qn_ref, kvn_ref, wuq_ref, cos_ref, sin_ref, q_ref, ckv_ref, kpe_ref, k_ref = refs
    cos = cos_ref[...]
    sin = sin_ref[...]
    cqn = _rms(p_ref[:, 0:q_lora], qn_ref[...]).astype(MXU_DTYPE)
    q = _dot(cqn, wuq_ref[...])
    if qscale != 1.0:
        q = q * qscale
    q_ref[...] = q.astype(q_ref.dtype)
    for lo in rope_slabs:
        xs = q[:, lo:lo + V7X_LANES]
        q_ref[:, lo:lo + V7X_LANES] = (xs * cos + _swap32(xs) * sin).astype(q_ref.dtype)
    ckv = _rms(p_ref[:, q_lora:q_lora + kv_lora], kvn_ref[...])
    ckv_ref[...] = ckv
    kp = p_ref[:, q_lora + kv_lora:q_lora + kv_lora + V7X_LANES]
    kpr = kp * cos + _swap32(kp) * sin
    kpe_ref[...] = kpr[:, 0:rope_dim]
    kpr_m = kpr.astype(k_ref.dtype)
    if expand:
        kv = _dot(ckv.astype(MXU_DTYPE), wukv_ref[...])
        hw = nope + V7X_LANES
        for h in range(heads):
            k_ref[:, h * hw:h * hw + nope] = kv[:, h * nope:(h + 1) * nope].astype(k_ref.dtype)
            k_ref[:, h * hw + nope:(h + 1) * hw] = kpr_m
        v_ref[...] = kv[:, heads * nope:].astype(v_ref.dtype)
    else:
        k_ref[:, 0:kv_lora] = ckv.astype(k_ref.dtype)
        k_ref[:, kv_lora:kv_lora + V7X_LANES] = kpr_m


def mla_prep(proj, q_norm, kv_norm, w_uq, cos_rows, sin_rows, w_ukv, *, q_lora, kv_lora, rope_dim,
             rope_slabs, qscale, heads, nope):
    n = proj.shape[0]
    qcols = w_uq.shape[1]
    tm = _pick(n, (512, 256, 128))
    expand = w_ukv is not None
    row = lambda w: pl.BlockSpec((tm, w), lambda i: (i, 0))
    full = lambda a: pl.BlockSpec(a.shape, lambda i: (0, 0))
    in_specs = [row(proj.shape[1]), pl.BlockSpec((1, q_lora), lambda i: (0, 0)),
                pl.BlockSpec((1, kv_lora), lambda i: (0, 0)), full(w_uq), row(V7X_LANES), row(V7X_LANES)]
    args = [proj, q_norm.reshape(1, -1), kv_norm.reshape(1, -1), w_uq, cos_rows, sin_rows]
    out_shape = [jax.ShapeDtypeStruct((n, qcols), MXU_DTYPE), jax.ShapeDtypeStruct((n, kv_lora), F32),
                 jax.ShapeDtypeStruct((n, rope_dim), F32)]
    out_specs = [row(qcols), row(kv_lora), row(rope_dim)]
    vmem = (2 * (_nbytes((tm, proj.shape[1]), F32) + _nbytes(w_uq.shape, w_uq.dtype)
                 + _nbytes((tm, qcols), MXU_DTYPE) + 3 * _nbytes((tm, kv_lora + V7X_LANES), F32))
            + 3 * _nbytes((tm, qcols), F32) + (8 << 20))
    if expand:
        kcols = heads * (nope + V7X_LANES)
        vcols = w_ukv.shape[1] - heads * nope
        in_specs.append(full(w_ukv))
        args.append(w_ukv)
        out_shape += [jax.ShapeDtypeStruct((n, kcols), MXU_DTYPE), jax.ShapeDtypeStruct((n, vcols), MXU_DTYPE)]
        out_specs += [row(kcols), row(vcols)]
        vmem += (2 * (_nbytes(w_ukv.shape, w_ukv.dtype) + _nbytes((tm, kcols + vcols), MXU_DTYPE))
                 + 2 * _nbytes((tm, w_ukv.shape[1]), F32))
    else:
        out_shape.append(jax.ShapeDtypeStruct((n, kv_lora + V7X_LANES), MXU_DTYPE))
        out_specs.append(row(kv_lora + V7X_LANES))
    return pl.pallas_call(
        functools.partial(_mla_prep_kernel, q_lora=q_lora, kv_lora=kv_lora, rope_dim=rope_dim,
                          rope_slabs=rope_slabs, qscale=qscale, heads=heads, nope=nope),
        out_shape=tuple(out_shape),
        grid=(n // tm,),
        in_specs=in_specs,
        out_specs=tuple(out_specs),
        compiler_params=_params(("parallel",), vmem),
        name="mla_prep",
    )(*args)


def _lanes(x, width):
    reps = width // V7X_LANES
    return x if reps == 1 else jnp.concatenate([x] * reps, axis=-1)


def _mha_attn_kernel(qi_ref, kj_ref, flag_ref, q_ref, k_ref, v_ref, o_ref, acc_ref, m_ref,
                     *, heads, hw, vd, tq, tk, q_offset, kv_len):
    step = pl.program_id(1)
    i = qi_ref[step]
    j = kj_ref[step]
    flags = flag_ref[step]
    aw = vd + V7X_LANES

    @pl.when(j == 0)
    def _():
        m_ref[...] = jnp.full_like(m_ref, NEG)
        acc_ref[...] = jnp.zeros_like(acc_ref)

    ones = jnp.ones((tk, V7X_LANES), MXU_DTYPE)

    def all_heads(bias):
        for h in range(heads):
            s = _dot_nt(q_ref[0, :, h * hw:(h + 1) * hw], k_ref[0, :, h * hw:(h + 1) * hw])
            if bias is not None:
                s = s + bias
            m_old = m_ref[h]
            m_new = jnp.maximum(m_old, jnp.max(s, axis=-1, keepdims=True))
            alpha = jnp.exp2(m_old - m_new)
            p = jnp.exp2(s - _lanes(m_new, tk)).astype(MXU_DTYPE)
            m_ref[h] = m_new
            v_ext = jnp.concatenate([v_ref[0, :, h * vd:(h + 1) * vd], ones], axis=-1)
            acs = slice(h * aw, (h + 1) * aw)
            acc_ref[:, acs] = _lanes(alpha, aw) * acc_ref[:, acs] + _dot(p, v_ext)

    @pl.when((flags & 2) == 0)
    def _():
        all_heads(None)

    @pl.when((flags & 2) != 0)
    def _():
        qpos = q_offset + i * tq + lax.broadcasted_iota(jnp.int32, (tq, tk), 0)
        kpos = j * tk + lax.broadcasted_iota(jnp.int32, (tq, tk), 1)
        visible = ((kpos // CHUNK) <= (qpos // CHUNK)) & (kpos < kv_len)
        all_heads(jnp.where(visible, 0.0, NEG))

    @pl.when((flags & 1) != 0)
    def _():
        for h in range(heads):
            num = acc_ref[:, h * aw:h * aw + vd]
            den = _lanes(acc_ref[:, h * aw + vd:(h + 1) * aw], vd)
            o_ref[0, :, h * vd:(h + 1) * vd] = (num / den).astype(o_ref.dtype)


def mha_attn(q, k, v, *, heads, tq, tk, q_offset, kv_len):
    b, t, _ = q.shape
    s = k.shape[1]
    hw = q.shape[2] // heads
    vd = v.shape[2] // heads
    nq, nk = t // tq, s // tk
    qi, kj, flags = [], [], []
    for i in range(nq):
        first_q = q_offset + i * tq
        last_key = ((first_q + tq - 1) // CHUNK + 1) * CHUNK - 1
        jl = min(last_key // tk, nk - 1)
        for j in range(jl + 1):
            fully_visible = ((j + 1) * tk - 1) // CHUNK <= first_q // CHUNK and (j + 1) * tk <= kv_len
            qi.append(i)
            kj.append(j)
            flags.append(int(j == jl) + 2 * int(not fully_visible))
    sched = [jnp.asarray(a, jnp.int32) for a in (qi, kj, flags)]
    aw = vd + V7X_LANES
    vmem = (2 * (2 * _nbytes((tq, heads * hw), q.dtype) + 2 * _nbytes((tk, heads * vd), v.dtype)
                 + _nbytes((tq, heads * vd), MXU_DTYPE))
            + _nbytes((tq, heads * aw), F32) + _nbytes((heads, tq, V7X_LANES), F32)
            + 6 * _nbytes((tq, tk), F32) + (8 << 20))
    grid_spec = pltpu.PrefetchScalarGridSpec(
        num_scalar_prefetch=3,
        grid=(b, len(qi)),
        in_specs=[pl.BlockSpec((1, tq, heads * hw), lambda bb, p, qi_r, kj_r, l_r: (bb, qi_r[p], 0)),
                  pl.BlockSpec((1, tk, heads * hw), lambda bb, p, qi_r, kj_r, l_r: (bb, kj_r[p], 0)),
                  pl.BlockSpec((1, tk, heads * vd), lambda bb, p, qi_r, kj_r, l_r: (bb, kj_r[p], 0))],
        out_specs=pl.BlockSpec((1, tq, heads * vd), lambda bb, p, qi_r, kj_r, l_r: (bb, qi_r[p], 0)),
        scratch_shapes=[pltpu.VMEM((tq, heads * aw), F32),
                        pltpu.VMEM((heads, tq, V7X_LANES), F32)])
    return pl.pallas_call(
        functools.partial(_mha_attn_kernel, heads=heads, hw=hw, vd=vd, tq=tq, tk=tk,
                          q_offset=q_offset, kv_len=kv_len),
        out_shape=jax.ShapeDtypeStruct((b, t, heads * vd), MXU_DTYPE),
        grid_spec=grid_spec,
        compiler_params=_params(("parallel", "arbitrary"), vmem),
        name="mha_attn",
    )(*sched, q, k, v)


def _mla_decode_kernel(q_ref, pc_ref, pr_ref, kn_ref, wuk_ref, wuv_ref, o_ref, qs_ref,
                       *, heads, t, nope, lat, rope_dim, q_offset, scale):
    past = pc_ref.shape[1]
    rows = heads * t
    rope_lo = heads * nope
    for h in range(heads):
        qn = q_ref[0, :, h * nope:(h + 1) * nope]
        qs_ref[h * t:(h + 1) * t, 0:lat] = (_dot(qn, wuk_ref[h]) * scale).astype(qs_ref.dtype)
        qr = q_ref[0, :, rope_lo + h * V7X_LANES:rope_lo + (h + 1) * V7X_LANES]
        qs_ref[h * t:(h + 1) * t, lat:lat + V7X_LANES] = (qr.astype(F32) * scale).astype(qs_ref.dtype)

    def bias(k0, n):
        qpos = q_offset + lax.broadcasted_iota(jnp.int32, (t, n), 0)
        kpos = k0 + lax.broadcasted_iota(jnp.int32, (t, n), 1)
        return jnp.where((kpos // CHUNK) <= (qpos // CHUNK), 0.0, NEG)

    def masked(s, b):
        return (s.reshape(heads, t, s.shape[1]) + b[None]).reshape(rows, s.shape[1])

    kc = pc_ref[0].astype(MXU_DTYPE)
    kr = pr_ref[0].astype(MXU_DTYPE)
    kn = kn_ref[0]
    s_past = masked(_dot_nt(qs_ref[:, 0:lat], kc) + _dot_nt(qs_ref[:, lat:lat + rope_dim], kr), bias(0, past))
    s_new = masked(_dot_nt(qs_ref[...], kn), bias(q_offset, t))
    m = jnp.maximum(jnp.max(s_past, axis=-1, keepdims=True), jnp.max(s_new, axis=-1, keepdims=True))
    p_past = jnp.exp(s_past - m)
    p_new = jnp.exp(s_new - m)
    l = jnp.sum(p_past, axis=-1, keepdims=True) + jnp.sum(p_new, axis=-1, keepdims=True)
    ol = (_dot(p_past.astype(MXU_DTYPE), kc) + _dot(p_new.astype(MXU_DTYPE), kn[:, 0:lat])) / l
    vd = wuv_ref.shape[2]
    for h in range(heads):
        o_ref[0, :, h * vd:(h + 1) * vd] = (
            _dot(ol[h * t:(h + 1) * t, :].astype(MXU_DTYPE), wuv_ref[h]).astype(o_ref.dtype))


def mla_decode(q, past_ckv, past_kpe, kcat, w_uk, w_uv, *, q_offset, scale):
    b, t, _ = q.shape
    past, rope_dim = past_kpe.shape[1], past_kpe.shape[2]
    heads, nope, lat = w_uk.shape
    vd = w_uv.shape[2]
    rows = heads * t
    vmem = (2 * (_nbytes((t, q.shape[2]), q.dtype) + _nbytes((past, lat + V7X_LANES), F32)
                 + 2 * _nbytes(w_uk.shape, w_uk.dtype) + _nbytes((t, heads * vd), MXU_DTYPE))
            + _nbytes((rows, lat + V7X_LANES), MXU_DTYPE) + _nbytes((past, lat + V7X_LANES), MXU_DTYPE)
            + 4 * _nbytes((rows, past), F32) + (8 << 20))
    assert vmem <= V7X_VMEM_BYTES, "cached rows must fit one VMEM block"
    return pl.pallas_call(
        functools.partial(_mla_decode_kernel, heads=heads, t=t, nope=nope, lat=lat, rope_dim=rope_dim,
                          q_offset=q_offset, scale=scale),
        out_shape=jax.ShapeDtypeStruct((b, t, heads * vd), MXU_DTYPE),
        grid=(b,),
        in_specs=[pl.BlockSpec((1, t, q.shape[2]), lambda bb: (bb, 0, 0)),
                  pl.BlockSpec((1, past, lat), lambda bb: (bb, 0, 0)),
                  pl.BlockSpec((1, past, rope_dim), lambda bb: (bb, 0, 0)),
                  pl.BlockSpec((1, t, kcat.shape[2]), lambda bb: (bb, 0, 0)),
                  pl.BlockSpec(w_uk.shape, lambda bb: (0, 0, 0)),
                  pl.BlockSpec(w_uv.shape, lambda bb: (0, 0, 0))],
        out_specs=pl.BlockSpec((1, t, heads * vd), lambda bb: (bb, 0, 0)),
        scratch_shapes=[pltpu.VMEM((rows, lat + V7X_LANES), MXU_DTYPE)],
        compiler_params=_params(("parallel",), vmem),
        name="mla_decode",
    )(q, past_ckv, past_kpe, kcat, w_uk, w_uv)


class _ScanPlan(NamedTuple):
    body: Callable
    args: list
    in_specs: list
    out_shape: list
    out_specs: list
    scratch: list
    vmem: int


def _ssd_kernel(*refs, L, groups, hpg, hdim, nstate, has_state):
    if has_state:
        (z_ref, xs_ref, b_ref, c_ref, dt_ref, cw_ref, cb_ref, dtb_ref, alog_ref, dsk_ref, nrm_ref,
         cst_ref, hst_ref, y_ref, ncv_ref, nst_ref, xbuf, h_scr, cum_scr, xdt_scr, yin_scr) = refs
    else:
        (z_ref, xs_ref, b_ref, c_ref, dt_ref, cw_ref, cb_ref, dtb_ref, alog_ref, dsk_ref, nrm_ref,
         cst_ref, y_ref, ncv_ref, nst_ref, xbuf, h_scr, cum_scr, xdt_scr, yin_scr) = refs
        hst_ref = None
    c = pl.program_id(1)
    nc = pl.num_programs(1)
    dx = groups * hpg * hdim
    dn = groups * nstate
    gw = hpg * hdim
    pad = CONV_PAD_ROWS

    @pl.when(c == 0)
    def _():
        xbuf[0:pad, :] = cst_ref[0]
        xbuf[pad:2 * pad, :] = jnp.zeros((pad, xbuf.shape[1]), F32)
        if has_state:
            h_scr[...] = hst_ref[0]
        else:
            h_scr[...] = jnp.zeros_like(h_scr)

    row = lax.broadcasted_iota(jnp.int32, (L, L), 0)
    col = lax.broadcasted_iota(jnp.int32, (L, L), 1)
    causal = row >= col

    xin = jnp.concatenate([xs_ref[...], b_ref[...], c_ref[...]], axis=1)
    xin_f = xin.astype(F32)
    acc = xin_f * cw_ref[CONV_K - 1:CONV_K, :] + cb_ref[...]
    for jj in range(CONV_K - 1):
        shift = jnp.where(row - col == CONV_K - 1 - jj, 1.0, 0.0).astype(MXU_DTYPE)
        acc = acc + _dot(shift, xin) * cw_ref[jj:jj + 1, :]
    corr = xbuf[pad - CONV_K + 1:2 * pad - CONV_K + 1, :] * cw_ref[0:1, :]
    for jj in range(1, CONV_K - 1):
        lo = pad - CONV_K + 1 + jj
        corr = corr + xbuf[lo:lo + pad, :] * cw_ref[jj:jj + 1, :]
    acc = jnp.concatenate([acc[0:pad] + corr, acc[pad:]], axis=0)
    xc = _silu(acc)
    tail = xin_f[L - pad:L, :]
    xbuf[0:pad, :] = tail

    @pl.when(c == nc - 1)
    def _():
        ncv_ref[0] = tail

    dtr = dt_ref[...] + dtb_ref[...]
    dt = jnp.maximum(dtr, 0.0) + jnp.log1p(jnp.exp(-jnp.abs(dtr)))
    la = dt * (-jnp.exp(alog_ref[...]) * LOG2E)
    tri = jnp.where(causal, 1.0, 0.0).astype(MXU_DTYPE)
    cum = sum(_dot(tri, piece) for piece in _split3(la))
    nh = groups * hpg
    eye = jnp.where(lax.broadcasted_iota(jnp.int32, (V7X_LANES, V7X_LANES), 0)
                    == lax.broadcasted_iota(jnp.int32, (V7X_LANES, V7X_LANES), 1), 1.0, 0.0).astype(MXU_DTYPE)
    cum_t = sum(_dot_nt(eye, piece) for piece in _split3(cum))

    half = lax.broadcasted_iota(jnp.int32, (L, V7X_LANES), 1) < hdim
    per_vreg = V7X_LANES // hdim
    for g in range(groups):
        bg = xc[:, dx + g * nstate:dx + (g + 1) * nstate]
        cg = xc[:, dx + dn + g * nstate:dx + dn + (g + 1) * nstate].astype(MXU_DTYPE)
        qk = _dot_nt(cg, bg.astype(MXU_DTYPE))
        for sl in range(gw // V7X_LANES):
            lane0 = g * gw + sl * V7X_LANES
            h0 = lane0 // hdim
            cb = [jnp.broadcast_to(cum[:, h0 + u:h0 + u + 1], (L, V7X_LANES)) for u in range(per_vreg)]
            db = [jnp.broadcast_to(dt[:, h0 + u:h0 + u + 1], (L, V7X_LANES)) for u in range(per_vreg)]
            cum_e = jnp.where(half, cb[0], cb[1])
            dt_e = jnp.where(half, db[0], db[1])
            xdt = xc[:, lane0:lane0 + V7X_LANES] * dt_e
            xdt_m = xdt.astype(MXU_DTYPE)
            ys = []
            for u in range(per_vreg):
                seg = cb[u][:, 0:L] - cum_t[h0 + u:h0 + u + 1, :]
                decay = jnp.exp2(jnp.where(causal, seg, NEG))
                ys.append(_dot((qk * decay).astype(MXU_DTYPE), xdt_m))
            cum_scr[:, lane0:lane0 + V7X_LANES] = cum_e
            xdt_scr[:, lane0:lane0 + V7X_LANES] = xdt
            yin_scr[:, lane0:lane0 + V7X_LANES] = jnp.where(half, ys[0], ys[1])

    for g in range(groups):
        gs = slice(g * gw, (g + 1) * gw)
        bg = xc[:, dx + g * nstate:dx + (g + 1) * nstate].astype(MXU_DTYPE)
        cg = xc[:, dx + dn + g * nstate:dx + dn + (g + 1) * nstate].astype(MXU_DTYPE)
        cum_g = cum_scr[:, gs]
        last = cum_scr[L - 1:L, gs]
        hg = h_scr[g]
        y_inter = _dot(cg, hg.astype(MXU_DTYPE)) * jnp.exp2(cum_g)
        wx = (jnp.exp2(last - cum_g) * xdt_scr[:, gs]).astype(MXU_DTYPE)
        h_scr[g] = jnp.exp2(last) * hg + _dot_tn(bg, wx)
        y = yin_scr[:, gs] + y_inter + dsk_ref[:, gs] * xc[:, gs]
        y = y * _silu(z_ref[:, gs].astype(F32))
        y_ref[:, gs] = (_rms(y) * nrm_ref[:, gs]).astype(y_ref.dtype)

    @pl.when(c == nc - 1)
    def _():
        nst_ref[0] = h_scr[...]


def ssd_scan(proj, dt_raw, col, conv_w, conv_b, dt_bias, a_log, d_skip_e, ssd_norm, conv_state, ssd_state,
             *, batch, L, groups, hpg, hdim, nstate):
    n = proj.shape[0]
    t = n // batch
    nc = t // L
    dx = groups * hpg * hdim
    dn = groups * nstate
    cdim = dx + 2 * dn
    has_state = ssd_state is not None
    rowmap = lambda b, c: b * nc + c
    in_specs = [pl.BlockSpec((L, dx), lambda b, c: (rowmap(b, c), col["z"])),
                pl.BlockSpec((L, dx), lambda b, c: (rowmap(b, c), col["xs"])),
                pl.BlockSpec((L, dn), lambda b, c: (rowmap(b, c), col["B"])),
                pl.BlockSpec((L, dn), lambda b, c: (rowmap(b, c), col["C"])),
                pl.BlockSpec((L, V7X_LANES), lambda b, c: (rowmap(b, c), 0)),
                pl.BlockSpec((CONV_K, cdim), lambda b, c: (0, 0)),
                pl.BlockSpec((1, cdim), lambda b, c: (0, 0)),
                pl.BlockSpec((1, V7X_LANES), lambda b, c: (0, 0)),
                pl.BlockSpec((1, V7X_LANES), lambda b, c: (0, 0)),
                pl.BlockSpec((1, dx), lambda b, c: (0, 0)),
                pl.BlockSpec((1, dx), lambda b, c: (0, 0)),
                pl.BlockSpec((1, CONV_PAD_ROWS, cdim), lambda b, c: (b, 0, 0))]
    args = [proj, proj, proj, proj, dt_raw, conv_w, conv_b.reshape(1, cdim), dt_bias, a_log,
            d_skip_e, ssd_norm.reshape(1, dx), conv_state]
    if has_state:
        in_specs.append(pl.BlockSpec((1, groups, nstate, hpg * hdim), lambda b, c: (b, 0, 0, 0)))
        args.append(ssd_state)
    state_bytes = _nbytes((groups, nstate, hpg * hdim), F32)
    vmem = (2 * (2 * _nbytes((L, dx), F32) + 2 * _nbytes((L, dn), F32) + _nbytes((L, dx), MXU_DTYPE)
                 + 2 * _nbytes((CONV_PAD_ROWS, cdim), F32) + 2 * state_bytes)
            + state_bytes + 12 * _nbytes((L + CONV_PAD_ROWS, cdim), F32) + (8 << 20))
    return _ScanPlan(
        body=functools.partial(_ssd_kernel, L=L, groups=groups, hpg=hpg, hdim=hdim, nstate=nstate,
                               has_state=has_state),
        args=args, in_specs=in_specs,
        out_shape=[jax.ShapeDtypeStruct((n, dx), MXU_DTYPE),
                   jax.ShapeDtypeStruct((batch, CONV_PAD_ROWS, cdim), F32),
                   jax.ShapeDtypeStruct((batch, groups, nstate, hpg * hdim), F32)],
        out_specs=[pl.BlockSpec((L, dx), lambda b, c: (rowmap(b, c), 0)),
                   pl.BlockSpec((1, CONV_PAD_ROWS, cdim), lambda b, c: (b, 0, 0)),
                   pl.BlockSpec((1, groups, nstate, hpg * hdim), lambda b, c: (b, 0, 0, 0))],
        scratch=[pltpu.VMEM((2 * CONV_PAD_ROWS, cdim), F32),
                 pltpu.VMEM((groups, nstate, hpg * hdim), F32),
                 pltpu.VMEM((L, dx), F32),
                 pltpu.VMEM((L, dx), F32),
                 pltpu.VMEM((L, dx), F32)],
        vmem=vmem)


def _ret_kernel(*refs, L, heads, hd, has_state):
    if has_state:
        (q_ref, k_ref, v_ref, gate_ref, cos_ref, sin_ref, dec_ref, ecum_ref, wv_ref, sdec_ref, st_ref,
         o_ref, nst_ref, s_scr) = refs
    else:
        (q_ref, k_ref, v_ref, gate_ref, cos_ref, sin_ref, dec_ref, ecum_ref, wv_ref, sdec_ref,
         o_ref, nst_ref, s_scr) = refs
        st_ref = None
    c = pl.program_id(1)
    nc = pl.num_programs(1)
    half = hd // 2

    @pl.when(c == 0)
    def _():
        if has_state:
            s_scr[...] = st_ref[0]
        else:
            s_scr[...] = jnp.zeros_like(s_scr)

    cos = cos_ref[...]
    sin = sin_ref[...]

    def rope(ref, h, mult):
        x1 = ref[:, h * hd:h * hd + half].astype(F32)
        x2 = ref[:, h * hd + half:(h + 1) * hd].astype(F32)
        out = jnp.concatenate([x1 * cos - x2 * sin, x2 * cos + x1 * sin], axis=-1)
        return (out * mult).astype(MXU_DTYPE) if mult != 1.0 else out.astype(MXU_DTYPE)

    for h in range(heads):
        hs = slice(h * hd, (h + 1) * hd)
        qr = rope(q_ref, h, 1.0)
        kr = rope(k_ref, h, hd ** -0.5)
        vf = v_ref[:, hs]
        qk = _dot_nt(qr, kr)
        y = _dot((qk * dec_ref[h]).astype(MXU_DTYPE), vf.astype(MXU_DTYPE))
        s_old = s_scr[h]
        y = y + _dot(qr, s_old.astype(MXU_DTYPE)) * ecum_ref[h]
        s_scr[h] = sdec_ref[h] * s_old + _dot_tn(kr, (wv_ref[h] * vf.astype(F32)).astype(MXU_DTYPE))
        o_ref[:, hs] = (_rms(y) * _silu(gate_ref[:, hs].astype(F32))).astype(o_ref.dtype)

    @pl.when(c == nc - 1)
    def _():
        nst_ref[0] = s_scr[...]


def ret_scan(proj, col, cos_rows, sin_rows, ret_state, *, batch, L, heads, hd):
    n = proj.shape[0]
    t = n // batch
    nc = t // L
    inner = heads * hd
    has_state = ret_state is not None
    lg = jnp.log1p(-jnp.exp2(-5.0 - jnp.arange(heads, dtype=F32)))[:, None, None]
    li = jnp.arange(L, dtype=F32)
    diff = li[:, None] - li[None, :]
    dec = jnp.where(diff >= 0, jnp.exp(jnp.where(diff >= 0, diff, 0.0)[None] * lg), 0.0)
    ecum = jnp.broadcast_to(jnp.exp((li[None, :, None] + 1.0) * lg), (heads, L, hd))
    wv = jnp.broadcast_to(jnp.exp((L - 1.0 - li[None, :, None]) * lg), (heads, L, hd))
    sdec = jnp.broadcast_to(jnp.exp(L * lg), (heads, 1, hd))
    rowmap = lambda b, c: b * nc + c
    in_specs = [pl.BlockSpec((L, inner), lambda b, c: (rowmap(b, c), col["q"])),
                pl.BlockSpec((L, inner), lambda b, c: (rowmap(b, c), col["k"])),
                pl.BlockSpec((L, inner), lambda b, c: (rowmap(b, c), col["v"])),
                pl.BlockSpec((L, inner), lambda b, c: (rowmap(b, c), col["gate"])),
                pl.BlockSpec((L, hd // 2), lambda b, c: (rowmap(b, c), 0)),
                pl.BlockSpec((L, hd // 2), lambda b, c: (rowmap(b, c), 0)),
                pl.BlockSpec((heads, L, L), lambda b, c: (0, 0, 0)),
                pl.BlockSpec((heads, L, hd), lambda b, c: (0, 0, 0)),
                pl.BlockSpec((heads, L, hd), lambda b, c: (0, 0, 0)),
                pl.BlockSpec((heads, 1, hd), lambda b, c: (0, 0, 0))]
    args = [proj, proj, proj, proj, cos_rows, sin_rows, dec, ecum, wv, sdec]
    if has_state:
        in_specs.append(pl.BlockSpec((1, heads, hd, hd), lambda b, c: (b, 0, 0, 0)))
        args.append(ret_state)
    state_bytes = _nbytes((heads, hd, hd), F32)
    vmem = (2 * (4 * _nbytes((L, inner), F32) + _nbytes((L, inner), MXU_DTYPE) + _nbytes((heads, L, L), F32)
                 + 2 * _nbytes((heads, L, hd), F32) + 2 * state_bytes)
            + state_bytes + 16 * _nbytes((L, hd), F32) + (8 << 20))
    return _ScanPlan(
        body=functools.partial(_ret_kernel, L=L, heads=heads, hd=hd, has_state=has_state),
        args=args, in_specs=in_specs,
        out_shape=[jax.ShapeDtypeStruct((n, inner), MXU_DTYPE),
                   jax.ShapeDtypeStruct((batch, heads, hd, hd), F32)],
        out_specs=[pl.BlockSpec((L, inner), lambda b, c: (rowmap(b, c), 0)),
                   pl.BlockSpec((1, heads, hd, hd), lambda b, c: (b, 0, 0, 0))],
        scratch=[pltpu.VMEM((heads, hd, hd), F32)],
        vmem=vmem)


def _scan_pair_kernel(*refs, bodies, n_in, n_out, n_scr):
    ins, outs, scrs = refs[:sum(n_in)], refs[sum(n_in):sum(n_in) + sum(n_out)], refs[sum(n_in) + sum(n_out):]
    for k, body in enumerate(bodies):
        take = lambda seq, counts: seq[sum(counts[:k]):sum(counts[:k + 1])]
        body(*take(ins, n_in), *take(outs, n_out), *take(scrs, n_scr))


def run_scans(plans, *, batch, nc, name):
    outs = pl.pallas_call(
        functools.partial(_scan_pair_kernel, bodies=tuple(p.body for p in plans),
                          n_in=tuple(len(p.args) for p in plans),
                          n_out=tuple(len(p.out_shape) for p in plans),
                          n_scr=tuple(len(p.scratch) for p in plans)),
        out_shape=tuple(s for p in plans for s in p.out_shape),
        grid=(batch, nc),
        in_specs=[s for p in plans for s in p.in_specs],
        out_specs=tuple(s for p in plans for s in p.out_specs),
        scratch_shapes=[s for p in plans for s in p.scratch],
        compiler_params=_params(("parallel", "arbitrary"), sum(p.vmem for p in plans)),
        name=name,
    )(*[a for p in plans for a in p.args])
    split, k = [], 0
    for p in plans:
        split.append(outs[k:k + len(p.out_shape)])
        k += len(p.out_shape)
    return split


def _rope_tables(pos, half, reps, batch):
    inv = ROPE_BASE ** (-jnp.arange(half, dtype=F32) / half)
    ang = pos.astype(F32)[:, None] * inv[None, :]
    cos, sin = jnp.cos(ang), jnp.sin(ang)
    if reps == 0:
        return jnp.tile(cos, (batch, 1)), jnp.tile(sin, (batch, 1))
    c = jnp.tile(jnp.concatenate([cos, cos], axis=-1), (batch, reps))
    s = jnp.tile(jnp.concatenate([-sin, sin], axis=-1), (batch, reps))
    return c, s


def _mixer_even(x, batch, pos, norm_g, prm, conv_state, ssd_state, ret_state, dims, L_ssd, L_ret):
    d = x.shape[1]
    groups, hpg, hdim, nstate, rheads, rhd = dims
    proj, dt_raw = rms_matmul(x, norm_g, prm["w_in"], out_dtype=MXU_DTYPE, w_side=prm["w_dt"])
    col = prm["col"]
    ssd_plan = ssd_scan(proj, dt_raw, col, prm["conv_w"], prm["conv_b"], prm["dt_bias"],
                        prm["a_log"], prm["d_skip_e"], prm["ssd_norm"], conv_state, ssd_state,
                        batch=batch, L=L_ssd, groups=groups, hpg=hpg, hdim=hdim, nstate=nstate)
    cos_rows, sin_rows = _rope_tables(pos, rhd // 2, 0, batch)
    ret_plan = ret_scan(proj, col, cos_rows, sin_rows, ret_state, batch=batch, L=L_ret, heads=rheads, hd=rhd)
    t = x.shape[0] // batch
    (y, new_conv, new_ssd), = run_scans([ssd_plan], batch=batch, nc=t // L_ssd, name="ssd_scan")
    (o, new_ret), = run_scans([ret_plan], batch=batch, nc=t // L_ret, name="ret_scan")
    x = matmul_residual(x, [y, o], [prm["w_out_ssd"], prm["w_out_ret"]])
    return x, (new_conv, new_ssd, new_ret)


def _prep_even(w_in, conv_w, conv_b, dt_bias, a_log, d_skip, ssd_norm, w_out, dims):
    groups, hpg, hdim, nstate, rheads, rhd = dims
    d = w_in.shape[0]
    dx = groups * hpg * hdim
    dn = groups * nstate
    nh = groups * hpg
    ri = rheads * rhd
    o_z, o_xs, o_b, o_c, o_dt = 0, dx, 2 * dx, 2 * dx + dn, 2 * dx + 2 * dn
    o_q = o_dt + nh
    seg = lambda lo, w: w_in[:, lo:lo + w]
    dt_pad = jnp.zeros((d, V7X_LANES - nh), w_in.dtype)
    w_main = jnp.concatenate([seg(o_z, dx), seg(o_q, ri), seg(o_q + ri, ri), seg(o_q + 2 * ri, ri),
                              seg(o_q + 3 * ri, ri), seg(o_xs, dx), seg(o_b, dn), seg(o_c, dn)],
                             axis=1).astype(MXU_DTYPE)
    w_dt = jnp.concatenate([seg(o_dt, nh), dt_pad], axis=1).astype(MXU_DTYPE)
    assert dx == ri and dx % dn == 0 and dn % V7X_LANES == 0
    col = {"z": 0, "q": 1, "k": 2, "v": 3, "gate": 4, "xs": 5, "B": 6 * dx // dn, "C": 6 * dx // dn + 1}
    pad1 = lambda v: jnp.pad(v.astype(F32), (0, V7X_LANES - nh)).reshape(1, V7X_LANES)
    return {"w_in": w_main, "w_dt": w_dt, "col": col, "conv_w": conv_w, "conv_b": conv_b,
            "dt_bias": pad1(dt_bias), "a_log": pad1(a_log),
            "d_skip_e": jnp.repeat(d_skip, hdim).reshape(1, dx), "ssd_norm": ssd_norm,
            "w_out_ssd": w_out[:dx].astype(MXU_DTYPE), "w_out_ret": w_out[dx:].astype(MXU_DTYPE)}


def _prep_odd(w_in, q_norm, kv_norm, w_uq, w_uk, w_uv, w_out, rope_dim):
    d = w_in.shape[0]
    q_lora = q_norm.shape[0]
    kv_lora, heads, nope = w_uk.shape
    kp = w_in[:, q_lora + kv_lora:]
    w_in_new = jnp.concatenate([w_in[:, :q_lora + kv_lora], kp, kp], axis=1).astype(MXU_DTYPE)
    assert 2 * rope_dim == V7X_LANES
    wq = w_uq.reshape(q_lora, heads, nope + rope_dim)
    wq_nope = wq[:, :, :nope]
    wq_rope = jnp.pad(wq[:, :, nope:], ((0, 0), (0, 0), (0, V7X_LANES - rope_dim)))
    hw = nope + V7X_LANES
    return {"w_in": w_in_new, "q_norm": q_norm, "kv_norm": kv_norm,
            "w_uq_lat": jnp.concatenate([wq_nope.reshape(q_lora, heads * nope),
                                         wq_rope.reshape(q_lora, heads * V7X_LANES)], axis=1).astype(MXU_DTYPE),
            "slabs_lat": tuple(heads * nope + h * V7X_LANES for h in range(heads)),
            "w_uq_head": jnp.concatenate([wq_nope, wq_rope], axis=2).reshape(q_lora, heads * hw).astype(MXU_DTYPE),
            "slabs_head": tuple(h * hw + nope for h in range(heads)),
            "w_ukv": jnp.concatenate([w_uk.reshape(kv_lora, heads * nope),
                                      w_uv.reshape(kv_lora, -1)], axis=1).astype(MXU_DTYPE),
            "w_uk": jnp.transpose(w_uk, (1, 2, 0)).astype(MXU_DTYPE),
            "w_uv": jnp.transpose(w_uv, (1, 0, 2)).astype(MXU_DTYPE),
            "w_out": w_out.astype(MXU_DTYPE)}


def _mixer_odd(x, batch, pos, norm_g, prm, past, rope_dim, q_offset):
    n, d = x.shape
    t = n // batch
    q_lora = prm["q_norm"].shape[0]
    kv_lora = prm["kv_norm"].shape[0]
    heads, nope, lat = prm["w_uk"].shape
    proj = rms_matmul(x, norm_g, prm["w_in"])
    cos_rows, sin_rows = _rope_tables(pos, rope_dim // 2, V7X_LANES // rope_dim, batch)
    scale = (nope + rope_dim) ** -0.5
    dims = dict(q_lora=q_lora, kv_lora=kv_lora, rope_dim=rope_dim, heads=heads, nope=nope)
    if past is None:
        q, ckv, kpe, k, v = mla_prep(proj, prm["q_norm"], prm["kv_norm"], prm["w_uq_head"], cos_rows, sin_rows,
                                     prm["w_ukv"], rope_slabs=prm["slabs_head"], qscale=scale * math.log2(math.e),
                                     **dims)
        tq = tk = _pick(t, (512, 256, 128, 64))
        shp = lambda a: a.reshape(batch, t, a.shape[1])
        o = mha_attn(shp(q), shp(k), shp(v), heads=heads, tq=tq, tk=tk, q_offset=q_offset, kv_len=t)
    else:
        q, ckv, kpe, kcat = mla_prep(proj, prm["q_norm"], prm["kv_norm"], prm["w_uq_lat"], cos_rows, sin_rows,
                                     None, rope_slabs=prm["slabs_lat"], qscale=1.0, **dims)
        q = q.reshape(batch, t, q.shape[1])
        kcat = kcat.reshape(batch, t, kcat.shape[1])
        o = mla_decode(q, past[0], past[1], kcat, prm["w_uk"], prm["w_uv"], q_offset=q_offset, scale=scale)
    x = matmul_residual(x, [o.reshape(n, o.shape[2])], [prm["w_out"]])
    return x, (ckv.reshape(batch, t, kv_lora), kpe.reshape(batch, t, rope_dim))


def kernel(x_prompt, x_sample, mem_prompt, state_conv, state_ssd, state_ret, cache_ckv, cache_kpe,
           cache_mem_k, cache_mem_v, norms, ffn_w1, ffn_w2, mem_norm, w_mq, w_mkv, w_mo,
           ab_w_in, ab_conv_w, ab_conv_b, ab_dt_bias, ab_a_log, ab_d_skip, ab_ssd_norm, ab_w_out,
           c_w_in, c_q_norm, c_kv_norm, c_w_uq, c_w_uk, c_w_uv, c_w_out, final_norm):
    bp, tp, d = x_prompt.shape
    bs, ts, _ = x_sample.shape
    depth = norms.shape[0]
    assert depth >= 1
    past_len = cache_ckv.shape[2]
    mem_tokens = mem_prompt.shape[1]
    mem_heads, mem_hd = cache_mem_k.shape[3], cache_mem_k.shape[4]
    mem_inner = mem_heads * mem_hd
    ssd_heads, nstate, hdim = state_ssd.shape[2], state_ssd.shape[3], state_ssd.shape[4]
    cdim = state_conv.shape[3]
    groups = (cdim - ssd_heads * hdim) // (2 * nstate)
    hpg = ssd_heads // groups
    rheads, rhd = state_ret.shape[2], state_ret.shape[3]
    dims = (groups, hpg, hdim, nstate, rheads, rhd)
    rope_dim = cache_kpe.shape[3]

    pos_p = jnp.arange(tp)
    pos_s = past_len + jnp.arange(ts)
    xp = x_prompt.reshape(bp * tp, d)
    xs = x_sample.reshape(bs * ts, d)
    L_ssd_p, L_ret_p = _pick(tp, (128, 64)), _pick(tp, (256, 128, 64))
    L_s = _pick(ts, (128, 64))

    outs = {k: [] for k in ("conv_p", "ssd_p", "ret_p", "ckv_p", "kpe_p", "memk_p", "memv_p",
                            "conv_s", "ssd_s", "ret_s", "ckv_s", "kpe_s")}

    def to_group_layout(st):
        b = st.shape[0]
        return st.reshape(b, groups, hpg, nstate, hdim).transpose(0, 1, 3, 2, 4).reshape(b, groups, nstate, hpg * hdim)

    def from_group_layout(st):
        b = st.shape[0]
        return st.reshape(b, groups, nstate, hpg, hdim).transpose(0, 1, 3, 2, 4).reshape(b, ssd_heads, nstate, hdim)

    w1 = ffn_w1.astype(MXU_DTYPE)
    w2 = (0.5 * ffn_w2).astype(MXU_DTYPE)
    for i in range(depth):
        j = i // 2
        closing = final_norm if i == depth - 1 else None
        wq_m = w_mq[i].astype(MXU_DTYPE)
        wo_m = w_mo[i].astype(MXU_DTYPE)
        mkv = rms_matmul(mem_prompt.reshape(bp * mem_tokens, d), mem_norm[i], w_mkv[i].astype(MXU_DTYPE))
        mk_p = mkv[:, :mem_inner].reshape(bp, mem_tokens, mem_inner)
        mv_p = mkv[:, mem_inner:].reshape(bp, mem_tokens, mem_inner)
        outs["memk_p"].append(mk_p.reshape(bp, mem_tokens, mem_heads, mem_hd))
        outs["memv_p"].append(mv_p.reshape(bp, mem_tokens, mem_heads, mem_hd))

        xp = ffn(xp, norms[i, 0], w1, w2, i, 0)
        xs = ffn(xs, norms[i, 0], w1, w2, i, 0)
        if i % 2 == 0:
            prm = _prep_even(ab_w_in[j], ab_conv_w[j], ab_conv_b[j], ab_dt_bias[j], ab_a_log[j],
                             ab_d_skip[j], ab_ssd_norm[j], ab_w_out[j], dims)
            zero_conv = jnp.zeros((bp, CONV_PAD_ROWS, cdim), F32)
            xp, st_p = _mixer_even(xp, bp, pos_p, norms[i, 1], prm, zero_conv, None, None, dims, L_ssd_p, L_ret_p)
            conv_in = jnp.pad(state_conv[j], ((0, 0), (CONV_PAD_ROWS - (CONV_K - 1), 0), (0, 0)))
            xs, st_s = _mixer_even(xs, bs, pos_s, norms[i, 1], prm, conv_in, to_group_layout(state_ssd[j]),
                                   state_ret[j], dims, L_s, L_s)
            for tag, st in (("p", st_p), ("s", st_s)):
                outs["conv_" + tag].append(st[0][:, CONV_PAD_ROWS - (CONV_K - 1):, :])
                outs["ssd_" + tag].append(from_group_layout(st[1]))
                outs["ret_" + tag].append(st[2])
        else:
            prm = _prep_odd(c_w_in[j], c_q_norm[j], c_kv_norm[j], c_w_uq[j], c_w_uk[j], c_w_uv[j], c_w_out[j],
                            rope_dim)
            xp, st_p = _mixer_odd(xp, bp, pos_p, norms[i, 1], prm, None, rope_dim, 0)
            xs, st_s = _mixer_odd(xs, bs, pos_s, norms[i, 1], prm, (cache_ckv[j], cache_kpe[j]), rope_dim, past_len)
            for tag, st in (("p", st_p), ("s", st_s)):
                outs["ckv_" + tag].append(st[0])
                outs["kpe_" + tag].append(st[1])
        xp = mem_attn(xp, norms[i, 2], wq_m, mk_p, mv_p, wo_m, batch=bp, heads=mem_heads)
        xs = mem_attn(xs, norms[i, 2], wq_m, cache_mem_k, cache_mem_v, wo_m, batch=bs, heads=mem_heads, layer=i)
        xp = ffn(xp, norms[i, 3], w1, w2, i, 1, final_g=closing)
        xs = ffn(xs, norms[i, 3], w1, w2, i, 1, final_g=closing)

    y_prompt = xp.reshape(bp, tp, d)
    y_sample = xs.reshape(bs, ts, d)
    st = lambda k: jnp.stack(outs[k])
    return (y_prompt, y_sample, st("conv_p"), st("ssd_p"), st("ret_p"), st("ckv_p"), st("kpe_p"),
            st("memk_p"), st("memv_p"), st("conv_s"), st("ssd_s"), st("ret_s"), st("ckv_s"), st("kpe_s"))
```

```python
import functools
import math
from typing import Callable, NamedTuple

import jax
import jax.numpy as jnp
from jax import lax
from jax.experimental import pallas as pl
from jax.experimental.pallas import tpu as pltpu

F32 = jnp.float32
MXU_DTYPE = jnp.bfloat16

EPS = 1e-6
CHUNK = 64
ROPE_BASE = 10000.0
CONV_K = 4
NEG = -1e30
LOG2E = math.log2(math.e)

V7X_VMEM_BYTES = 64 * 1024 * 1024
TILE_VMEM_BUDGET = 48 * 1024 * 1024
V7X_LANES = 128
V7X_SUBLANES = 8
CONV_PAD_ROWS = 8


def _params(semantics, vmem_bytes):
    limit = min(int(vmem_bytes), V7X_VMEM_BYTES - (4 << 20))
    return pltpu.CompilerParams(dimension_semantics=semantics, vmem_limit_bytes=limit)


def _nbytes(shape, dtype):
    return math.prod(shape) * jnp.dtype(dtype).itemsize


def _pick(n, prefs):
    for p in prefs:
        if n % p == 0:
            return p
    return n


def _dot(a, b):
    return jnp.dot(a, b, preferred_element_type=F32)


def _dot_nt(a, b):
    return lax.dot_general(a, b, (((1,), (1,)), ((), ())), preferred_element_type=F32)


def _dot_tn(a, b):
    return lax.dot_general(a, b, (((0,), (0,)), ((), ())), preferred_element_type=F32)


def _split3(x):
    hi = x.astype(MXU_DTYPE)
    r = x - hi.astype(F32)
    mid = r.astype(MXU_DTYPE)
    lo = (r - mid.astype(F32)).astype(MXU_DTYPE)
    return hi, mid, lo


def _rms(xf, g=None):
    y = xf * lax.rsqrt(jnp.mean(xf * xf, axis=-1, keepdims=True) + EPS)
    return y if g is None else y * g


def _silu(a):
    return a * (1.0 / (1.0 + jnp.exp(-a)))


def _swap32(x):
    w = x.shape[-1]
    lane = lax.broadcasted_iota(jnp.int32, x.shape, x.ndim - 1)
    fwd = pltpu.roll(x, w - 32, x.ndim - 1)
    bwd = pltpu.roll(x, 32, x.ndim - 1)
    return jnp.where((lane & 63) < 32, fwd, bwd)


def _rms_matmul_kernel(*refs, side):
    if side:
        x_ref, g_ref, w_ref, ws_ref, o_ref, os_ref, xn_ref = refs
    else:
        x_ref, g_ref, w_ref, o_ref, xn_ref = refs

    @pl.when(pl.program_id(1) == 0)
    def _():
        xn_ref[...] = _rms(x_ref[...], g_ref[...]).astype(xn_ref.dtype)
        if side:
            os_ref[...] = _dot(xn_ref[...], ws_ref[...])

    o_ref[...] = _dot(xn_ref[...], w_ref[...]).astype(o_ref.dtype)


def rms_matmul(x, g, w, *, out_dtype=F32, w_side=None):
    n, d = x.shape
    nout = w.shape[1]
    side = w_side is not None
    tn = _pick(nout, (1920, 1152, 1024, 512, 256, 128))

    def vmem_for(tm):
        return (2 * (_nbytes((tm, d), F32) + _nbytes((d, tn), w.dtype) + _nbytes((tm, tn), out_dtype))
                + _nbytes((tm, d), MXU_DTYPE) + _nbytes((tm, tn), F32) + (8 << 20))

    tm = next(t for t in (1024, 512, 256, 128, n) if n % t == 0 and (vmem_for(t) <= TILE_VMEM_BUDGET or t <= 128))
    vmem = vmem_for(tm)
    in_specs = [pl.BlockSpec((tm, d), lambda i, j: (i, 0)),
                pl.BlockSpec((1, d), lambda i, j: (0, 0)),
                pl.BlockSpec((d, tn), lambda i, j: (0, j))]
    args = [x, g.reshape(1, d), w]
    out_shape = jax.ShapeDtypeStruct((n, nout), out_dtype)
    out_specs = pl.BlockSpec((tm, tn), lambda i, j: (i, j))
    if side:
        ns = w_side.shape[1]
        in_specs.append(pl.BlockSpec((d, ns), lambda i, j: (0, 0)))
        args.append(w_side)
        out_shape = (out_shape, jax.ShapeDtypeStruct((n, ns), F32))
        out_specs = (out_specs, pl.BlockSpec((tm, ns), lambda i, j: (i, 0)))
        vmem += 2 * (_nbytes((d, ns), w_side.dtype) + _nbytes((tm, ns), F32))
    return pl.pallas_call(
        functools.partial(_rms_matmul_kernel, side=side),
        out_shape=out_shape,
        grid=(n // tm, nout // tn),
        in_specs=in_specs,
        out_specs=out_specs,
        scratch_shapes=[pltpu.VMEM((tm, d), MXU_DTYPE)],
        compiler_params=_params(("parallel", "arbitrary"), vmem),
        name="rms_matmul",
    )(*args)


def _ffn_kernel(*refs, final_norm, nf):
    x_ref, g_ref = refs[0:2]
    blocks = (refs[2:5], refs[5:8])
    gf_ref = refs[8] if final_norm else None
    o_ref, xn_ref = refs[-2:]
    s = pl.program_id(1)

    @pl.when(s == 0)
    def _():
        xf = x_ref[...]
        xn_ref[...] = _rms(xf, g_ref[...]).astype(xn_ref.dtype)
        o_ref[...] = xf

    def down(w1a_ref, w1b_ref, w2_ref):
        xn = xn_ref[...]
        a = _dot(xn, w1a_ref[...])
        b = _dot(xn, w1b_ref[...])
        return _dot((_silu(a) * b).astype(MXU_DTYPE), w2_ref[...])

    if nf % 2 == 0:
        o_ref[...] += down(*blocks[0]) + down(*blocks[1])
    else:
        @pl.when(s == 0)
        def _():
            o_ref[...] += down(*blocks[0])

        @pl.when(s > 0)
        def _():
            o_ref[...] += down(*blocks[0]) + down(*blocks[1])

    if final_norm:
        @pl.when(s == pl.num_programs(1) - 1)
        def _():
            o_ref[...] = _rms(o_ref[...], gf_ref[...])


def _ffn_cast_kernel(*refs, final_norm):
    x_ref, g_ref, w1a_ref, w1b_ref, w2_ref = refs[0:5]
    gf_ref = refs[5] if final_norm else None
    o_ref, w1a_o, w1b_o, w2_o, xn_ref = refs[-5:]
    s = pl.program_id(1)

    @pl.when(s == 0)
    def _():
        xf = x_ref[...]
        xn_ref[...] = _rms(xf, g_ref[...]).astype(xn_ref.dtype)
        o_ref[...] = xf

    w1a_o[...] = w1a_ref[...].astype(w1a_o.dtype)
    w1b_o[...] = w1b_ref[...].astype(w1b_o.dtype)
    w2_o[...] = (0.5 * w2_ref[...]).astype(w2_o.dtype)
    xn = xn_ref[...]
    h = (_silu(_dot(xn, w1a_o[...])) * _dot(xn, w1b_o[...])).astype(MXU_DTYPE)
    o_ref[...] += _dot(h, w2_o[...])

    if final_norm:
        @pl.when(s == pl.num_programs(1) - 1)
        def _():
            o_ref[...] = _rms(o_ref[...], gf_ref[...])


def ffn_cast(x, g, w1, w2, layer, which, final_g=None):
    n, d = x.shape
    dff = w2.shape[2]
    tm = _pick(n, (512, 256, 128))
    tf = _pick(dff, (256, 128))
    nf = dff // tf
    vmem = (2 * (2 * _nbytes((tm, d), F32) + 3 * _nbytes((d, tf), F32) + 3 * _nbytes((d, tf), MXU_DTYPE))
            + _nbytes((tm, d), MXU_DTYPE) + _nbytes((tm, d), F32) + 6 * _nbytes((tm, tf), F32) + (6 << 20))
    in_specs = [pl.BlockSpec((tm, d), lambda i, s: (i, 0)),
                pl.BlockSpec((1, d), lambda i, s: (0, 0)),
                pl.BlockSpec((None, None, d, tf), lambda i, s: (layer, which, 0, s)),
                pl.BlockSpec((None, None, d, tf), lambda i, s: (layer, which, 0, s + nf)),
                pl.BlockSpec((None, None, tf, d), lambda i, s: (layer, which, s, 0))]
    args = [x, g.reshape(1, d), w1, w1, w2]
    if final_g is not None:
        in_specs.append(pl.BlockSpec((1, d), lambda i, s: (0, 0)))
        args.append(final_g.reshape(1, d))
    return pl.pallas_call(
        functools.partial(_ffn_cast_kernel, final_norm=final_g is not None),
        out_shape=(jax.ShapeDtypeStruct((n, d), F32),
                   jax.ShapeDtypeStruct((d, dff), MXU_DTYPE),
                   jax.ShapeDtypeStruct((d, dff), MXU_DTYPE),
                   jax.ShapeDtypeStruct((dff, d), MXU_DTYPE)),
        grid=(n // tm, nf),
        in_specs=in_specs,
        out_specs=(pl.BlockSpec((tm, d), lambda i, s: (i, 0)),
                   pl.BlockSpec((d, tf), lambda i, s: (0, s)),
                   pl.BlockSpec((d, tf), lambda i, s: (0, s)),
                   pl.BlockSpec((tf, d), lambda i, s: (s, 0))),
        scratch_shapes=[pltpu.VMEM((tm, d), MXU_DTYPE)],
        compiler_params=_params(("arbitrary", "arbitrary"), vmem),
        name="ffn_cast",
    )(*args)


def ffn(x, g, w1a, w1b, w2h, final_g=None):
    n, d = x.shape
    dff = w2h.shape[0]
    tm = _pick(n, (512, 256, 128))
    tf = _pick(dff, (512, 256, 128))
    nf = dff // tf
    vmem = (2 * (2 * _nbytes((tm, d), F32) + 6 * _nbytes((d, tf), w1a.dtype))
            + _nbytes((tm, d), MXU_DTYPE) + _nbytes((tm, d), F32) + 6 * _nbytes((tm, tf), F32) + (6 << 20))
    in_specs = [pl.BlockSpec((tm, d), lambda i, s: (i, 0)),
                pl.BlockSpec((1, d), lambda i, s: (0, 0))]
    args = [x, g.reshape(1, d)]
    lead = nf % 2
    for half in range(2):
        blk = lambda s, half=half: jnp.maximum(2 * s - lead + half, half * (lead + 1))
        in_specs += [pl.BlockSpec((d, tf), lambda i, s, blk=blk: (0, blk(s))),
                     pl.BlockSpec((d, tf), lambda i, s, blk=blk: (0, blk(s))),
                     pl.BlockSpec((tf, d), lambda i, s, blk=blk: (blk(s), 0))]
        args += [w1a, w1b, w2h]
    if final_g is not None:
        in_specs.append(pl.BlockSpec((1, d), lambda i, s: (0, 0)))
        args.append(final_g.reshape(1, d))
    return pl.pallas_call(
        functools.partial(_ffn_kernel, final_norm=final_g is not None, nf=nf),
        out_shape=jax.ShapeDtypeStruct((n, d), F32),
        grid=(n // tm, -(-nf // 2)),
        in_specs=in_specs,
        out_specs=pl.BlockSpec((tm, d), lambda i, f: (i, 0)),
        scratch_shapes=[pltpu.VMEM((tm, d), MXU_DTYPE)],
        compiler_params=_params(("parallel", "arbitrary"), vmem),
        name="ffn",
    )(*args)


def _matmul_residual_kernel(*refs, n_in):
    x_ref = refs[0]
    o_ref = refs[1 + 2 * n_in]
    acc = x_ref[...]
    for h_ref, w_ref in zip(refs[1:1 + n_in], refs[1 + n_in:1 + 2 * n_in]):
        acc = acc + _dot(h_ref[...], w_ref[...])
    o_ref[...] = acc


def matmul_residual(x, hs, ws):
    n, d = x.shape

    def vmem_for(tm, tn):
        return (2 * (2 * _nbytes((tm, tn), F32) + sum(_nbytes((tm, h.shape[1]), h.dtype) for h in hs)
                     + sum(_nbytes((w.shape[0], tn), w.dtype) for w in ws))
                + _nbytes((tm, tn), F32) + (8 << 20))

    tm, tn = next((a, b) for a, b in ((512, d), (1024, 1024), (512, 1024), (256, 512), (128, 128), (n, d))
                  if n % a == 0 and d % b == 0 and (vmem_for(a, b) <= TILE_VMEM_BUDGET or a <= 128))
    vmem = vmem_for(tm, tn)
    in_specs = [pl.BlockSpec((tm, tn), lambda i, j: (i, j))]
    for h in hs:
        in_specs.append(pl.BlockSpec((tm, h.shape[1]), lambda i, j: (i, 0)))
    for w in ws:
        in_specs.append(pl.BlockSpec((w.shape[0], tn), lambda i, j: (0, j)))
    return pl.pallas_call(
        functools.partial(_matmul_residual_kernel, n_in=len(hs)),
        out_shape=jax.ShapeDtypeStruct((n, d), F32),
        grid=(n // tm, d // tn),
        in_specs=in_specs,
        out_specs=pl.BlockSpec((tm, tn), lambda i, j: (i, j)),
        compiler_params=_params(("parallel", "arbitrary"), vmem),
        name="matmul_residual",
    )(x, *hs, *ws)


def _mem_attn_kernel(x_ref, g_ref, wq_ref, k_ref, v_ref, wo_ref, o_ref, att_ref, *, bt, tq, heads, hd):
    head_axis = len(k_ref.shape) == 4
    xf = x_ref[...]
    xn = _rms(xf, g_ref[...]).astype(MXU_DTYPE)
    q = _dot(xn, wq_ref[...]).astype(MXU_DTYPE)
    scale = hd ** -0.5
    for b in range(bt):
        for h in range(heads):
            qh = q[b * tq:(b + 1) * tq, h * hd:(h + 1) * hd]
            if head_axis:
                kh = k_ref[b, :, h, :].astype(MXU_DTYPE)
                vh = v_ref[b, :, h, :].astype(MXU_DTYPE)
            else:
                kh = k_ref[b, :, h * hd:(h + 1) * hd].astype(MXU_DTYPE)
                vh = v_ref[b, :, h * hd:(h + 1) * hd].astype(MXU_DTYPE)
            s = _dot_nt(qh, kh) * scale
            m = jnp.max(s, axis=-1, keepdims=True)
            p = jnp.exp(s - m)
            l = jnp.sum(p, axis=-1, keepdims=True)
            oh = _dot(p.astype(MXU_DTYPE), vh) / l
            att_ref[b * tq:(b + 1) * tq, h * hd:(h + 1) * hd] = oh.astype(att_ref.dtype)
    o_ref[...] = xf + _dot(att_ref[...], wo_ref[...])


def mem_attn(x, g, wq, mem_k, mem_v, wo, *, batch, heads, layer=None):
    n, d = x.shape
    t = n // batch
    cached = layer is not None
    m = mem_k.shape[2] if cached else mem_k.shape[1]
    inner = wq.shape[1]
    hd = inner // heads
    if t >= 128:
        bt, tq = 1, _pick(t, (512, 256, 128))
    else:
        bt, tq = _pick(batch, (4, 2, 1)), t
    nt = t // tq
    rows = bt * tq
    if cached:
        kv_spec = pl.BlockSpec((None, bt, m, heads, hd), lambda b, i: (layer, b, 0, 0, 0))
        kv_bytes = _nbytes((bt, m, max(heads, V7X_SUBLANES), hd), mem_k.dtype)
    else:
        kv_spec = pl.BlockSpec((bt, m, inner), lambda b, i: (b, 0, 0))
        kv_bytes = _nbytes((bt, m, inner), mem_k.dtype)
    vmem = (2 * (2 * _nbytes((rows, d), F32) + 2 * kv_bytes + 2 * _nbytes((d, inner), wq.dtype))
            + 2 * _nbytes((rows, d), F32) + (8 << 20))
    return pl.pallas_call(
        functools.partial(_mem_attn_kernel, bt=bt, tq=tq, heads=heads, hd=hd),
        out_shape=jax.ShapeDtypeStruct((n, d), F32),
        grid=(batch // bt, nt),
        in_specs=[pl.BlockSpec((rows, d), lambda b, i: (b * nt + i, 0)),
                  pl.BlockSpec((1, d), lambda b, i: (0, 0)),
                  pl.BlockSpec((d, inner), lambda b, i: (0, 0)),
                  kv_spec, kv_spec,
                  pl.BlockSpec((inner, d), lambda b, i: (0, 0))],
        out_specs=pl.BlockSpec((rows, d), lambda b, i: (b * nt + i, 0)),
        scratch_shapes=[pltpu.VMEM((rows, inner), MXU_DTYPE)],
        compiler_params=_params(("parallel", "arbitrary"), vmem),
        name="mem_attn",
    )(x, g.reshape(1, d), wq, mem_k, mem_v, wo)


def _mla_prep_kernel(*refs, q_lora, kv_lora, rope_dim, rope_slabs, qscale, heads, nope):
    expand = len(refs) == 12
    if expand:
        (p_ref, qn_ref, kvn_ref, wuq_ref, cos_ref, sin_ref, wukv_ref,
         q_ref, ckv_ref, kpe_ref, k_ref, v_ref) = refs
    else:
        p_ref, qn_ref, kvn_ref, wuq_ref, cos_ref, sin_ref, q_ref, ckv_ref, kpe_ref, k_ref = refs
    cos = cos_ref[...]
    sin = sin_ref[...]
    cqn = _rms(p_ref[:, 0:q_lora], qn_ref[...]).astype(MXU_DTYPE)
    q = _dot(cqn, wuq_ref[...])
    if qscale != 1.0:
        q = q * qscale
    q_ref[...] = q.astype(q_ref.dtype)
    for lo in rope_slabs:
        xs = q[:, lo:lo + V7X_LANES]
        q_ref[:, lo:lo + V7X_LANES] = (xs * cos + _swap32(xs) * sin).astype(q_ref.dtype)
    ckv = _rms(p_ref[:, q_lora:q_lora + kv_lora], kvn_ref[...])
    ckv_ref[...] = ckv
    kp = p_ref[:, q_lora + kv_lora:q_lora + kv_lora + V7X_LANES]
    kpr = kp * cos + _swap32(kp) * sin
    kpe_ref[...] = kpr[:, 0:rope_dim]
    kpr_m = kpr.astype(k_ref.dtype)
    if expand:
        kv = _dot(ckv.astype(MXU_DTYPE), wukv_ref[...])
        hw = nope + V7X_LANES
        for h in range(heads):
            k_ref[:, h * hw:h * hw + nope] = kv[:, h * nope:(h + 1) * nope].astype(k_ref.dtype)
            k_ref[:, h * hw + nope:(h + 1) * hw] = kpr_m
        v_ref[...] = kv[:, heads * nope:].astype(v_ref.dtype)
    else:
        k_ref[:, 0:kv_lora] = ckv.astype(k_ref.dtype)
        k_ref[:, kv_lora:kv_lora + V7X_LANES] = kpr_m


def mla_prep(proj, q_norm, kv_norm, w_uq, cos_rows, sin_rows, w_ukv, *, q_lora, kv_lora, rope_dim,
             rope_slabs, qscale, heads, nope):
    n = proj.shape[0]
    qcols = w_uq.shape[1]
    tm = _pick(n, (512, 256, 128))
    expand = w_ukv is not None
    row = lambda w: pl.BlockSpec((tm, w), lambda i: (i, 0))
    full = lambda a: pl.BlockSpec(a.shape, lambda i: (0, 0))
    in_specs = [row(proj.shape[1]), pl.BlockSpec((1, q_lora), lambda i: (0, 0)),
                pl.BlockSpec((1, kv_lora), lambda i: (0, 0)), full(w_uq), row(V7X_LANES), row(V7X_LANES)]
    args = [proj, q_norm.reshape(1, -1), kv_norm.reshape(1, -1), w_uq, cos_rows, sin_rows]
    out_shape = [jax.ShapeDtypeStruct((n, qcols), MXU_DTYPE), jax.ShapeDtypeStruct((n, kv_lora), F32),
                 jax.ShapeDtypeStruct((n, rope_dim), F32)]
    out_specs = [row(qcols), row(kv_lora), row(rope_dim)]
    vmem = (2 * (_nbytes((tm, proj.shape[1]), F32) + _nbytes(w_uq.shape, w_uq.dtype)
                 + _nbytes((tm, qcols), MXU_DTYPE) + 3 * _nbytes((tm, kv_lora + V7X_LANES), F32))
            + 3 * _nbytes((tm, qcols), F32) + (8 << 20))
    if expand:
        kcols = heads * (nope + V7X_LANES)
        vcols = w_ukv.shape[1] - heads * nope
        in_specs.append(full(w_ukv))
        args.append(w_ukv)
        out_shape += [jax.ShapeDtypeStruct((n, kcols), MXU_DTYPE), jax.ShapeDtypeStruct((n, vcols), MXU_DTYPE)]
        out_specs += [row(kcols), row(vcols)]
        vmem += (2 * (_nbytes(w_ukv.shape, w_ukv.dtype) + _nbytes((tm, kcols + vcols), MXU_DTYPE))
                 + 2 * _nbytes((tm, w_ukv.shape[1]), F32))
    else:
        out_shape.append(jax.ShapeDtypeStruct((n, kv_lora + V7X_LANES), MXU_DTYPE))
        out_specs.append(row(kv_lora + V7X_LANES))
    return pl.pallas_call(
        functools.partial(_mla_prep_kernel, q_lora=q_lora, kv_lora=kv_lora, rope_dim=rope_dim,
                          rope_slabs=rope_slabs, qscale=qscale, heads=heads, nope=nope),
        out_shape=tuple(out_shape),
        grid=(n // tm,),
        in_specs=in_specs,
        out_specs=tuple(out_specs),
        compiler_params=_params(("parallel",), vmem),
        name="mla_prep",
    )(*args)


def _lanes(x, width):
    reps = width // V7X_LANES
    return x if reps == 1 else jnp.concatenate([x] * reps, axis=-1)


def _mha_attn_kernel(qi_ref, kj_ref, flag_ref, q_ref, k_ref, v_ref, o_ref, acc_ref, m_ref,
                     *, heads, hw, vd, tq, tk, q_offset, kv_len):
    step = pl.program_id(1)
    i = qi_ref[step]
    j = kj_ref[step]
    flags = flag_ref[step]
    aw = vd + V7X_LANES

    @pl.when(j == 0)
    def _():
        m_ref[...] = jnp.full_like(m_ref, NEG)
        acc_ref[...] = jnp.zeros_like(acc_ref)

    ones = jnp.ones((tk, V7X_LANES), MXU_DTYPE)

    def all_heads(bias):
        for h in range(heads):
            s = _dot_nt(q_ref[0, :, h * hw:(h + 1) * hw], k_ref[0, :, h * hw:(h + 1) * hw])
            if bias is not None:
                s = s + bias
            m_old = m_ref[h]
            m_new = jnp.maximum(m_old, jnp.max(s, axis=-1, keepdims=True))
            alpha = jnp.exp2(m_old - m_new)
            p = jnp.exp2(s - _lanes(m_new, tk)).astype(MXU_DTYPE)
            m_ref[h] = m_new
            v_ext = jnp.concatenate([v_ref[0, :, h * vd:(h + 1) * vd], ones], axis=-1)
            acs = slice(h * aw, (h + 1) * aw)
            acc_ref[:, acs] = _lanes(alpha, aw) * acc_ref[:, acs] + _dot(p, v_ext)

    @pl.when((flags & 2) == 0)
    def _():
        all_heads(None)

    @pl.when((flags & 2) != 0)
    def _():
        qpos = q_offset + i * tq + lax.broadcasted_iota(jnp.int32, (tq, tk), 0)
        kpos = j * tk + lax.broadcasted_iota(jnp.int32, (tq, tk), 1)
        visible = ((kpos // CHUNK) <= (qpos // CHUNK)) & (kpos < kv_len)
        all_heads(jnp.where(visible, 0.0, NEG))

    @pl.when((flags & 1) != 0)
    def _():
        for h in range(heads):
            num = acc_ref[:, h * aw:h * aw + vd]
            den = _lanes(acc_ref[:, h * aw + vd:(h + 1) * aw], vd)
            o_ref[0, :, h * vd:(h + 1) * vd] = (num / den).astype(o_ref.dtype)


def mha_attn(q, k, v, *, heads, tq, tk, q_offset, kv_len):
    b, t, _ = q.shape
    s = k.shape[1]
    hw = q.shape[2] // heads
    vd = v.shape[2] // heads
    nq, nk = t // tq, s // tk
    qi, kj, flags = [], [], []
    for i in range(nq):
        first_q = q_offset + i * tq
        last_key = ((first_q + tq - 1) // CHUNK + 1) * CHUNK - 1
        jl = min(last_key // tk, nk - 1)
        for j in range(jl + 1):
            fully_visible = ((j + 1) * tk - 1) // CHUNK <= first_q // CHUNK and (j + 1) * tk <= kv_len
            qi.append(i)
            kj.append(j)
            flags.append(int(j == jl) + 2 * int(not fully_visible))
    sched = [jnp.asarray(a, jnp.int32) for a in (qi, kj, flags)]
    aw = vd + V7X_LANES
    vmem = (2 * (2 * _nbytes((tq, heads * hw), q.dtype) + 2 * _nbytes((tk, heads * vd), v.dtype)
                 + _nbytes((tq, heads * vd), MXU_DTYPE))
            + _nbytes((tq, heads * aw), F32) + _nbytes((heads, tq, V7X_LANES), F32)
            + 6 * _nbytes((tq, tk), F32) + (8 << 20))
    grid_spec = pltpu.PrefetchScalarGridSpec(
        num_scalar_prefetch=3,
        grid=(b, len(qi)),
        in_specs=[pl.BlockSpec((1, tq, heads * hw), lambda bb, p, qi_r, kj_r, l_r: (bb, qi_r[p], 0)),
                  pl.BlockSpec((1, tk, heads * hw), lambda bb, p, qi_r, kj_r, l_r: (bb, kj_r[p], 0)),
                  pl.BlockSpec((1, tk, heads * vd), lambda bb, p, qi_r, kj_r, l_r: (bb, kj_r[p], 0))],
        out_specs=pl.BlockSpec((1, tq, heads * vd), lambda bb, p, qi_r, kj_r, l_r: (bb, qi_r[p], 0)),
        scratch_shapes=[pltpu.VMEM((tq, heads * aw), F32),
                        pltpu.VMEM((heads, tq, V7X_LANES), F32)])
    return pl.pallas_call(
        functools.partial(_mha_attn_kernel, heads=heads, hw=hw, vd=vd, tq=tq, tk=tk,
                          q_offset=q_offset, kv_len=kv_len),
        out_shape=jax.ShapeDtypeStruct((b, t, heads * vd), MXU_DTYPE),
        grid_spec=grid_spec,
        compiler_params=_params(("parallel", "arbitrary"), vmem),
        name="mha_attn",
    )(*sched, q, k, v)


def _mla_decode_kernel(q_ref, pc_ref, pr_ref, kn_ref, wuk_ref, wuv_ref, o_ref, qs_ref,
                       *, heads, t, nope, lat, rope_dim, q_offset, scale):
    past = pc_ref.shape[1]
    rows = heads * t
    rope_lo = heads * nope
    for h in range(heads):
        qn = q_ref[0, :, h * nope:(h + 1) * nope]
        qs_ref[h * t:(h + 1) * t, 0:lat] = (_dot(qn, wuk_ref[h]) * scale).astype(qs_ref.dtype)
        qr = q_ref[0, :, rope_lo + h * V7X_LANES:rope_lo + (h + 1) * V7X_LANES]
        qs_ref[h * t:(h + 1) * t, lat:lat + V7X_LANES] = (qr.astype(F32) * scale).astype(qs_ref.dtype)

    def bias(k0, n):
        qpos = q_offset + lax.broadcasted_iota(jnp.int32, (t, n), 0)
        kpos = k0 + lax.broadcasted_iota(jnp.int32, (t, n), 1)
        return jnp.where((kpos // CHUNK) <= (qpos // CHUNK), 0.0, NEG)

    def masked(s, b):
        return (s.reshape(heads, t, s.shape[1]) + b[None]).reshape(rows, s.shape[1])

    kc = pc_ref[0].astype(MXU_DTYPE)
    kr = pr_ref[0].astype(MXU_DTYPE)
    kn = kn_ref[0]
    s_past = masked(_dot_nt(qs_ref[:, 0:lat], kc) + _dot_nt(qs_ref[:, lat:lat + rope_dim], kr), bias(0, past))
    s_new = masked(_dot_nt(qs_ref[...], kn), bias(q_offset, t))
    m = jnp.maximum(jnp.max(s_past, axis=-1, keepdims=True), jnp.max(s_new, axis=-1, keepdims=True))
    p_past = jnp.exp(s_past - m)
    p_new = jnp.exp(s_new - m)
    l = jnp.sum(p_past, axis=-1, keepdims=True) + jnp.sum(p_new, axis=-1, keepdims=True)
    ol = (_dot(p_past.astype(MXU_DTYPE), kc) + _dot(p_new.astype(MXU_DTYPE), kn[:, 0:lat])) / l
    vd = wuv_ref.shape[2]
    for h in range(heads):
        o_ref[0, :, h * vd:(h + 1) * vd] = (
            _dot(ol[h * t:(h + 1) * t, :].astype(MXU_DTYPE), wuv_ref[h]).astype(o_ref.dtype))


def mla_decode(q, past_ckv, past_kpe, kcat, w_uk, w_uv, *, q_offset, scale):
    b, t, _ = q.shape
    past, rope_dim = past_kpe.shape[1], past_kpe.shape[2]
    heads, nope, lat = w_uk.shape
    vd = w_uv.shape[2]
    rows = heads * t
    vmem = (2 * (_nbytes((t, q.shape[2]), q.dtype) + _nbytes((past, lat + V7X_LANES), F32)
                 + 2 * _nbytes(w_uk.shape, w_uk.dtype) + _nbytes((t, heads * vd), MXU_DTYPE))
            + _nbytes((rows, lat + V7X_LANES), MXU_DTYPE) + _nbytes((past, lat + V7X_LANES), MXU_DTYPE)
            + 4 * _nbytes((rows, past), F32) + (8 << 20))
    assert vmem <= V7X_VMEM_BYTES, "cached rows must fit one VMEM block"
    return pl.pallas_call(
        functools.partial(_mla_decode_kernel, heads=heads, t=t, nope=nope, lat=lat, rope_dim=rope_dim,
                          q_offset=q_offset, scale=scale),
        out_shape=jax.ShapeDtypeStruct((b, t, heads * vd), MXU_DTYPE),
        grid=(b,),
        in_specs=[pl.BlockSpec((1, t, q.shape[2]), lambda bb: (bb, 0, 0)),
                  pl.BlockSpec((1, past, lat), lambda bb: (bb, 0, 0)),
                  pl.BlockSpec((1, past, rope_dim), lambda bb: (bb, 0, 0)),
                  pl.BlockSpec((1, t, kcat.shape[2]), lambda bb: (bb, 0, 0)),
                  pl.BlockSpec(w_uk.shape, lambda bb: (0, 0, 0)),
                  pl.BlockSpec(w_uv.shape, lambda bb: (0, 0, 0))],
        out_specs=pl.BlockSpec((1, t, heads * vd), lambda bb: (bb, 0, 0)),
        scratch_shapes=[pltpu.VMEM((rows, lat + V7X_LANES), MXU_DTYPE)],
        compiler_params=_params(("parallel",), vmem),
        name="mla_decode",
    )(q, past_ckv, past_kpe, kcat, w_uk, w_uv)


class _ScanPlan(NamedTuple):
    body: Callable
    args: list
    in_specs: list
    out_shape: list
    out_specs: list
    scratch: list
    vmem: int


def _ssd_kernel(*refs, L, groups, hpg, hdim, nstate, has_state):
    if has_state:
        (z_ref, xs_ref, b_ref, c_ref, dt_ref, cw_ref, cb_ref, dtb_ref, alog_ref, dsk_ref, nrm_ref,
         cst_ref, hst_ref, y_ref, ncv_ref, nst_ref, xbuf, h_scr, cum_scr, xdt_scr, yin_scr) = refs
    else:
        (z_ref, xs_ref, b_ref, c_ref, dt_ref, cw_ref, cb_ref, dtb_ref, alog_ref, dsk_ref, nrm_ref,
         cst_ref, y_ref, ncv_ref, nst_ref, xbuf, h_scr, cum_scr, xdt_scr, yin_scr) = refs
        hst_ref = None
    c = pl.program_id(1)
    nc = pl.num_programs(1)
    dx = groups * hpg * hdim
    dn = groups * nstate
    gw = hpg * hdim
    pad = CONV_PAD_ROWS

    @pl.when(c == 0)
    def _():
        xbuf[0:pad, :] = cst_ref[0]
        xbuf[pad:2 * pad, :] = jnp.zeros((pad, xbuf.shape[1]), F32)
        if has_state:
            h_scr[...] = hst_ref[0]
        else:
            h_scr[...] = jnp.zeros_like(h_scr)

    row = lax.broadcasted_iota(jnp.int32, (L, L), 0)
    col = lax.broadcasted_iota(jnp.int32, (L, L), 1)
    causal = row >= col

    xin = jnp.concatenate([xs_ref[...], b_ref[...], c_ref[...]], axis=1)
    xin_f = xin.astype(F32)
    acc = xin_f * cw_ref[CONV_K - 1:CONV_K, :] + cb_ref[...]
    for jj in range(CONV_K - 1):
        shift = jnp.where(row - col == CONV_K - 1 - jj, 1.0, 0.0).astype(MXU_DTYPE)
        acc = acc + _dot(shift, xin) * cw_ref[jj:jj + 1, :]
    corr = xbuf[pad - CONV_K + 1:2 * pad - CONV_K + 1, :] * cw_ref[0:1, :]
    for jj in range(1, CONV_K - 1):
        lo = pad - CONV_K + 1 + jj
        corr = corr + xbuf[lo:lo + pad, :] * cw_ref[jj:jj + 1, :]
    acc = jnp.concatenate([acc[0:pad] + corr, acc[pad:]], axis=0)
    xc = _silu(acc)
    tail = xin_f[L - pad:L, :]
    xbuf[0:pad, :] = tail

    @pl.when(c == nc - 1)
    def _():
        ncv_ref[0] = tail

    dtr = dt_ref[...] + dtb_ref[...]
    dt = jnp.maximum(dtr, 0.0) + jnp.log1p(jnp.exp(-jnp.abs(dtr)))
    la = dt * (-jnp.exp(alog_ref[...]) * LOG2E)
    tri = jnp.where(causal, 1.0, 0.0).astype(MXU_DTYPE)
    cum = sum(_dot(tri, piece) for piece in _split3(la))
    nh = groups * hpg
    eye = jnp.where(lax.broadcasted_iota(jnp.int32, (V7X_LANES, V7X_LANES), 0)
                    == lax.broadcasted_iota(jnp.int32, (V7X_LANES, V7X_LANES), 1), 1.0, 0.0).astype(MXU_DTYPE)
    cum_t = sum(_dot_nt(eye, piece) for piece in _split3(cum))

    half = lax.broadcasted_iota(jnp.int32, (L, V7X_LANES), 1) < hdim
    per_vreg = V7X_LANES // hdim
    for g in range(groups):
        bg = xc[:, dx + g * nstate:dx + (g + 1) * nstate]
        cg = xc[:, dx + dn + g * nstate:dx + dn + (g + 1) * nstate].astype(MXU_DTYPE)
        qk = _dot_nt(cg, bg.astype(MXU_DTYPE))
        for sl in range(gw // V7X_LANES):
            lane0 = g * gw + sl * V7X_LANES
            h0 = lane0 // hdim
            cb = [jnp.broadcast_to(cum[:, h0 + u:h0 + u + 1], (L, V7X_LANES)) for u in range(per_vreg)]
            db = [jnp.broadcast_to(dt[:, h0 + u:h0 + u + 1], (L, V7X_LANES)) for u in range(per_vreg)]
            cum_e = jnp.where(half, cb[0], cb[1])
            dt_e = jnp.where(half, db[0], db[1])
            xdt = xc[:, lane0:lane0 + V7X_LANES] * dt_e
            xdt_m = xdt.astype(MXU_DTYPE)
            ys = []
            for u in range(per_vreg):
                seg = cb[u][:, 0:L] - cum_t[h0 + u:h0 + u + 1, :]
                decay = jnp.exp2(jnp.where(causal, seg, NEG))
                ys.append(_dot((qk * decay).astype(MXU_DTYPE), xdt_m))
            cum_scr[:, lane0:lane0 + V7X_LANES] = cum_e
            xdt_scr[:, lane0:lane0 + V7X_LANES] = xdt
            yin_scr[:, lane0:lane0 + V7X_LANES] = jnp.where(half, ys[0], ys[1])

    for g in range(groups):
        gs = slice(g * gw, (g + 1) * gw)
        bg = xc[:, dx + g * nstate:dx + (g + 1) * nstate].astype(MXU_DTYPE)
        cg = xc[:, dx + dn + g * nstate:dx + dn + (g + 1) * nstate].astype(MXU_DTYPE)
        cum_g = cum_scr[:, gs]
        last = cum_scr[L - 1:L, gs]
        hg = h_scr[g]
        y_inter = _dot(cg, hg.astype(MXU_DTYPE)) * jnp.exp2(cum_g)
        wx = (jnp.exp2(last - cum_g) * xdt_scr[:, gs]).astype(MXU_DTYPE)
        h_scr[g] = jnp.exp2(last) * hg + _dot_tn(bg, wx)
        y = yin_scr[:, gs] + y_inter + dsk_ref[:, gs] * xc[:, gs]
        y = y * _silu(z_ref[:, gs].astype(F32))
        y_ref[:, gs] = (_rms(y) * nrm_ref[:, gs]).astype(y_ref.dtype)

    @pl.when(c == nc - 1)
    def _():
        nst_ref[0] = h_scr[...]


def ssd_scan(proj, dt_raw, col, conv_w, conv_b, dt_bias, a_log, d_skip_e, ssd_norm, conv_state, ssd_state,
             *, batch, L, groups, hpg, hdim, nstate):
    n = proj.shape[0]
    t = n // batch
    nc = t // L
    dx = groups * hpg * hdim
    dn = groups * nstate
    cdim = dx + 2 * dn
    has_state = ssd_state is not None
    rowmap = lambda b, c: b * nc + c
    in_specs = [pl.BlockSpec((L, dx), lambda b, c: (rowmap(b, c), col["z"])),
                pl.BlockSpec((L, dx), lambda b, c: (rowmap(b, c), col["xs"])),
                pl.BlockSpec((L, dn), lambda b, c: (rowmap(b, c), col["B"])),
                pl.BlockSpec((L, dn), lambda b, c: (rowmap(b, c), col["C"])),
                pl.BlockSpec((L, V7X_LANES), lambda b, c: (rowmap(b, c), 0)),
                pl.BlockSpec((CONV_K, cdim), lambda b, c: (0, 0)),
                pl.BlockSpec((1, cdim), lambda b, c: (0, 0)),
                pl.BlockSpec((1, V7X_LANES), lambda b, c: (0, 0)),
                pl.BlockSpec((1, V7X_LANES), lambda b, c: (0, 0)),
                pl.BlockSpec((1, dx), lambda b, c: (0, 0)),
                pl.BlockSpec((1, dx), lambda b, c: (0, 0)),
                pl.BlockSpec((1, CONV_PAD_ROWS, cdim), lambda b, c: (b, 0, 0))]
    args = [proj, proj, proj, proj, dt_raw, conv_w, conv_b.reshape(1, cdim), dt_bias, a_log,
            d_skip_e, ssd_norm.reshape(1, dx), conv_state]
    if has_state:
        in_specs.append(pl.BlockSpec((1, groups, nstate, hpg * hdim), lambda b, c: (b, 0, 0, 0)))
        args.append(ssd_state)
    state_bytes = _nbytes((groups, nstate, hpg * hdim), F32)
    vmem = (2 * (2 * _nbytes((L, dx), F32) + 2 * _nbytes((L, dn), F32) + _nbytes((L, dx), MXU_DTYPE)
                 + 2 * _nbytes((CONV_PAD_ROWS, cdim), F32) + 2 * state_bytes)
            + state_bytes + 12 * _nbytes((L + CONV_PAD_ROWS, cdim), F32) + (8 << 20))
    return _ScanPlan(
        body=functools.partial(_ssd_kernel, L=L, groups=groups, hpg=hpg, hdim=hdim, nstate=nstate,
                               has_state=has_state),
        args=args, in_specs=in_specs,
        out_shape=[jax.ShapeDtypeStruct((n, dx), MXU_DTYPE),
                   jax.ShapeDtypeStruct((batch, CONV_PAD_ROWS, cdim), F32),
                   jax.ShapeDtypeStruct((batch, groups, nstate, hpg * hdim), F32)],
        out_specs=[pl.BlockSpec((L, dx), lambda b, c: (rowmap(b, c), 0)),
                   pl.BlockSpec((1, CONV_PAD_ROWS, cdim), lambda b, c: (b, 0, 0)),
                   pl.BlockSpec((1, groups, nstate, hpg * hdim), lambda b, c: (b, 0, 0, 0))],
        scratch=[pltpu.VMEM((2 * CONV_PAD_ROWS, cdim), F32),
                 pltpu.VMEM((groups, nstate, hpg * hdim), F32),
                 pltpu.VMEM((L, dx), F32),
                 pltpu.VMEM((L, dx), F32),
                 pltpu.VMEM((L, dx), F32)],
        vmem=vmem)


def _ret_kernel(*refs, L, heads, hd, has_state):
    if has_state:
        (q_ref, k_ref, v_ref, gate_ref, cos_ref, sin_ref, dec_ref, ecum_ref, wv_ref, sdec_ref, st_ref,
         o_ref, nst_ref, s_scr) = refs
    else:
        (q_ref, k_ref, v_ref, gate_ref, cos_ref, sin_ref, dec_ref, ecum_ref, wv_ref, sdec_ref,
         o_ref, nst_ref, s_scr) = refs
        st_ref = None
    c = pl.program_id(1)
    nc = pl.num_programs(1)
    half = hd // 2

    @pl.when(c == 0)
    def _():
        if has_state:
            s_scr[...] = st_ref[0]
        else:
            s_scr[...] = jnp.zeros_like(s_scr)

    cos = cos_ref[...]
    sin = sin_ref[...]

    def rope(ref, h, mult):
        x1 = ref[:, h * hd:h * hd + half].astype(F32)
        x2 = ref[:, h * hd + half:(h + 1) * hd].astype(F32)
        out = jnp.concatenate([x1 * cos - x2 * sin, x2 * cos + x1 * sin], axis=-1)
        return (out * mult).astype(MXU_DTYPE) if mult != 1.0 else out.astype(MXU_DTYPE)

    for h in range(heads):
        hs = slice(h * hd, (h + 1) * hd)
        qr = rope(q_ref, h, 1.0)
        kr = rope(k_ref, h, hd ** -0.5)
        vf = v_ref[:, hs]
        qk = _dot_nt(qr, kr)
        y = _dot((qk * dec_ref[h]).astype(MXU_DTYPE), vf.astype(MXU_DTYPE))
        s_old = s_scr[h]
        y = y + _dot(qr, s_old.astype(MXU_DTYPE)) * ecum_ref[h]
        s_scr[h] = sdec_ref[h] * s_old + _dot_tn(kr, (wv_ref[h] * vf.astype(F32)).astype(MXU_DTYPE))
        o_ref[:, hs] = (_rms(y) * _silu(gate_ref[:, hs].astype(F32))).astype(o_ref.dtype)

    @pl.when(c == nc - 1)
    def _():
        nst_ref[0] = s_scr[...]


def ret_scan(proj, col, cos_rows, sin_rows, ret_state, *, batch, L, heads, hd):
    n = proj.shape[0]
    t = n // batch
    nc = t // L
    inner = heads * hd
    has_state = ret_state is not None
    lg = jnp.log1p(-jnp.exp2(-5.0 - jnp.arange(heads, dtype=F32)))[:, None, None]
    li = jnp.arange(L, dtype=F32)
    diff = li[:, None] - li[None, :]
    dec = jnp.where(diff >= 0, jnp.exp(jnp.where(diff >= 0, diff, 0.0)[None] * lg), 0.0)
    ecum = jnp.broadcast_to(jnp.exp((li[None, :, None] + 1.0) * lg), (heads, L, hd))
    wv = jnp.broadcast_to(jnp.exp((L - 1.0 - li[None, :, None]) * lg), (heads, L, hd))
    sdec = jnp.broadcast_to(jnp.exp(L * lg), (heads, 1, hd))
    rowmap = lambda b, c: b * nc + c
    in_specs = [pl.BlockSpec((L, inner), lambda b, c: (rowmap(b, c), col["q"])),
                pl.BlockSpec((L, inner), lambda b, c: (rowmap(b, c), col["k"])),
                pl.BlockSpec((L, inner), lambda b, c: (rowmap(b, c), col["v"])),
                pl.BlockSpec((L, inner), lambda b, c: (rowmap(b, c), col["gate"])),
                pl.BlockSpec((L, hd // 2), lambda b, c: (rowmap(b, c), 0)),
                pl.BlockSpec((L, hd // 2), lambda b, c: (rowmap(b, c), 0)),
                pl.BlockSpec((heads, L, L), lambda b, c: (0, 0, 0)),
                pl.BlockSpec((heads, L, hd), lambda b, c: (0, 0, 0)),
                pl.BlockSpec((heads, L, hd), lambda b, c: (0, 0, 0)),
                pl.BlockSpec((heads, 1, hd), lambda b, c: (0, 0, 0))]
    args = [proj, proj, proj, proj, cos_rows, sin_rows, dec, ecum, wv, sdec]
    if has_state:
        in_specs.append(pl.BlockSpec((1, heads, hd, hd), lambda b, c: (b, 0, 0, 0)))
        args.append(ret_state)
    state_bytes = _nbytes((heads, hd, hd), F32)
    vmem = (2 * (4 * _nbytes((L, inner), F32) + _nbytes((L, inner), MXU_DTYPE) + _nbytes((heads, L, L), F32)
                 + 2 * _nbytes((heads, L, hd), F32) + 2 * state_bytes)
            + state_bytes + 16 * _nbytes((L, hd), F32) + (8 << 20))
    return _ScanPlan(
        body=functools.partial(_ret_kernel, L=L, heads=heads, hd=hd, has_state=has_state),
        args=args, in_specs=in_specs,
        out_shape=[jax.ShapeDtypeStruct((n, inner), MXU_DTYPE),
                   jax.ShapeDtypeStruct((batch, heads, hd, hd), F32)],
        out_specs=[pl.BlockSpec((L, inner), lambda b, c: (rowmap(b, c), 0)),
                   pl.BlockSpec((1, heads, hd, hd), lambda b, c: (b, 0, 0, 0))],
        scratch=[pltpu.VMEM((heads, hd, hd), F32)],
        vmem=vmem)


def _scan_pair_kernel(*refs, bodies, n_in, n_out, n_scr):
    ins, outs, scrs = refs[:sum(n_in)], refs[sum(n_in):sum(n_in) + sum(n_out)], refs[sum(n_in) + sum(n_out):]
    for k, body in enumerate(bodies):
        take = lambda seq, counts: seq[sum(counts[:k]):sum(counts[:k + 1])]
        body(*take(ins, n_in), *take(outs, n_out), *take(scrs, n_scr))


def run_scans(plans, *, batch, nc, name):
    outs = pl.pallas_call(
        functools.partial(_scan_pair_kernel, bodies=tuple(p.body for p in plans),
                          n_in=tuple(len(p.args) for p in plans),
                          n_out=tuple(len(p.out_shape) for p in plans),
                          n_scr=tuple(len(p.scratch) for p in plans)),
        out_shape=tuple(s for p in plans for s in p.out_shape),
        grid=(batch, nc),
        in_specs=[s for p in plans for s in p.in_specs],
        out_specs=tuple(s for p in plans for s in p.out_specs),
        scratch_shapes=[s for p in plans for s in p.scratch],
        compiler_params=_params(("parallel", "arbitrary"), sum(p.vmem for p in plans)),
        name=name,
    )(*[a for p in plans for a in p.args])
    split, k = [], 0
    for p in plans:
        split.append(outs[k:k + len(p.out_shape)])
        k += len(p.out_shape)
    return split


def _rope_tables(pos, half, reps, batch):
    inv = ROPE_BASE ** (-jnp.arange(half, dtype=F32) / half)
    ang = pos.astype(F32)[:, None] * inv[None, :]
    cos, sin = jnp.cos(ang), jnp.sin(ang)
    if reps == 0:
        return jnp.tile(cos, (batch, 1)), jnp.tile(sin, (batch, 1))
    c = jnp.tile(jnp.concatenate([cos, cos], axis=-1), (batch, reps))
    s = jnp.tile(jnp.concatenate([-sin, sin], axis=-1), (batch, reps))
    return c, s


def _mixer_even(x, batch, pos, norm_g, prm, conv_state, ssd_state, ret_state, dims, L_ssd, L_ret):
    d = x.shape[1]
    groups, hpg, hdim, nstate, rheads, rhd = dims
    proj, dt_raw = rms_matmul(x, norm_g, prm["w_in"], out_dtype=MXU_DTYPE, w_side=prm["w_dt"])
    col = prm["col"]
    ssd_plan = ssd_scan(proj, dt_raw, col, prm["conv_w"], prm["conv_b"], prm["dt_bias"],
                        prm["a_log"], prm["d_skip_e"], prm["ssd_norm"], conv_state, ssd_state,
                        batch=batch, L=L_ssd, groups=groups, hpg=hpg, hdim=hdim, nstate=nstate)
    cos_rows, sin_rows = _rope_tables(pos, rhd // 2, 0, batch)
    ret_plan = ret_scan(proj, col, cos_rows, sin_rows, ret_state, batch=batch, L=L_ret, heads=rheads, hd=rhd)
    t = x.shape[0] // batch
    (y, new_conv, new_ssd), = run_scans([ssd_plan], batch=batch, nc=t // L_ssd, name="ssd_scan")
    (o, new_ret), = run_scans([ret_plan], batch=batch, nc=t // L_ret, name="ret_scan")
    x = matmul_residual(x, [y, o], [prm["w_out_ssd"], prm["w_out_ret"]])
    return x, (new_conv, new_ssd, new_ret)


def _prep_even(w_in, conv_w, conv_b, dt_bias, a_log, d_skip, ssd_norm, w_out, dims):
    groups, hpg, hdim, nstate, rheads, rhd = dims
    d = w_in.shape[0]
    dx = groups * hpg * hdim
    dn = groups * nstate
    nh = groups * hpg
    ri = rheads * rhd
    o_z, o_xs, o_b, o_c, o_dt = 0, dx, 2 * dx, 2 * dx + dn, 2 * dx + 2 * dn
    o_q = o_dt + nh
    seg = lambda lo, w: w_in[:, lo:lo + w]
    dt_pad = jnp.zeros((d, V7X_LANES - nh), w_in.dtype)
    w_main = jnp.concatenate([seg(o_z, dx), seg(o_q, ri), seg(o_q + ri, ri), seg(o_q + 2 * ri, ri),
                              seg(o_q + 3 * ri, ri), seg(o_xs, dx), seg(o_b, dn), seg(o_c, dn)],
                             axis=1).astype(MXU_DTYPE)
    w_dt = jnp.concatenate([seg(o_dt, nh), dt_pad], axis=1).astype(MXU_DTYPE)
    assert dx == ri and dx % dn == 0 and dn % V7X_LANES == 0
    col = {"z": 0, "q": 1, "k": 2, "v": 3, "gate": 4, "xs": 5, "B": 6 * dx // dn, "C": 6 * dx // dn + 1}
    pad1 = lambda v: jnp.pad(v.astype(F32), (0, V7X_LANES - nh)).reshape(1, V7X_LANES)
    return {"w_in": w_main, "w_dt": w_dt, "col": col, "conv_w": conv_w, "conv_b": conv_b,
            "dt_bias": pad1(dt_bias), "a_log": pad1(a_log),
            "d_skip_e": jnp.repeat(d_skip, hdim).reshape(1, dx), "ssd_norm": ssd_norm,
            "w_out_ssd": w_out[:dx].astype(MXU_DTYPE), "w_out_ret": w_out[dx:].astype(MXU_DTYPE)}


def _prep_odd(w_in, q_norm, kv_norm, w_uq, w_uk, w_uv, w_out, rope_dim):
    d = w_in.shape[0]
    q_lora = q_norm.shape[0]
    kv_lora, heads, nope = w_uk.shape
    kp = w_in[:, q_lora + kv_lora:]
    w_in_new = jnp.concatenate([w_in[:, :q_lora + kv_lora], kp, kp], axis=1).astype(MXU_DTYPE)
    assert 2 * rope_dim == V7X_LANES
    wq = w_uq.reshape(q_lora, heads, nope + rope_dim)
    wq_nope = wq[:, :, :nope]
    wq_rope = jnp.pad(wq[:, :, nope:], ((0, 0), (0, 0), (0, V7X_LANES - rope_dim)))
    hw = nope + V7X_LANES
    return {"w_in": w_in_new, "q_norm": q_norm, "kv_norm": kv_norm,
            "w_uq_lat": jnp.concatenate([wq_nope.reshape(q_lora, heads * nope),
                                         wq_rope.reshape(q_lora, heads * V7X_LANES)], axis=1).astype(MXU_DTYPE),
            "slabs_lat": tuple(heads * nope + h * V7X_LANES for h in range(heads)),
            "w_uq_head": jnp.concatenate([wq_nope, wq_rope], axis=2).reshape(q_lora, heads * hw).astype(MXU_DTYPE),
            "slabs_head": tuple(h * hw + nope for h in range(heads)),
            "w_ukv": jnp.concatenate([w_uk.reshape(kv_lora, heads * nope),
                                      w_uv.reshape(kv_lora, -1)], axis=1).astype(MXU_DTYPE),
            "w_uk": jnp.transpose(w_uk, (1, 2, 0)).astype(MXU_DTYPE),
            "w_uv": jnp.transpose(w_uv, (1, 0, 2)).astype(MXU_DTYPE),
            "w_out": w_out.astype(MXU_DTYPE)}


def _mixer_odd(x, batch, pos, norm_g, prm, past, rope_dim, q_offset):
    n, d = x.shape
    t = n // batch
    q_lora = prm["q_norm"].shape[0]
    kv_lora = prm["kv_norm"].shape[0]
    heads, nope, lat = prm["w_uk"].shape
    proj = rms_matmul(x, norm_g, prm["w_in"])
    cos_rows, sin_rows = _rope_tables(pos, rope_dim // 2, V7X_LANES // rope_dim, batch)
    scale = (nope + rope_dim) ** -0.5
    dims = dict(q_lora=q_lora, kv_lora=kv_lora, rope_dim=rope_dim, heads=heads, nope=nope)
    if past is None:
        q, ckv, kpe, k, v = mla_prep(proj, prm["q_norm"], prm["kv_norm"], prm["w_uq_head"], cos_rows, sin_rows,
                                     prm["w_ukv"], rope_slabs=prm["slabs_head"], qscale=scale * math.log2(math.e),
                                     **dims)
        tq = tk = _pick(t, (512, 256, 128, 64))
        shp = lambda a: a.reshape(batch, t, a.shape[1])
        o = mha_attn(shp(q), shp(k), shp(v), heads=heads, tq=tq, tk=tk, q_offset=q_offset, kv_len=t)
    else:
        q, ckv, kpe, kcat = mla_prep(proj, prm["q_norm"], prm["kv_norm"], prm["w_uq_lat"], cos_rows, sin_rows,
                                     None, rope_slabs=prm["slabs_lat"], qscale=1.0, **dims)
        q = q.reshape(batch, t, q.shape[1])
        kcat = kcat.reshape(batch, t, kcat.shape[1])
        o = mla_decode(q, past[0], past[1], kcat, prm["w_uk"], prm["w_uv"], q_offset=q_offset, scale=scale)
    x = matmul_residual(x, [o.reshape(n, o.shape[2])], [prm["w_out"]])
    return x, (ckv.reshape(batch, t, kv_lora), kpe.reshape(batch, t, rope_dim))


def kernel(x_prompt, x_sample, mem_prompt, state_conv, state_ssd, state_ret, cache_ckv, cache_kpe,
           cache_mem_k, cache_mem_v, norms, ffn_w1, ffn_w2, mem_norm, w_mq, w_mkv, w_mo,
           ab_w_in, ab_conv_w, ab_conv_b, ab_dt_bias, ab_a_log, ab_d_skip, ab_ssd_norm, ab_w_out,
           c_w_in, c_q_norm, c_kv_norm, c_w_uq, c_w_uk, c_w_uv, c_w_out, final_norm):
    bp, tp, d = x_prompt.shape
    bs, ts, _ = x_sample.shape
    depth = norms.shape[0]
    assert depth >= 1
    past_len = cache_ckv.shape[2]
    mem_tokens = mem_prompt.shape[1]
    mem_heads, mem_hd = cache_mem_k.shape[3], cache_mem_k.shape[4]
    mem_inner = mem_heads * mem_hd
    ssd_heads, nstate, hdim = state_ssd.shape[2], state_ssd.shape[3], state_ssd.shape[4]
    cdim = state_conv.shape[3]
    groups = (cdim - ssd_heads * hdim) // (2 * nstate)
    hpg = ssd_heads // groups
    rheads, rhd = state_ret.shape[2], state_ret.shape[3]
    dims = (groups, hpg, hdim, nstate, rheads, rhd)
    rope_dim = cache_kpe.shape[3]

    pos_p = jnp.arange(tp)
    pos_s = past_len + jnp.arange(ts)
    xp = x_prompt.reshape(bp * tp, d)
    xs = x_sample.reshape(bs * ts, d)
    L_ssd_p, L_ret_p = _pick(tp, (128, 64)), _pick(tp, (256, 128, 64))
    L_s = _pick(ts, (128, 64))

    outs = {k: [] for k in ("conv_p", "ssd_p", "ret_p", "ckv_p", "kpe_p", "memk_p", "memv_p",
                            "conv_s", "ssd_s", "ret_s", "ckv_s", "kpe_s")}

    def to_group_layout(st):
        b = st.shape[0]
        return st.reshape(b, groups, hpg, nstate, hdim).transpose(0, 1, 3, 2, 4).reshape(b, groups, nstate, hpg * hdim)

    def from_group_layout(st):
        b = st.shape[0]
        return st.reshape(b, groups, nstate, hpg, hdim).transpose(0, 1, 3, 2, 4).reshape(b, ssd_heads, nstate, hdim)

    for i in range(depth):
        j = i // 2
        closing = final_norm if i == depth - 1 else None
        wq_m = w_mq[i].astype(MXU_DTYPE)
        wo_m = w_mo[i].astype(MXU_DTYPE)
        mkv = rms_matmul(mem_prompt.reshape(bp * mem_tokens, d), mem_norm[i], w_mkv[i].astype(MXU_DTYPE))
        mk_p = mkv[:, :mem_inner].reshape(bp, mem_tokens, mem_inner)
        mv_p = mkv[:, mem_inner:].reshape(bp, mem_tokens, mem_inner)
        outs["memk_p"].append(mk_p.reshape(bp, mem_tokens, mem_heads, mem_hd))
        outs["memv_p"].append(mv_p.reshape(bp, mem_tokens, mem_heads, mem_hd))

        xs, w1a, w1b, w2h = ffn_cast(xs, norms[i, 0], ffn_w1, ffn_w2, i, 0)
        xp = ffn(xp, norms[i, 0], w1a, w1b, w2h)
        if i % 2 == 0:
            prm = _prep_even(ab_w_in[j], ab_conv_w[j], ab_conv_b[j], ab_dt_bias[j], ab_a_log[j],
                             ab_d_skip[j], ab_ssd_norm[j], ab_w_out[j], dims)
            zero_conv = jnp.zeros((bp, CONV_PAD_ROWS, cdim), F32)
            xp, st_p = _mixer_even(xp, bp, pos_p, norms[i, 1], prm, zero_conv, None, None, dims, L_ssd_p, L_ret_p)
            conv_in = jnp.pad(state_conv[j], ((0, 0), (CONV_PAD_ROWS - (CONV_K - 1), 0), (0, 0)))
            xs, st_s = _mixer_even(xs, bs, pos_s, norms[i, 1], prm, conv_in, to_group_layout(state_ssd[j]),
                                   state_ret[j], dims, L_s, L_s)
            for tag, st in (("p", st_p), ("s", st_s)):
                outs["conv_" + tag].append(st[0][:, CONV_PAD_ROWS - (CONV_K - 1):, :])
                outs["ssd_" + tag].append(from_group_layout(st[1]))
                outs["ret_" + tag].append(st[2])
        else:
            prm = _prep_odd(c_w_in[j], c_q_norm[j], c_kv_norm[j], c_w_uq[j], c_w_uk[j], c_w_uv[j], c_w_out[j],
                            rope_dim)
            xp, st_p = _mixer_odd(xp, bp, pos_p, norms[i, 1], prm, None, rope_dim, 0)
            xs, st_s = _mixer_odd(xs, bs, pos_s, norms[i, 1], prm, (cache_ckv[j], cache_kpe[j]), rope_dim, past_len)
            for tag, st in (("p", st_p), ("s", st_s)):
                outs["ckv_" + tag].append(st[0])
                outs["kpe_" + tag].append(st[1])
        xp = mem_attn(xp, norms[i, 2], wq_m, mk_p, mv_p, wo_m, batch=bp, heads=mem_heads)
        xs = mem_attn(xs, norms[i, 2], wq_m, cache_mem_k, cache_mem_v, wo_m, batch=bs, heads=mem_heads, layer=i)
        xs, w1a, w1b, w2h = ffn_cast(xs, norms[i, 3], ffn_w1, ffn_w2, i, 1, final_g=closing)
        xp = ffn(xp, norms[i, 3], w1a, w1b, w2h, final_g=closing)

    y_prompt = xp.reshape(bp, tp, d)
    y_sample = xs.reshape(bs, ts, d)
    st = lambda k: jnp.stack(outs[k])
    return (y_prompt, y_sample, st("conv_p"), st("ssd_p"), st("ret_p"), st("ckv_p"), st("kpe_p"),
            st("memk_p"), st("memv_p"), st("conv_s"), st("ssd_s"), st("ret_s"), st("ckv_s"), st("kpe_s"))
```

```python
import functools
import math
from typing import Callable, NamedTuple

import jax
import jax.numpy as jnp
from jax import lax
from jax.experimental import pallas as pl
from jax.experimental.pallas import tpu as pltpu

F32 = jnp.float32
MXU_DTYPE = jnp.bfloat16

EPS = 1e-6
CHUNK = 64
ROPE_BASE = 10000.0
CONV_K = 4
NEG = -1e30
LOG2E = math.log2(math.e)

V7X_VMEM_BYTES = 64 * 1024 * 1024
TILE_VMEM_BUDGET = 48 * 1024 * 1024
V7X_LANES = 128
V7X_SUBLANES = 8
CONV_PAD_ROWS = 8


def _params(semantics, vmem_bytes):
    limit = min(int(vmem_bytes), V7X_VMEM_BYTES - (4 << 20))
    return pltpu.CompilerParams(dimension_semantics=semantics, vmem_limit_bytes=limit)


def _nbytes(shape, dtype):
    return math.prod(shape) * jnp.dtype(dtype).itemsize


def _pick(n, prefs):
    for p in prefs:
        if n % p == 0:
            return p
    return n


def _dot(a, b):
    return jnp.dot(a, b, preferred_element_type=F32)


def _dot_nt(a, b):
    return lax.dot_general(a, b, (((1,), (1,)), ((), ())), preferred_element_type=F32)


def _dot_tn(a, b):
    return lax.dot_general(a, b, (((0,), (0,)), ((), ())), preferred_element_type=F32)


def _split3(x):
    hi = x.astype(MXU_DTYPE)
    r = x - hi.astype(F32)
    mid = r.astype(MXU_DTYPE)
    lo = (r - mid.astype(F32)).astype(MXU_DTYPE)
    return hi, mid, lo


def _rms(xf, g=None):
    y = xf * lax.rsqrt(jnp.mean(xf * xf, axis=-1, keepdims=True) + EPS)
    return y if g is None else y * g


def _silu(a):
    return a * (1.0 / (1.0 + jnp.exp(-a)))


def _swap32(x):
    w = x.shape[-1]
    lane = lax.broadcasted_iota(jnp.int32, x.shape, x.ndim - 1)
    fwd = pltpu.roll(x, w - 32, x.ndim - 1)
    bwd = pltpu.roll(x, 32, x.ndim - 1)
    return jnp.where((lane & 63) < 32, fwd, bwd)


def _rms_matmul_kernel(*refs, side):
    if side:
        x_ref, g_ref, w_ref, ws_ref, o_ref, os_ref, xn_ref = refs
    else:
        x_ref, g_ref, w_ref, o_ref, xn_ref = refs

    @pl.when(pl.program_id(1) == 0)
    def _():
        xn_ref[...] = _rms(x_ref[...], g_ref[...]).astype(xn_ref.dtype)
        if side:
            os_ref[...] = _dot(xn_ref[...], ws_ref[...])

    o_ref[...] = _dot(xn_ref[...], w_ref[...]).astype(o_ref.dtype)


def rms_matmul(x, g, w, *, out_dtype=F32, w_side=None):
    n, d = x.shape
    nout = w.shape[1]
    side = w_side is not None
    tn = _pick(nout, (1920, 1152, 1024, 512, 256, 128))

    def vmem_for(tm):
        return (2 * (_nbytes((tm, d), F32) + _nbytes((d, tn), w.dtype) + _nbytes((tm, tn), out_dtype))
                + _nbytes((tm, d), MXU_DTYPE) + _nbytes((tm, tn), F32) + (8 << 20))

    tm = next(t for t in (1024, 512, 256, 128, n) if n % t == 0 and (vmem_for(t) <= TILE_VMEM_BUDGET or t <= 128))
    vmem = vmem_for(tm)
    in_specs = [pl.BlockSpec((tm, d), lambda i, j: (i, 0)),
                pl.BlockSpec((1, d), lambda i, j: (0, 0)),
                pl.BlockSpec((d, tn), lambda i, j: (0, j))]
    args = [x, g.reshape(1, d), w]
    out_shape = jax.ShapeDtypeStruct((n, nout), out_dtype)
    out_specs = pl.BlockSpec((tm, tn), lambda i, j: (i, j))
    if side:
        ns = w_side.shape[1]
        in_specs.append(pl.BlockSpec((d, ns), lambda i, j: (0, 0)))
        args.append(w_side)
        out_shape = (out_shape, jax.ShapeDtypeStruct((n, ns), F32))
        out_specs = (out_specs, pl.BlockSpec((tm, ns), lambda i, j: (i, 0)))
        vmem += 2 * (_nbytes((d, ns), w_side.dtype) + _nbytes((tm, ns), F32))
    return pl.pallas_call(
        functools.partial(_rms_matmul_kernel, side=side),
        out_shape=out_shape,
        grid=(n // tm, nout // tn),
        in_specs=in_specs,
        out_specs=out_specs,
        scratch_shapes=[pltpu.VMEM((tm, d), MXU_DTYPE)],
        compiler_params=_params(("parallel", "arbitrary"), vmem),
        name="rms_matmul",
    )(*args)


def _ffn_kernel(*refs, final_norm, nf):
    x_ref, g_ref = refs[0:2]
    blocks = (refs[2:5], refs[5:8])
    gf_ref = refs[8] if final_norm else None
    o_ref, xn_ref = refs[-2:]
    s = pl.program_id(1)

    @pl.when(s == 0)
    def _():
        xf = x_ref[...]
        xn_ref[...] = _rms(xf, g_ref[...]).astype(xn_ref.dtype)
        o_ref[...] = xf

    def down(w1a_ref, w1b_ref, w2_ref):
        xn = xn_ref[...]
        a = _dot(xn, w1a_ref[...])
        b = _dot(xn, w1b_ref[...])
        return _dot((_silu(a) * b).astype(MXU_DTYPE), w2_ref[...])

    if nf % 2 == 0:
        o_ref[...] += down(*blocks[0]) + down(*blocks[1])
    else:
        @pl.when(s == 0)
        def _():
            o_ref[...] += down(*blocks[0])

        @pl.when(s > 0)
        def _():
            o_ref[...] += down(*blocks[0]) + down(*blocks[1])

    if final_norm:
        @pl.when(s == pl.num_programs(1) - 1)
        def _():
            o_ref[...] = _rms(o_ref[...], gf_ref[...])


def _ffn_cast_kernel(*refs, final_norm):
    x_ref, g_ref, w1a_ref, w1b_ref, w2_ref = refs[0:5]
    gf_ref = refs[5] if final_norm else None
    o_ref, w1a_o, w1b_o, w2_o, xn_ref = refs[-5:]
    s = pl.program_id(1)

    @pl.when(s == 0)
    def _():
        xf = x_ref[...]
        xn_ref[...] = _rms(xf, g_ref[...]).astype(xn_ref.dtype)
        o_ref[...] = xf

    w1a_o[...] = w1a_ref[...].astype(w1a_o.dtype)
    w1b_o[...] = w1b_ref[...].astype(w1b_o.dtype)
    w2_o[...] = (0.5 * w2_ref[...]).astype(w2_o.dtype)
    xn = xn_ref[...]
    h = (_silu(_dot(xn, w1a_o[...])) * _dot(xn, w1b_o[...])).astype(MXU_DTYPE)
    o_ref[...] += _dot(h, w2_o[...])

    if final_norm:
        @pl.when(s == pl.num_programs(1) - 1)
        def _():
            o_ref[...] = _rms(o_ref[...], gf_ref[...])


def ffn_cast(x, g, w1, w2, layer, which, final_g=None):
    n, d = x.shape
    dff = w2.shape[2]
    tm = _pick(n, (512, 256, 128))
    tf = _pick(dff, (256, 128))
    nf = dff // tf
    vmem = (2 * (2 * _nbytes((tm, d), F32) + 3 * _nbytes((d, tf), F32) + 3 * _nbytes((d, tf), MXU_DTYPE))
            + _nbytes((tm, d), MXU_DTYPE) + _nbytes((tm, d), F32) + 6 * _nbytes((tm, tf), F32) + (6 << 20))
    in_specs = [pl.BlockSpec((tm, d), lambda i, s: (i, 0)),
                pl.BlockSpec((1, d), lambda i, s: (0, 0)),
                pl.BlockSpec((None, None, d, tf), lambda i, s: (layer, which, 0, s)),
                pl.BlockSpec((None, None, d, tf), lambda i, s: (layer, which, 0, s + nf)),
                pl.BlockSpec((None, None, tf, d), lambda i, s: (layer, which, s, 0))]
    args = [x, g.reshape(1, d), w1, w1, w2]
    if final_g is not None:
        in_specs.append(pl.BlockSpec((1, d), lambda i, s: (0, 0)))
        args.append(final_g.reshape(1, d))
    return pl.pallas_call(
        functools.partial(_ffn_cast_kernel, final_norm=final_g is not None),
        out_shape=(jax.ShapeDtypeStruct((n, d), F32),
                   jax.ShapeDtypeStruct((d, dff), MXU_DTYPE),
                   jax.ShapeDtypeStruct((d, dff), MXU_DTYPE),
                   jax.ShapeDtypeStruct((dff, d), MXU_DTYPE)),
        grid=(n // tm, nf),
        in_specs=in_specs,
        out_specs=(pl.BlockSpec((tm, d), lambda i, s: (i, 0)),
                   pl.BlockSpec((d, tf), lambda i, s: (0, s)),
                   pl.BlockSpec((d, tf), lambda i, s: (0, s)),
                   pl.BlockSpec((tf, d), lambda i, s: (s, 0))),
        scratch_shapes=[pltpu.VMEM((tm, d), MXU_DTYPE)],
        compiler_params=_params(("arbitrary", "arbitrary"), vmem),
        name="ffn_cast",
    )(*args)


def ffn(x, g, w1a, w1b, w2h, final_g=None):
    n, d = x.shape
    dff = w2h.shape[0]
    tm = _pick(n, (512, 256, 128))
    tf = _pick(dff, (512, 256, 128))
    nf = dff // tf
    vmem = (2 * (2 * _nbytes((tm, d), F32) + 6 * _nbytes((d, tf), w1a.dtype))
            + _nbytes((tm, d), MXU_DTYPE) + _nbytes((tm, d), F32) + 6 * _nbytes((tm, tf), F32) + (6 << 20))
    in_specs = [pl.BlockSpec((tm, d), lambda i, s: (i, 0)),
                pl.BlockSpec((1, d), lambda i, s: (0, 0))]
    args = [x, g.reshape(1, d)]
    lead = nf % 2
    for half in range(2):
        blk = lambda s, half=half: jnp.maximum(2 * s - lead + half, half * (lead + 1))
        in_specs += [pl.BlockSpec((d, tf), lambda i, s, blk=blk: (0, blk(s))),
                     pl.BlockSpec((d, tf), lambda i, s, blk=blk: (0, blk(s))),
                     pl.BlockSpec((tf, d), lambda i, s, blk=blk: (blk(s), 0))]
        args += [w1a, w1b, w2h]
    if final_g is not None:
        in_specs.append(pl.BlockSpec((1, d), lambda i, s: (0, 0)))
        args.append(final_g.reshape(1, d))
    return pl.pallas_call(
        functools.partial(_ffn_kernel, final_norm=final_g is not None, nf=nf),
        out_shape=jax.ShapeDtypeStruct((n, d), F32),
        grid=(n // tm, -(-nf // 2)),
        in_specs=in_specs,
        out_specs=pl.BlockSpec((tm, d), lambda i, f: (i, 0)),
        scratch_shapes=[pltpu.VMEM((tm, d), MXU_DTYPE)],
        compiler_params=_params(("parallel", "arbitrary"), vmem),
        name="ffn",
    )(*args)


def _matmul_residual_kernel(*refs, n_in):
    x_ref = refs[0]
    o_ref = refs[1 + 2 * n_in]
    acc = x_ref[...]
    for h_ref, w_ref in zip(refs[1:1 + n_in], refs[1 + n_in:1 + 2 * n_in]):
        acc = acc + _dot(h_ref[...], w_ref[...])
    o_ref[...] = acc


def matmul_residual(x, hs, ws):
    n, d = x.shape

    def vmem_for(tm, tn):
        return (2 * (2 * _nbytes((tm, tn), F32) + sum(_nbytes((tm, h.shape[1]), h.dtype) for h in hs)
                     + sum(_nbytes((w.shape[0], tn), w.dtype) for w in ws))
                + _nbytes((tm, tn), F32) + (8 << 20))

    tm, tn = next((a, b) for a, b in ((512, d), (1024, 1024), (512, 1024), (256, 512), (128, 128), (n, d))
                  if n % a == 0 and d % b == 0 and (vmem_for(a, b) <= TILE_VMEM_BUDGET or a <= 128))
    vmem = vmem_for(tm, tn)
    in_specs = [pl.BlockSpec((tm, tn), lambda i, j: (i, j))]
    for h in hs:
        in_specs.append(pl.BlockSpec((tm, h.shape[1]), lambda i, j: (i, 0)))
    for w in ws:
        in_specs.append(pl.BlockSpec((w.shape[0], tn), lambda i, j: (0, j)))
    return pl.pallas_call(
        functools.partial(_matmul_residual_kernel, n_in=len(hs)),
        out_shape=jax.ShapeDtypeStruct((n, d), F32),
        grid=(n // tm, d // tn),
        in_specs=in_specs,
        out_specs=pl.BlockSpec((tm, tn), lambda i, j: (i, j)),
        compiler_params=_params(("parallel", "arbitrary"), vmem),
        name="matmul_residual",
    )(x, *hs, *ws)


def _mem_attn_kernel(x_ref, g_ref, wq_ref, k_ref, v_ref, wo_ref, o_ref, att_ref, *, bt, tq, heads, hd):
    head_axis = len(k_ref.shape) == 4
    xf = x_ref[...]
    xn = _rms(xf, g_ref[...]).astype(MXU_DTYPE)
    q = _dot(xn, wq_ref[...]).astype(MXU_DTYPE)
    scale = hd ** -0.5
    for b in range(bt):
        for h in range(heads):
            qh = q[b * tq:(b + 1) * tq, h * hd:(h + 1) * hd]
            if head_axis:
                kh = k_ref[b, :, h, :].astype(MXU_DTYPE)
                vh = v_ref[b, :, h, :].astype(MXU_DTYPE)
            else:
                kh = k_ref[b, :, h * hd:(h + 1) * hd].astype(MXU_DTYPE)
                vh = v_ref[b, :, h * hd:(h + 1) * hd].astype(MXU_DTYPE)
            s = _dot_nt(qh, kh) * scale
            m = jnp.max(s, axis=-1, keepdims=True)
            p = jnp.exp(s - m)
            l = jnp.sum(p, axis=-1, keepdims=True)
            oh = _dot(p.astype(MXU_DTYPE), vh) / l
            att_ref[b * tq:(b + 1) * tq, h * hd:(h + 1) * hd] = oh.astype(att_ref.dtype)
    o_ref[...] = xf + _dot(att_ref[...], wo_ref[...])


def mem_attn(x, g, wq, mem_k, mem_v, wo, *, batch, heads, layer=None):
    n, d = x.shape
    t = n // batch
    cached = layer is not None
    m = mem_k.shape[2] if cached else mem_k.shape[1]
    inner = wq.shape[1]
    hd = inner // heads
    if t >= 128:
        bt, tq = 1, _pick(t, (512, 256, 128))
    else:
        bt, tq = _pick(batch, (4, 2, 1)), t
    nt = t // tq
    rows = bt * tq
    if cached:
        kv_spec = pl.BlockSpec((None, bt, m, heads, hd), lambda b, i: (layer, b, 0, 0, 0))
        kv_bytes = _nbytes((bt, m, max(heads, V7X_SUBLANES), hd), mem_k.dtype)
    else:
        kv_spec = pl.BlockSpec((bt, m, inner), lambda b, i: (b, 0, 0))
        kv_bytes = _nbytes((bt, m, inner), mem_k.dtype)
    vmem = (2 * (2 * _nbytes((rows, d), F32) + 2 * kv_bytes + 2 * _nbytes((d, inner), wq.dtype))
            + 2 * _nbytes((rows, d), F32) + (8 << 20))
    return pl.pallas_call(
        functools.partial(_mem_attn_kernel, bt=bt, tq=tq, heads=heads, hd=hd),
        out_shape=jax.ShapeDtypeStruct((n, d), F32),
        grid=(batch // bt, nt),
        in_specs=[pl.BlockSpec((rows, d), lambda b, i: (b * nt + i, 0)),
                  pl.BlockSpec((1, d), lambda b, i: (0, 0)),
                  pl.BlockSpec((d, inner), lambda b, i: (0, 0)),
                  kv_spec, kv_spec,
                  pl.BlockSpec((inner, d), lambda b, i: (0, 0))],
        out_specs=pl.BlockSpec((rows, d), lambda b, i: (b * nt + i, 0)),
        scratch_shapes=[pltpu.VMEM((rows, inner), MXU_DTYPE)],
        compiler_params=_params(("parallel", "arbitrary"), vmem),
        name="mem_attn",
    )(x, g.reshape(1, d), wq, mem_k, mem_v, wo)


def _mla_prep_kernel(*refs, q_lora, kv_lora, rope_dim, rope_slabs, qscale, heads, nope):
    expand = len(refs) == 12
    if expand:
        (p_ref, qn_ref, kvn_ref, wuq_ref, cos_ref, sin_ref, wukv_ref,
         q_ref, ckv_ref, kpe_ref, k_ref, v_ref) = refs
    else:
        p_ref, qn_ref, kvn_ref, wuq_ref, cos_ref, sin_ref, q_ref, ckv_ref, kpe_ref, k_ref = refs
    cos = cos_ref[...]
    sin = sin_ref[...]
    cqn = _rms(p_ref[:, 0:q_lora], qn_ref[...]).astype(MXU_DTYPE)
    q = _dot(cqn, wuq_ref[...])
    if qscale != 1.0:
        q = q * qscale
    q_ref[...] = q.astype(q_ref.dtype)
    for lo in rope_slabs:
        xs = q[:, lo:lo + V7X_LANES]
        q_ref[:, lo:lo + V7X_LANES] = (xs * cos + _swap32(xs) * sin).astype(q_ref.dtype)
    ckv = _rms(p_ref[:, q_lora:q_lora + kv_lora], kvn_ref[...])
    ckv_ref[...] = ckv
    kp = p_ref[:, q_lora + kv_lora:q_lora + kv_lora + V7X_LANES]
    kpr = kp * cos + _swap32(kp) * sin
    kpe_ref[...] = kpr[:, 0:rope_dim]
    kpr_m = kpr.astype(k_ref.dtype)
    if expand:
        kv = _dot(ckv.astype(MXU_DTYPE), wukv_ref[...])
        hw = nope + V7X_LANES
        for h in range(heads):
            k_ref[:, h * hw:h * hw + nope] = kv[:, h * nope:(h + 1) * nope].astype(k_ref.dtype)
            k_ref[:, h * hw + nope:(h + 1) * hw] = kpr_m
        v_ref[...] = kv[:, heads * nope:].astype(v_ref.dtype)
    else:
        k_ref[:, 0:kv_lora] = ckv.astype(k_ref.dtype)
        k_ref[:, kv_lora:kv_lora + V7X_LANES] = kpr_m


def mla_prep(proj, q_norm, kv_norm, w_uq, cos_rows, sin_rows, w_ukv, *, q_lora, kv_lora, rope_dim,
             rope_slabs, qscale, heads, nope):
    n = proj.shape[0]
    qcols = w_uq.shape[1]
    tm = _pick(n, (512, 256, 128))
    expand = w_ukv is not None
    row = lambda w: pl.BlockSpec((tm, w), lambda i: (i, 0))
    full = lambda a: pl.BlockSpec(a.shape, lambda i: (0, 0))
    t = cos_rows.shape[0]
    if t % tm == 0:
        table = pl.BlockSpec((tm, V7X_LANES), lambda i: (i % (t // tm), 0))
    else:
        cos_rows, sin_rows = jnp.tile(cos_rows, (n // t, 1)), jnp.tile(sin_rows, (n // t, 1))
        table = row(V7X_LANES)
    in_specs = [row(proj.shape[1]), pl.BlockSpec((1, q_lora), lambda i: (0, 0)),
                pl.BlockSpec((1, kv_lora), lambda i: (0, 0)), full(w_uq), table, table]
    args = [proj, q_norm.reshape(1, -1), kv_norm.reshape(1, -1), w_uq, cos_rows, sin_rows]
    out_shape = [jax.ShapeDtypeStruct((n, qcols), MXU_DTYPE), jax.ShapeDtypeStruct((n, kv_lora), F32),
                 jax.ShapeDtypeStruct((n, rope_dim), F32)]
    out_specs = [row(qcols), row(kv_lora), row(rope_dim)]
    vmem = (2 * (_nbytes((tm, proj.shape[1]), F32) + _nbytes(w_uq.shape, w_uq.dtype)
                 + _nbytes((tm, qcols), MXU_DTYPE) + 3 * _nbytes((tm, kv_lora + V7X_LANES), F32))
            + 3 * _nbytes((tm, qcols), F32) + (8 << 20))
    if expand:
        kcols = heads * (nope + V7X_LANES)
        vcols = w_ukv.shape[1] - heads * nope
        in_specs.append(full(w_ukv))
        args.append(w_ukv)
        out_shape += [jax.ShapeDtypeStruct((n, kcols), MXU_DTYPE), jax.ShapeDtypeStruct((n, vcols), MXU_DTYPE)]
        out_specs += [row(kcols), row(vcols)]
        vmem += (2 * (_nbytes(w_ukv.shape, w_ukv.dtype) + _nbytes((tm, kcols + vcols), MXU_DTYPE))
                 + 2 * _nbytes((tm, w_ukv.shape[1]), F32))
    else:
        out_shape.append(jax.ShapeDtypeStruct((n, kv_lora + V7X_LANES), MXU_DTYPE))
        out_specs.append(row(kv_lora + V7X_LANES))
    return pl.pallas_call(
        functools.partial(_mla_prep_kernel, q_lora=q_lora, kv_lora=kv_lora, rope_dim=rope_dim,
                          rope_slabs=rope_slabs, qscale=qscale, heads=heads, nope=nope),
        out_shape=tuple(out_shape),
        grid=(n // tm,),
        in_specs=in_specs,
        out_specs=tuple(out_specs),
        compiler_params=_params(("parallel",), vmem),
        name="mla_prep",
    )(*args)


def _lanes(x, width):
    reps = width // V7X_LANES
    return x if reps == 1 else jnp.concatenate([x] * reps, axis=-1)


def _mha_attn_kernel(qi_ref, kj_ref, flag_ref, q_ref, k_ref, v_ref, o_ref, acc_ref, m_ref,
                     *, heads, hw, vd, tq, tk, q_offset, kv_len):
    step = pl.program_id(1)
    i = qi_ref[step]
    j = kj_ref[step]
    flags = flag_ref[step]
    aw = vd + V7X_LANES

    @pl.when(j == 0)
    def _():
        m_ref[...] = jnp.full_like(m_ref, NEG)
        acc_ref[...] = jnp.zeros_like(acc_ref)

    ones = jnp.ones((tk, V7X_LANES), MXU_DTYPE)

    def all_heads(bias):
        for h in range(heads):
            s = _dot_nt(q_ref[0, :, h * hw:(h + 1) * hw], k_ref[0, :, h * hw:(h + 1) * hw])
            if bias is not None:
                s = s + bias
            m_old = m_ref[h]
            m_new = jnp.maximum(m_old, jnp.max(s, axis=-1, keepdims=True))
            alpha = jnp.exp2(m_old - m_new)
            p = jnp.exp2(s - _lanes(m_new, tk)).astype(MXU_DTYPE)
            m_ref[h] = m_new
            v_ext = jnp.concatenate([v_ref[0, :, h * vd:(h + 1) * vd], ones], axis=-1)
            acs = slice(h * aw, (h + 1) * aw)
            acc_ref[:, acs] = _lanes(alpha, aw) * acc_ref[:, acs] + _dot(p, v_ext)

    @pl.when((flags & 2) == 0)
    def _():
        all_heads(None)

    @pl.when((flags & 2) != 0)
    def _():
        qpos = q_offset + i * tq + lax.broadcasted_iota(jnp.int32, (tq, tk), 0)
        kpos = j * tk + lax.broadcasted_iota(jnp.int32, (tq, tk), 1)
        visible = ((kpos // CHUNK) <= (qpos // CHUNK)) & (kpos < kv_len)
        all_heads(jnp.where(visible, 0.0, NEG))

    @pl.when((flags & 1) != 0)
    def _():
        for h in range(heads):
            num = acc_ref[:, h * aw:h * aw + vd]
            den = _lanes(acc_ref[:, h * aw + vd:(h + 1) * aw], vd)
            o_ref[0, :, h * vd:(h + 1) * vd] = (num / den).astype(o_ref.dtype)


def mha_attn(q, k, v, *, heads, tq, tk, q_offset, kv_len):
    b, t, _ = q.shape
    s = k.shape[1]
    hw = q.shape[2] // heads
    vd = v.shape[2] // heads
    nq, nk = t // tq, s // tk
    qi, kj, flags = [], [], []
    for i in range(nq):
        first_q = q_offset + i * tq
        last_key = ((first_q + tq - 1) // CHUNK + 1) * CHUNK - 1
        jl = min(last_key // tk, nk - 1)
        for j in range(jl + 1):
            fully_visible = ((j + 1) * tk - 1) // CHUNK <= first_q // CHUNK and (j + 1) * tk <= kv_len
            qi.append(i)
            kj.append(j)
            flags.append(int(j == jl) + 2 * int(not fully_visible))
    sched = [jnp.asarray(a, jnp.int32) for a in (qi, kj, flags)]
    aw = vd + V7X_LANES
    vmem = (2 * (2 * _nbytes((tq, heads * hw), q.dtype) + 2 * _nbytes((tk, heads * vd), v.dtype)
                 + _nbytes((tq, heads * vd), MXU_DTYPE))
            + _nbytes((tq, heads * aw), F32) + _nbytes((heads, tq, V7X_LANES), F32)
            + 6 * _nbytes((tq, tk), F32) + (8 << 20))
    grid_spec = pltpu.PrefetchScalarGridSpec(
        num_scalar_prefetch=3,
        grid=(b, len(qi)),
        in_specs=[pl.BlockSpec((1, tq, heads * hw), lambda bb, p, qi_r, kj_r, l_r: (bb, qi_r[p], 0)),
                  pl.BlockSpec((1, tk, heads * hw), lambda bb, p, qi_r, kj_r, l_r: (bb, kj_r[p], 0)),
                  pl.BlockSpec((1, tk, heads * vd), lambda bb, p, qi_r, kj_r, l_r: (bb, kj_r[p], 0))],
        out_specs=pl.BlockSpec((1, tq, heads * vd), lambda bb, p, qi_r, kj_r, l_r: (bb, qi_r[p], 0)),
        scratch_shapes=[pltpu.VMEM((tq, heads * aw), F32),
                        pltpu.VMEM((heads, tq, V7X_LANES), F32)])
    return pl.pallas_call(
        functools.partial(_mha_attn_kernel, heads=heads, hw=hw, vd=vd, tq=tq, tk=tk,
                          q_offset=q_offset, kv_len=kv_len),
        out_shape=jax.ShapeDtypeStruct((b, t, heads * vd), MXU_DTYPE),
        grid_spec=grid_spec,
        compiler_params=_params(("parallel", "arbitrary"), vmem),
        name="mha_attn",
    )(*sched, q, k, v)


def _mla_decode_kernel(q_ref, pc_ref, pr_ref, kn_ref, wuk_ref, wuv_ref, o_ref, qs_ref,
                       *, heads, t, nope, lat, rope_dim, q_offset, scale):
    past = pc_ref.shape[1]
    rows = heads * t
    rope_lo = heads * nope
    for h in range(heads):
        qn = q_ref[0, :, h * nope:(h + 1) * nope]
        qs_ref[h * t:(h + 1) * t, 0:lat] = (_dot(qn, wuk_ref[h]) * scale).astype(qs_ref.dtype)
        qr = q_ref[0, :, rope_lo + h * V7X_LANES:rope_lo + (h + 1) * V7X_LANES]
        qs_ref[h * t:(h + 1) * t, lat:lat + V7X_LANES] = (qr.astype(F32) * scale).astype(qs_ref.dtype)

    def bias(k0, n):
        qpos = q_offset + lax.broadcasted_iota(jnp.int32, (t, n), 0)
        kpos = k0 + lax.broadcasted_iota(jnp.int32, (t, n), 1)
        return jnp.where((kpos // CHUNK) <= (qpos // CHUNK), 0.0, NEG)

    def masked(s, b):
        return (s.reshape(heads, t, s.shape[1]) + b[None]).reshape(rows, s.shape[1])

    kc = pc_ref[0].astype(MXU_DTYPE)
    kr = pr_ref[0].astype(MXU_DTYPE)
    kn = kn_ref[0]
    s_past = masked(_dot_nt(qs_ref[:, 0:lat], kc) + _dot_nt(qs_ref[:, lat:lat + rope_dim], kr), bias(0, past))
    s_new = masked(_dot_nt(qs_ref[...], kn), bias(q_offset, t))
    m = jnp.maximum(jnp.max(s_past, axis=-1, keepdims=True), jnp.max(s_new, axis=-1, keepdims=True))
    p_past = jnp.exp(s_past - m)
    p_new = jnp.exp(s_new - m)
    l = jnp.sum(p_past, axis=-1, keepdims=True) + jnp.sum(p_new, axis=-1, keepdims=True)
    ol = (_dot(p_past.astype(MXU_DTYPE), kc) + _dot(p_new.astype(MXU_DTYPE), kn[:, 0:lat])) / l
    vd = wuv_ref.shape[2]
    for h in range(heads):
        o_ref[0, :, h * vd:(h + 1) * vd] = (
            _dot(ol[h * t:(h + 1) * t, :].astype(MXU_DTYPE), wuv_ref[h]).astype(o_ref.dtype))


def mla_decode(q, past_ckv, past_kpe, kcat, w_uk, w_uv, *, q_offset, scale):
    b, t, _ = q.shape
    past, rope_dim = past_kpe.shape[1], past_kpe.shape[2]
    heads, nope, lat = w_uk.shape
    vd = w_uv.shape[2]
    rows = heads * t
    vmem = (2 * (_nbytes((t, q.shape[2]), q.dtype) + _nbytes((past, lat + V7X_LANES), F32)
                 + 2 * _nbytes(w_uk.shape, w_uk.dtype) + _nbytes((t, heads * vd), MXU_DTYPE))
            + _nbytes((rows, lat + V7X_LANES), MXU_DTYPE) + _nbytes((past, lat + V7X_LANES), MXU_DTYPE)
            + 4 * _nbytes((rows, past), F32) + (8 << 20))
    assert vmem <= V7X_VMEM_BYTES, "cached rows must fit one VMEM block"
    return pl.pallas_call(
        functools.partial(_mla_decode_kernel, heads=heads, t=t, nope=nope, lat=lat, rope_dim=rope_dim,
                          q_offset=q_offset, scale=scale),
        out_shape=jax.ShapeDtypeStruct((b, t, heads * vd), MXU_DTYPE),
        grid=(b,),
        in_specs=[pl.BlockSpec((1, t, q.shape[2]), lambda bb: (bb, 0, 0)),
                  pl.BlockSpec((1, past, lat), lambda bb: (bb, 0, 0)),
                  pl.BlockSpec((1, past, rope_dim), lambda bb: (bb, 0, 0)),
                  pl.BlockSpec((1, t, kcat.shape[2]), lambda bb: (bb, 0, 0)),
                  pl.BlockSpec(w_uk.shape, lambda bb: (0, 0, 0)),
                  pl.BlockSpec(w_uv.shape, lambda bb: (0, 0, 0))],
        out_specs=pl.BlockSpec((1, t, heads * vd), lambda bb: (bb, 0, 0)),
        scratch_shapes=[pltpu.VMEM((rows, lat + V7X_LANES), MXU_DTYPE)],
        compiler_params=_params(("parallel",), vmem),
        name="mla_decode",
    )(q, past_ckv, past_kpe, kcat, w_uk, w_uv)


class _ScanPlan(NamedTuple):
    body: Callable
    args: list
    in_specs: list
    out_shape: list
    out_specs: list
    scratch: list
    vmem: int


def _ssd_kernel(*refs, L, groups, hpg, hdim, nstate, has_state):
    if has_state:
        (z_ref, xs_ref, b_ref, c_ref, dt_ref, cw_ref, cb_ref, dtb_ref, alog_ref, dsk_ref, nrm_ref,
         cst_ref, hst_ref, y_ref, ncv_ref, nst_ref, xbuf, h_scr, cum_scr, xdt_scr, yin_scr) = refs
    else:
        (z_ref, xs_ref, b_ref, c_ref, dt_ref, cw_ref, cb_ref, dtb_ref, alog_ref, dsk_ref, nrm_ref,
         cst_ref, y_ref, ncv_ref, nst_ref, xbuf, h_scr, cum_scr, xdt_scr, yin_scr) = refs
        hst_ref = None
    c = pl.program_id(1)
    nc = pl.num_programs(1)
    dx = groups * hpg * hdim
    dn = groups * nstate
    gw = hpg * hdim
    pad = CONV_PAD_ROWS

    @pl.when(c == 0)
    def _():
        xbuf[0:pad, :] = cst_ref[0]
        xbuf[pad:2 * pad, :] = jnp.zeros((pad, xbuf.shape[1]), F32)
        if has_state:
            h_scr[...] = hst_ref[0]
        else:
            h_scr[...] = jnp.zeros_like(h_scr)

    row = lax.broadcasted_iota(jnp.int32, (L, L), 0)
    col = lax.broadcasted_iota(jnp.int32, (L, L), 1)
    causal = row >= col

    xin = jnp.concatenate([xs_ref[...], b_ref[...], c_ref[...]], axis=1)
    xin_f = xin.astype(F32)
    acc = xin_f * cw_ref[CONV_K - 1:CONV_K, :] + cb_ref[...]
    for jj in range(CONV_K - 1):
        shift = jnp.where(row - col == CONV_K - 1 - jj, 1.0, 0.0).astype(MXU_DTYPE)
        acc = acc + _dot(shift, xin) * cw_ref[jj:jj + 1, :]
    corr = xbuf[pad - CONV_K + 1:2 * pad - CONV_K + 1, :] * cw_ref[0:1, :]
    for jj in range(1, CONV_K - 1):
        lo = pad - CONV_K + 1 + jj
        corr = corr + xbuf[lo:lo + pad, :] * cw_ref[jj:jj + 1, :]
    acc = jnp.concatenate([acc[0:pad] + corr, acc[pad:]], axis=0)
    xc = _silu(acc)
    tail = xin_f[L - pad:L, :]
    xbuf[0:pad, :] = tail

    @pl.when(c == nc - 1)
    def _():
        ncv_ref[0] = tail

    dtr = dt_ref[...] + dtb_ref[...]
    dt = jnp.maximum(dtr, 0.0) + jnp.log1p(jnp.exp(-jnp.abs(dtr)))
    la = dt * (-jnp.exp(alog_ref[...]) * LOG2E)
    tri = jnp.where(causal, 1.0, 0.0).astype(MXU_DTYPE)
    cum = sum(_dot(tri, piece) for piece in _split3(la))
    nh = groups * hpg
    eye = jnp.where(lax.broadcasted_iota(jnp.int32, (V7X_LANES, V7X_LANES), 0)
                    == lax.broadcasted_iota(jnp.int32, (V7X_LANES, V7X_LANES), 1), 1.0, 0.0).astype(MXU_DTYPE)
    cum_t = sum(_dot_nt(eye, piece) for piece in _split3(cum))

    half = lax.broadcasted_iota(jnp.int32, (L, V7X_LANES), 1) < hdim
    per_vreg = V7X_LANES // hdim
    for g in range(groups):
        bg = xc[:, dx + g * nstate:dx + (g + 1) * nstate]
        cg = xc[:, dx + dn + g * nstate:dx + dn + (g + 1) * nstate].astype(MXU_DTYPE)
        qk = _dot_nt(cg, bg.astype(MXU_DTYPE))
        for sl in range(gw // V7X_LANES):
            lane0 = g * gw + sl * V7X_LANES
            h0 = lane0 // hdim
            cb = [jnp.broadcast_to(cum[:, h0 + u:h0 + u + 1], (L, V7X_LANES)) for u in range(per_vreg)]
            db = [jnp.broadcast_to(dt[:, h0 + u:h0 + u + 1], (L, V7X_LANES)) for u in range(per_vreg)]
            cum_e = jnp.where(half, cb[0], cb[1])
            dt_e = jnp.where(half, db[0], db[1])
            xdt = xc[:, lane0:lane0 + V7X_LANES] * dt_e
            xdt_m = xdt.astype(MXU_DTYPE)
            ys = []
            for u in range(per_vreg):
                seg = cb[u][:, 0:L] - cum_t[h0 + u:h0 + u + 1, :]
                decay = jnp.exp2(jnp.where(causal, seg, NEG))
                ys.append(_dot((qk * decay).astype(MXU_DTYPE), xdt_m))
            cum_scr[:, lane0:lane0 + V7X_LANES] = cum_e
            xdt_scr[:, lane0:lane0 + V7X_LANES] = xdt
            yin_scr[:, lane0:lane0 + V7X_LANES] = jnp.where(half, ys[0], ys[1])

    for g in range(groups):
        gs = slice(g * gw, (g + 1) * gw)
        bg = xc[:, dx + g * nstate:dx + (g + 1) * nstate].astype(MXU_DTYPE)
        cg = xc[:, dx + dn + g * nstate:dx + dn + (g + 1) * nstate].astype(MXU_DTYPE)
        cum_g = cum_scr[:, gs]
        last = cum_scr[L - 1:L, gs]
        hg = h_scr[g]
        y_inter = _dot(cg, hg.astype(MXU_DTYPE)) * jnp.exp2(cum_g)
        wx = (jnp.exp2(last - cum_g) * xdt_scr[:, gs]).astype(MXU_DTYPE)
        h_scr[g] = jnp.exp2(last) * hg + _dot_tn(bg, wx)
        y = yin_scr[:, gs] + y_inter + dsk_ref[:, gs] * xc[:, gs]
        y = y * _silu(z_ref[:, gs].astype(F32))
        y_ref[:, gs] = (_rms(y) * nrm_ref[:, gs]).astype(y_ref.dtype)

    @pl.when(c == nc - 1)
    def _():
        nst_ref[0] = h_scr[...]


def ssd_scan(proj, dt_raw, col, conv_w, conv_b, dt_bias, a_log, d_skip_e, ssd_norm, conv_state, ssd_state,
             *, batch, L, groups, hpg, hdim, nstate):
    n = proj.shape[0]
    t = n // batch
    nc = t // L
    dx = groups * hpg * hdim
    dn = groups * nstate
    cdim = dx + 2 * dn
    has_state = ssd_state is not None
    rowmap = lambda b, c: b * nc + c
    in_specs = [pl.BlockSpec((L, dx), lambda b, c: (rowmap(b, c), col["z"])),
                pl.BlockSpec((L, dx), lambda b, c: (rowmap(b, c), col["xs"])),
                pl.BlockSpec((L, dn), lambda b, c: (rowmap(b, c), col["B"])),
                pl.BlockSpec((L, dn), lambda b, c: (rowmap(b, c), col["C"])),
                pl.BlockSpec((L, V7X_LANES), lambda b, c: (rowmap(b, c), 0)),
                pl.BlockSpec((CONV_K, cdim), lambda b, c: (0, 0)),
                pl.BlockSpec((1, cdim), lambda b, c: (0, 0)),
                pl.BlockSpec((1, V7X_LANES), lambda b, c: (0, 0)),
                pl.BlockSpec((1, V7X_LANES), lambda b, c: (0, 0)),
                pl.BlockSpec((1, dx), lambda b, c: (0, 0)),
                pl.BlockSpec((1, dx), lambda b, c: (0, 0)),
                pl.BlockSpec((1, CONV_PAD_ROWS, cdim), lambda b, c: (b, 0, 0))]
    args = [proj, proj, proj, proj, dt_raw, conv_w, conv_b.reshape(1, cdim), dt_bias, a_log,
            d_skip_e, ssd_norm.reshape(1, dx), conv_state]
    if has_state:
        in_specs.append(pl.BlockSpec((1, groups, nstate, hpg * hdim), lambda b, c: (b, 0, 0, 0)))
        args.append(ssd_state)
    state_bytes = _nbytes((groups, nstate, hpg * hdim), F32)
    vmem = (2 * (2 * _nbytes((L, dx), F32) + 2 * _nbytes((L, dn), F32) + _nbytes((L, dx), MXU_DTYPE)
                 + 2 * _nbytes((CONV_PAD_ROWS, cdim), F32) + 2 * state_bytes)
            + state_bytes + 12 * _nbytes((L + CONV_PAD_ROWS, cdim), F32) + (8 << 20))
    return _ScanPlan(
        body=functools.partial(_ssd_kernel, L=L, groups=groups, hpg=hpg, hdim=hdim, nstate=nstate,
                               has_state=has_state),
        args=args, in_specs=in_specs,
        out_shape=[jax.ShapeDtypeStruct((n, dx), MXU_DTYPE),
                   jax.ShapeDtypeStruct((batch, CONV_PAD_ROWS, cdim), F32),
                   jax.ShapeDtypeStruct((batch, groups, nstate, hpg * hdim), F32)],
        out_specs=[pl.BlockSpec((L, dx), lambda b, c: (rowmap(b, c), 0)),
                   pl.BlockSpec((1, CONV_PAD_ROWS, cdim), lambda b, c: (b, 0, 0)),
                   pl.BlockSpec((1, groups, nstate, hpg * hdim), lambda b, c: (b, 0, 0, 0))],
        scratch=[pltpu.VMEM((2 * CONV_PAD_ROWS, cdim), F32),
                 pltpu.VMEM((groups, nstate, hpg * hdim), F32),
                 pltpu.VMEM((L, dx), F32),
                 pltpu.VMEM((L, dx), F32),
                 pltpu.VMEM((L, dx), F32)],
        vmem=vmem)


def _ret_kernel(*refs, L, heads, hd, has_state):
    if has_state:
        (q_ref, k_ref, v_ref, gate_ref, cos_ref, sin_ref, dec_ref, ecum_ref, wv_ref, sdec_ref, st_ref,
         o_ref, nst_ref, s_scr) = refs
    else:
        (q_ref, k_ref, v_ref, gate_ref, cos_ref, sin_ref, dec_ref, ecum_ref, wv_ref, sdec_ref,
         o_ref, nst_ref, s_scr) = refs
        st_ref = None
    c = pl.program_id(1)
    nc = pl.num_programs(1)
    half = hd // 2

    @pl.when(c == 0)
    def _():
        if has_state:
            s_scr[...] = st_ref[0]
        else:
            s_scr[...] = jnp.zeros_like(s_scr)

    cos = cos_ref[...]
    sin = sin_ref[...]

    def rope(ref, h, mult):
        x1 = ref[:, h * hd:h * hd + half].astype(F32)
        x2 = ref[:, h * hd + half:(h + 1) * hd].astype(F32)
        out = jnp.concatenate([x1 * cos - x2 * sin, x2 * cos + x1 * sin], axis=-1)
        return (out * mult).astype(MXU_DTYPE) if mult != 1.0 else out.astype(MXU_DTYPE)

    for h in range(heads):
        hs = slice(h * hd, (h + 1) * hd)
        qr = rope(q_ref, h, 1.0)
        kr = rope(k_ref, h, hd ** -0.5)
        vf = v_ref[:, hs]
        qk = _dot_nt(qr, kr)
        y = _dot((qk * dec_ref[h]).astype(MXU_DTYPE), vf.astype(MXU_DTYPE))
        s_old = s_scr[h]
        y = y + _dot(qr, s_old.astype(MXU_DTYPE)) * ecum_ref[h]
        s_scr[h] = sdec_ref[h] * s_old + _dot_tn(kr, (wv_ref[h] * vf.astype(F32)).astype(MXU_DTYPE))
        o_ref[:, hs] = (_rms(y) * _silu(gate_ref[:, hs].astype(F32))).astype(o_ref.dtype)

    @pl.when(c == nc - 1)
    def _():
        nst_ref[0] = s_scr[...]


def ret_scan(proj, col, cos_rows, sin_rows, ret_state, *, batch, L, heads, hd):
    n = proj.shape[0]
    t = n // batch
    nc = t // L
    inner = heads * hd
    has_state = ret_state is not None
    lg = jnp.log1p(-jnp.exp2(-5.0 - jnp.arange(heads, dtype=F32)))[:, None, None]
    li = jnp.arange(L, dtype=F32)
    diff = li[:, None] - li[None, :]
    dec = jnp.where(diff >= 0, jnp.exp(jnp.where(diff >= 0, diff, 0.0)[None] * lg), 0.0)
    ecum = jnp.broadcast_to(jnp.exp((li[None, :, None] + 1.0) * lg), (heads, L, hd))
    wv = jnp.broadcast_to(jnp.exp((L - 1.0 - li[None, :, None]) * lg), (heads, L, hd))
    sdec = jnp.broadcast_to(jnp.exp(L * lg), (heads, 1, hd))
    rowmap = lambda b, c: b * nc + c
    in_specs = [pl.BlockSpec((L, inner), lambda b, c: (rowmap(b, c), col["q"])),
                pl.BlockSpec((L, inner), lambda b, c: (rowmap(b, c), col["k"])),
                pl.BlockSpec((L, inner), lambda b, c: (rowmap(b, c), col["v"])),
                pl.BlockSpec((L, inner), lambda b, c: (rowmap(b, c), col["gate"])),
                pl.BlockSpec((L, hd // 2), lambda b, c: (c, 0)),
                pl.BlockSpec((L, hd // 2), lambda b, c: (c, 0)),
                pl.BlockSpec((heads, L, L), lambda b, c: (0, 0, 0)),
                pl.BlockSpec((heads, L, hd), lambda b, c: (0, 0, 0)),
                pl.BlockSpec((heads, L, hd), lambda b, c: (0, 0, 0)),
                pl.BlockSpec((heads, 1, hd), lambda b, c: (0, 0, 0))]
    args = [proj, proj, proj, proj, cos_rows, sin_rows, dec, ecum, wv, sdec]
    if has_state:
        in_specs.append(pl.BlockSpec((1, heads, hd, hd), lambda b, c: (b, 0, 0, 0)))
        args.append(ret_state)
    state_bytes = _nbytes((heads, hd, hd), F32)
    vmem = (2 * (4 * _nbytes((L, inner), F32) + _nbytes((L, inner), MXU_DTYPE) + _nbytes((heads, L, L), F32)
                 + 2 * _nbytes((heads, L, hd), F32) + 2 * state_bytes)
            + state_bytes + 16 * _nbytes((L, hd), F32) + (8 << 20))
    return _ScanPlan(
        body=functools.partial(_ret_kernel, L=L, heads=heads, hd=hd, has_state=has_state),
        args=args, in_specs=in_specs,
        out_shape=[jax.ShapeDtypeStruct((n, inner), MXU_DTYPE),
                   jax.ShapeDtypeStruct((batch, heads, hd, hd), F32)],
        out_specs=[pl.BlockSpec((L, inner), lambda b, c: (rowmap(b, c), 0)),
                   pl.BlockSpec((1, heads, hd, hd), lambda b, c: (b, 0, 0, 0))],
        scratch=[pltpu.VMEM((heads, hd, hd), F32)],
        vmem=vmem)


def _scan_pair_kernel(*refs, bodies, n_in, n_out, n_scr):
    ins, outs, scrs = refs[:sum(n_in)], refs[sum(n_in):sum(n_in) + sum(n_out)], refs[sum(n_in) + sum(n_out):]
    for k, body in enumerate(bodies):
        take = lambda seq, counts: seq[sum(counts[:k]):sum(counts[:k + 1])]
        body(*take(ins, n_in), *take(outs, n_out), *take(scrs, n_scr))


def run_scans(plans, *, batch, nc, name):
    outs = pl.pallas_call(
        functools.partial(_scan_pair_kernel, bodies=tuple(p.body for p in plans),
                          n_in=tuple(len(p.args) for p in plans),
                          n_out=tuple(len(p.out_shape) for p in plans),
                          n_scr=tuple(len(p.scratch) for p in plans)),
        out_shape=tuple(s for p in plans for s in p.out_shape),
        grid=(batch, nc),
        in_specs=[s for p in plans for s in p.in_specs],
        out_specs=tuple(s for p in plans for s in p.out_specs),
        scratch_shapes=[s for p in plans for s in p.scratch],
        compiler_params=_params(("parallel", "arbitrary"), sum(p.vmem for p in plans)),
        name=name,
    )(*[a for p in plans for a in p.args])
    split, k = [], 0
    for p in plans:
        split.append(outs[k:k + len(p.out_shape)])
        k += len(p.out_shape)
    return split


def _rope_tables(pos, half, reps):
    inv = ROPE_BASE ** (-jnp.arange(half, dtype=F32) / half)
    ang = pos.astype(F32)[:, None] * inv[None, :]
    cos, sin = jnp.cos(ang), jnp.sin(ang)
    if reps == 0:
        return cos, sin
    c = jnp.tile(jnp.concatenate([cos, cos], axis=-1), (1, reps))
    s = jnp.tile(jnp.concatenate([-sin, sin], axis=-1), (1, reps))
    return c, s


def _mixer_even(x, batch, pos, norm_g, prm, conv_state, ssd_state, ret_state, dims, L_ssd, L_ret):
    d = x.shape[1]
    groups, hpg, hdim, nstate, rheads, rhd = dims
    proj, dt_raw = rms_matmul(x, norm_g, prm["w_in"], out_dtype=MXU_DTYPE, w_side=prm["w_dt"])
    col = prm["col"]
    ssd_plan = ssd_scan(proj, dt_raw, col, prm["conv_w"], prm["conv_b"], prm["dt_bias"],
                        prm["a_log"], prm["d_skip_e"], prm["ssd_norm"], conv_state, ssd_state,
                        batch=batch, L=L_ssd, groups=groups, hpg=hpg, hdim=hdim, nstate=nstate)
    cos_rows, sin_rows = _rope_tables(pos, rhd // 2, 0)
    ret_plan = ret_scan(proj, col, cos_rows, sin_rows, ret_state, batch=batch, L=L_ret, heads=rheads, hd=rhd)
    t = x.shape[0] // batch
    (y, new_conv, new_ssd), = run_scans([ssd_plan], batch=batch, nc=t // L_ssd, name="ssd_scan")
    (o, new_ret), = run_scans([ret_plan], batch=batch, nc=t // L_ret, name="ret_scan")
    x = matmul_residual(x, [y, o], [prm["w_out_ssd"], prm["w_out_ret"]])
    return x, (new_conv, new_ssd, new_ret)


def _prep_even(w_in, conv_w, conv_b, dt_bias, a_log, d_skip, ssd_norm, w_out, dims):
    groups, hpg, hdim, nstate, rheads, rhd = dims
    d = w_in.shape[0]
    dx = groups * hpg * hdim
    dn = groups * nstate
    nh = groups * hpg
    ri = rheads * rhd
    o_z, o_xs, o_b, o_c, o_dt = 0, dx, 2 * dx, 2 * dx + dn, 2 * dx + 2 * dn
    o_q = o_dt + nh
    seg = lambda lo, w: w_in[:, lo:lo + w]
    dt_pad = jnp.zeros((d, V7X_LANES - nh), w_in.dtype)
    w_main = jnp.concatenate([seg(o_z, dx), seg(o_q, ri), seg(o_q + ri, ri), seg(o_q + 2 * ri, ri),
                              seg(o_q + 3 * ri, ri), seg(o_xs, dx), seg(o_b, dn), seg(o_c, dn)],
                             axis=1).astype(MXU_DTYPE)
    w_dt = jnp.concatenate([seg(o_dt, nh), dt_pad], axis=1).astype(MXU_DTYPE)
    assert dx == ri and dx % dn == 0 and dn % V7X_LANES == 0
    col = {"z": 0, "q": 1, "k": 2, "v": 3, "gate": 4, "xs": 5, "B": 6 * dx // dn, "C": 6 * dx // dn + 1}
    pad1 = lambda v: jnp.pad(v.astype(F32), (0, V7X_LANES - nh)).reshape(1, V7X_LANES)
    return {"w_in": w_main, "w_dt": w_dt, "col": col, "conv_w": conv_w, "conv_b": conv_b,
            "dt_bias": pad1(dt_bias), "a_log": pad1(a_log),
            "d_skip_e": jnp.repeat(d_skip, hdim).reshape(1, dx), "ssd_norm": ssd_norm,
            "w_out_ssd": w_out[:dx].astype(MXU_DTYPE), "w_out_ret": w_out[dx:].astype(MXU_DTYPE)}


def _prep_odd(w_in, q_norm, kv_norm, w_uq, w_uk, w_uv, w_out, rope_dim):
    d = w_in.shape[0]
    q_lora = q_norm.shape[0]
    kv_lora, heads, nope = w_uk.shape
    kp = w_in[:, q_lora + kv_lora:]
    w_in_new = jnp.concatenate([w_in[:, :q_lora + kv_lora], kp, kp], axis=1).astype(MXU_DTYPE)
    assert 2 * rope_dim == V7X_LANES
    wq = w_uq.reshape(q_lora, heads, nope + rope_dim)
    wq_nope = wq[:, :, :nope]
    wq_rope = jnp.pad(wq[:, :, nope:], ((0, 0), (0, 0), (0, V7X_LANES - rope_dim)))
    hw = nope + V7X_LANES
    return {"w_in": w_in_new, "q_norm": q_norm, "kv_norm": kv_norm,
            "w_uq_lat": jnp.concatenate([wq_nope.reshape(q_lora, heads * nope),
                                         wq_rope.reshape(q_lora, heads * V7X_LANES)], axis=1).astype(MXU_DTYPE),
            "slabs_lat": tuple(heads * nope + h * V7X_LANES for h in range(heads)),
            "w_uq_head": jnp.concatenate([wq_nope, wq_rope], axis=2).reshape(q_lora, heads * hw).astype(MXU_DTYPE),
            "slabs_head": tuple(h * hw + nope for h in range(heads)),
            "w_ukv": jnp.concatenate([w_uk.reshape(kv_lora, heads * nope),
                                      w_uv.reshape(kv_lora, -1)], axis=1).astype(MXU_DTYPE),
            "w_uk": jnp.transpose(w_uk, (1, 2, 0)).astype(MXU_DTYPE),
            "w_uv": jnp.transpose(w_uv, (1, 0, 2)).astype(MXU_DTYPE),
            "w_out": w_out.astype(MXU_DTYPE)}


def _mixer_odd(x, batch, pos, norm_g, prm, past, rope_dim, q_offset):
    n, d = x.shape
    t = n // batch
    q_lora = prm["q_norm"].shape[0]
    kv_lora = prm["kv_norm"].shape[0]
    heads, nope, lat = prm["w_uk"].shape
    proj = rms_matmul(x, norm_g, prm["w_in"])
    cos_rows, sin_rows = _rope_tables(pos, rope_dim // 2, V7X_LANES // rope_dim)
    scale = (nope + rope_dim) ** -0.5
    dims = dict(q_lora=q_lora, kv_lora=kv_lora, rope_dim=rope_dim, heads=heads, nope=nope)
    if past is None:
        q, ckv, kpe, k, v = mla_prep(proj, prm["q_norm"], prm["kv_norm"], prm["w_uq_head"], cos_rows, sin_rows,
                                     prm["w_ukv"], rope_slabs=prm["slabs_head"], qscale=scale * math.log2(math.e),
                                     **dims)
        tq = tk = _pick(t, (512, 256, 128, 64))
        shp = lambda a: a.reshape(batch, t, a.shape[1])
        o = mha_attn(shp(q), shp(k), shp(v), heads=heads, tq=tq, tk=tk, q_offset=q_offset, kv_len=t)
    else:
        q, ckv, kpe, kcat = mla_prep(proj, prm["q_norm"], prm["kv_norm"], prm["w_uq_lat"], cos_rows, sin_rows,
                                     None, rope_slabs=prm["slabs_lat"], qscale=1.0, **dims)
        q = q.reshape(batch, t, q.shape[1])
        kcat = kcat.reshape(batch, t, kcat.shape[1])
        o = mla_decode(q, past[0], past[1], kcat, prm["w_uk"], prm["w_uv"], q_offset=q_offset, scale=scale)
    x = matmul_residual(x, [o.reshape(n, o.shape[2])], [prm["w_out"]])
    return x, (ckv.reshape(batch, t, kv_lora), kpe.reshape(batch, t, rope_dim))


def kernel(x_prompt, x_sample, mem_prompt, state_conv, state_ssd, state_ret, cache_ckv, cache_kpe,
           cache_mem_k, cache_mem_v, norms, ffn_w1, ffn_w2, mem_norm, w_mq, w_mkv, w_mo,
           ab_w_in, ab_conv_w, ab_conv_b, ab_dt_bias, ab_a_log, ab_d_skip, ab_ssd_norm, ab_w_out,
           c_w_in, c_q_norm, c_kv_norm, c_w_uq, c_w_uk, c_w_uv, c_w_out, final_norm):
    bp, tp, d = x_prompt.shape
    bs, ts, _ = x_sample.shape
    depth = norms.shape[0]
    assert depth >= 1
    past_len = cache_ckv.shape[2]
    mem_tokens = mem_prompt.shape[1]
    mem_heads, mem_hd = cache_mem_k.shape[3], cache_mem_k.shape[4]
    mem_inner = mem_heads * mem_hd
    ssd_heads, nstate, hdim = state_ssd.shape[2], state_ssd.shape[3], state_ssd.shape[4]
    cdim = state_conv.shape[3]
    groups = (cdim - ssd_heads * hdim) // (2 * nstate)
    hpg = ssd_heads // groups
    rheads, rhd = state_ret.shape[2], state_ret.shape[3]
    dims = (groups, hpg, hdim, nstate, rheads, rhd)
    rope_dim = cache_kpe.shape[3]

    pos_p = jnp.arange(tp)
    pos_s = past_len + jnp.arange(ts)
    xp = x_prompt.reshape(bp * tp, d)
    xs = x_sample.reshape(bs * ts, d)
    L_ssd_p, L_ret_p = _pick(tp, (128, 64)), _pick(tp, (256, 128, 64))
    L_s = _pick(ts, (128, 64))

    outs = {k: [] for k in ("conv_p", "ssd_p", "ret_p", "ckv_p", "kpe_p", "memk_p", "memv_p",
                            "conv_s", "ssd_s", "ret_s", "ckv_s", "kpe_s")}

    def to_group_layout(st):
        b = st.shape[0]
        return st.reshape(b, groups, hpg, nstate, hdim).transpose(0, 1, 3, 2, 4).reshape(b, groups, nstate, hpg * hdim)

    def from_group_layout(st):
        b = st.shape[0]
        return st.reshape(b, groups, nstate, hpg, hdim).transpose(0, 1, 3, 2, 4).reshape(b, ssd_heads, nstate, hdim)

    for i in range(depth):
        j = i // 2
        closing = final_norm if i == depth - 1 else None
        wq_m = w_mq[i].astype(MXU_DTYPE)
        wo_m = w_mo[i].astype(MXU_DTYPE)
        mkv = rms_matmul(mem_prompt.reshape(bp * mem_tokens, d), mem_norm[i], w_mkv[i].astype(MXU_DTYPE))
        mk_p = mkv[:, :mem_inner].reshape(bp, mem_tokens, mem_inner)
        mv_p = mkv[:, mem_inner:].reshape(bp, mem_tokens, mem_inner)
        outs["memk_p"].append(mk_p.reshape(bp, mem_tokens, mem_heads, mem_hd))
        outs["memv_p"].append(mv_p.reshape(bp, mem_tokens, mem_heads, mem_hd))

        xs, w1a, w1b, w2h = ffn_cast(xs, norms[i, 0], ffn_w1, ffn_w2, i, 0)
        xp = ffn(xp, norms[i, 0], w1a, w1b, w2h)
        if i % 2 == 0:
            prm = _prep_even(ab_w_in[j], ab_conv_w[j], ab_conv_b[j], ab_dt_bias[j], ab_a_log[j],
                             ab_d_skip[j], ab_ssd_norm[j], ab_w_out[j], dims)
            zero_conv = jnp.zeros((bp, CONV_PAD_ROWS, cdim), F32)
            xp, st_p = _mixer_even(xp, bp, pos_p, norms[i, 1], prm, zero_conv, None, None, dims, L_ssd_p, L_ret_p)
            conv_in = jnp.pad(state_conv[j], ((0, 0), (CONV_PAD_ROWS - (CONV_K - 1), 0), (0, 0)))
            xs, st_s = _mixer_even(xs, bs, pos_s, norms[i, 1], prm, conv_in, to_group_layout(state_ssd[j]),
                                   state_ret[j], dims, L_s, L_s)
            for tag, st in (("p", st_p), ("s", st_s)):
                outs["conv_" + tag].append(st[0][:, CONV_PAD_ROWS - (CONV_K - 1):, :])
                outs["ssd_" + tag].append(from_group_layout(st[1]))
                outs["ret_" + tag].append(st[2])
        else:
            prm = _prep_odd(c_w_in[j], c_q_norm[j], c_kv_norm[j], c_w_uq[j], c_w_uk[j], c_w_uv[j], c_w_out[j],
                            rope_dim)
            xp, st_p = _mixer_odd(xp, bp, pos_p, norms[i, 1], prm, None, rope_dim, 0)
            xs, st_s = _mixer_odd(xs, bs, pos_s, norms[i, 1], prm, (cache_ckv[j], cache_kpe[j]), rope_dim, past_len)
            for tag, st in (("p", st_p), ("s", st_s)):
                outs["ckv_" + tag].append(st[0])
                outs["kpe_" + tag].append(st[1])
        xp = mem_attn(xp, norms[i, 2], wq_m, mk_p, mv_p, wo_m, batch=bp, heads=mem_heads)
        xs = mem_attn(xs, norms[i, 2], wq_m, cache_mem_k, cache_mem_v, wo_m, batch=bs, heads=mem_heads, layer=i)
        xs, w1a, w1b, w2h = ffn_cast(xs, norms[i, 3], ffn_w1, ffn_w2, i, 1, final_g=closing)
        xp = ffn(xp, norms[i, 3], w1a, w1b, w2h, final_g=closing)

    y_prompt = xp.reshape(bp, tp, d)
    y_sample = xs.reshape(bs, ts, d)
    st = lambda k: jnp.stack(outs[k])
    return (y_prompt, y_sample, st("conv_p"), st("ssd_p"), st("ret_p"), st("ckv_p"), st("kpe_p"),
            st("memk_p"), st("memv_p"), st("conv_s"), st("ssd_s"), st("ret_s"), st("ckv_s"), st("kpe_s"))
```

```python
import functools
import math
from typing import Callable, NamedTuple

import jax
import jax.numpy as jnp
from jax import lax
from jax.experimental import pallas as pl
from jax.experimental.pallas import tpu as pltpu

F32 = jnp.float32
MXU_DTYPE = jnp.bfloat16

EPS = 1e-6
CHUNK = 64
ROPE_BASE = 10000.0
CONV_K = 4
NEG = -1e30
LOG2E = math.log2(math.e)

V7X_VMEM_BYTES = 64 * 1024 * 1024
TILE_VMEM_BUDGET = 48 * 1024 * 1024
V7X_LANES = 128
V7X_SUBLANES = 8
CONV_PAD_ROWS = 8


def _params(semantics, vmem_bytes):
    limit = min(int(vmem_bytes), V7X_VMEM_BYTES - (4 << 20))
    return pltpu.CompilerParams(dimension_semantics=semantics, vmem_limit_bytes=limit)


def _nbytes(shape, dtype):
    return math.prod(shape) * jnp.dtype(dtype).itemsize


def _pick(n, prefs):
    for p in prefs:
        if n % p == 0:
            return p
    return n


def _dot(a, b):
    return jnp.dot(a, b, preferred_element_type=F32)


def _dot_nt(a, b):
    return lax.dot_general(a, b, (((1,), (1,)), ((), ())), preferred_element_type=F32)


def _dot_tn(a, b):
    return lax.dot_general(a, b, (((0,), (0,)), ((), ())), preferred_element_type=F32)


def _split3(x):
    hi = x.astype(MXU_DTYPE)
    r = x - hi.astype(F32)
    mid = r.astype(MXU_DTYPE)
    lo = (r - mid.astype(F32)).astype(MXU_DTYPE)
    return hi, mid, lo


def _rms(xf, g=None):
    y = xf * lax.rsqrt(jnp.mean(xf * xf, axis=-1, keepdims=True) + EPS)
    return y if g is None else y * g


def _silu(a):
    return a * (1.0 / (1.0 + jnp.exp(-a)))


def _swap32(x):
    w = x.shape[-1]
    lane = lax.broadcasted_iota(jnp.int32, x.shape, x.ndim - 1)
    fwd = pltpu.roll(x, w - 32, x.ndim - 1)
    bwd = pltpu.roll(x, 32, x.ndim - 1)
    return jnp.where((lane & 63) < 32, fwd, bwd)


def _rms_matmul_kernel(*refs, side):
    if side:
        x_ref, g_ref, w_ref, ws_ref, o_ref, os_ref, xn_ref = refs
    else:
        x_ref, g_ref, w_ref, o_ref, xn_ref = refs

    @pl.when(pl.program_id(1) == 0)
    def _():
        xn_ref[...] = _rms(x_ref[...], g_ref[...]).astype(xn_ref.dtype)
        if side:
            os_ref[...] = _dot(xn_ref[...], ws_ref[...])

    o_ref[...] = _dot(xn_ref[...], w_ref[...]).astype(o_ref.dtype)


def rms_matmul(x, g, w, *, out_dtype=F32, w_side=None):
    n, d = x.shape
    nout = w.shape[1]
    side = w_side is not None
    tn = _pick(nout, (1920, 1152, 1024, 512, 256, 128))

    def vmem_for(tm):
        return (2 * (_nbytes((tm, d), F32) + _nbytes((d, tn), w.dtype) + _nbytes((tm, tn), out_dtype))
                + _nbytes((tm, d), MXU_DTYPE) + _nbytes((tm, tn), F32) + (8 << 20))

    tm = next(t for t in (1024, 512, 256, 128, n) if n % t == 0 and (vmem_for(t) <= TILE_VMEM_BUDGET or t <= 128))
    vmem = vmem_for(tm)
    in_specs = [pl.BlockSpec((tm, d), lambda i, j: (i, 0)),
                pl.BlockSpec((1, d), lambda i, j: (0, 0)),
                pl.BlockSpec((d, tn), lambda i, j: (0, j))]
    args = [x, g.reshape(1, d), w]
    out_shape = jax.ShapeDtypeStruct((n, nout), out_dtype)
    out_specs = pl.BlockSpec((tm, tn), lambda i, j: (i, j))
    if side:
        ns = w_side.shape[1]
        in_specs.append(pl.BlockSpec((d, ns), lambda i, j: (0, 0)))
        args.append(w_side)
        out_shape = (out_shape, jax.ShapeDtypeStruct((n, ns), F32))
        out_specs = (out_specs, pl.BlockSpec((tm, ns), lambda i, j: (i, 0)))
        vmem += 2 * (_nbytes((d, ns), w_side.dtype) + _nbytes((tm, ns), F32))
    return pl.pallas_call(
        functools.partial(_rms_matmul_kernel, side=side),
        out_shape=out_shape,
        grid=(n // tm, nout // tn),
        in_specs=in_specs,
        out_specs=out_specs,
        scratch_shapes=[pltpu.VMEM((tm, d), MXU_DTYPE)],
        compiler_params=_params(("parallel", "arbitrary"), vmem),
        name="rms_matmul",
    )(*args)


def _ffn_kernel(*refs, final_norm, nf):
    x_ref, g_ref = refs[0:2]
    blocks = (refs[2:5], refs[5:8])
    gf_ref = refs[8] if final_norm else None
    o_ref, xn_ref = refs[-2:]
    s = pl.program_id(1)

    @pl.when(s == 0)
    def _():
        xf = x_ref[...]
        xn_ref[...] = _rms(xf, g_ref[...]).astype(xn_ref.dtype)
        o_ref[...] = xf

    def down(w1a_ref, w1b_ref, w2_ref):
        xn = xn_ref[...]
        a = _dot(xn, w1a_ref[...])
        b = _dot(xn, w1b_ref[...])
        return _dot((_silu(a) * b).astype(MXU_DTYPE), w2_ref[...])

    if nf % 2 == 0:
        o_ref[...] += down(*blocks[0]) + down(*blocks[1])
    else:
        @pl.when(s == 0)
        def _():
            o_ref[...] += down(*blocks[0])

        @pl.when(s > 0)
        def _():
            o_ref[...] += down(*blocks[0]) + down(*blocks[1])

    if final_norm:
        @pl.when(s == pl.num_programs(1) - 1)
        def _():
            o_ref[...] = _rms(o_ref[...], gf_ref[...])


def _ffn_cast_kernel(*refs, final_norm):
    x_ref, g_ref, w1a_ref, w1b_ref, w2_ref = refs[0:5]
    gf_ref = refs[5] if final_norm else None
    o_ref, w1a_o, w1b_o, w2_o, xn_ref = refs[-5:]
    s = pl.program_id(1)

    @pl.when(s == 0)
    def _():
        xf = x_ref[...]
        xn_ref[...] = _rms(xf, g_ref[...]).astype(xn_ref.dtype)
        o_ref[...] = xf

    w1a_o[...] = w1a_ref[...].astype(w1a_o.dtype)
    w1b_o[...] = w1b_ref[...].astype(w1b_o.dtype)
    w2_o[...] = (0.5 * w2_ref[...]).astype(w2_o.dtype)
    xn = xn_ref[...]
    h = (_silu(_dot(xn, w1a_o[...])) * _dot(xn, w1b_o[...])).astype(MXU_DTYPE)
    o_ref[...] += _dot(h, w2_o[...])

    if final_norm:
        @pl.when(s == pl.num_programs(1) - 1)
        def _():
            o_ref[...] = _rms(o_ref[...], gf_ref[...])


def ffn_cast(x, g, w1, w2, layer, which, final_g=None):
    n, d = x.shape
    dff = w2.shape[2]
    tm = _pick(n, (512, 256, 128))
    tf = _pick(dff, (256, 128))
    nf = dff // tf
    vmem = (2 * (2 * _nbytes((tm, d), F32) + 3 * _nbytes((d, tf), F32) + 3 * _nbytes((d, tf), MXU_DTYPE))
            + _nbytes((tm, d), MXU_DTYPE) + _nbytes((tm, d), F32) + 6 * _nbytes((tm, tf), F32) + (6 << 20))
    in_specs = [pl.BlockSpec((tm, d), lambda i, s: (i, 0)),
                pl.BlockSpec((1, d), lambda i, s: (0, 0)),
                pl.BlockSpec((None, None, d, tf), lambda i, s: (layer, which, 0, s)),
                pl.BlockSpec((None, None, d, tf), lambda i, s: (layer, which, 0, s + nf)),
                pl.BlockSpec((None, None, tf, d), lambda i, s: (layer, which, s, 0))]
    args = [x, g.reshape(1, d), w1, w1, w2]
    if final_g is not None:
        in_specs.append(pl.BlockSpec((1, d), lambda i, s: (0, 0)))
        args.append(final_g.reshape(1, d))
    return pl.pallas_call(
        functools.partial(_ffn_cast_kernel, final_norm=final_g is not None),
        out_shape=(jax.ShapeDtypeStruct((n, d), F32),
                   jax.ShapeDtypeStruct((d, dff), MXU_DTYPE),
                   jax.ShapeDtypeStruct((d, dff), MXU_DTYPE),
                   jax.ShapeDtypeStruct((dff, d), MXU_DTYPE)),
        grid=(n // tm, nf),
        in_specs=in_specs,
        out_specs=(pl.BlockSpec((tm, d), lambda i, s: (i, 0)),
                   pl.BlockSpec((d, tf), lambda i, s: (0, s)),
                   pl.BlockSpec((d, tf), lambda i, s: (0, s)),
                   pl.BlockSpec((tf, d), lambda i, s: (s, 0))),
        scratch_shapes=[pltpu.VMEM((tm, d), MXU_DTYPE)],
        compiler_params=_params(("arbitrary", "arbitrary"), vmem),
        name="ffn_cast",
    )(*args)


def ffn(x, g, w1a, w1b, w2h, final_g=None):
    n, d = x.shape
    dff = w2h.shape[0]
    tm = _pick(n, (512, 256, 128))
    tf = _pick(dff, (512, 256, 128))
    nf = dff // tf
    vmem = (2 * (2 * _nbytes((tm, d), F32) + 6 * _nbytes((d, tf), w1a.dtype))
            + _nbytes((tm, d), MXU_DTYPE) + _nbytes((tm, d), F32) + 6 * _nbytes((tm, tf), F32) + (6 << 20))
    in_specs = [pl.BlockSpec((tm, d), lambda i, s: (i, 0)),
                pl.BlockSpec((1, d), lambda i, s: (0, 0))]
    args = [x, g.reshape(1, d)]
    lead = nf % 2
    for half in range(2):
        blk = lambda s, half=half: jnp.maximum(2 * s - lead + half, half * (lead + 1))
        in_specs += [pl.BlockSpec((d, tf), lambda i, s, blk=blk: (0, blk(s))),
                     pl.BlockSpec((d, tf), lambda i, s, blk=blk: (0, blk(s))),
                     pl.BlockSpec((tf, d), lambda i, s, blk=blk: (blk(s), 0))]
        args += [w1a, w1b, w2h]
    if final_g is not None:
        in_specs.append(pl.BlockSpec((1, d), lambda i, s: (0, 0)))
        args.append(final_g.reshape(1, d))
    return pl.pallas_call(
        functools.partial(_ffn_kernel, final_norm=final_g is not None, nf=nf),
        out_shape=jax.ShapeDtypeStruct((n, d), F32),
        grid=(n // tm, -(-nf // 2)),
        in_specs=in_specs,
        out_specs=pl.BlockSpec((tm, d), lambda i, f: (i, 0)),
        scratch_shapes=[pltpu.VMEM((tm, d), MXU_DTYPE)],
        compiler_params=_params(("parallel", "arbitrary"), vmem),
        name="ffn",
    )(*args)


def _matmul_residual_kernel(*refs, n_in):
    x_ref = refs[0]
    o_ref = refs[1 + 2 * n_in]
    acc = x_ref[...]
    for h_ref, w_ref in zip(refs[1:1 + n_in], refs[1 + n_in:1 + 2 * n_in]):
        acc = acc + _dot(h_ref[...], w_ref[...])
    o_ref[...] = acc


def matmul_residual(x, hs, ws):
    n, d = x.shape

    def vmem_for(tm, tn):
        return (2 * (2 * _nbytes((tm, tn), F32) + sum(_nbytes((tm, h.shape[1]), h.dtype) for h in hs)
                     + sum(_nbytes((w.shape[0], tn), w.dtype) for w in ws))
                + _nbytes((tm, tn), F32) + (8 << 20))

    tm, tn = next((a, b) for a, b in ((512, d), (1024, 1024), (512, 1024), (256, 512), (128, 128), (n, d))
                  if n % a == 0 and d % b == 0 and (vmem_for(a, b) <= TILE_VMEM_BUDGET or a <= 128))
    vmem = vmem_for(tm, tn)
    in_specs = [pl.BlockSpec((tm, tn), lambda i, j: (i, j))]
    for h in hs:
        in_specs.append(pl.BlockSpec((tm, h.shape[1]), lambda i, j: (i, 0)))
    for w in ws:
        in_specs.append(pl.BlockSpec((w.shape[0], tn), lambda i, j: (0, j)))
    return pl.pallas_call(
        functools.partial(_matmul_residual_kernel, n_in=len(hs)),
        out_shape=jax.ShapeDtypeStruct((n, d), F32),
        grid=(n // tm, d // tn),
        in_specs=in_specs,
        out_specs=pl.BlockSpec((tm, tn), lambda i, j: (i, j)),
        compiler_params=_params(("parallel", "arbitrary"), vmem),
        name="matmul_residual",
    )(x, *hs, *ws)


def _mem_attn_kernel(x_ref, g_ref, wq_ref, k_ref, v_ref, wo_ref, o_ref, att_ref, *, bt, tq, heads, hd):
    head_axis = len(k_ref.shape) == 4
    xf = x_ref[...]
    xn = _rms(xf, g_ref[...]).astype(MXU_DTYPE)
    q = _dot(xn, wq_ref[...]).astype(MXU_DTYPE)
    scale = hd ** -0.5
    for b in range(bt):
        for h in range(heads):
            qh = q[b * tq:(b + 1) * tq, h * hd:(h + 1) * hd]
            if head_axis:
                kh = k_ref[b, :, h, :].astype(MXU_DTYPE)
                vh = v_ref[b, :, h, :].astype(MXU_DTYPE)
            else:
                kh = k_ref[b, :, h * hd:(h + 1) * hd].astype(MXU_DTYPE)
                vh = v_ref[b, :, h * hd:(h + 1) * hd].astype(MXU_DTYPE)
            s = _dot_nt(qh, kh) * scale
            m = jnp.max(s, axis=-1, keepdims=True)
            p = jnp.exp(s - m)
            l = jnp.sum(p, axis=-1, keepdims=True)
            oh = _dot(p.astype(MXU_DTYPE), vh) / l
            att_ref[b * tq:(b + 1) * tq, h * hd:(h + 1) * hd] = oh.astype(att_ref.dtype)
    o_ref[...] = xf + _dot(att_ref[...], wo_ref[...])


def mem_attn(x, g, wq, mem_k, mem_v, wo, *, batch, heads, layer=None):
    n, d = x.shape
    t = n // batch
    cached = layer is not None
    m = mem_k.shape[2] if cached else mem_k.shape[1]
    inner = wq.shape[1]
    hd = inner // heads
    if t >= 128:
        bt, tq = 1, _pick(t, (512, 256, 128))
    else:
        bt, tq = _pick(batch, (4, 2, 1)), t
    nt = t // tq
    rows = bt * tq
    if cached:
        kv_spec = pl.BlockSpec((None, bt, m, heads, hd), lambda b, i: (layer, b, 0, 0, 0))
        kv_bytes = _nbytes((bt, m, max(heads, V7X_SUBLANES), hd), mem_k.dtype)
    else:
        kv_spec = pl.BlockSpec((bt, m, inner), lambda b, i: (b, 0, 0))
        kv_bytes = _nbytes((bt, m, inner), mem_k.dtype)
    vmem = (2 * (2 * _nbytes((rows, d), F32) + 2 * kv_bytes + 2 * _nbytes((d, inner), wq.dtype))
            + 2 * _nbytes((rows, d), F32) + (8 << 20))
    return pl.pallas_call(
        functools.partial(_mem_attn_kernel, bt=bt, tq=tq, heads=heads, hd=hd),
        out_shape=jax.ShapeDtypeStruct((n, d), F32),
        grid=(batch // bt, nt),
        in_specs=[pl.BlockSpec((rows, d), lambda b, i: (b * nt + i, 0)),
                  pl.BlockSpec((1, d), lambda b, i: (0, 0)),
                  pl.BlockSpec((d, inner), lambda b, i: (0, 0)),
                  kv_spec, kv_spec,
                  pl.BlockSpec((inner, d), lambda b, i: (0, 0))],
        out_specs=pl.BlockSpec((rows, d), lambda b, i: (b * nt + i, 0)),
        scratch_shapes=[pltpu.VMEM((rows, inner), MXU_DTYPE)],
        compiler_params=_params(("parallel", "arbitrary"), vmem),
        name="mem_attn",
    )(x, g.reshape(1, d), wq, mem_k, mem_v, wo)


def _mla_prep_kernel(*refs, q_lora, kv_lora, rope_dim, rope_slabs, qscale, heads, nope):
    expand = len(refs) == 12
    if expand:
        (p_ref, qn_ref, kvn_ref, wuq_ref, cos_ref, sin_ref, wukv_ref,
         q_ref, ckv_ref, kpe_ref, k_ref, v_ref) = refs
    else:
        p_ref, qn_ref, kvn_ref, wuq_ref, cos_ref, sin_ref, q_ref, ckv_ref, kpe_ref, k_ref = refs
    cos = cos_ref[...]
    sin = sin_ref[...]
    cqn = _rms(p_ref[:, 0:q_lora], qn_ref[...]).astype(MXU_DTYPE)
    q = _dot(cqn, wuq_ref[...])
    if qscale != 1.0:
        q = q * qscale
    q_ref[...] = q.astype(q_ref.dtype)
    for lo in rope_slabs:
        xs = q[:, lo:lo + V7X_LANES]
        q_ref[:, lo:lo + V7X_LANES] = (xs * cos + _swap32(xs) * sin).astype(q_ref.dtype)
    ckv = _rms(p_ref[:, q_lora:q_lora + kv_lora], kvn_ref[...])
    ckv_ref[...] = ckv
    kp = p_ref[:, q_lora + kv_lora:q_lora + kv_lora + V7X_LANES]
    kpr = kp * cos + _swap32(kp) * sin
    kpe_ref[...] = kpr[:, 0:rope_dim]
    kpr_m = kpr.astype(k_ref.dtype)
    if expand:
        kv = _dot(ckv.astype(MXU_DTYPE), wukv_ref[...])
        hw = nope + V7X_LANES
        for h in range(heads):
            k_ref[:, h * hw:h * hw + nope] = kv[:, h * nope:(h + 1) * nope].astype(k_ref.dtype)
            k_ref[:, h * hw + nope:(h + 1) * hw] = kpr_m
        v_ref[...] = kv[:, heads * nope:].astype(v_ref.dtype)
    else:
        k_ref[:, 0:kv_lora] = ckv.astype(k_ref.dtype)
        k_ref[:, kv_lora:kv_lora + V7X_LANES] = kpr_m


def mla_prep(proj, q_norm, kv_norm, w_uq, cos_rows, sin_rows, w_ukv, *, q_lora, kv_lora, rope_dim,
             rope_slabs, qscale, heads, nope):
    n = proj.shape[0]
    qcols = w_uq.shape[1]
    tm = _pick(n, (512, 256, 128))
    expand = w_ukv is not None
    row = lambda w: pl.BlockSpec((tm, w), lambda i: (i, 0))
    full = lambda a: pl.BlockSpec(a.shape, lambda i: (0, 0))
    t = cos_rows.shape[0]
    if t % tm == 0:
        table = pl.BlockSpec((tm, V7X_LANES), lambda i: (i % (t // tm), 0))
    else:
        cos_rows, sin_rows = jnp.tile(cos_rows, (n // t, 1)), jnp.tile(sin_rows, (n // t, 1))
        table = row(V7X_LANES)
    in_specs = [row(proj.shape[1]), pl.BlockSpec((1, q_lora), lambda i: (0, 0)),
                pl.BlockSpec((1, kv_lora), lambda i: (0, 0)), full(w_uq), table, table]
    args = [proj, q_norm.reshape(1, -1), kv_norm.reshape(1, -1), w_uq, cos_rows, sin_rows]
    out_shape = [jax.ShapeDtypeStruct((n, qcols), MXU_DTYPE), jax.ShapeDtypeStruct((n, kv_lora), F32),
                 jax.ShapeDtypeStruct((n, rope_dim), F32)]
    out_specs = [row(qcols), row(kv_lora), row(rope_dim)]
    vmem = (2 * (_nbytes((tm, proj.shape[1]), F32) + _nbytes(w_uq.shape, w_uq.dtype)
                 + _nbytes((tm, qcols), MXU_DTYPE) + 3 * _nbytes((tm, kv_lora + V7X_LANES), F32))
            + 3 * _nbytes((tm, qcols), F32) + (8 << 20))
    if expand:
        kcols = heads * (nope + V7X_LANES)
        vcols = w_ukv.shape[1] - heads * nope
        in_specs.append(full(w_ukv))
        args.append(w_ukv)
        out_shape += [jax.ShapeDtypeStruct((n, kcols), MXU_DTYPE), jax.ShapeDtypeStruct((n, vcols), MXU_DTYPE)]
        out_specs += [row(kcols), row(vcols)]
        vmem += (2 * (_nbytes(w_ukv.shape, w_ukv.dtype) + _nbytes((tm, kcols + vcols), MXU_DTYPE))
                 + 2 * _nbytes((tm, w_ukv.shape[1]), F32))
    else:
        out_shape.append(jax.ShapeDtypeStruct((n, kv_lora + V7X_LANES), MXU_DTYPE))
        out_specs.append(row(kv_lora + V7X_LANES))
    return pl.pallas_call(
        functools.partial(_mla_prep_kernel, q_lora=q_lora, kv_lora=kv_lora, rope_dim=rope_dim,
                          rope_slabs=rope_slabs, qscale=qscale, heads=heads, nope=nope),
        out_shape=tuple(out_shape),
        grid=(n // tm,),
        in_specs=in_specs,
        out_specs=tuple(out_specs),
        compiler_params=_params(("parallel",), vmem),
        name="mla_prep",
    )(*args)


def _lanes(x, width):
    reps = width // V7X_LANES
    return x if reps == 1 else jnp.concatenate([x] * reps, axis=-1)


def _mha_attn_kernel(qi_ref, kj_ref, flag_ref, q_ref, k_ref, v_ref, o_ref, acc_ref, m_ref,
                     *, heads, hw, vd, tq, tk, q_offset, kv_len):
    step = pl.program_id(1)
    i = qi_ref[step]
    j = kj_ref[step]
    flags = flag_ref[step]
    aw = vd + V7X_LANES

    @pl.when(j == 0)
    def _():
        m_ref[...] = jnp.full_like(m_ref, NEG)
        acc_ref[...] = jnp.zeros_like(acc_ref)

    ones = jnp.ones((tk, V7X_LANES), MXU_DTYPE)

    def all_heads(bias):
        for h in range(heads):
            s = _dot_nt(q_ref[0, :, h * hw:(h + 1) * hw], k_ref[0, :, h * hw:(h + 1) * hw])
            if bias is not None:
                s = s + bias
            m_old = m_ref[h]
            m_new = jnp.maximum(m_old, jnp.max(s, axis=-1, keepdims=True))
            alpha = jnp.exp2(m_old - m_new)
            p = jnp.exp2(s - _lanes(m_new, tk)).astype(MXU_DTYPE)
            m_ref[h] = m_new
            v_ext = jnp.concatenate([v_ref[0, :, h * vd:(h + 1) * vd], ones], axis=-1)
            acs = slice(h * aw, (h + 1) * aw)
            acc_ref[:, acs] = _lanes(alpha, aw) * acc_ref[:, acs] + _dot(p, v_ext)

    @pl.when((flags & 2) == 0)
    def _():
        all_heads(None)

    @pl.when((flags & 2) != 0)
    def _():
        qpos = q_offset + i * tq + lax.broadcasted_iota(jnp.int32, (tq, tk), 0)
        kpos = j * tk + lax.broadcasted_iota(jnp.int32, (tq, tk), 1)
        visible = ((kpos // CHUNK) <= (qpos // CHUNK)) & (kpos < kv_len)
        all_heads(jnp.where(visible, 0.0, NEG))

    @pl.when((flags & 1) != 0)
    def _():
        for h in range(heads):
            num = acc_ref[:, h * aw:h * aw + vd]
            den = _lanes(acc_ref[:, h * aw + vd:(h + 1) * aw], vd)
            o_ref[0, :, h * vd:(h + 1) * vd] = (num / den).astype(o_ref.dtype)


def mha_attn(q, k, v, *, heads, tq, tk, q_offset, kv_len):
    b, t, _ = q.shape
    s = k.shape[1]
    hw = q.shape[2] // heads
    vd = v.shape[2] // heads
    nq, nk = t // tq, s // tk
    qi, kj, flags = [], [], []
    for i in range(nq):
        first_q = q_offset + i * tq
        last_key = ((first_q + tq - 1) // CHUNK + 1) * CHUNK - 1
        jl = min(last_key // tk, nk - 1)
        for j in range(jl + 1):
            fully_visible = ((j + 1) * tk - 1) // CHUNK <= first_q // CHUNK and (j + 1) * tk <= kv_len
            qi.append(i)
            kj.append(j)
            flags.append(int(j == jl) + 2 * int(not fully_visible))
    sched = [jnp.asarray(a, jnp.int32) for a in (qi, kj, flags)]
    aw = vd + V7X_LANES
    vmem = (2 * (2 * _nbytes((tq, heads * hw), q.dtype) + 2 * _nbytes((tk, heads * vd), v.dtype)
                 + _nbytes((tq, heads * vd), MXU_DTYPE))
            + _nbytes((tq, heads * aw), F32) + _nbytes((heads, tq, V7X_LANES), F32)
            + 6 * _nbytes((tq, tk), F32) + (8 << 20))
    grid_spec = pltpu.PrefetchScalarGridSpec(
        num_scalar_prefetch=3,
        grid=(b, len(qi)),
        in_specs=[pl.BlockSpec((1, tq, heads * hw), lambda bb, p, qi_r, kj_r, l_r: (bb, qi_r[p], 0)),
                  pl.BlockSpec((1, tk, heads * hw), lambda bb, p, qi_r, kj_r, l_r: (bb, kj_r[p], 0)),
                  pl.BlockSpec((1, tk, heads * vd), lambda bb, p, qi_r, kj_r, l_r: (bb, kj_r[p], 0))],
        out_specs=pl.BlockSpec((1, tq, heads * vd), lambda bb, p, qi_r, kj_r, l_r: (bb, qi_r[p], 0)),
        scratch_shapes=[pltpu.VMEM((tq, heads * aw), F32),
                        pltpu.VMEM((heads, tq, V7X_LANES), F32)])
    return pl.pallas_call(
        functools.partial(_mha_attn_kernel, heads=heads, hw=hw, vd=vd, tq=tq, tk=tk,
                          q_offset=q_offset, kv_len=kv_len),
        out_shape=jax.ShapeDtypeStruct((b, t, heads * vd), MXU_DTYPE),
        grid_spec=grid_spec,
        compiler_params=_params(("parallel", "arbitrary"), vmem),
        name="mha_attn",
    )(*sched, q, k, v)


def _mla_decode_kernel(q_ref, pc_ref, pr_ref, kn_ref, wuk_ref, wuv_ref, o_ref, qs_ref,
                       *, heads, t, nope, lat, rope_dim, q_offset, scale):
    past = pc_ref.shape[1]
    rows = heads * t
    rope_lo = heads * nope
    for h in range(heads):
        qn = q_ref[0, :, h * nope:(h + 1) * nope]
        qs_ref[h * t:(h + 1) * t, 0:lat] = (_dot(qn, wuk_ref[h]) * scale).astype(qs_ref.dtype)
        qr = q_ref[0, :, rope_lo + h * V7X_LANES:rope_lo + (h + 1) * V7X_LANES]
        qs_ref[h * t:(h + 1) * t, lat:lat + V7X_LANES] = (qr.astype(F32) * scale).astype(qs_ref.dtype)

    def bias(k0, n):
        qpos = q_offset + lax.broadcasted_iota(jnp.int32, (t, n), 0)
        kpos = k0 + lax.broadcasted_iota(jnp.int32, (t, n), 1)
        return jnp.where((kpos // CHUNK) <= (qpos // CHUNK), 0.0, NEG)

    def masked(s, b):
        return (s.reshape(heads, t, s.shape[1]) + b[None]).reshape(rows, s.shape[1])

    kc = pc_ref[0].astype(MXU_DTYPE)
    kr = pr_ref[0].astype(MXU_DTYPE)
    kn = kn_ref[0]
    s_past = masked(_dot_nt(qs_ref[:, 0:lat], kc) + _dot_nt(qs_ref[:, lat:lat + rope_dim], kr), bias(0, past))
    s_new = masked(_dot_nt(qs_ref[...], kn), bias(q_offset, t))
    m = jnp.maximum(jnp.max(s_past, axis=-1, keepdims=True), jnp.max(s_new, axis=-1, keepdims=True))
    p_past = jnp.exp(s_past - m)
    p_new = jnp.exp(s_new - m)
    l = jnp.sum(p_past, axis=-1, keepdims=True) + jnp.sum(p_new, axis=-1, keepdims=True)
    ol = (_dot(p_past.astype(MXU_DTYPE), kc) + _dot(p_new.astype(MXU_DTYPE), kn[:, 0:lat])) / l
    vd = wuv_ref.shape[2]
    for h in range(heads):
        o_ref[0, :, h * vd:(h + 1) * vd] = (
            _dot(ol[h * t:(h + 1) * t, :].astype(MXU_DTYPE), wuv_ref[h]).astype(o_ref.dtype))


def mla_decode(q, past_ckv, past_kpe, kcat, w_uk, w_uv, *, q_offset, scale):
    b, t, _ = q.shape
    past, rope_dim = past_kpe.shape[1], past_kpe.shape[2]
    heads, nope, lat = w_uk.shape
    vd = w_uv.shape[2]
    rows = heads * t
    vmem = (2 * (_nbytes((t, q.shape[2]), q.dtype) + _nbytes((past, lat + V7X_LANES), F32)
                 + 2 * _nbytes(w_uk.shape, w_uk.dtype) + _nbytes((t, heads * vd), MXU_DTYPE))
            + _nbytes((rows, lat + V7X_LANES), MXU_DTYPE) + _nbytes((past, lat + V7X_LANES), MXU_DTYPE)
            + 4 * _nbytes((rows, past), F32) + (8 << 20))
    assert vmem <= V7X_VMEM_BYTES, "cached rows must fit one VMEM block"
    return pl.pallas_call(
        functools.partial(_mla_decode_kernel, heads=heads, t=t, nope=nope, lat=lat, rope_dim=rope_dim,
                          q_offset=q_offset, scale=scale),
        out_shape=jax.ShapeDtypeStruct((b, t, heads * vd), MXU_DTYPE),
        grid=(b,),
        in_specs=[pl.BlockSpec((1, t, q.shape[2]), lambda bb: (bb, 0, 0)),
                  pl.BlockSpec((1, past, lat), lambda bb: (bb, 0, 0)),
                  pl.BlockSpec((1, past, rope_dim), lambda bb: (bb, 0, 0)),
                  pl.BlockSpec((1, t, kcat.shape[2]), lambda bb: (bb, 0, 0)),
                  pl.BlockSpec(w_uk.shape, lambda bb: (0, 0, 0)),
                  pl.BlockSpec(w_uv.shape, lambda bb: (0, 0, 0))],
        out_specs=pl.BlockSpec((1, t, heads * vd), lambda bb: (bb, 0, 0)),
        scratch_shapes=[pltpu.VMEM((rows, lat + V7X_LANES), MXU_DTYPE)],
        compiler_params=_params(("parallel",), vmem),
        name="mla_decode",
    )(q, past_ckv, past_kpe, kcat, w_uk, w_uv)


class _ScanPlan(NamedTuple):
    body: Callable
    args: list
    in_specs: list
    out_shape: list
    out_specs: list
    scratch: list
    vmem: int


def _ssd_kernel(*refs, L, groups, hpg, hdim, nstate, has_state):
    if has_state:
        (z_ref, xs_ref, b_ref, c_ref, dt_ref, cw_ref, cb_ref, dtb_ref, alog_ref, dsk_ref, nrm_ref,
         cst_ref, hst_ref, y_ref, ncv_ref, nst_ref, xbuf, h_scr, cum_scr, xdt_scr, yin_scr) = refs
    else:
        (z_ref, xs_ref, b_ref, c_ref, dt_ref, cw_ref, cb_ref, dtb_ref, alog_ref, dsk_ref, nrm_ref,
         cst_ref, y_ref, ncv_ref, nst_ref, xbuf, h_scr, cum_scr, xdt_scr, yin_scr) = refs
        hst_ref = None
    c = pl.program_id(1)
    nc = pl.num_programs(1)
    dx = groups * hpg * hdim
    dn = groups * nstate
    gw = hpg * hdim
    pad = CONV_PAD_ROWS

    @pl.when(c == 0)
    def _():
        xbuf[0:pad, :] = cst_ref[0]
        xbuf[pad:2 * pad, :] = jnp.zeros((pad, xbuf.shape[1]), F32)
        if has_state:
            for hh in range(groups * hpg):
                h_scr[hh // hpg, :, (hh % hpg) * hdim:(hh % hpg + 1) * hdim] = hst_ref[0, hh]
        else:
            h_scr[...] = jnp.zeros_like(h_scr)

    row = lax.broadcasted_iota(jnp.int32, (L, L), 0)
    col = lax.broadcasted_iota(jnp.int32, (L, L), 1)
    causal = row >= col

    xin = jnp.concatenate([xs_ref[...], b_ref[...], c_ref[...]], axis=1)
    xin_f = xin.astype(F32)
    acc = xin_f * cw_ref[CONV_K - 1:CONV_K, :] + cb_ref[...]
    for jj in range(CONV_K - 1):
        shift = jnp.where(row - col == CONV_K - 1 - jj, 1.0, 0.0).astype(MXU_DTYPE)
        acc = acc + _dot(shift, xin) * cw_ref[jj:jj + 1, :]
    corr = xbuf[pad - CONV_K + 1:2 * pad - CONV_K + 1, :] * cw_ref[0:1, :]
    for jj in range(1, CONV_K - 1):
        lo = pad - CONV_K + 1 + jj
        corr = corr + xbuf[lo:lo + pad, :] * cw_ref[jj:jj + 1, :]
    acc = jnp.concatenate([acc[0:pad] + corr, acc[pad:]], axis=0)
    xc = _silu(acc)
    tail = xin_f[L - pad:L, :]
    xbuf[0:pad, :] = tail

    @pl.when(c == nc - 1)
    def _():
        ncv_ref[0] = tail

    dtr = dt_ref[...] + dtb_ref[...]
    dt = jnp.maximum(dtr, 0.0) + jnp.log1p(jnp.exp(-jnp.abs(dtr)))
    la = dt * (-jnp.exp(alog_ref[...]) * LOG2E)
    tri = jnp.where(causal, 1.0, 0.0).astype(MXU_DTYPE)
    cum = sum(_dot(tri, piece) for piece in _split3(la))
    nh = groups * hpg
    eye = jnp.where(lax.broadcasted_iota(jnp.int32, (V7X_LANES, V7X_LANES), 0)
                    == lax.broadcasted_iota(jnp.int32, (V7X_LANES, V7X_LANES), 1), 1.0, 0.0).astype(MXU_DTYPE)
    cum_t = sum(_dot_nt(eye, piece) for piece in _split3(cum))

    half = lax.broadcasted_iota(jnp.int32, (L, V7X_LANES), 1) < hdim
    per_vreg = V7X_LANES // hdim
    for g in range(groups):
        bg = xc[:, dx + g * nstate:dx + (g + 1) * nstate]
        cg = xc[:, dx + dn + g * nstate:dx + dn + (g + 1) * nstate].astype(MXU_DTYPE)
        qk = _dot_nt(cg, bg.astype(MXU_DTYPE))
        for sl in range(gw // V7X_LANES):
            lane0 = g * gw + sl * V7X_LANES
            h0 = lane0 // hdim
            cb = [jnp.broadcast_to(cum[:, h0 + u:h0 + u + 1], (L, V7X_LANES)) for u in range(per_vreg)]
            db = [jnp.broadcast_to(dt[:, h0 + u:h0 + u + 1], (L, V7X_LANES)) for u in range(per_vreg)]
            cum_e = jnp.where(half, cb[0], cb[1])
            dt_e = jnp.where(half, db[0], db[1])
            xdt = xc[:, lane0:lane0 + V7X_LANES] * dt_e
            xdt_m = xdt.astype(MXU_DTYPE)
            ys = []
            for u in range(per_vreg):
                seg = cb[u][:, 0:L] - cum_t[h0 + u:h0 + u + 1, :]
                decay = jnp.exp2(jnp.where(causal, seg, NEG))
                ys.append(_dot((qk * decay).astype(MXU_DTYPE), xdt_m))
            cum_scr[:, lane0:lane0 + V7X_LANES] = cum_e
            xdt_scr[:, lane0:lane0 + V7X_LANES] = xdt
            yin_scr[:, lane0:lane0 + V7X_LANES] = jnp.where(half, ys[0], ys[1])

    for g in range(groups):
        gs = slice(g * gw, (g + 1) * gw)
        bg = xc[:, dx + g * nstate:dx + (g + 1) * nstate].astype(MXU_DTYPE)
        cg = xc[:, dx + dn + g * nstate:dx + dn + (g + 1) * nstate].astype(MXU_DTYPE)
        cum_g = cum_scr[:, gs]
        last = cum_scr[L - 1:L, gs]
        hg = h_scr[g]
        y_inter = _dot(cg, hg.astype(MXU_DTYPE)) * jnp.exp2(cum_g)
        wx = (jnp.exp2(last - cum_g) * xdt_scr[:, gs]).astype(MXU_DTYPE)
        h_scr[g] = jnp.exp2(last) * hg + _dot_tn(bg, wx)
        y = yin_scr[:, gs] + y_inter + dsk_ref[:, gs] * xc[:, gs]
        y = y * _silu(z_ref[:, gs].astype(F32))
        y_ref[:, gs] = (_rms(y) * nrm_ref[:, gs]).astype(y_ref.dtype)

    @pl.when(c == nc - 1)
    def _():
        for hh in range(groups * hpg):
            nst_ref[0, hh] = h_scr[hh // hpg, :, (hh % hpg) * hdim:(hh % hpg + 1) * hdim]


def ssd_scan(proj, dt_raw, col, conv_w, conv_b, dt_bias, a_log, d_skip_e, ssd_norm, conv_state, ssd_state,
             *, batch, L, groups, hpg, hdim, nstate):
    n = proj.shape[0]
    t = n // batch
    nc = t // L
    dx = groups * hpg * hdim
    dn = groups * nstate
    cdim = dx + 2 * dn
    has_state = ssd_state is not None
    rowmap = lambda b, c: b * nc + c
    in_specs = [pl.BlockSpec((L, dx), lambda b, c: (rowmap(b, c), col["z"])),
                pl.BlockSpec((L, dx), lambda b, c: (rowmap(b, c), col["xs"])),
                pl.BlockSpec((L, dn), lambda b, c: (rowmap(b, c), col["B"])),
                pl.BlockSpec((L, dn), lambda b, c: (rowmap(b, c), col["C"])),
                pl.BlockSpec((L, V7X_LANES), lambda b, c: (rowmap(b, c), 0)),
                pl.BlockSpec((CONV_K, cdim), lambda b, c: (0, 0)),
                pl.BlockSpec((1, cdim), lambda b, c: (0, 0)),
                pl.BlockSpec((1, V7X_LANES), lambda b, c: (0, 0)),
                pl.BlockSpec((1, V7X_LANES), lambda b, c: (0, 0)),
                pl.BlockSpec((1, dx), lambda b, c: (0, 0)),
                pl.BlockSpec((1, dx), lambda b, c: (0, 0)),
                pl.BlockSpec((1, CONV_PAD_ROWS, cdim), lambda b, c: (b, 0, 0))]
    args = [proj, proj, proj, proj, dt_raw, conv_w, conv_b.reshape(1, cdim), dt_bias, a_log,
            d_skip_e, ssd_norm.reshape(1, dx), conv_state]
    if has_state:
        in_specs.append(pl.BlockSpec((1, groups * hpg, nstate, hdim), lambda b, c: (b, 0, 0, 0)))
        args.append(ssd_state)
    state_bytes = _nbytes((groups * hpg, nstate, V7X_LANES), F32)
    vmem = (2 * (2 * _nbytes((L, dx), F32) + 2 * _nbytes((L, dn), F32) + _nbytes((L, dx), MXU_DTYPE)
                 + 2 * _nbytes((CONV_PAD_ROWS, cdim), F32) + 2 * state_bytes)
            + state_bytes + 12 * _nbytes((L + CONV_PAD_ROWS, cdim), F32) + (8 << 20))
    return _ScanPlan(
        body=functools.partial(_ssd_kernel, L=L, groups=groups, hpg=hpg, hdim=hdim, nstate=nstate,
                               has_state=has_state),
        args=args, in_specs=in_specs,
        out_shape=[jax.ShapeDtypeStruct((n, dx), MXU_DTYPE),
                   jax.ShapeDtypeStruct((batch, CONV_PAD_ROWS, cdim), F32),
                   jax.ShapeDtypeStruct((batch, groups * hpg, nstate, hdim), F32)],
        out_specs=[pl.BlockSpec((L, dx), lambda b, c: (rowmap(b, c), 0)),
                   pl.BlockSpec((1, CONV_PAD_ROWS, cdim), lambda b, c: (b, 0, 0)),
                   pl.BlockSpec((1, groups * hpg, nstate, hdim), lambda b, c: (b, 0, 0, 0))],
        scratch=[pltpu.VMEM((2 * CONV_PAD_ROWS, cdim), F32),
                 pltpu.VMEM((groups, nstate, hpg * hdim), F32),
                 pltpu.VMEM((L, dx), F32),
                 pltpu.VMEM((L, dx), F32),
                 pltpu.VMEM((L, dx), F32)],
        vmem=vmem)


def _ret_kernel(*refs, L, heads, hd, has_state):
    if has_state:
        (q_ref, k_ref, v_ref, gate_ref, cos_ref, sin_ref, dec_ref, ecum_ref, wv_ref, sdec_ref, st_ref,
         o_ref, nst_ref, s_scr) = refs
    else:
        (q_ref, k_ref, v_ref, gate_ref, cos_ref, sin_ref, dec_ref, ecum_ref, wv_ref, sdec_ref,
         o_ref, nst_ref, s_scr) = refs
        st_ref = None
    c = pl.program_id(1)
    nc = pl.num_programs(1)
    half = hd // 2

    @pl.when(c == 0)
    def _():
        if has_state:
            s_scr[...] = st_ref[0]
        else:
            s_scr[...] = jnp.zeros_like(s_scr)

    cos = cos_ref[...]
    sin = sin_ref[...]

    def rope(ref, h, mult):
        x1 = ref[:, h * hd:h * hd + half].astype(F32)
        x2 = ref[:, h * hd + half:(h + 1) * hd].astype(F32)
        out = jnp.concatenate([x1 * cos - x2 * sin, x2 * cos + x1 * sin], axis=-1)
        return (out * mult).astype(MXU_DTYPE) if mult != 1.0 else out.astype(MXU_DTYPE)

    for h in range(heads):
        hs = slice(h * hd, (h + 1) * hd)
        qr = rope(q_ref, h, 1.0)
        kr = rope(k_ref, h, hd ** -0.5)
        vf = v_ref[:, hs]
        qk = _dot_nt(qr, kr)
        y = _dot((qk * dec_ref[h]).astype(MXU_DTYPE), vf.astype(MXU_DTYPE))
        s_old = s_scr[h]
        y = y + _dot(qr, s_old.astype(MXU_DTYPE)) * ecum_ref[h]
        s_scr[h] = sdec_ref[h] * s_old + _dot_tn(kr, (wv_ref[h] * vf.astype(F32)).astype(MXU_DTYPE))
        o_ref[:, hs] = (_rms(y) * _silu(gate_ref[:, hs].astype(F32))).astype(o_ref.dtype)

    @pl.when(c == nc - 1)
    def _():
        nst_ref[0] = s_scr[...]


def ret_scan(proj, col, cos_rows, sin_rows, ret_state, *, batch, L, heads, hd):
    n = proj.shape[0]
    t = n // batch
    nc = t // L
    inner = heads * hd
    has_state = ret_state is not None
    lg = jnp.log1p(-jnp.exp2(-5.0 - jnp.arange(heads, dtype=F32)))[:, None, None]
    li = jnp.arange(L, dtype=F32)
    diff = li[:, None] - li[None, :]
    dec = jnp.where(diff >= 0, jnp.exp(jnp.where(diff >= 0, diff, 0.0)[None] * lg), 0.0)
    ecum = jnp.broadcast_to(jnp.exp((li[None, :, None] + 1.0) * lg), (heads, L, hd))
    wv = jnp.broadcast_to(jnp.exp((L - 1.0 - li[None, :, None]) * lg), (heads, L, hd))
    sdec = jnp.broadcast_to(jnp.exp(L * lg), (heads, 1, hd))
    rowmap = lambda b, c: b * nc + c
    in_specs = [pl.BlockSpec((L, inner), lambda b, c: (rowmap(b, c), col["q"])),
                pl.BlockSpec((L, inner), lambda b, c: (rowmap(b, c), col["k"])),
                pl.BlockSpec((L, inner), lambda b, c: (rowmap(b, c), col["v"])),
                pl.BlockSpec((L, inner), lambda b, c: (rowmap(b, c), col["gate"])),
                pl.BlockSpec((L, hd // 2), lambda b, c: (c, 0)),
                pl.BlockSpec((L, hd // 2), lambda b, c: (c, 0)),
                pl.BlockSpec((heads, L, L), lambda b, c: (0, 0, 0)),
                pl.BlockSpec((heads, L, hd), lambda b, c: (0, 0, 0)),
                pl.BlockSpec((heads, L, hd), lambda b, c: (0, 0, 0)),
                pl.BlockSpec((heads, 1, hd), lambda b, c: (0, 0, 0))]
    args = [proj, proj, proj, proj, cos_rows, sin_rows, dec, ecum, wv, sdec]
    if has_state:
        in_specs.append(pl.BlockSpec((1, heads, hd, hd), lambda b, c: (b, 0, 0, 0)))
        args.append(ret_state)
    state_bytes = _nbytes((heads, hd, hd), F32)
    vmem = (2 * (4 * _nbytes((L, inner), F32) + _nbytes((L, inner), MXU_DTYPE) + _nbytes((heads, L, L), F32)
                 + 2 * _nbytes((heads, L, hd), F32) + 2 * state_bytes)
            + state_bytes + 16 * _nbytes((L, hd), F32) + (8 << 20))
    return _ScanPlan(
        body=functools.partial(_ret_kernel, L=L, heads=heads, hd=hd, has_state=has_state),
        args=args, in_specs=in_specs,
        out_shape=[jax.ShapeDtypeStruct((n, inner), MXU_DTYPE),
                   jax.ShapeDtypeStruct((batch, heads, hd, hd), F32)],
        out_specs=[pl.BlockSpec((L, inner), lambda b, c: (rowmap(b, c), 0)),
                   pl.BlockSpec((1, heads, hd, hd), lambda b, c: (b, 0, 0, 0))],
        scratch=[pltpu.VMEM((heads, hd, hd), F32)],
        vmem=vmem)


def _scan_pair_kernel(*refs, bodies, n_in, n_out, n_scr):
    ins, outs, scrs = refs[:sum(n_in)], refs[sum(n_in):sum(n_in) + sum(n_out)], refs[sum(n_in) + sum(n_out):]
    for k, body in enumerate(bodies):
        take = lambda seq, counts: seq[sum(counts[:k]):sum(counts[:k + 1])]
        body(*take(ins, n_in), *take(outs, n_out), *take(scrs, n_scr))


def run_scans(plans, *, batch, nc, name):
    outs = pl.pallas_call(
        functools.partial(_scan_pair_kernel, bodies=tuple(p.body for p in plans),
                          n_in=tuple(len(p.args) for p in plans),
                          n_out=tuple(len(p.out_shape) for p in plans),
                          n_scr=tuple(len(p.scratch) for p in plans)),
        out_shape=tuple(s for p in plans for s in p.out_shape),
        grid=(batch, nc),
        in_specs=[s for p in plans for s in p.in_specs],
        out_specs=tuple(s for p in plans for s in p.out_specs),
        scratch_shapes=[s for p in plans for s in p.scratch],
        compiler_params=_params(("parallel", "arbitrary"), sum(p.vmem for p in plans)),
        name=name,
    )(*[a for p in plans for a in p.args])
    split, k = [], 0
    for p in plans:
        split.append(outs[k:k + len(p.out_shape)])
        k += len(p.out_shape)
    return split


def _rope_tables(pos, half, reps):
    inv = ROPE_BASE ** (-jnp.arange(half, dtype=F32) / half)
    ang = pos.astype(F32)[:, None] * inv[None, :]
    cos, sin = jnp.cos(ang), jnp.sin(ang)
    if reps == 0:
        return cos, sin
    c = jnp.tile(jnp.concatenate([cos, cos], axis=-1), (1, reps))
    s = jnp.tile(jnp.concatenate([-sin, sin], axis=-1), (1, reps))
    return c, s


def _mixer_even(x, batch, pos, norm_g, prm, conv_state, ssd_state, ret_state, dims, L_ssd, L_ret):
    d = x.shape[1]
    groups, hpg, hdim, nstate, rheads, rhd = dims
    proj, dt_raw = rms_matmul(x, norm_g, prm["w_in"], out_dtype=MXU_DTYPE, w_side=prm["w_dt"])
    col = prm["col"]
    ssd_plan = ssd_scan(proj, dt_raw, col, prm["conv_w"], prm["conv_b"], prm["dt_bias"],
                        prm["a_log"], prm["d_skip_e"], prm["ssd_norm"], conv_state, ssd_state,
                        batch=batch, L=L_ssd, groups=groups, hpg=hpg, hdim=hdim, nstate=nstate)
    cos_rows, sin_rows = _rope_tables(pos, rhd // 2, 0)
    ret_plan = ret_scan(proj, col, cos_rows, sin_rows, ret_state, batch=batch, L=L_ret, heads=rheads, hd=rhd)
    t = x.shape[0] // batch
    (y, new_conv, new_ssd), = run_scans([ssd_plan], batch=batch, nc=t // L_ssd, name="ssd_scan")
    (o, new_ret), = run_scans([ret_plan], batch=batch, nc=t // L_ret, name="ret_scan")
    x = matmul_residual(x, [y, o], [prm["w_out_ssd"], prm["w_out_ret"]])
    return x, (new_conv, new_ssd, new_ret)


def _prep_even(w_in, conv_w, conv_b, dt_bias, a_log, d_skip, ssd_norm, w_out, dims):
    groups, hpg, hdim, nstate, rheads, rhd = dims
    d = w_in.shape[0]
    dx = groups * hpg * hdim
    dn = groups * nstate
    nh = groups * hpg
    ri = rheads * rhd
    o_z, o_xs, o_b, o_c, o_dt = 0, dx, 2 * dx, 2 * dx + dn, 2 * dx + 2 * dn
    o_q = o_dt + nh
    seg = lambda lo, w: w_in[:, lo:lo + w]
    dt_pad = jnp.zeros((d, V7X_LANES - nh), w_in.dtype)
    w_main = jnp.concatenate([seg(o_z, dx), seg(o_q, ri), seg(o_q + ri, ri), seg(o_q + 2 * ri, ri),
                              seg(o_q + 3 * ri, ri), seg(o_xs, dx), seg(o_b, dn), seg(o_c, dn)],
                             axis=1).astype(MXU_DTYPE)
    w_dt = jnp.concatenate([seg(o_dt, nh), dt_pad], axis=1).astype(MXU_DTYPE)
    assert dx == ri and dx % dn == 0 and dn % V7X_LANES == 0
    col = {"z": 0, "q": 1, "k": 2, "v": 3, "gate": 4, "xs": 5, "B": 6 * dx // dn, "C": 6 * dx // dn + 1}
    pad1 = lambda v: jnp.pad(v.astype(F32), (0, V7X_LANES - nh)).reshape(1, V7X_LANES)
    return {"w_in": w_main, "w_dt": w_dt, "col": col, "conv_w": conv_w, "conv_b": conv_b,
            "dt_bias": pad1(dt_bias), "a_log": pad1(a_log),
            "d_skip_e": jnp.repeat(d_skip, hdim).reshape(1, dx), "ssd_norm": ssd_norm,
            "w_out_ssd": w_out[:dx].astype(MXU_DTYPE), "w_out_ret": w_out[dx:].astype(MXU_DTYPE)}


def _prep_odd(w_in, q_norm, kv_norm, w_uq, w_uk, w_uv, w_out, rope_dim):
    d = w_in.shape[0]
    q_lora = q_norm.shape[0]
    kv_lora, heads, nope = w_uk.shape
    kp = w_in[:, q_lora + kv_lora:]
    w_in_new = jnp.concatenate([w_in[:, :q_lora + kv_lora], kp, kp], axis=1).astype(MXU_DTYPE)
    assert 2 * rope_dim == V7X_LANES
    wq = w_uq.reshape(q_lora, heads, nope + rope_dim)
    wq_nope = wq[:, :, :nope]
    wq_rope = jnp.pad(wq[:, :, nope:], ((0, 0), (0, 0), (0, V7X_LANES - rope_dim)))
    hw = nope + V7X_LANES
    return {"w_in": w_in_new, "q_norm": q_norm, "kv_norm": kv_norm,
            "w_uq_lat": jnp.concatenate([wq_nope.reshape(q_lora, heads * nope),
                                         wq_rope.reshape(q_lora, heads * V7X_LANES)], axis=1).astype(MXU_DTYPE),
            "slabs_lat": tuple(heads * nope + h * V7X_LANES for h in range(heads)),
            "w_uq_head": jnp.concatenate([wq_nope, wq_rope], axis=2).reshape(q_lora, heads * hw).astype(MXU_DTYPE),
            "slabs_head": tuple(h * hw + nope for h in range(heads)),
            "w_ukv": jnp.concatenate([w_uk.reshape(kv_lora, heads * nope),
                                      w_uv.reshape(kv_lora, -1)], axis=1).astype(MXU_DTYPE),
            "w_uk": jnp.transpose(w_uk, (1, 2, 0)).astype(MXU_DTYPE),
            "w_uv": jnp.transpose(w_uv, (1, 0, 2)).astype(MXU_DTYPE),
            "w_out": w_out.astype(MXU_DTYPE)}


def _mixer_odd(x, batch, pos, norm_g, prm, past, rope_dim, q_offset):
    n, d = x.shape
    t = n // batch
    q_lora = prm["q_norm"].shape[0]
    kv_lora = prm["kv_norm"].shape[0]
    heads, nope, lat = prm["w_uk"].shape
    proj = rms_matmul(x, norm_g, prm["w_in"])
    cos_rows, sin_rows = _rope_tables(pos, rope_dim // 2, V7X_LANES // rope_dim)
    scale = (nope + rope_dim) ** -0.5
    dims = dict(q_lora=q_lora, kv_lora=kv_lora, rope_dim=rope_dim, heads=heads, nope=nope)
    if past is None:
        q, ckv, kpe, k, v = mla_prep(proj, prm["q_norm"], prm["kv_norm"], prm["w_uq_head"], cos_rows, sin_rows,
                                     prm["w_ukv"], rope_slabs=prm["slabs_head"], qscale=scale * math.log2(math.e),
                                     **dims)
        tq = tk = _pick(t, (512, 256, 128, 64))
        shp = lambda a: a.reshape(batch, t, a.shape[1])
        o = mha_attn(shp(q), shp(k), shp(v), heads=heads, tq=tq, tk=tk, q_offset=q_offset, kv_len=t)
    else:
        q, ckv, kpe, kcat = mla_prep(proj, prm["q_norm"], prm["kv_norm"], prm["w_uq_lat"], cos_rows, sin_rows,
                                     None, rope_slabs=prm["slabs_lat"], qscale=1.0, **dims)
        q = q.reshape(batch, t, q.shape[1])
        kcat = kcat.reshape(batch, t, kcat.shape[1])
        o = mla_decode(q, past[0], past[1], kcat, prm["w_uk"], prm["w_uv"], q_offset=q_offset, scale=scale)
    x = matmul_residual(x, [o.reshape(n, o.shape[2])], [prm["w_out"]])
    return x, (ckv.reshape(batch, t, kv_lora), kpe.reshape(batch, t, rope_dim))


def kernel(x_prompt, x_sample, mem_prompt, state_conv, state_ssd, state_ret, cache_ckv, cache_kpe,
           cache_mem_k, cache_mem_v, norms, ffn_w1, ffn_w2, mem_norm, w_mq, w_mkv, w_mo,
           ab_w_in, ab_conv_w, ab_conv_b, ab_dt_bias, ab_a_log, ab_d_skip, ab_ssd_norm, ab_w_out,
           c_w_in, c_q_norm, c_kv_norm, c_w_uq, c_w_uk, c_w_uv, c_w_out, final_norm):
    bp, tp, d = x_prompt.shape
    bs, ts, _ = x_sample.shape
    depth = norms.shape[0]
    assert depth >= 1
    past_len = cache_ckv.shape[2]
    mem_tokens = mem_prompt.shape[1]
    mem_heads, mem_hd = cache_mem_k.shape[3], cache_mem_k.shape[4]
    mem_inner = mem_heads * mem_hd
    ssd_heads, nstate, hdim = state_ssd.shape[2], state_ssd.shape[3], state_ssd.shape[4]
    cdim = state_conv.shape[3]
    groups = (cdim - ssd_heads * hdim) // (2 * nstate)
    hpg = ssd_heads // groups
    rheads, rhd = state_ret.shape[2], state_ret.shape[3]
    dims = (groups, hpg, hdim, nstate, rheads, rhd)
    rope_dim = cache_kpe.shape[3]

    pos_p = jnp.arange(tp)
    pos_s = past_len + jnp.arange(ts)
    xp = x_prompt.reshape(bp * tp, d)
    xs = x_sample.reshape(bs * ts, d)
    L_ssd_p, L_ret_p = _pick(tp, (128, 64)), _pick(tp, (256, 128, 64))
    L_s = _pick(ts, (128, 64))

    outs = {k: [] for k in ("conv_p", "ssd_p", "ret_p", "ckv_p", "kpe_p", "memk_p", "memv_p",
                            "conv_s", "ssd_s", "ret_s", "ckv_s", "kpe_s")}

    for i in range(depth):
        j = i // 2
        closing = final_norm if i == depth - 1 else None
        wq_m = w_mq[i].astype(MXU_DTYPE)
        wo_m = w_mo[i].astype(MXU_DTYPE)
        mkv = rms_matmul(mem_prompt.reshape(bp * mem_tokens, d), mem_norm[i], w_mkv[i].astype(MXU_DTYPE))
        mk_p = mkv[:, :mem_inner].reshape(bp, mem_tokens, mem_inner)
        mv_p = mkv[:, mem_inner:].reshape(bp, mem_tokens, mem_inner)
        outs["memk_p"].append(mk_p.reshape(bp, mem_tokens, mem_heads, mem_hd))
        outs["memv_p"].append(mv_p.reshape(bp, mem_tokens, mem_heads, mem_hd))

        xs, w1a, w1b, w2h = ffn_cast(xs, norms[i, 0], ffn_w1, ffn_w2, i, 0)
        xp = ffn(xp, norms[i, 0], w1a, w1b, w2h)
        if i % 2 == 0:
            prm = _prep_even(ab_w_in[j], ab_conv_w[j], ab_conv_b[j], ab_dt_bias[j], ab_a_log[j],
                             ab_d_skip[j], ab_ssd_norm[j], ab_w_out[j], dims)
            zero_conv = jnp.zeros((bp, CONV_PAD_ROWS, cdim), F32)
            xp, st_p = _mixer_even(xp, bp, pos_p, norms[i, 1], prm, zero_conv, None, None, dims, L_ssd_p, L_ret_p)
            conv_in = jnp.pad(state_conv[j], ((0, 0), (CONV_PAD_ROWS - (CONV_K - 1), 0), (0, 0)))
            xs, st_s = _mixer_even(xs, bs, pos_s, norms[i, 1], prm, conv_in, state_ssd[j],
                                   state_ret[j], dims, L_s, L_s)
            for tag, st in (("p", st_p), ("s", st_s)):
                outs["conv_" + tag].append(st[0][:, CONV_PAD_ROWS - (CONV_K - 1):, :])
                outs["ssd_" + tag].append(st[1])
                outs["ret_" + tag].append(st[2])
        else:
            prm = _prep_odd(c_w_in[j], c_q_norm[j], c_kv_norm[j], c_w_uq[j], c_w_uk[j], c_w_uv[j], c_w_out[j],
                            rope_dim)
            xp, st_p = _mixer_odd(xp, bp, pos_p, norms[i, 1], prm, None, rope_dim, 0)
            xs, st_s = _mixer_odd(xs, bs, pos_s, norms[i, 1], prm, (cache_ckv[j], cache_kpe[j]), rope_dim, past_len)
            for tag, st in (("p", st_p), ("s", st_s)):
                outs["ckv_" + tag].append(st[0])
                outs["kpe_" + tag].append(st[1])
        xp = mem_attn(xp, norms[i, 2], wq_m, mk_p, mv_p, wo_m, batch=bp, heads=mem_heads)
        xs = mem_attn(xs, norms[i, 2], wq_m, cache_mem_k, cache_mem_v, wo_m, batch=bs, heads=mem_heads, layer=i)
        xs, w1a, w1b, w2h = ffn_cast(xs, norms[i, 3], ffn_w1, ffn_w2, i, 1, final_g=closing)
        xp = ffn(xp, norms[i, 3], w1a, w1b, w2h, final_g=closing)

    y_prompt = xp.reshape(bp, tp, d)
    y_sample = xs.reshape(bs, ts, d)
    st = lambda k: jnp.stack(outs[k])
    return (y_prompt, y_sample, st("conv_p"), st("ssd_p"), st("ret_p"), st("ckv_p"), st("kpe_p"),
            st("memk_p"), st("memv_p"), st("conv_s"), st("ssd_s"), st("ret_s"), st("ckv_s"), st("kpe_s"))
```

```python
import functools
import math
from typing import Callable, NamedTuple

import jax
import jax.numpy as jnp
from jax import lax
from jax.experimental import pallas as pl
from jax.experimental.pallas import tpu as pltpu

F32 = jnp.float32
MXU_DTYPE = jnp.bfloat16

EPS = 1e-6
CHUNK = 64
ROPE_BASE = 10000.0
CONV_K = 4
NEG = -1e30
LOG2E = math.log2(math.e)

V7X_VMEM_BYTES = 64 * 1024 * 1024
TILE_VMEM_BUDGET = 48 * 1024 * 1024
V7X_LANES = 128
V7X_SUBLANES = 8
CONV_PAD_ROWS = 8


def _params(semantics, vmem_bytes):
    limit = min(int(vmem_bytes), V7X_VMEM_BYTES - (4 << 20))
    return pltpu.CompilerParams(dimension_semantics=semantics, vmem_limit_bytes=limit)


def _nbytes(shape, dtype):
    return math.prod(shape) * jnp.dtype(dtype).itemsize


def _pick(n, prefs):
    for p in prefs:
        if n % p == 0:
            return p
    return n


def _dot(a, b):
    return jnp.dot(a, b, preferred_element_type=F32)


def _dot_nt(a, b):
    return lax.dot_general(a, b, (((1,), (1,)), ((), ())), preferred_element_type=F32)


def _dot_tn(a, b):
    return lax.dot_general(a, b, (((0,), (0,)), ((), ())), preferred_element_type=F32)


def _split3(x):
    hi = x.astype(MXU_DTYPE)
    r = x - hi.astype(F32)
    mid = r.astype(MXU_DTYPE)
    lo = (r - mid.astype(F32)).astype(MXU_DTYPE)
    return hi, mid, lo


def _rms(xf, g=None):
    y = xf * lax.rsqrt(jnp.mean(xf * xf, axis=-1, keepdims=True) + EPS)
    return y if g is None else y * g


def _silu(a):
    return a * (1.0 / (1.0 + jnp.exp(-a)))


def _swap32(x):
    w = x.shape[-1]
    lane = lax.broadcasted_iota(jnp.int32, x.shape, x.ndim - 1)
    fwd = pltpu.roll(x, w - 32, x.ndim - 1)
    bwd = pltpu.roll(x, 32, x.ndim - 1)
    return jnp.where((lane & 63) < 32, fwd, bwd)


def _rms_matmul_kernel(*refs, side):
    if side:
        x_ref, g_ref, w_ref, ws_ref, o_ref, os_ref, xn_ref = refs
    else:
        x_ref, g_ref, w_ref, o_ref, xn_ref = refs

    @pl.when(pl.program_id(1) == 0)
    def _():
        xn_ref[...] = _rms(x_ref[...], g_ref[...]).astype(xn_ref.dtype)
        if side:
            os_ref[...] = _dot(xn_ref[...], ws_ref[...])

    o_ref[...] = _dot(xn_ref[...], w_ref[...]).astype(o_ref.dtype)


def rms_matmul(x, g, w, *, out_dtype=F32, w_side=None):
    n, d = x.shape
    nout = w.shape[1]
    side = w_side is not None
    tn = _pick(nout, (1920, 1152, 1024, 512, 256, 128))

    def vmem_for(tm):
        return (2 * (_nbytes((tm, d), F32) + _nbytes((d, tn), w.dtype) + _nbytes((tm, tn), out_dtype))
                + _nbytes((tm, d), MXU_DTYPE) + _nbytes((tm, tn), F32) + (8 << 20))

    tm = next(t for t in (1024, 512, 256, 128, n) if n % t == 0 and (vmem_for(t) <= TILE_VMEM_BUDGET or t <= 128))
    vmem = vmem_for(tm)
    in_specs = [pl.BlockSpec((tm, d), lambda i, j: (i, 0)),
                pl.BlockSpec((1, d), lambda i, j: (0, 0)),
                pl.BlockSpec((d, tn), lambda i, j: (0, j))]
    args = [x, g.reshape(1, d), w]
    out_shape = jax.ShapeDtypeStruct((n, nout), out_dtype)
    out_specs = pl.BlockSpec((tm, tn), lambda i, j: (i, j))
    if side:
        ns = w_side.shape[1]
        in_specs.append(pl.BlockSpec((d, ns), lambda i, j: (0, 0)))
        args.append(w_side)
        out_shape = (out_shape, jax.ShapeDtypeStruct((n, ns), F32))
        out_specs = (out_specs, pl.BlockSpec((tm, ns), lambda i, j: (i, 0)))
        vmem += 2 * (_nbytes((d, ns), w_side.dtype) + _nbytes((tm, ns), F32))
    return pl.pallas_call(
        functools.partial(_rms_matmul_kernel, side=side),
        out_shape=out_shape,
        grid=(n // tm, nout // tn),
        in_specs=in_specs,
        out_specs=out_specs,
        scratch_shapes=[pltpu.VMEM((tm, d), MXU_DTYPE)],
        compiler_params=_params(("parallel", "arbitrary"), vmem),
        name="rms_matmul",
    )(*args)


def _ffn_kernel(*refs, final_norm, nf):
    x_ref, g_ref = refs[0:2]
    blocks = (refs[2:5], refs[5:8])
    gf_ref = refs[8] if final_norm else None
    o_ref, xn_ref = refs[-2:]
    s = pl.program_id(1)

    @pl.when(s == 0)
    def _():
        xf = x_ref[...]
        xn_ref[...] = _rms(xf, g_ref[...]).astype(xn_ref.dtype)
        o_ref[...] = xf

    def down(w1a_ref, w1b_ref, w2_ref):
        xn = xn_ref[...]
        a = _dot(xn, w1a_ref[...])
        b = _dot(xn, w1b_ref[...])
        return _dot((_silu(a) * b).astype(MXU_DTYPE), w2_ref[...])

    if nf % 2 == 0:
        o_ref[...] += down(*blocks[0]) + down(*blocks[1])
    else:
        @pl.when(s == 0)
        def _():
            o_ref[...] += down(*blocks[0])

        @pl.when(s > 0)
        def _():
            o_ref[...] += down(*blocks[0]) + down(*blocks[1])

    if final_norm:
        @pl.when(s == pl.num_programs(1) - 1)
        def _():
            o_ref[...] = _rms(o_ref[...], gf_ref[...])


def _ffn_cast_kernel(*refs, final_norm):
    x_ref, g_ref, w1a_ref, w1b_ref, w2_ref = refs[0:5]
    gf_ref = refs[5] if final_norm else None
    o_ref, w1a_o, w1b_o, w2_o, xn_ref = refs[-5:]
    s = pl.program_id(1)

    @pl.when(s == 0)
    def _():
        xf = x_ref[...]
        xn_ref[...] = _rms(xf, g_ref[...]).astype(xn_ref.dtype)
        o_ref[...] = xf

    w1a_o[...] = w1a_ref[...].astype(w1a_o.dtype)
    w1b_o[...] = w1b_ref[...].astype(w1b_o.dtype)
    w2_o[...] = (0.5 * w2_ref[...]).astype(w2_o.dtype)
    xn = xn_ref[...]
    h = (_silu(_dot(xn, w1a_o[...])) * _dot(xn, w1b_o[...])).astype(MXU_DTYPE)
    o_ref[...] += _dot(h, w2_o[...])

    if final_norm:
        @pl.when(s == pl.num_programs(1) - 1)
        def _():
            o_ref[...] = _rms(o_ref[...], gf_ref[...])


def ffn_cast(x, g, w1, w2, layer, which, final_g=None):
    n, d = x.shape
    dff = w2.shape[2]
    tm = _pick(n, (512, 256, 128))
    tf = _pick(dff, (256, 128))
    nf = dff // tf
    vmem = (2 * (2 * _nbytes((tm, d), F32) + 3 * _nbytes((d, tf), F32) + 3 * _nbytes((d, tf), MXU_DTYPE))
            + _nbytes((tm, d), MXU_DTYPE) + _nbytes((tm, d), F32) + 6 * _nbytes((tm, tf), F32) + (6 << 20))
    in_specs = [pl.BlockSpec((tm, d), lambda i, s: (i, 0)),
                pl.BlockSpec((1, d), lambda i, s: (0, 0)),
                pl.BlockSpec((None, None, d, tf), lambda i, s: (layer, which, 0, s)),
                pl.BlockSpec((None, None, d, tf), lambda i, s: (layer, which, 0, s + nf)),
                pl.BlockSpec((None, None, tf, d), lambda i, s: (layer, which, s, 0))]
    args = [x, g.reshape(1, d), w1, w1, w2]
    if final_g is not None:
        in_specs.append(pl.BlockSpec((1, d), lambda i, s: (0, 0)))
        args.append(final_g.reshape(1, d))
    return pl.pallas_call(
        functools.partial(_ffn_cast_kernel, final_norm=final_g is not None),
        out_shape=(jax.ShapeDtypeStruct((n, d), F32),
                   jax.ShapeDtypeStruct((d, dff), MXU_DTYPE),
                   jax.ShapeDtypeStruct((d, dff), MXU_DTYPE),
                   jax.ShapeDtypeStruct((dff, d), MXU_DTYPE)),
        grid=(n // tm, nf),
        in_specs=in_specs,
        out_specs=(pl.BlockSpec((tm, d), lambda i, s: (i, 0)),
                   pl.BlockSpec((d, tf), lambda i, s: (0, s)),
                   pl.BlockSpec((d, tf), lambda i, s: (0, s)),
                   pl.BlockSpec((tf, d), lambda i, s: (s, 0))),
        scratch_shapes=[pltpu.VMEM((tm, d), MXU_DTYPE)],
        compiler_params=_params(("arbitrary", "arbitrary"), vmem),
        name="ffn_cast",
    )(*args)


def ffn(x, g, w1a, w1b, w2h, final_g=None):
    n, d = x.shape
    dff = w2h.shape[0]
    tm = _pick(n, (512, 256, 128))
    tf = _pick(dff, (512, 256, 128))
    nf = dff // tf
    vmem = (2 * (2 * _nbytes((tm, d), F32) + 6 * _nbytes((d, tf), w1a.dtype))
            + _nbytes((tm, d), MXU_DTYPE) + _nbytes((tm, d), F32) + 6 * _nbytes((tm, tf), F32) + (6 << 20))
    in_specs = [pl.BlockSpec((tm, d), lambda i, s: (i, 0)),
                pl.BlockSpec((1, d), lambda i, s: (0, 0))]
    args = [x, g.reshape(1, d)]
    lead = nf % 2
    for half in range(2):
        blk = lambda s, half=half: jnp.maximum(2 * s - lead + half, half * (lead + 1))
        in_specs += [pl.BlockSpec((d, tf), lambda i, s, blk=blk: (0, blk(s))),
                     pl.BlockSpec((d, tf), lambda i, s, blk=blk: (0, blk(s))),
                     pl.BlockSpec((tf, d), lambda i, s, blk=blk: (blk(s), 0))]
        args += [w1a, w1b, w2h]
    if final_g is not None:
        in_specs.append(pl.BlockSpec((1, d), lambda i, s: (0, 0)))
        args.append(final_g.reshape(1, d))
    return pl.pallas_call(
        functools.partial(_ffn_kernel, final_norm=final_g is not None, nf=nf),
        out_shape=jax.ShapeDtypeStruct((n, d), F32),
        grid=(n // tm, -(-nf // 2)),
        in_specs=in_specs,
        out_specs=pl.BlockSpec((tm, d), lambda i, f: (i, 0)),
        scratch_shapes=[pltpu.VMEM((tm, d), MXU_DTYPE)],
        compiler_params=_params(("parallel", "arbitrary"), vmem),
        name="ffn",
    )(*args)


def _matmul_residual_kernel(*refs, n_in):
    x_ref = refs[0]
    o_ref = refs[1 + 2 * n_in]
    acc = x_ref[...]
    for h_ref, w_ref in zip(refs[1:1 + n_in], refs[1 + n_in:1 + 2 * n_in]):
        acc = acc + _dot(h_ref[...], w_ref[...])
    o_ref[...] = acc


def matmul_residual(x, hs, ws):
    n, d = x.shape

    def vmem_for(tm, tn):
        w_bufs = 1 if tn == d else 2
        return (2 * (2 * _nbytes((tm, tn), F32) + sum(_nbytes((tm, h.shape[1]), h.dtype) for h in hs))
                + w_bufs * sum(_nbytes((w.shape[0], tn), w.dtype) for w in ws)
                + _nbytes((tm, tn), F32) + (4 << 20))

    tm, tn = next((a, b) for a, b in ((512, d), (1024, 1024), (512, 1024), (256, 512), (128, 128), (n, d))
                  if n % a == 0 and d % b == 0 and (vmem_for(a, b) <= TILE_VMEM_BUDGET or a <= 128))
    vmem = vmem_for(tm, tn) + (4 << 20)
    in_specs = [pl.BlockSpec((tm, tn), lambda i, j: (i, j))]
    for h in hs:
        in_specs.append(pl.BlockSpec((tm, h.shape[1]), lambda i, j: (i, 0)))
    for w in ws:
        if tn == d:
            in_specs.append(pl.BlockSpec((w.shape[0], tn), lambda i, j: (0, j), pipeline_mode=pl.Buffered(1)))
        else:
            in_specs.append(pl.BlockSpec((w.shape[0], tn), lambda i, j: (0, j)))
    return pl.pallas_call(
        functools.partial(_matmul_residual_kernel, n_in=len(hs)),
        out_shape=jax.ShapeDtypeStruct((n, d), F32),
        grid=(n // tm, d // tn),
        in_specs=in_specs,
        out_specs=pl.BlockSpec((tm, tn), lambda i, j: (i, j)),
        compiler_params=_params(("parallel", "arbitrary"), vmem),
        name="matmul_residual",
    )(x, *hs, *ws)


def _mem_attn_kernel(x_ref, g_ref, wq_ref, k_ref, v_ref, wo_ref, o_ref, att_ref, *, bt, tq, heads, hd):
    head_axis = len(k_ref.shape) == 4
    xf = x_ref[...]
    xn = _rms(xf, g_ref[...]).astype(MXU_DTYPE)
    q = _dot(xn, wq_ref[...]).astype(MXU_DTYPE)
    scale = hd ** -0.5
    for b in range(bt):
        for h in range(heads):
            qh = q[b * tq:(b + 1) * tq, h * hd:(h + 1) * hd]
            if head_axis:
                kh = k_ref[b, :, h, :].astype(MXU_DTYPE)
                vh = v_ref[b, :, h, :].astype(MXU_DTYPE)
            else:
                kh = k_ref[b, :, h * hd:(h + 1) * hd].astype(MXU_DTYPE)
                vh = v_ref[b, :, h * hd:(h + 1) * hd].astype(MXU_DTYPE)
            s = _dot_nt(qh, kh) * scale
            m = jnp.max(s, axis=-1, keepdims=True)
            p = jnp.exp(s - m)
            l = jnp.sum(p, axis=-1, keepdims=True)
            oh = _dot(p.astype(MXU_DTYPE), vh) / l
            att_ref[b * tq:(b + 1) * tq, h * hd:(h + 1) * hd] = oh.astype(att_ref.dtype)
    o_ref[...] = xf + _dot(att_ref[...], wo_ref[...])


def mem_attn(x, g, wq, mem_k, mem_v, wo, *, batch, heads, layer=None):
    n, d = x.shape
    t = n // batch
    cached = layer is not None
    m = mem_k.shape[2] if cached else mem_k.shape[1]
    inner = wq.shape[1]
    hd = inner // heads
    if t >= 128:
        bt, tq = 1, _pick(t, (512, 256, 128))
    else:
        bt, tq = _pick(batch, (4, 2, 1)), t
    nt = t // tq
    rows = bt * tq
    if cached:
        kv_spec = pl.BlockSpec((None, bt, m, heads, hd), lambda b, i: (layer, b, 0, 0, 0))
        kv_bytes = _nbytes((bt, m, max(heads, V7X_SUBLANES), hd), mem_k.dtype)
    else:
        kv_spec = pl.BlockSpec((bt, m, inner), lambda b, i: (b, 0, 0))
        kv_bytes = _nbytes((bt, m, inner), mem_k.dtype)
    vmem = (2 * (2 * _nbytes((rows, d), F32) + 2 * kv_bytes + 2 * _nbytes((d, inner), wq.dtype))
            + 2 * _nbytes((rows, d), F32) + (8 << 20))
    return pl.pallas_call(
        functools.partial(_mem_attn_kernel, bt=bt, tq=tq, heads=heads, hd=hd),
        out_shape=jax.ShapeDtypeStruct((n, d), F32),
        grid=(batch // bt, nt),
        in_specs=[pl.BlockSpec((rows, d), lambda b, i: (b * nt + i, 0)),
                  pl.BlockSpec((1, d), lambda b, i: (0, 0)),
                  pl.BlockSpec((d, inner), lambda b, i: (0, 0)),
                  kv_spec, kv_spec,
                  pl.BlockSpec((inner, d), lambda b, i: (0, 0))],
        out_specs=pl.BlockSpec((rows, d), lambda b, i: (b * nt + i, 0)),
        scratch_shapes=[pltpu.VMEM((rows, inner), MXU_DTYPE)],
        compiler_params=_params(("parallel", "arbitrary"), vmem),
        name="mem_attn",
    )(x, g.reshape(1, d), wq, mem_k, mem_v, wo)


def _mla_prep_kernel(*refs, q_lora, kv_lora, rope_dim, rope_slabs, qscale, heads, nope):
    expand = len(refs) == 12
    if expand:
        (p_ref, qn_ref, kvn_ref, wuq_ref, cos_ref, sin_ref, wukv_ref,
         q_ref, ckv_ref, kpe_ref, k_ref, v_ref) = refs
    else:
        p_ref, qn_ref, kvn_ref, wuq_ref, cos_ref, sin_ref, q_ref, ckv_ref, kpe_ref, k_ref = refs
    cos = cos_ref[...]
    sin = sin_ref[...]
    cqn = _rms(p_ref[:, 0:q_lora], qn_ref[...]).astype(MXU_DTYPE)
    q = _dot(cqn, wuq_ref[...])
    if qscale != 1.0:
        q = q * qscale
    q_ref[...] = q.astype(q_ref.dtype)
    for lo in rope_slabs:
        xs = q[:, lo:lo + V7X_LANES]
        q_ref[:, lo:lo + V7X_LANES] = (xs * cos + _swap32(xs) * sin).astype(q_ref.dtype)
    ckv = _rms(p_ref[:, q_lora:q_lora + kv_lora], kvn_ref[...])
    ckv_ref[...] = ckv
    kp = p_ref[:, q_lora + kv_lora:q_lora + kv_lora + V7X_LANES]
    kpr = kp * cos + _swap32(kp) * sin
    kpe_ref[...] = kpr[:, 0:rope_dim]
    kpr_m = kpr.astype(k_ref.dtype)
    if expand:
        kv = _dot(ckv.astype(MXU_DTYPE), wukv_ref[...])
        hw = nope + V7X_LANES
        for h in range(heads):
            k_ref[:, h * hw:h * hw + nope] = kv[:, h * nope:(h + 1) * nope].astype(k_ref.dtype)
            k_ref[:, h * hw + nope:(h + 1) * hw] = kpr_m
        v_ref[...] = kv[:, heads * nope:].astype(v_ref.dtype)
    else:
        k_ref[:, 0:kv_lora] = ckv.astype(k_ref.dtype)
        k_ref[:, kv_lora:kv_lora + V7X_LANES] = kpr_m


def mla_prep(proj, q_norm, kv_norm, w_uq, cos_rows, sin_rows, w_ukv, *, q_lora, kv_lora, rope_dim,
             rope_slabs, qscale, heads, nope):
    n = proj.shape[0]
    qcols = w_uq.shape[1]
    tm = _pick(n, (512, 256, 128))
    expand = w_ukv is not None
    row = lambda w: pl.BlockSpec((tm, w), lambda i: (i, 0))
    full = lambda a: pl.BlockSpec(a.shape, lambda i: (0, 0))
    t = cos_rows.shape[0]
    if t % tm == 0:
        table = pl.BlockSpec((tm, V7X_LANES), lambda i: (i % (t // tm), 0))
    else:
        cos_rows, sin_rows = jnp.tile(cos_rows, (n // t, 1)), jnp.tile(sin_rows, (n // t, 1))
        table = row(V7X_LANES)
    in_specs = [row(proj.shape[1]), pl.BlockSpec((1, q_lora), lambda i: (0, 0)),
                pl.BlockSpec((1, kv_lora), lambda i: (0, 0)), full(w_uq), table, table]
    args = [proj, q_norm.reshape(1, -1), kv_norm.reshape(1, -1), w_uq, cos_rows, sin_rows]
    out_shape = [jax.ShapeDtypeStruct((n, qcols), MXU_DTYPE), jax.ShapeDtypeStruct((n, kv_lora), F32),
                 jax.ShapeDtypeStruct((n, rope_dim), F32)]
    out_specs = [row(qcols), row(kv_lora), row(rope_dim)]
    vmem = (2 * (_nbytes((tm, proj.shape[1]), F32) + _nbytes(w_uq.shape, w_uq.dtype)
                 + _nbytes((tm, qcols), MXU_DTYPE) + 3 * _nbytes((tm, kv_lora + V7X_LANES), F32))
            + 3 * _nbytes((tm, qcols), F32) + (8 << 20))
    if expand:
        kcols = heads * (nope + V7X_LANES)
        vcols = w_ukv.shape[1] - heads * nope
        in_specs.append(full(w_ukv))
        args.append(w_ukv)
        out_shape += [jax.ShapeDtypeStruct((n, kcols), MXU_DTYPE), jax.ShapeDtypeStruct((n, vcols), MXU_DTYPE)]
        out_specs += [row(kcols), row(vcols)]
        vmem += (2 * (_nbytes(w_ukv.shape, w_ukv.dtype) + _nbytes((tm, kcols + vcols), MXU_DTYPE))
                 + 2 * _nbytes((tm, w_ukv.shape[1]), F32))
    else:
        out_shape.append(jax.ShapeDtypeStruct((n, kv_lora + V7X_LANES), MXU_DTYPE))
        out_specs.append(row(kv_lora + V7X_LANES))
    return pl.pallas_call(
        functools.partial(_mla_prep_kernel, q_lora=q_lora, kv_lora=kv_lora, rope_dim=rope_dim,
                          rope_slabs=rope_slabs, qscale=qscale, heads=heads, nope=nope),
        out_shape=tuple(out_shape),
        grid=(n // tm,),
        in_specs=in_specs,
        out_specs=tuple(out_specs),
        compiler_params=_params(("parallel",), vmem),
        name="mla_prep",
    )(*args)


def _lanes(x, width):
    reps = width // V7X_LANES
    return x if reps == 1 else jnp.concatenate([x] * reps, axis=-1)


def _mha_attn_kernel(qi_ref, kj_ref, flag_ref, q_ref, k_ref, v_ref, o_ref, acc_ref, m_ref,
                     *, heads, hw, vd, tq, tk, q_offset, kv_len):
    step = pl.program_id(1)
    i = qi_ref[step]
    j = kj_ref[step]
    flags = flag_ref[step]
    aw = vd + V7X_LANES

    @pl.when(j == 0)
    def _():
        m_ref[...] = jnp.full_like(m_ref, NEG)
        acc_ref[...] = jnp.zeros_like(acc_ref)

    ones = jnp.ones((tk, V7X_LANES), MXU_DTYPE)

    def all_heads(bias):
        for h in range(heads):
            s = _dot_nt(q_ref[0, :, h * hw:(h + 1) * hw], k_ref[0, :, h * hw:(h + 1) * hw])
            if bias is not None:
                s = s + bias
            m_old = m_ref[h]
            m_new = jnp.maximum(m_old, jnp.max(s, axis=-1, keepdims=True))
            alpha = jnp.exp2(m_old - m_new)
            p = jnp.exp2(s - _lanes(m_new, tk)).astype(MXU_DTYPE)
            m_ref[h] = m_new
            v_ext = jnp.concatenate([v_ref[0, :, h * vd:(h + 1) * vd], ones], axis=-1)
            acs = slice(h * aw, (h + 1) * aw)
            acc_ref[:, acs] = _lanes(alpha, aw) * acc_ref[:, acs] + _dot(p, v_ext)

    @pl.when((flags & 2) == 0)
    def _():
        all_heads(None)

    @pl.when((flags & 2) != 0)
    def _():
        qpos = q_offset + i * tq + lax.broadcasted_iota(jnp.int32, (tq, tk), 0)
        kpos = j * tk + lax.broadcasted_iota(jnp.int32, (tq, tk), 1)
        visible = ((kpos // CHUNK) <= (qpos // CHUNK)) & (kpos < kv_len)
        all_heads(jnp.where(visible, 0.0, NEG))

    @pl.when((flags & 1) != 0)
    def _():
        for h in range(heads):
            num = acc_ref[:, h * aw:h * aw + vd]
            den = _lanes(acc_ref[:, h * aw + vd:(h + 1) * aw], vd)
            o_ref[0, :, h * vd:(h + 1) * vd] = (num / den).astype(o_ref.dtype)


def mha_attn(q, k, v, *, heads, tq, tk, q_offset, kv_len):
    b, t, _ = q.shape
    s = k.shape[1]
    hw = q.shape[2] // heads
    vd = v.shape[2] // heads
    nq, nk = t // tq, s // tk
    qi, kj, flags = [], [], []
    for i in range(nq):
        first_q = q_offset + i * tq
        last_key = ((first_q + tq - 1) // CHUNK + 1) * CHUNK - 1
        jl = min(last_key // tk, nk - 1)
        for j in range(jl + 1):
            fully_visible = ((j + 1) * tk - 1) // CHUNK <= first_q // CHUNK and (j + 1) * tk <= kv_len
            qi.append(i)
            kj.append(j)
            flags.append(int(j == jl) + 2 * int(not fully_visible))
    sched = [jnp.asarray(a, jnp.int32) for a in (qi, kj, flags)]
    aw = vd + V7X_LANES
    vmem = (2 * (2 * _nbytes((tq, heads * hw), q.dtype) + 2 * _nbytes((tk, heads * vd), v.dtype)
                 + _nbytes((tq, heads * vd), MXU_DTYPE))
            + _nbytes((tq, heads * aw), F32) + _nbytes((heads, tq, V7X_LANES), F32)
            + 6 * _nbytes((tq, tk), F32) + (8 << 20))
    grid_spec = pltpu.PrefetchScalarGridSpec(
        num_scalar_prefetch=3,
        grid=(b, len(qi)),
        in_specs=[pl.BlockSpec((1, tq, heads * hw), lambda bb, p, qi_r, kj_r, l_r: (bb, qi_r[p], 0)),
                  pl.BlockSpec((1, tk, heads * hw), lambda bb, p, qi_r, kj_r, l_r: (bb, kj_r[p], 0)),
                  pl.BlockSpec((1, tk, heads * vd), lambda bb, p, qi_r, kj_r, l_r: (bb, kj_r[p], 0))],
        out_specs=pl.BlockSpec((1, tq, heads * vd), lambda bb, p, qi_r, kj_r, l_r: (bb, qi_r[p], 0)),
        scratch_shapes=[pltpu.VMEM((tq, heads * aw), F32),
                        pltpu.VMEM((heads, tq, V7X_LANES), F32)])
    return pl.pallas_call(
        functools.partial(_mha_attn_kernel, heads=heads, hw=hw, vd=vd, tq=tq, tk=tk,
                          q_offset=q_offset, kv_len=kv_len),
        out_shape=jax.ShapeDtypeStruct((b, t, heads * vd), MXU_DTYPE),
        grid_spec=grid_spec,
        compiler_params=_params(("parallel", "arbitrary"), vmem),
        name="mha_attn",
    )(*sched, q, k, v)


def _mla_decode_kernel(q_ref, pc_ref, pr_ref, kn_ref, wuk_ref, wuv_ref, o_ref, qs_ref,
                       *, heads, t, nope, lat, rope_dim, q_offset, scale):
    past = pc_ref.shape[1]
    rows = heads * t
    rope_lo = heads * nope
    for h in range(heads):
        qn = q_ref[0, :, h * nope:(h + 1) * nope]
        qs_ref[h * t:(h + 1) * t, 0:lat] = (_dot(qn, wuk_ref[h]) * scale).astype(qs_ref.dtype)
        qr = q_ref[0, :, rope_lo + h * V7X_LANES:rope_lo + (h + 1) * V7X_LANES]
        qs_ref[h * t:(h + 1) * t, lat:lat + V7X_LANES] = (qr.astype(F32) * scale).astype(qs_ref.dtype)

    def bias(k0, n):
        qpos = q_offset + lax.broadcasted_iota(jnp.int32, (t, n), 0)
        kpos = k0 + lax.broadcasted_iota(jnp.int32, (t, n), 1)
        return jnp.where((kpos // CHUNK) <= (qpos // CHUNK), 0.0, NEG)

    def masked(s, b):
        return (s.reshape(heads, t, s.shape[1]) + b[None]).reshape(rows, s.shape[1])

    kc = pc_ref[0].astype(MXU_DTYPE)
    kr = pr_ref[0].astype(MXU_DTYPE)
    kn = kn_ref[0]
    s_past = masked(_dot_nt(qs_ref[:, 0:lat], kc) + _dot_nt(qs_ref[:, lat:lat + rope_dim], kr), bias(0, past))
    s_new = masked(_dot_nt(qs_ref[...], kn), bias(q_offset, t))
    m = jnp.maximum(jnp.max(s_past, axis=-1, keepdims=True), jnp.max(s_new, axis=-1, keepdims=True))
    p_past = jnp.exp(s_past - m)
    p_new = jnp.exp(s_new - m)
    l = jnp.sum(p_past, axis=-1, keepdims=True) + jnp.sum(p_new, axis=-1, keepdims=True)
    ol = (_dot(p_past.astype(MXU_DTYPE), kc) + _dot(p_new.astype(MXU_DTYPE), kn[:, 0:lat])) / l
    vd = wuv_ref.shape[2]
    for h in range(heads):
        o_ref[0, :, h * vd:(h + 1) * vd] = (
            _dot(ol[h * t:(h + 1) * t, :].astype(MXU_DTYPE), wuv_ref[h]).astype(o_ref.dtype))


def mla_decode(q, past_ckv, past_kpe, kcat, w_uk, w_uv, *, q_offset, scale):
    b, t, _ = q.shape
    past, rope_dim = past_kpe.shape[1], past_kpe.shape[2]
    heads, nope, lat = w_uk.shape
    vd = w_uv.shape[2]
    rows = heads * t
    vmem = (2 * (_nbytes((t, q.shape[2]), q.dtype) + _nbytes((past, lat + V7X_LANES), F32)
                 + 2 * _nbytes(w_uk.shape, w_uk.dtype) + _nbytes((t, heads * vd), MXU_DTYPE))
            + _nbytes((rows, lat + V7X_LANES), MXU_DTYPE) + _nbytes((past, lat + V7X_LANES), MXU_DTYPE)
            + 4 * _nbytes((rows, past), F32) + (8 << 20))
    assert vmem <= V7X_VMEM_BYTES, "cached rows must fit one VMEM block"
    return pl.pallas_call(
        functools.partial(_mla_decode_kernel, heads=heads, t=t, nope=nope, lat=lat, rope_dim=rope_dim,
                          q_offset=q_offset, scale=scale),
        out_shape=jax.ShapeDtypeStruct((b, t, heads * vd), MXU_DTYPE),
        grid=(b,),
        in_specs=[pl.BlockSpec((1, t, q.shape[2]), lambda bb: (bb, 0, 0)),
                  pl.BlockSpec((1, past, lat), lambda bb: (bb, 0, 0)),
                  pl.BlockSpec((1, past, rope_dim), lambda bb: (bb, 0, 0)),
                  pl.BlockSpec((1, t, kcat.shape[2]), lambda bb: (bb, 0, 0)),
                  pl.BlockSpec(w_uk.shape, lambda bb: (0, 0, 0)),
                  pl.BlockSpec(w_uv.shape, lambda bb: (0, 0, 0))],
        out_specs=pl.BlockSpec((1, t, heads * vd), lambda bb: (bb, 0, 0)),
        scratch_shapes=[pltpu.VMEM((rows, lat + V7X_LANES), MXU_DTYPE)],
        compiler_params=_params(("parallel",), vmem),
        name="mla_decode",
    )(q, past_ckv, past_kpe, kcat, w_uk, w_uv)


class _ScanPlan(NamedTuple):
    body: Callable
    args: list
    in_specs: list
    out_shape: list
    out_specs: list
    scratch: list
    vmem: int


def _ssd_kernel(*refs, L, groups, hpg, hdim, nstate, has_state):
    if has_state:
        (z_ref, xs_ref, b_ref, c_ref, dt_ref, cw_ref, cb_ref, dtb_ref, alog_ref, dsk_ref, nrm_ref,
         cst_ref, hst_ref, y_ref, ncv_ref, nst_ref, xbuf, h_scr, cum_scr, xdt_scr, yin_scr) = refs
    else:
        (z_ref, xs_ref, b_ref, c_ref, dt_ref, cw_ref, cb_ref, dtb_ref, alog_ref, dsk_ref, nrm_ref,
         cst_ref, y_ref, ncv_ref, nst_ref, xbuf, h_scr, cum_scr, xdt_scr, yin_scr) = refs
        hst_ref = None
    c = pl.program_id(1)
    nc = pl.num_programs(1)
    dx = groups * hpg * hdim
    dn = groups * nstate
    gw = hpg * hdim
    pad = CONV_PAD_ROWS

    @pl.when(c == 0)
    def _():
        xbuf[0:pad, :] = cst_ref[0]
        xbuf[pad:2 * pad, :] = jnp.zeros((pad, xbuf.shape[1]), F32)
        if has_state:
            h_scr[...] = hst_ref[0]
        else:
            h_scr[...] = jnp.zeros_like(h_scr)

    row = lax.broadcasted_iota(jnp.int32, (L, L), 0)
    col = lax.broadcasted_iota(jnp.int32, (L, L), 1)
    causal = row >= col

    xin = jnp.concatenate([xs_ref[...], b_ref[...], c_ref[...]], axis=1)
    xin_f = xin.astype(F32)
    acc = xin_f * cw_ref[CONV_K - 1:CONV_K, :] + cb_ref[...]
    for jj in range(CONV_K - 1):
        shift = jnp.where(row - col == CONV_K - 1 - jj, 1.0, 0.0).astype(MXU_DTYPE)
        acc = acc + _dot(shift, xin) * cw_ref[jj:jj + 1, :]
    corr = xbuf[pad - CONV_K + 1:2 * pad - CONV_K + 1, :] * cw_ref[0:1, :]
    for jj in range(1, CONV_K - 1):
        lo = pad - CONV_K + 1 + jj
        corr = corr + xbuf[lo:lo + pad, :] * cw_ref[jj:jj + 1, :]
    acc = jnp.concatenate([acc[0:pad] + corr, acc[pad:]], axis=0)
    xc = _silu(acc)
    tail = xin_f[L - pad:L, :]
    xbuf[0:pad, :] = tail

    @pl.when(c == nc - 1)
    def _():
        ncv_ref[0] = tail

    dtr = dt_ref[...] + dtb_ref[...]
    dt = jnp.maximum(dtr, 0.0) + jnp.log1p(jnp.exp(-jnp.abs(dtr)))
    la = dt * (-jnp.exp(alog_ref[...]) * LOG2E)
    tri = jnp.where(causal, 1.0, 0.0).astype(MXU_DTYPE)
    cum = sum(_dot(tri, piece) for piece in _split3(la))
    nh = groups * hpg
    eye = jnp.where(lax.broadcasted_iota(jnp.int32, (V7X_LANES, V7X_LANES), 0)
                    == lax.broadcasted_iota(jnp.int32, (V7X_LANES, V7X_LANES), 1), 1.0, 0.0).astype(MXU_DTYPE)
    cum_t = sum(_dot_nt(eye, piece) for piece in _split3(cum))

    half = lax.broadcasted_iota(jnp.int32, (L, V7X_LANES), 1) < hdim
    per_vreg = V7X_LANES // hdim
    for g in range(groups):
        bg = xc[:, dx + g * nstate:dx + (g + 1) * nstate]
        cg = xc[:, dx + dn + g * nstate:dx + dn + (g + 1) * nstate].astype(MXU_DTYPE)
        qk = _dot_nt(cg, bg.astype(MXU_DTYPE))
        for sl in range(gw // V7X_LANES):
            lane0 = g * gw + sl * V7X_LANES
            h0 = lane0 // hdim
            cb = [jnp.broadcast_to(cum[:, h0 + u:h0 + u + 1], (L, V7X_LANES)) for u in range(per_vreg)]
            db = [jnp.broadcast_to(dt[:, h0 + u:h0 + u + 1], (L, V7X_LANES)) for u in range(per_vreg)]
            cum_e = jnp.where(half, cb[0], cb[1])
            dt_e = jnp.where(half, db[0], db[1])
            xdt = xc[:, lane0:lane0 + V7X_LANES] * dt_e
            xdt_m = xdt.astype(MXU_DTYPE)
            ys = []
            for u in range(per_vreg):
                seg = cb[u][:, 0:L] - cum_t[h0 + u:h0 + u + 1, :]
                decay = jnp.exp2(jnp.where(causal, seg, NEG))
                ys.append(_dot((qk * decay).astype(MXU_DTYPE), xdt_m))
            cum_scr[:, lane0:lane0 + V7X_LANES] = cum_e
            xdt_scr[:, lane0:lane0 + V7X_LANES] = xdt
            yin_scr[:, lane0:lane0 + V7X_LANES] = jnp.where(half, ys[0], ys[1])

    for g in range(groups):
        gs = slice(g * gw, (g + 1) * gw)
        bg = xc[:, dx + g * nstate:dx + (g + 1) * nstate].astype(MXU_DTYPE)
        cg = xc[:, dx + dn + g * nstate:dx + dn + (g + 1) * nstate].astype(MXU_DTYPE)
        cum_g = cum_scr[:, gs]
        last = cum_scr[L - 1:L, gs]
        hg = h_scr[g]
        y_inter = _dot(cg, hg.astype(MXU_DTYPE)) * jnp.exp2(cum_g)
        wx = (jnp.exp2(last - cum_g) * xdt_scr[:, gs]).astype(MXU_DTYPE)
        h_scr[g] = jnp.exp2(last) * hg + _dot_tn(bg, wx)
        y = yin_scr[:, gs] + y_inter + dsk_ref[:, gs] * xc[:, gs]
        y = y * _silu(z_ref[:, gs].astype(F32))
        y_ref[:, gs] = (_rms(y) * nrm_ref[:, gs]).astype(y_ref.dtype)

    @pl.when(c == nc - 1)
    def _():
        nst_ref[0] = h_scr[...]


def ssd_scan(proj, dt_raw, col, conv_w, conv_b, dt_bias, a_log, d_skip_e, ssd_norm, conv_state, ssd_state,
             *, batch, L, groups, hpg, hdim, nstate):
    n = proj.shape[0]
    t = n // batch
    nc = t // L
    dx = groups * hpg * hdim
    dn = groups * nstate
    cdim = dx + 2 * dn
    has_state = ssd_state is not None
    rowmap = lambda b, c: b * nc + c
    in_specs = [pl.BlockSpec((L, dx), lambda b, c: (rowmap(b, c), col["z"])),
                pl.BlockSpec((L, dx), lambda b, c: (rowmap(b, c), col["xs"])),
                pl.BlockSpec((L, dn), lambda b, c: (rowmap(b, c), col["B"])),
                pl.BlockSpec((L, dn), lambda b, c: (rowmap(b, c), col["C"])),
                pl.BlockSpec((L, V7X_LANES), lambda b, c: (rowmap(b, c), 0)),
                pl.BlockSpec((CONV_K, cdim), lambda b, c: (0, 0)),
                pl.BlockSpec((1, cdim), lambda b, c: (0, 0)),
                pl.BlockSpec((1, V7X_LANES), lambda b, c: (0, 0)),
                pl.BlockSpec((1, V7X_LANES), lambda b, c: (0, 0)),
                pl.BlockSpec((1, dx), lambda b, c: (0, 0)),
                pl.BlockSpec((1, dx), lambda b, c: (0, 0)),
                pl.BlockSpec((1, CONV_PAD_ROWS, cdim), lambda b, c: (b, 0, 0))]
    args = [proj, proj, proj, proj, dt_raw, conv_w, conv_b.reshape(1, cdim), dt_bias, a_log,
            d_skip_e, ssd_norm.reshape(1, dx), conv_state]
    if has_state:
        in_specs.append(pl.BlockSpec((1, groups, nstate, hpg * hdim), lambda b, c: (b, 0, 0, 0)))
        args.append(ssd_state)
    state_bytes = _nbytes((groups, nstate, hpg * hdim), F32)
    vmem = (2 * (2 * _nbytes((L, dx), F32) + 2 * _nbytes((L, dn), F32) + _nbytes((L, dx), MXU_DTYPE)
                 + 2 * _nbytes((CONV_PAD_ROWS, cdim), F32) + 2 * state_bytes)
            + state_bytes + 12 * _nbytes((L + CONV_PAD_ROWS, cdim), F32) + (8 << 20))
    return _ScanPlan(
        body=functools.partial(_ssd_kernel, L=L, groups=groups, hpg=hpg, hdim=hdim, nstate=nstate,
                               has_state=has_state),
        args=args, in_specs=in_specs,
        out_shape=[jax.ShapeDtypeStruct((n, dx), MXU_DTYPE),
                   jax.ShapeDtypeStruct((batch, CONV_PAD_ROWS, cdim), F32),
                   jax.ShapeDtypeStruct((batch, groups, nstate, hpg * hdim), F32)],
        out_specs=[pl.BlockSpec((L, dx), lambda b, c: (rowmap(b, c), 0)),
                   pl.BlockSpec((1, CONV_PAD_ROWS, cdim), lambda b, c: (b, 0, 0)),
                   pl.BlockSpec((1, groups, nstate, hpg * hdim), lambda b, c: (b, 0, 0, 0))],
        scratch=[pltpu.VMEM((2 * CONV_PAD_ROWS, cdim), F32),
                 pltpu.VMEM((groups, nstate, hpg * hdim), F32),
                 pltpu.VMEM((L, dx), F32),
                 pltpu.VMEM((L, dx), F32),
                 pltpu.VMEM((L, dx), F32)],
        vmem=vmem)


def _ret_kernel(*refs, L, heads, hd, has_state):
    if has_state:
        (q_ref, k_ref, v_ref, gate_ref, cos_ref, sin_ref, dec_ref, ecum_ref, wv_ref, sdec_ref, st_ref,
         o_ref, nst_ref, s_scr) = refs
    else:
        (q_ref, k_ref, v_ref, gate_ref, cos_ref, sin_ref, dec_ref, ecum_ref, wv_ref, sdec_ref,
         o_ref, nst_ref, s_scr) = refs
        st_ref = None
    c = pl.program_id(1)
    nc = pl.num_programs(1)
    half = hd // 2

    @pl.when(c == 0)
    def _():
        if has_state:
            s_scr[...] = st_ref[0]
        else:
            s_scr[...] = jnp.zeros_like(s_scr)

    cos = cos_ref[...]
    sin = sin_ref[...]

    def rope(ref, h, mult):
        x1 = ref[:, h * hd:h * hd + half].astype(F32)
        x2 = ref[:, h * hd + half:(h + 1) * hd].astype(F32)
        out = jnp.concatenate([x1 * cos - x2 * sin, x2 * cos + x1 * sin], axis=-1)
        return (out * mult).astype(MXU_DTYPE) if mult != 1.0 else out.astype(MXU_DTYPE)

    for h in range(heads):
        hs = slice(h * hd, (h + 1) * hd)
        qr = rope(q_ref, h, 1.0)
        kr = rope(k_ref, h, hd ** -0.5)
        vf = v_ref[:, hs]
        qk = _dot_nt(qr, kr)
        y = _dot((qk * dec_ref[h]).astype(MXU_DTYPE), vf.astype(MXU_DTYPE))
        s_old = s_scr[h]
        y = y + _dot(qr, s_old.astype(MXU_DTYPE)) * ecum_ref[h]
        s_scr[h] = sdec_ref[h] * s_old + _dot_tn(kr, (wv_ref[h] * vf.astype(F32)).astype(MXU_DTYPE))
        o_ref[:, hs] = (_rms(y) * _silu(gate_ref[:, hs].astype(F32))).astype(o_ref.dtype)

    @pl.when(c == nc - 1)
    def _():
        nst_ref[0] = s_scr[...]


def ret_scan(proj, col, cos_rows, sin_rows, ret_state, *, batch, L, heads, hd):
    n = proj.shape[0]
    t = n // batch
    nc = t // L
    inner = heads * hd
    has_state = ret_state is not None
    lg = jnp.log1p(-jnp.exp2(-5.0 - jnp.arange(heads, dtype=F32)))[:, None, None]
    li = jnp.arange(L, dtype=F32)
    diff = li[:, None] - li[None, :]
    dec = jnp.where(diff >= 0, jnp.exp(jnp.where(diff >= 0, diff, 0.0)[None] * lg), 0.0)
    ecum = jnp.broadcast_to(jnp.exp((li[None, :, None] + 1.0) * lg), (heads, L, hd))
    wv = jnp.broadcast_to(jnp.exp((L - 1.0 - li[None, :, None]) * lg), (heads, L, hd))
    sdec = jnp.broadcast_to(jnp.exp(L * lg), (heads, 1, hd))
    rowmap = lambda b, c: b * nc + c
    in_specs = [pl.BlockSpec((L, inner), lambda b, c: (rowmap(b, c), col["q"])),
                pl.BlockSpec((L, inner), lambda b, c: (rowmap(b, c), col["k"])),
                pl.BlockSpec((L, inner), lambda b, c: (rowmap(b, c), col["v"])),
                pl.BlockSpec((L, inner), lambda b, c: (rowmap(b, c), col["gate"])),
                pl.BlockSpec((L, hd // 2), lambda b, c: (c, 0)),
                pl.BlockSpec((L, hd // 2), lambda b, c: (c, 0)),
                pl.BlockSpec((heads, L, L), lambda b, c: (0, 0, 0)),
                pl.BlockSpec((heads, L, hd), lambda b, c: (0, 0, 0)),
                pl.BlockSpec((heads, L, hd), lambda b, c: (0, 0, 0)),
                pl.BlockSpec((heads, 1, hd), lambda b, c: (0, 0, 0))]
    args = [proj, proj, proj, proj, cos_rows, sin_rows, dec, ecum, wv, sdec]
    if has_state:
        in_specs.append(pl.BlockSpec((1, heads, hd, hd), lambda b, c: (b, 0, 0, 0)))
        args.append(ret_state)
    state_bytes = _nbytes((heads, hd, hd), F32)
    vmem = (2 * (4 * _nbytes((L, inner), F32) + _nbytes((L, inner), MXU_DTYPE) + _nbytes((heads, L, L), F32)
                 + 2 * _nbytes((heads, L, hd), F32) + 2 * state_bytes)
            + state_bytes + 16 * _nbytes((L, hd), F32) + (8 << 20))
    return _ScanPlan(
        body=functools.partial(_ret_kernel, L=L, heads=heads, hd=hd, has_state=has_state),
        args=args, in_specs=in_specs,
        out_shape=[jax.ShapeDtypeStruct((n, inner), MXU_DTYPE),
                   jax.ShapeDtypeStruct((batch, heads, hd, hd), F32)],
        out_specs=[pl.BlockSpec((L, inner), lambda b, c: (rowmap(b, c), 0)),
                   pl.BlockSpec((1, heads, hd, hd), lambda b, c: (b, 0, 0, 0))],
        scratch=[pltpu.VMEM((heads, hd, hd), F32)],
        vmem=vmem)


def _scan_pair_kernel(*refs, bodies, n_in, n_out, n_scr):
    ins, outs, scrs = refs[:sum(n_in)], refs[sum(n_in):sum(n_in) + sum(n_out)], refs[sum(n_in) + sum(n_out):]
    for k, body in enumerate(bodies):
        take = lambda seq, counts: seq[sum(counts[:k]):sum(counts[:k + 1])]
        body(*take(ins, n_in), *take(outs, n_out), *take(scrs, n_scr))


def run_scans(plans, *, batch, nc, name):
    outs = pl.pallas_call(
        functools.partial(_scan_pair_kernel, bodies=tuple(p.body for p in plans),
                          n_in=tuple(len(p.args) for p in plans),
                          n_out=tuple(len(p.out_shape) for p in plans),
                          n_scr=tuple(len(p.scratch) for p in plans)),
        out_shape=tuple(s for p in plans for s in p.out_shape),
        grid=(batch, nc),
        in_specs=[s for p in plans for s in p.in_specs],
        out_specs=tuple(s for p in plans for s in p.out_specs),
        scratch_shapes=[s for p in plans for s in p.scratch],
        compiler_params=_params(("parallel", "arbitrary"), sum(p.vmem for p in plans)),
        name=name,
    )(*[a for p in plans for a in p.args])
    split, k = [], 0
    for p in plans:
        split.append(outs[k:k + len(p.out_shape)])
        k += len(p.out_shape)
    return split


def _rope_tables(pos, half, reps):
    inv = ROPE_BASE ** (-jnp.arange(half, dtype=F32) / half)
    ang = pos.astype(F32)[:, None] * inv[None, :]
    cos, sin = jnp.cos(ang), jnp.sin(ang)
    if reps == 0:
        return cos, sin
    c = jnp.tile(jnp.concatenate([cos, cos], axis=-1), (1, reps))
    s = jnp.tile(jnp.concatenate([-sin, sin], axis=-1), (1, reps))
    return c, s


def _mixer_even(x, batch, pos, norm_g, prm, conv_state, ssd_state, ret_state, dims, L_ssd, L_ret):
    d = x.shape[1]
    groups, hpg, hdim, nstate, rheads, rhd = dims
    proj, dt_raw = rms_matmul(x, norm_g, prm["w_in"], out_dtype=MXU_DTYPE, w_side=prm["w_dt"])
    col = prm["col"]
    ssd_plan = ssd_scan(proj, dt_raw, col, prm["conv_w"], prm["conv_b"], prm["dt_bias"],
                        prm["a_log"], prm["d_skip_e"], prm["ssd_norm"], conv_state, ssd_state,
                        batch=batch, L=L_ssd, groups=groups, hpg=hpg, hdim=hdim, nstate=nstate)
    cos_rows, sin_rows = _rope_tables(pos, rhd // 2, 0)
    ret_plan = ret_scan(proj, col, cos_rows, sin_rows, ret_state, batch=batch, L=L_ret, heads=rheads, hd=rhd)
    t = x.shape[0] // batch
    (y, new_conv, new_ssd), = run_scans([ssd_plan], batch=batch, nc=t // L_ssd, name="ssd_scan")
    (o, new_ret), = run_scans([ret_plan], batch=batch, nc=t // L_ret, name="ret_scan")
    x = matmul_residual(x, [y, o], [prm["w_out_ssd"], prm["w_out_ret"]])
    return x, (new_conv, new_ssd, new_ret)


def _prep_even(w_in, conv_w, conv_b, dt_bias, a_log, d_skip, ssd_norm, w_out, dims):
    groups, hpg, hdim, nstate, rheads, rhd = dims
    d = w_in.shape[0]
    dx = groups * hpg * hdim
    dn = groups * nstate
    nh = groups * hpg
    ri = rheads * rhd
    o_z, o_xs, o_b, o_c, o_dt = 0, dx, 2 * dx, 2 * dx + dn, 2 * dx + 2 * dn
    o_q = o_dt + nh
    seg = lambda lo, w: w_in[:, lo:lo + w]
    dt_pad = jnp.zeros((d, V7X_LANES - nh), w_in.dtype)
    w_main = jnp.concatenate([seg(o_z, dx), seg(o_q, ri), seg(o_q + ri, ri), seg(o_q + 2 * ri, ri),
                              seg(o_q + 3 * ri, ri), seg(o_xs, dx), seg(o_b, dn), seg(o_c, dn)],
                             axis=1).astype(MXU_DTYPE)
    w_dt = jnp.concatenate([seg(o_dt, nh), dt_pad], axis=1).astype(MXU_DTYPE)
    assert dx == ri and dx % dn == 0 and dn % V7X_LANES == 0
    col = {"z": 0, "q": 1, "k": 2, "v": 3, "gate": 4, "xs": 5, "B": 6 * dx // dn, "C": 6 * dx // dn + 1}
    pad1 = lambda v: jnp.pad(v.astype(F32), (0, V7X_LANES - nh)).reshape(1, V7X_LANES)
    return {"w_in": w_main, "w_dt": w_dt, "col": col, "conv_w": conv_w, "conv_b": conv_b,
            "dt_bias": pad1(dt_bias), "a_log": pad1(a_log),
            "d_skip_e": jnp.repeat(d_skip, hdim).reshape(1, dx), "ssd_norm": ssd_norm,
            "w_out_ssd": w_out[:dx].astype(MXU_DTYPE), "w_out_ret": w_out[dx:].astype(MXU_DTYPE)}


def _prep_odd(w_in, q_norm, kv_norm, w_uq, w_uk, w_uv, w_out, rope_dim):
    d = w_in.shape[0]
    q_lora = q_norm.shape[0]
    kv_lora, heads, nope = w_uk.shape
    kp = w_in[:, q_lora + kv_lora:]
    w_in_new = jnp.concatenate([w_in[:, :q_lora + kv_lora], kp, kp], axis=1).astype(MXU_DTYPE)
    assert 2 * rope_dim == V7X_LANES
    wq = w_uq.reshape(q_lora, heads, nope + rope_dim)
    wq_nope = wq[:, :, :nope]
    wq_rope = jnp.pad(wq[:, :, nope:], ((0, 0), (0, 0), (0, V7X_LANES - rope_dim)))
    hw = nope + V7X_LANES
    return {"w_in": w_in_new, "q_norm": q_norm, "kv_norm": kv_norm,
            "w_uq_lat": jnp.concatenate([wq_nope.reshape(q_lora, heads * nope),
                                         wq_rope.reshape(q_lora, heads * V7X_LANES)], axis=1).astype(MXU_DTYPE),
            "slabs_lat": tuple(heads * nope + h * V7X_LANES for h in range(heads)),
            "w_uq_head": jnp.concatenate([wq_nope, wq_rope], axis=2).reshape(q_lora, heads * hw).astype(MXU_DTYPE),
            "slabs_head": tuple(h * hw + nope for h in range(heads)),
            "w_ukv": jnp.concatenate([w_uk.reshape(kv_lora, heads * nope),
                                      w_uv.reshape(kv_lora, -1)], axis=1).astype(MXU_DTYPE),
            "w_uk": jnp.transpose(w_uk, (1, 2, 0)).astype(MXU_DTYPE),
            "w_uv": jnp.transpose(w_uv, (1, 0, 2)).astype(MXU_DTYPE),
            "w_out": w_out.astype(MXU_DTYPE)}


def _mixer_odd(x, batch, pos, norm_g, prm, past, rope_dim, q_offset):
    n, d = x.shape
    t = n // batch
    q_lora = prm["q_norm"].shape[0]
    kv_lora = prm["kv_norm"].shape[0]
    heads, nope, lat = prm["w_uk"].shape
    proj = rms_matmul(x, norm_g, prm["w_in"])
    cos_rows, sin_rows = _rope_tables(pos, rope_dim // 2, V7X_LANES // rope_dim)
    scale = (nope + rope_dim) ** -0.5
    dims = dict(q_lora=q_lora, kv_lora=kv_lora, rope_dim=rope_dim, heads=heads, nope=nope)
    if past is None:
        q, ckv, kpe, k, v = mla_prep(proj, prm["q_norm"], prm["kv_norm"], prm["w_uq_head"], cos_rows, sin_rows,
                                     prm["w_ukv"], rope_slabs=prm["slabs_head"], qscale=scale * math.log2(math.e),
                                     **dims)
        tq = tk = _pick(t, (512, 256, 128, 64))
        shp = lambda a: a.reshape(batch, t, a.shape[1])
        o = mha_attn(shp(q), shp(k), shp(v), heads=heads, tq=tq, tk=tk, q_offset=q_offset, kv_len=t)
    else:
        q, ckv, kpe, kcat = mla_prep(proj, prm["q_norm"], prm["kv_norm"], prm["w_uq_lat"], cos_rows, sin_rows,
                                     None, rope_slabs=prm["slabs_lat"], qscale=1.0, **dims)
        q = q.reshape(batch, t, q.shape[1])
        kcat = kcat.reshape(batch, t, kcat.shape[1])
        o = mla_decode(q, past[0], past[1], kcat, prm["w_uk"], prm["w_uv"], q_offset=q_offset, scale=scale)
    x = matmul_residual(x, [o.reshape(n, o.shape[2])], [prm["w_out"]])
    return x, (ckv.reshape(batch, t, kv_lora), kpe.reshape(batch, t, rope_dim))


def kernel(x_prompt, x_sample, mem_prompt, state_conv, state_ssd, state_ret, cache_ckv, cache_kpe,
           cache_mem_k, cache_mem_v, norms, ffn_w1, ffn_w2, mem_norm, w_mq, w_mkv, w_mo,
           ab_w_in, ab_conv_w, ab_conv_b, ab_dt_bias, ab_a_log, ab_d_skip, ab_ssd_norm, ab_w_out,
           c_w_in, c_q_norm, c_kv_norm, c_w_uq, c_w_uk, c_w_uv, c_w_out, final_norm):
    bp, tp, d = x_prompt.shape
    bs, ts, _ = x_sample.shape
    depth = norms.shape[0]
    assert depth >= 1
    past_len = cache_ckv.shape[2]
    mem_tokens = mem_prompt.shape[1]
    mem_heads, mem_hd = cache_mem_k.shape[3], cache_mem_k.shape[4]
    mem_inner = mem_heads * mem_hd
    ssd_heads, nstate, hdim = state_ssd.shape[2], state_ssd.shape[3], state_ssd.shape[4]
    cdim = state_conv.shape[3]
    groups = (cdim - ssd_heads * hdim) // (2 * nstate)
    hpg = ssd_heads // groups
    rheads, rhd = state_ret.shape[2], state_ret.shape[3]
    dims = (groups, hpg, hdim, nstate, rheads, rhd)
    rope_dim = cache_kpe.shape[3]

    pos_p = jnp.arange(tp)
    pos_s = past_len + jnp.arange(ts)
    xp = x_prompt.reshape(bp * tp, d)
    xs = x_sample.reshape(bs * ts, d)
    L_ssd_p, L_ret_p = _pick(tp, (128, 64)), _pick(tp, (256, 128, 64))
    L_s = _pick(ts, (128, 64))

    outs = {k: [] for k in ("conv_p", "ssd_p", "ret_p", "ckv_p", "kpe_p", "memk_p", "memv_p",
                            "conv_s", "ssd_s", "ret_s", "ckv_s", "kpe_s")}

    def to_group_layout(st):
        b = st.shape[0]
        return st.reshape(b, groups, hpg, nstate, hdim).transpose(0, 1, 3, 2, 4).reshape(b, groups, nstate, hpg * hdim)

    def from_group_layout(st):
        b = st.shape[0]
        return st.reshape(b, groups, nstate, hpg, hdim).transpose(0, 1, 3, 2, 4).reshape(b, ssd_heads, nstate, hdim)

    for i in range(depth):
        j = i // 2
        closing = final_norm if i == depth - 1 else None
        wq_m = w_mq[i].astype(MXU_DTYPE)
        wo_m = w_mo[i].astype(MXU_DTYPE)
        mkv = rms_matmul(mem_prompt.reshape(bp * mem_tokens, d), mem_norm[i], w_mkv[i].astype(MXU_DTYPE))
        mk_p = mkv[:, :mem_inner].reshape(bp, mem_tokens, mem_inner)
        mv_p = mkv[:, mem_inner:].reshape(bp, mem_tokens, mem_inner)
        outs["memk_p"].append(mk_p.reshape(bp, mem_tokens, mem_heads, mem_hd))
        outs["memv_p"].append(mv_p.reshape(bp, mem_tokens, mem_heads, mem_hd))

        xs, w1a, w1b, w2h = ffn_cast(xs, norms[i, 0], ffn_w1, ffn_w2, i, 0)
        xp = ffn(xp, norms[i, 0], w1a, w1b, w2h)
        if i % 2 == 0:
            prm = _prep_even(ab_w_in[j], ab_conv_w[j], ab_conv_b[j], ab_dt_bias[j], ab_a_log[j],
                             ab_d_skip[j], ab_ssd_norm[j], ab_w_out[j], dims)
            zero_conv = jnp.zeros((bp, CONV_PAD_ROWS, cdim), F32)
            xp, st_p = _mixer_even(xp, bp, pos_p, norms[i, 1], prm, zero_conv, None, None, dims, L_ssd_p, L_ret_p)
            conv_in = jnp.pad(state_conv[j], ((0, 0), (CONV_PAD_ROWS - (CONV_K - 1), 0), (0, 0)))
            xs, st_s = _mixer_even(xs, bs, pos_s, norms[i, 1], prm, conv_in, to_group_layout(state_ssd[j]),
                                   state_ret[j], dims, L_s, L_s)
            for tag, st in (("p", st_p), ("s", st_s)):
                outs["conv_" + tag].append(st[0][:, CONV_PAD_ROWS - (CONV_K - 1):, :])
                outs["ssd_" + tag].append(from_group_layout(st[1]))
                outs["ret_" + tag].append(st[2])
        else:
            prm = _prep_odd(c_w_in[j], c_q_norm[j], c_kv_norm[j], c_w_uq[j], c_w_uk[j], c_w_uv[j], c_w_out[j],
                            rope_dim)
            xp, st_p = _mixer_odd(xp, bp, pos_p, norms[i, 1], prm, None, rope_dim, 0)
            xs, st_s = _mixer_odd(xs, bs, pos_s, norms[i, 1], prm, (cache_ckv[j], cache_kpe[j]), rope_dim, past_len)
            for tag, st in (("p", st_p), ("s", st_s)):
                outs["ckv_" + tag].append(st[0])
                outs["kpe_" + tag].append(st[1])
        xp = mem_attn(xp, norms[i, 2], wq_m, mk_p, mv_p, wo_m, batch=bp, heads=mem_heads)
        xs = mem_attn(xs, norms[i, 2], wq_m, cache_mem_k, cache_mem_v, wo_m, batch=bs, heads=mem_heads, layer=i)
        xs, w1a, w1b, w2h = ffn_cast(xs, norms[i, 3], ffn_w1, ffn_w2, i, 1, final_g=closing)
        xp = ffn(xp, norms[i, 3], w1a, w1b, w2h, final_g=closing)

    y_prompt = xp.reshape(bp, tp, d)
    y_sample = xs.reshape(bs, ts, d)
    st = lambda k: jnp.stack(outs[k])
    return (y_prompt, y_sample, st("conv_p"), st("ssd_p"), st("ret_p"), st("ckv_p"), st("kpe_p"),
            st("memk_p"), st("memv_p"), st("conv_s"), st("ssd_s"), st("ret_s"), st("ckv_s"), st("kpe_s"))
```
